```python
import math
import jax, jax.numpy as jnp
from jax import lax
import numpy as np

D_MODEL = 1024
BATCH = 16
SEQ = 2048
DEPTH = 1
DEC_BATCH = 32
DEC_SEQ = 32
PAST_LEN = 1024

CHUNK = 64
SSM_WIDTH = 512
SSM_GROUP = 16
SSM_GROUPS = SSM_WIDTH // SSM_GROUP
SSM_STATE = 64
DT_MIN = 1e-3
DT_MAX = 1e-1
MLA_HEADS = 8
NOPE_DIM = 64
ROPE_DIM = 32
V_DIM = 64
MLA_WIDTH = MLA_HEADS * V_DIM
Q_LORA = 256
KV_LORA = 128
MIX_WIDTH = SSM_WIDTH + MLA_WIDTH
IN_WIDTH = 2 * SSM_WIDTH + Q_LORA + KV_LORA + ROPE_DIM + MLA_WIDTH
ROPE_THETA = 10000.0
Q_BLOCK = 128
EPS = 1e-6

kernel_name = 'hymba_s5_mla_streaming_step'


def rmsnorm(x, g):
    xf = x.astype(jnp.float32)
    y = xf * lax.rsqrt(jnp.mean(xf * xf, axis=-1, keepdims=True) + EPS)
    return (y * g.astype(jnp.float32)).astype(x.dtype)


def rope(x, pos):
    half = ROPE_DIM // 2
    inv = ROPE_THETA ** (-jnp.arange(half, dtype=jnp.float32) / half)
    ang = pos.astype(jnp.float32)[:, None] * inv[None, :]
    ang = ang.reshape(ang.shape[0], *([1] * (x.ndim - 3)), half)
    cos, sin = jnp.cos(ang), jnp.sin(ang)
    xf = x.astype(jnp.float32)
    x1, x2 = xf[..., :half], xf[..., half:]
    return jnp.concatenate([x1 * cos - x2 * sin, x1 * sin + x2 * cos], axis=-1).astype(x.dtype)


def _ssm_combine(left, right):
    a_l, b_l = left
    a_r, b_r = right
    return a_r * a_l, a_r * b_l + b_r


def s5_branch(u, h0, p):
    b, s, _ = u.shape
    lam = lax.complex(p['ssm_a_re'].astype(jnp.float32), p['ssm_a_im'].astype(jnp.float32))
    dt = jnp.exp(p['ssm_log_dt'].astype(jnp.float32))[:, None]
    a_bar = jnp.exp(lam * dt)
    b_mat = lax.complex(p['ssm_b_re'].astype(jnp.float32), p['ssm_b_im'].astype(jnp.float32))
    b_bar = ((a_bar - 1.0) / lam)[..., None] * b_mat
    c_mat = lax.complex(p['ssm_c_re'].astype(jnp.float32), p['ssm_c_im'].astype(jnp.float32))
    ug = u.astype(jnp.float32).reshape(b, s, SSM_GROUPS, SSM_GROUP)
    bu = jnp.einsum('gpc,bsgc->bsgp', b_bar, ug.astype(jnp.complex64))
    if h0 is not None:
        bu = bu.at[:, 0].add(a_bar * h0)
    a_seq = jnp.broadcast_to(a_bar, bu.shape)
    _, states = lax.associative_scan(_ssm_combine, (a_seq, bu), axis=1)
    y = jnp.einsum('gcp,bsgp->bsgc', c_mat, states).real
    y = y + p['ssm_d'].astype(jnp.float32).reshape(SSM_GROUPS, SSM_GROUP) * ug
    y = y.reshape(b, s, SSM_WIDTH)
    yg = jax.nn.gelu(y)
    out = yg * jax.nn.sigmoid(yg @ p['w_glu'].astype(jnp.float32) + p['b_glu'].astype(jnp.float32))
    return out.astype(u.dtype), states[:, -1]


def mla_expand(ckv, w_ukv, k_nope_norm):
    b, t, _ = ckv.shape
    kv = (ckv @ w_ukv).reshape(b, t, MLA_HEADS, NOPE_DIM + V_DIM)
    return rmsnorm(kv[..., :NOPE_DIM], k_nope_norm), kv[..., NOPE_DIM:]


def mla_attend(q_nope, q_rope, k_nope, k_rope, v, mask):
    s = jnp.einsum('bqhn,bkhn->bhqk', q_nope, k_nope) + jnp.einsum('bqhr,bkr->bhqk', q_rope, k_rope)
    s = s.astype(jnp.float32) * (NOPE_DIM + ROPE_DIM) ** -0.5
    if mask is not None:
        s = jnp.where(mask, s, -jnp.inf)
    pr = jax.nn.softmax(s, axis=-1)
    return jnp.einsum('bhqk,bkhv->bqhv', pr.astype(v.dtype), v)


def prompt_attention(q_nope, q_rope, k_nope, k_rope, v):
    b, s = q_nope.shape[:2]
    nb = s // Q_BLOCK

    def blockify(t):
        return jnp.moveaxis(t.reshape(b, nb, Q_BLOCK, *t.shape[2:]), 1, 0)

    k_chunk = jnp.arange(s) // CHUNK
    q_chunk = (jnp.arange(s) // CHUNK).reshape(nb, Q_BLOCK)

    def one_block(args):
        qn, qr, qc = args
        mask = qc[:, None] >= k_chunk[None, :]
        return mla_attend(qn, qr, k_nope, k_rope, v, mask)

    o = lax.map(one_block, (blockify(q_nope), blockify(q_rope), q_chunk))
    return jnp.moveaxis(o, 0, 1).reshape(b, s, MLA_WIDTH)


def mixer_layer(x, pos, h0, past_ckv, past_krope, p):
    b, s, _ = x.shape
    h = rmsnorm(x, p['norm_in'])
    z = h @ p['w_in']
    cuts = np.cumsum([SSM_WIDTH, SSM_WIDTH, Q_LORA, KV_LORA, ROPE_DIM]).tolist()
    u, g_ssm, c_q, c_kv, k_rope_raw, g_mla = jnp.split(z, cuts, axis=-1)
    y_ssm, h_last = s5_branch(u, h0, p)
    q = (rmsnorm(c_q, p['q_lora_norm']) @ p['w_uq']).reshape(b, s, MLA_HEADS, NOPE_DIM + ROPE_DIM)
    q_nope = rmsnorm(q[..., :NOPE_DIM], p['q_nope_norm'])
    q_rope = rope(rmsnorm(q[..., NOPE_DIM:], p['q_rope_norm']), pos)
    ckv = rmsnorm(c_kv, p['kv_lora_norm'])
    krope = rope(rmsnorm(k_rope_raw, p['k_rope_norm']), pos)
    if past_ckv is None:
        k_nope, v = mla_expand(ckv, p['w_ukv'], p['k_nope_norm'])
        attn = prompt_attention(q_nope, q_rope, k_nope, krope, v)
    else:
        ckv_all = jnp.concatenate([past_ckv, ckv], axis=1)
        krope_all = jnp.concatenate([past_krope, krope], axis=1)
        k_nope, v = mla_expand(ckv_all, p['w_ukv'], p['k_nope_norm'])
        attn = mla_attend(q_nope, q_rope, k_nope, krope_all, v, None).reshape(b, s, MLA_WIDTH)
    mix = jnp.concatenate([rmsnorm(y_ssm, p['out_norm_ssm']) * jax.nn.silu(g_ssm),
                           rmsnorm(attn, p['out_norm_mla']) * jax.nn.silu(g_mla)], axis=-1)
    return x + mix @ p['w_out'], ckv, krope, h_last


def setup_inputs(seed: int = 0) -> dict:
    key = jax.random.key(seed)
    ks = jax.random.split(key, 32)
    f32 = jnp.float32

    def nrm(k, shape, scale):
        return jax.random.normal(k, shape, f32) * scale

    def gain(k, shape):
        return 1.0 + 0.05 * jax.random.normal(k, shape, f32)

    n = jnp.arange(SSM_STATE, dtype=f32)
    return {
        'x_prompt': nrm(ks[0], (BATCH, SEQ, D_MODEL), 1.0),
        'x_sample': nrm(ks[1], (DEC_BATCH, DEC_SEQ, D_MODEL), 1.0),
        'cache_ckv': nrm(ks[2], (DEPTH, DEC_BATCH, PAST_LEN, KV_LORA), 1.0),
        'cache_krope': nrm(ks[3], (DEPTH, DEC_BATCH, PAST_LEN, ROPE_DIM), 1.0),
        'state_ssm_re': nrm(ks[4], (DEPTH, DEC_BATCH, SSM_GROUPS, SSM_STATE), 0.1),
        'state_ssm_im': nrm(ks[5], (DEPTH, DEC_BATCH, SSM_GROUPS, SSM_STATE), 0.1),
        'norm_in': gain(ks[6], (DEPTH, D_MODEL)),
        'w_in': nrm(ks[7], (DEPTH, D_MODEL, IN_WIDTH), D_MODEL ** -0.5),
        'ssm_a_re': -0.5 + nrm(ks[8], (DEPTH, SSM_GROUPS, SSM_STATE), 0.01),
        'ssm_a_im': jnp.pi * n + nrm(ks[9], (DEPTH, SSM_GROUPS, SSM_STATE), 0.01),
        'ssm_log_dt': jax.random.uniform(ks[10], (DEPTH, SSM_GROUPS), f32, math.log(DT_MIN), math.log(DT_MAX)),
        'ssm_b_re': nrm(ks[11], (DEPTH, SSM_GROUPS, SSM_STATE, SSM_GROUP), (2 * SSM_GROUP) ** -0.5),
        'ssm_b_im': nrm(ks[12], (DEPTH, SSM_GROUPS, SSM_STATE, SSM_GROUP), (2 * SSM_GROUP) ** -0.5),
        'ssm_c_re': nrm(ks[13], (DEPTH, SSM_GROUPS, SSM_GROUP, SSM_STATE), (2 * SSM_STATE) ** -0.5),
        'ssm_c_im': nrm(ks[14], (DEPTH, SSM_GROUPS, SSM_GROUP, SSM_STATE), (2 * SSM_STATE) ** -0.5),
        'ssm_d': nrm(ks[15], (DEPTH, SSM_WIDTH), 1.0),
        'w_glu': nrm(ks[16], (DEPTH, SSM_WIDTH, SSM_WIDTH), SSM_WIDTH ** -0.5),
        'b_glu': nrm(ks[17], (DEPTH, SSM_WIDTH), 0.01),
        'q_lora_norm': gain(ks[18], (DEPTH, Q_LORA)),
        'kv_lora_norm': gain(ks[19], (DEPTH, KV_LORA)),
        'w_uq': nrm(ks[20], (DEPTH, Q_LORA, MLA_HEADS * (NOPE_DIM + ROPE_DIM)), Q_LORA ** -0.5),
        'w_ukv': nrm(ks[21], (DEPTH, KV_LORA, MLA_HEADS * (NOPE_DIM + V_DIM)), KV_LORA ** -0.5),
        'q_nope_norm': gain(ks[22], (DEPTH, NOPE_DIM)),
        'k_nope_norm': gain(ks[23], (DEPTH, NOPE_DIM)),
        'q_rope_norm': gain(ks[24], (DEPTH, ROPE_DIM)),
        'k_rope_norm': gain(ks[25], (DEPTH, ROPE_DIM)),
        'out_norm_ssm': gain(ks[26], (DEPTH, SSM_WIDTH)),
        'out_norm_mla': gain(ks[27], (DEPTH, MLA_WIDTH)),
        'w_out': nrm(ks[28], (DEPTH, MIX_WIDTH, D_MODEL), MIX_WIDTH ** -0.5),
    }


def reference(x_prompt, x_sample, cache_ckv, cache_krope, state_ssm_re, state_ssm_im,
              norm_in, w_in, ssm_a_re, ssm_a_im, ssm_log_dt, ssm_b_re, ssm_b_im, ssm_c_re, ssm_c_im,
              ssm_d, w_glu, b_glu, q_lora_norm, kv_lora_norm, w_uq, w_ukv,
              q_nope_norm, k_nope_norm, q_rope_norm, k_rope_norm, out_norm_ssm, out_norm_mla, w_out):
    pos_p = jnp.arange(x_prompt.shape[1])
    pos_s = PAST_LEN + jnp.arange(x_sample.shape[1])
    yp, ys = x_prompt, x_sample
    ckv_p, kr_p, re_p, im_p = [], [], [], []
    ckv_s, kr_s, re_s, im_s = [], [], [], []
    for layer in range(DEPTH):
        p = dict(norm_in=norm_in[layer], w_in=w_in[layer], ssm_a_re=ssm_a_re[layer], ssm_a_im=ssm_a_im[layer],
                 ssm_log_dt=ssm_log_dt[layer], ssm_b_re=ssm_b_re[layer], ssm_b_im=ssm_b_im[layer],
                 ssm_c_re=ssm_c_re[layer], ssm_c_im=ssm_c_im[layer], ssm_d=ssm_d[layer], w_glu=w_glu[layer],
                 b_glu=b_glu[layer], q_lora_norm=q_lora_norm[layer], kv_lora_norm=kv_lora_norm[layer],
                 w_uq=w_uq[layer], w_ukv=w_ukv[layer], q_nope_norm=q_nope_norm[layer],
                 k_nope_norm=k_nope_norm[layer], q_rope_norm=q_rope_norm[layer], k_rope_norm=k_rope_norm[layer],
                 out_norm_ssm=out_norm_ssm[layer], out_norm_mla=out_norm_mla[layer], w_out=w_out[layer])
        yp, ckv1, kr1, h1 = mixer_layer(yp, pos_p, None, None, None, p)
        ckv_p.append(ckv1)
        kr_p.append(kr1)
        re_p.append(jnp.real(h1).astype(yp.dtype))
        im_p.append(jnp.imag(h1).astype(yp.dtype))
        h0 = lax.complex(state_ssm_re[layer].astype(jnp.float32), state_ssm_im[layer].astype(jnp.float32))
        ys, ckv2, kr2, h2 = mixer_layer(ys, pos_s, h0, cache_ckv[layer], cache_krope[layer], p)
        ckv_s.append(ckv2)
        kr_s.append(kr2)
        re_s.append(jnp.real(h2).astype(ys.dtype))
        im_s.append(jnp.imag(h2).astype(ys.dtype))
    return (yp, ys,
            jnp.stack(ckv_p), jnp.stack(kr_p), jnp.stack(re_p), jnp.stack(im_p),
            jnp.stack(ckv_s), jnp.stack(kr_s), jnp.stack(re_s), jnp.stack(im_s))
```

```python
import functools

import numpy as np
import jax
import jax.numpy as jnp
from jax import lax
from jax.experimental import pallas as pl
from jax.experimental.pallas import tpu as pltpu

F32 = jnp.float32
BF16 = jnp.bfloat16

CHUNK = 64
SSM_GROUP = 16
SSM_STATE = 64
MLA_HEADS = 8
NOPE_DIM = 64
ROPE_DIM = 32
V_DIM = 64
Q_LORA = 256
KV_LORA = 128
ROPE_THETA = 10000.0
EPS = 1e-6

LANES = 128
HEAD_W = MLA_HEADS * LANES
GROUPS_PER_BLOCK = 8
VMEM_LIMIT = 56 * 1024 * 1024


def _rms(x, gain):
    return x * lax.rsqrt(jnp.mean(x * x, axis=-1, keepdims=True) + EPS) * gain


def _seg_rms(x, seg):
    ms = jnp.dot((x * x).astype(BF16), seg, preferred_element_type=F32)
    return x * lax.rsqrt(ms + EPS)


def _rope_block(x, cos_t, sin_a, sin_b):
    return (x * cos_t + pltpu.roll(x, LANES - ROPE_DIM // 2, 1) * sin_a
            + pltpu.roll(x, ROPE_DIM // 2, 1) * sin_b)


def _expand_kv(ckv, kr_blk, w_ukv_ref, gkn_ref, seg_ref, k_ref, v_ref):
    kv = jnp.dot(ckv.astype(BF16), w_ukv_ref[...], preferred_element_type=F32)
    gkn = gkn_ref[...]
    for p in range(MLA_HEADS // 2):
        kn = _seg_rms(kv[:, 2 * LANES * p:2 * LANES * (p + 1)], seg_ref[...])
        for j in range(2):
            h = 2 * p + j
            blk = kn[:, LANES * j:LANES * (j + 1)] * gkn + kr_blk
            k_ref[0, :, LANES * h:LANES * (h + 1)] = blk.astype(BF16)
    v_ref[0] = kv[:, HEAD_W:].astype(BF16)


def _proj_kernel(x_ref, cos_ref, sa_ref, sb_ref, norm_in_ref, w_in_ref, qln_ref, kvln_ref,
                 w_uq_ref, w_ukv_ref, gq_ref, gkr_ref, gkn_ref, seg_ref,
                 u_ref, gs_ref, gm_ref, q_ref, k_ref, v_ref, ckv_ref, kr_ref, *, ssm_w):
    x = x_ref[0]
    h = _rms(x, norm_in_ref[...])
    z = jnp.dot(h.astype(BF16), w_in_ref[...], preferred_element_type=F32)
    o = 0
    u_ref[...] = z[:, o:o + ssm_w]
    o += ssm_w
    gs_ref[...] = z[:, o:o + ssm_w]
    o += ssm_w
    c_q = z[:, o:o + Q_LORA]
    o += Q_LORA
    c_kv = z[:, o:o + KV_LORA]
    o += KV_LORA
    kr_raw = z[:, o:o + LANES]
    o += LANES
    gm_ref[0] = z[:, o:]

    cos_t, sin_a, sin_b = cos_ref[...], sa_ref[...], sb_ref[...]
    seg = seg_ref[...]

    kr_ms = jnp.dot((kr_raw * kr_raw).astype(BF16), seg[:LANES, :LANES], preferred_element_type=F32)
    kr_blk = _rope_block(kr_raw * lax.rsqrt(kr_ms + EPS) * gkr_ref[...], cos_t, sin_a, sin_b)
    kr_ref[0] = kr_blk[:, :ROPE_DIM]

    q = jnp.dot(_rms(c_q, qln_ref[...]).astype(BF16), w_uq_ref[...], preferred_element_type=F32)
    gq = gq_ref[...]
    for p in range(MLA_HEADS // 2):
        qn = _seg_rms(q[:, 2 * LANES * p:2 * LANES * (p + 1)], seg)
        for j in range(2):
            hd = 2 * p + j
            blk = _rope_block(qn[:, LANES * j:LANES * (j + 1)] * gq, cos_t, sin_a, sin_b)
            q_ref[0, :, LANES * hd:LANES * (hd + 1)] = blk.astype(BF16)

    ckv = _rms(c_kv, kvln_ref[...])
    ckv_ref[0] = ckv
    _expand_kv(ckv, kr_blk, w_ukv_ref, gkn_ref, seg_ref, k_ref, v_ref)


def _expand_kernel(ckv_ref, kr_ref, w_ukv_ref, gkn_ref, seg_ref, k_ref, v_ref):
    _expand_kv(ckv_ref[0], kr_ref[0], w_ukv_ref, gkn_ref, seg_ref, k_ref, v_ref)


def _full(shape):
    n = len(shape)
    return pl.BlockSpec(shape, lambda *_: (0,) * n)


def _proj_call(x, tables, wts, *, tq):
    b, s, d = x.shape
    ssm_w = wts['ssm_w']
    cos_t, sin_a, sin_b = tables
    grid = (s // tq, b)
    row = lambda i, j: (j, i, 0)
    tab = pl.BlockSpec((tq, LANES), lambda i, j: (i, 0))
    in_specs = [pl.BlockSpec((1, tq, d), row), tab, tab, tab,
                _full(wts['norm_in'].shape), _full(wts['w_in'].shape), _full(wts['qln'].shape),
                _full(wts['kvln'].shape), _full(wts['w_uq'].shape), _full(wts['w_ukv'].shape),
                _full(wts['gq'].shape), _full(wts['gkr'].shape), _full(wts['gkn'].shape),
                _full(wts['seg'].shape)]
    tb_spec = pl.BlockSpec((tq, ssm_w), lambda i, j: (i, j))
    out_shape = [jax.ShapeDtypeStruct((s, b * ssm_w), F32),
                 jax.ShapeDtypeStruct((s, b * ssm_w), F32),
                 jax.ShapeDtypeStruct((b, s, ssm_w), F32),
                 jax.ShapeDtypeStruct((b, s, HEAD_W), BF16),
                 jax.ShapeDtypeStruct((b, s, HEAD_W), BF16),
                 jax.ShapeDtypeStruct((b, s, HEAD_W), BF16),
                 jax.ShapeDtypeStruct((b, s, KV_LORA), F32),
                 jax.ShapeDtypeStruct((b, s, ROPE_DIM), F32)]
    out_specs = [tb_spec, tb_spec,
                 pl.BlockSpec((1, tq, ssm_w), row),
                 pl.BlockSpec((1, tq, HEAD_W), row), pl.BlockSpec((1, tq, HEAD_W), row),
                 pl.BlockSpec((1, tq, HEAD_W), row),
                 pl.BlockSpec((1, tq, KV_LORA), row), pl.BlockSpec((1, tq, ROPE_DIM), row)]
    return pl.pallas_call(
        functools.partial(_proj_kernel, ssm_w=ssm_w),
        out_shape=out_shape, grid=grid, in_specs=in_specs, out_specs=out_specs,
        compiler_params=pltpu.CompilerParams(
            dimension_semantics=("arbitrary", "arbitrary"), vmem_limit_bytes=VMEM_LIMIT),
        name="proj",
    )(x, cos_t, sin_a, sin_b, wts['norm_in'], wts['w_in'], wts['qln'], wts['kvln'], wts['w_uq'],
      wts['w_ukv'], wts['gq'], wts['gkr'], wts['gkn'], wts['seg'])


def _expand_call(ckv, kr_blk, wts, *, tq):
    b, t, _ = ckv.shape
    row = lambda j, i: (j, i, 0)
    return pl.pallas_call(
        _expand_kernel,
        out_shape=[jax.ShapeDtypeStruct((b, t, HEAD_W), BF16)] * 2,
        grid=(b, t // tq),
        in_specs=[pl.BlockSpec((1, tq, KV_LORA), row), pl.BlockSpec((1, tq, LANES), row),
                  _full(wts['w_ukv'].shape), _full(wts['gkn'].shape), _full(wts['seg'].shape)],
        out_specs=[pl.BlockSpec((1, tq, HEAD_W), row)] * 2,
        compiler_params=pltpu.CompilerParams(
            dimension_semantics=("arbitrary", "arbitrary"), vmem_limit_bytes=VMEM_LIMIT),
        name="expand",
    )(ckv, kr_blk, wts['w_ukv'], wts['gkn'], wts['seg'])


def _ssm_kernel(u_ref, gs_ref, h0re_ref, h0im_ref, are_ref, aim_ref, bbig_ref, cbig_ref, d_ref,
                w_glu_ref, b_glu_ref, onorm_ref, mix_ref, hre_ref, him_ref, xs_ref, y_ref,
                *, batch, steps):
    @pl.when(pl.program_id(0) == 0)
    def _():
        hre_ref[...] = h0re_ref[...]
        him_ref[...] = h0im_ref[...]

    u = u_ref[...]
    ub = u.astype(BF16)
    n_blocks = bbig_ref.shape[0]
    cin = bbig_ref.shape[1]
    half = bbig_ref.shape[2] // 2
    for gb in range(n_blocks):
        xs_ref[...] = jnp.dot(ub[:, cin * gb:cin * (gb + 1)], bbig_ref[gb],
                              preferred_element_type=F32)
        cols = slice(half * gb, half * (gb + 1))
        a_re = jnp.broadcast_to(are_ref[:, cols], (batch, half))
        a_im = jnp.broadcast_to(aim_ref[:, cols], (batch, half))

        def step(t, carry):
            h_re, h_im = carry
            rows = pl.ds(pl.multiple_of(t * batch, batch), batch)
            n_re = a_re * h_re - a_im * h_im + xs_ref[rows, :half]
            n_im = a_re * h_im + a_im * h_re + xs_ref[rows, half:]
            xs_ref[rows, :half] = n_re
            xs_ref[rows, half:] = n_im
            return n_re, n_im

        h_re, h_im = lax.fori_loop(0, steps, step, (hre_ref[:, cols], him_ref[:, cols]), unroll=2)
        hre_ref[:, cols] = h_re
        him_ref[:, cols] = h_im
        y_ref[:, cin * gb:cin * (gb + 1)] = jnp.dot(xs_ref[...].astype(BF16), cbig_ref[gb],
                                                    preferred_element_type=F32)

    y = y_ref[...] + d_ref[...] * u
    yg = jax.nn.gelu(y)
    glu = jnp.dot(yg.astype(BF16), w_glu_ref[...], preferred_element_type=F32) + b_glu_ref[...]
    out = yg * jax.nn.sigmoid(glu)
    gs = gs_ref[...]
    mix_ref[...] = (_rms(out, onorm_ref[...]) * (gs * jax.nn.sigmoid(gs))).astype(BF16)


def _ssm_call(u2d, gs2d, h0re, h0im, wts, *, batch, steps):
    rows, ssm_w = u2d.shape
    seq = rows // batch
    tile = batch * steps
    n_state = h0re.shape[1]
    blk = pl.BlockSpec((tile, ssm_w), lambda i: (i, 0))
    names = ['a_re', 'a_im', 'bbig', 'cbig', 'ssm_d', 'w_glu', 'b_glu', 'onorm_ssm']
    return pl.pallas_call(
        functools.partial(_ssm_kernel, batch=batch, steps=steps),
        out_shape=[jax.ShapeDtypeStruct((rows, ssm_w), BF16),
                   jax.ShapeDtypeStruct((batch, n_state), F32),
                   jax.ShapeDtypeStruct((batch, n_state), F32)],
        grid=(seq // steps,),
        in_specs=[blk, blk, _full(h0re.shape), _full(h0im.shape)] + [_full(wts[n].shape) for n in names],
        out_specs=[blk, _full((batch, n_state)), _full((batch, n_state))],
        scratch_shapes=[pltpu.VMEM((tile, wts['bbig'].shape[2]), F32),
                        pltpu.VMEM((tile, ssm_w), F32)],
        compiler_params=pltpu.CompilerParams(
            dimension_semantics=("arbitrary",), vmem_limit_bytes=VMEM_LIMIT),
        name="ssm",
    )(u2d, gs2d, h0re, h0im, *[wts[n] for n in names])


def _attn_kernel(q_ref, k_ref, v_ref, gm_ref, ms_ref, x_ref, onorm_ref, w_out_ref, y_ref,
                 *, tq, tk, last_len, causal):
    i = pl.program_id(1)
    n_full = i if causal else 0
    last_start = pl.multiple_of(i * tq, tq) if causal else 0
    if causal:
        qc = lax.broadcasted_iota(jnp.int32, (tq, last_len), 0) // CHUNK
        kc = lax.broadcasted_iota(jnp.int32, (tq, last_len), 1) // CHUNK
        mask = qc >= kc
    dn = (((1,), (1,)), ((), ()))

    def tile_update(carry, qh, kj, vj, masked):
        m, l, acc = carry
        s = lax.dot_general(qh, kj, dn, preferred_element_type=F32)
        if masked:
            s = jnp.where(mask, s, -jnp.inf)
        m_new = jnp.maximum(m, jnp.max(s, axis=-1, keepdims=True))
        alpha = jnp.exp(m - m_new)
        p = jnp.exp(s - m_new)
        l = alpha * l + jnp.sum(p, axis=-1, keepdims=True)
        acc = alpha * acc + jnp.dot(p.astype(BF16), vj, preferred_element_type=F32)
        return m_new, l, acc

    pairs = []
    for p_idx in range(MLA_HEADS // 2):
        pair = None
        for j in range(2):
            hd = 2 * p_idx + j
            cols = slice(LANES * hd, LANES * (hd + 1))
            qh = q_ref[0, :, cols]
            carry = (jnp.full((tq, 1), -jnp.inf, F32), jnp.zeros((tq, 1), F32),
                     jnp.zeros((tq, LANES), F32))
            if causal:
                def full_step(jt, c, qh=qh, cols=cols):
                    rows = pl.ds(pl.multiple_of(jt * tk, tk), tk)
                    return tile_update(c, qh, k_ref[0, rows, cols], v_ref[0, rows, cols], False)
                carry = lax.fori_loop(0, n_full, full_step, carry)
            rows = pl.ds(last_start, last_len)
            m, l, acc = tile_update(carry, qh, k_ref[0, rows, cols], v_ref[0, rows, cols], causal)
            o = acc / l
            pair = o if pair is None else pair + o
        pairs.append(pair)
    attn = jnp.concatenate(pairs, axis=-1)
    gm = gm_ref[0]
    mla = _rms(attn, onorm_ref[...]) * (gm * jax.nn.sigmoid(gm))
    mix = jnp.concatenate([ms_ref[...], mla.astype(BF16)], axis=-1)
    y_ref[0] = x_ref[0] + jnp.dot(mix, w_out_ref[...], preferred_element_type=F32)


def _attn_call(q, k, v, gm, mix_ssm, x, wts, *, tq, tk, causal):
    b, s, d = x.shape
    t_keys = k.shape[1]
    ssm_w = gm.shape[2]
    last_len = tq if causal else t_keys
    row = lambda j, i: (j, i, 0)
    res = lambda j, i: (j, 0, 0)
    return pl.pallas_call(
        functools.partial(_attn_kernel, tq=tq, tk=tk, last_len=last_len, causal=causal),
        out_shape=jax.ShapeDtypeStruct((b, s, d), F32),
        grid=(b, s // tq),
        in_specs=[pl.BlockSpec((1, tq, HEAD_W), row),
                  pl.BlockSpec((1, t_keys, HEAD_W), res), pl.BlockSpec((1, t_keys, HEAD_W), res),
                  pl.BlockSpec((1, tq, ssm_w), row),
                  pl.BlockSpec((tq, ssm_w), lambda j, i: (i, j)),
                  pl.BlockSpec((1, tq, d), row),
                  _full(wts['onorm_mla'].shape), _full(wts['w_out'].shape)],
        out_specs=pl.BlockSpec((1, tq, d), row),
        compiler_params=pltpu.CompilerParams(
            dimension_semantics=("arbitrary", "arbitrary"), vmem_limit_bytes=VMEM_LIMIT),
        name="attn",
    )(q, k, v, gm, mix_ssm, x, wts['onorm_mla'], wts['w_out'])


def _head_block_cols(w, offsets, widths, dst):
    k = w.shape[0]
    stride = w.shape[1] // MLA_HEADS
    out = jnp.zeros((k, MLA_HEADS, LANES), w.dtype)
    w3 = w.reshape(k, MLA_HEADS, stride)
    for off, wd, ds in zip(offsets, widths, dst):
        out = out.at[:, :, ds:ds + wd].set(w3[:, :, off:off + wd])
    return out.reshape(k, HEAD_W)


def _prepare_weights(norm_in, w_in, ssm_a_re, ssm_a_im, ssm_log_dt, ssm_b_re, ssm_b_im, ssm_c_re,
                     ssm_c_im, ssm_d, w_glu, b_glu, q_lora_norm, kv_lora_norm, w_uq, w_ukv,
                     q_nope_norm, k_nope_norm, q_rope_norm, k_rope_norm, out_norm_ssm,
                     out_norm_mla, w_out):
    groups, n_state = ssm_a_re.shape
    ssm_w = groups * SSM_GROUP
    d_model = w_in.shape[0]
    row = lambda v: v.reshape(1, -1).astype(F32)

    cuts = np.cumsum([ssm_w, ssm_w, Q_LORA, KV_LORA, ROPE_DIM]).tolist()
    w_u, w_gs, w_cq, w_ckv, w_kr, w_gm = jnp.split(w_in, cuts, axis=1)
    w_kr = jnp.pad(w_kr, ((0, 0), (0, LANES - ROPE_DIM)))
    w_in_p = jnp.concatenate([w_u, w_gs, w_cq, w_ckv, w_kr, w_gm], axis=1).astype(BF16)

    w_uq_p = _head_block_cols(w_uq, (NOPE_DIM, 0), (ROPE_DIM, NOPE_DIM), (0, ROPE_DIM)).astype(BF16)
    w_uk_p = _head_block_cols(w_ukv, (0,), (NOPE_DIM,), (ROPE_DIM,))
    w3 = w_ukv.reshape(KV_LORA, MLA_HEADS, NOPE_DIM + V_DIM)[:, :, NOPE_DIM:]
    w_uv_p = jnp.zeros((KV_LORA, MLA_HEADS, LANES), w_ukv.dtype)
    w_uv_p = w_uv_p.at[:, 0::2, :V_DIM].set(w3[:, 0::2]).at[:, 1::2, V_DIM:].set(w3[:, 1::2])
    w_ukv_p = jnp.concatenate([w_uk_p, w_uv_p.reshape(KV_LORA, HEAD_W)], axis=1).astype(BF16)

    scale = (NOPE_DIM + ROPE_DIM) ** -0.5
    zeros = lambda n: jnp.zeros((n,), F32)
    tail = LANES - ROPE_DIM - NOPE_DIM
    gq = jnp.concatenate([q_rope_norm, q_nope_norm, zeros(tail)]) * scale
    gkr = jnp.concatenate([k_rope_norm, zeros(LANES - ROPE_DIM)])
    gkn = jnp.concatenate([zeros(ROPE_DIM), k_nope_norm, zeros(tail)])

    seg = np.zeros((LANES, LANES), np.float32)
    seg[:ROPE_DIM, :ROPE_DIM] = 1.0 / ROPE_DIM
    seg[ROPE_DIM:ROPE_DIM + NOPE_DIM, ROPE_DIM:ROPE_DIM + NOPE_DIM] = 1.0 / NOPE_DIM
    seg[ROPE_DIM + NOPE_DIM:, ROPE_DIM + NOPE_DIM:] = 1.0 / tail
    seg = jnp.asarray(np.kron(np.eye(2, dtype=np.float32), seg), BF16)

    lam = lax.complex(ssm_a_re.astype(F32), ssm_a_im.astype(F32))
    dt = jnp.exp(ssm_log_dt.astype(F32))[:, None]
    a_bar = jnp.exp(lam * dt)
    b_bar = ((a_bar - 1.0) / lam)[..., None] * lax.complex(ssm_b_re.astype(F32), ssm_b_im.astype(F32))
    n_blocks = groups // GROUPS_PER_BLOCK
    eye = jnp.eye(GROUPS_PER_BLOCK, dtype=F32)

    def b_block(part):
        p5 = part.reshape(n_blocks, GROUPS_PER_BLOCK, n_state, SSM_GROUP)
        return jnp.einsum('ngpc,gh->ngchp', p5, eye).reshape(
            n_blocks, GROUPS_PER_BLOCK * SSM_GROUP, GROUPS_PER_BLOCK * n_state)

    def c_block(part):
        p5 = part.reshape(n_blocks, GROUPS_PER_BLOCK, SSM_GROUP, n_state)
        return jnp.einsum('ngcp,gh->ngphc', p5, eye).reshape(
            n_blocks, GROUPS_PER_BLOCK * n_state, GROUPS_PER_BLOCK * SSM_GROUP)

    bbig = jnp.concatenate([b_block(jnp.real(b_bar)), b_block(jnp.imag(b_bar))], axis=2).astype(BF16)
    cbig = jnp.concatenate([c_block(ssm_c_re.astype(F32)), -c_block(ssm_c_im.astype(F32))],
                           axis=1).astype(BF16)

    return dict(
        ssm_w=ssm_w, d_model=d_model,
        norm_in=row(norm_in), w_in=w_in_p, qln=row(q_lora_norm), kvln=row(kv_lora_norm),
        w_uq=w_uq_p, w_ukv=w_ukv_p, gq=row(gq), gkr=row(gkr), gkn=row(gkn), seg=seg,
        a_re=row(jnp.real(a_bar)), a_im=row(jnp.imag(a_bar)), bbig=bbig, cbig=cbig,
        ssm_d=row(ssm_d), w_glu=w_glu.astype(BF16), b_glu=row(b_glu), onorm_ssm=row(out_norm_ssm),
        onorm_mla=row(out_norm_mla), w_out=w_out.astype(BF16))


def _rope_tables(pos):
    half = ROPE_DIM // 2
    inv = ROPE_THETA ** (-jnp.arange(half, dtype=F32) / half)
    ang = pos.astype(F32)[:, None] * inv[None, :]
    cos, sin = jnp.cos(ang), jnp.sin(ang)
    n = pos.shape[0]
    cos_t = jnp.concatenate([cos, cos, jnp.ones((n, LANES - ROPE_DIM), F32)], axis=1)
    sin_a = jnp.concatenate([-sin, jnp.zeros((n, LANES - half), F32)], axis=1)
    sin_b = jnp.concatenate([jnp.zeros((n, half), F32), sin, jnp.zeros((n, LANES - ROPE_DIM), F32)],
                            axis=1)
    return cos_t, sin_a, sin_b


def _mixer(x, pos, h0re, h0im, past, wts, *, proj_tq, ssm_steps, attn_tq):
    b, s, _ = x.shape
    ssm_w = wts['ssm_w']
    u, gs, gm, q, k, v, ckv, kr = _proj_call(x, _rope_tables(pos), wts, tq=proj_tq)
    mix_ssm, hre, him = _ssm_call(u.reshape(s * b, ssm_w), gs.reshape(s * b, ssm_w), h0re, h0im, wts,
                                  batch=b, steps=ssm_steps)
    mix_ssm = mix_ssm.reshape(s, b * ssm_w)
    if past is None:
        y = _attn_call(q, k, v, gm, mix_ssm, x, wts, tq=attn_tq, tk=attn_tq, causal=True)
    else:
        past_ckv, past_kr = past
        kr_blk = jnp.pad(past_kr, ((0, 0), (0, 0), (0, LANES - ROPE_DIM)))
        k_past, v_past = _expand_call(past_ckv, kr_blk, wts, tq=512)
        k_all = jnp.concatenate([k_past, k], axis=1)
        v_all = jnp.concatenate([v_past, v], axis=1)
        y = _attn_call(q, k_all, v_all, gm, mix_ssm, x, wts, tq=attn_tq, tk=attn_tq, causal=False)
    return y, ckv, kr, hre, him


def kernel(x_prompt, x_sample, cache_ckv, cache_krope, state_ssm_re, state_ssm_im, norm_in, w_in, ssm_a_re, ssm_a_im, ssm_log_dt, ssm_b_re, ssm_b_im, ssm_c_re, ssm_c_im, ssm_d, w_glu, b_glu, q_lora_norm, kv_lora_norm, w_uq, w_ukv, q_nope_norm, k_nope_norm, q_rope_norm, k_rope_norm, out_norm_ssm, out_norm_mla, w_out):
    depth = norm_in.shape[0]
    assert depth == 1, "single mixer layer"
    params = (norm_in, w_in, ssm_a_re, ssm_a_im, ssm_log_dt, ssm_b_re, ssm_b_im, ssm_c_re, ssm_c_im,
              ssm_d, w_glu, b_glu, q_lora_norm, kv_lora_norm, w_uq, w_ukv, q_nope_norm, k_nope_norm,
              q_rope_norm, k_rope_norm, out_norm_ssm, out_norm_mla, w_out)
    wts = _prepare_weights(*[p[0] for p in params])
    groups, n_state = ssm_a_re.shape[1:]
    bp, sp, _ = x_prompt.shape
    bs, ss, _ = x_sample.shape
    past_len = cache_ckv.shape[2]

    zero_state = jnp.zeros((bp, groups * n_state), F32)
    yp, ckv_p, kr_p, re_p, im_p = _mixer(
        x_prompt, jnp.arange(sp), zero_state, zero_state, None, wts,
        proj_tq=512, ssm_steps=64, attn_tq=256)
    ys, ckv_s, kr_s, re_s, im_s = _mixer(
        x_sample, past_len + jnp.arange(ss),
        state_ssm_re[0].reshape(bs, groups * n_state), state_ssm_im[0].reshape(bs, groups * n_state),
        (cache_ckv[0], cache_krope[0]), wts,
        proj_tq=ss, ssm_steps=ss, attn_tq=ss)

    st = lambda h, bb: h.reshape(1, bb, groups, n_state)
    return (yp, ys, ckv_p[None], kr_p[None], st(re_p, bp), st(im_p, bp),
            ckv_s[None], kr_s[None], st(re_s, bs), st(im_s, bs))
```

```python
import functools

import numpy as np
import jax
import jax.numpy as jnp
from jax import lax
from jax.experimental import pallas as pl
from jax.experimental.pallas import tpu as pltpu

F32 = jnp.float32
BF16 = jnp.bfloat16

CHUNK = 64
SSM_GROUP = 16
SSM_STATE = 64
MLA_HEADS = 8
NOPE_DIM = 64
ROPE_DIM = 32
V_DIM = 64
Q_LORA = 256
KV_LORA = 128
ROPE_THETA = 10000.0
EPS = 1e-6

LANES = 128
HEAD_W = MLA_HEADS * LANES
GROUPS_PER_BLOCK = 8
VMEM_LIMIT = 56 * 1024 * 1024


def _rms(x, gain):
    return x * lax.rsqrt(jnp.mean(x * x, axis=-1, keepdims=True) + EPS) * gain


def _seg_rms(x, seg):
    ms = jnp.dot((x * x).astype(BF16), seg, preferred_element_type=F32)
    return x * lax.rsqrt(ms + EPS)


def _rope_block(x, cos_t, sin_a, sin_b):
    return (x * cos_t + pltpu.roll(x, LANES - ROPE_DIM // 2, 1) * sin_a
            + pltpu.roll(x, ROPE_DIM // 2, 1) * sin_b)


def _expand_kv(ckv, kr_blk, w_ukv_ref, gkn_ref, seg_ref, vones_ref, k_ref, v_ref):
    kv = jnp.dot(ckv.astype(BF16), w_ukv_ref[...], preferred_element_type=F32)
    gkn = gkn_ref[...]
    for p in range(MLA_HEADS // 2):
        kn = _seg_rms(kv[:, 2 * LANES * p:2 * LANES * (p + 1)], seg_ref[...])
        for j in range(2):
            h = 2 * p + j
            blk = kn[:, LANES * j:LANES * (j + 1)] * gkn + kr_blk
            k_ref[0, :, LANES * h:LANES * (h + 1)] = blk.astype(BF16)
    v_ref[0] = (kv[:, HEAD_W:] + vones_ref[...]).astype(BF16)


def _proj_kernel(x_ref, cos_ref, sa_ref, sb_ref, norm_in_ref, w_in_ref, qln_ref, kvln_ref,
                 w_uq_ref, w_ukv_ref, gq_ref, gkr_ref, gkn_ref, seg_ref, vones_ref,
                 u_ref, gs_ref, gm_ref, q_ref, k_ref, v_ref, ckv_ref, kr_ref, *, ssm_w):
    x = x_ref[0]
    h = _rms(x, norm_in_ref[...])
    z = jnp.dot(h.astype(BF16), w_in_ref[...], preferred_element_type=F32)
    o = 0
    u_ref[...] = z[:, o:o + ssm_w]
    o += ssm_w
    gs_ref[...] = z[:, o:o + ssm_w]
    o += ssm_w
    c_q = z[:, o:o + Q_LORA]
    o += Q_LORA
    c_kv = z[:, o:o + KV_LORA]
    o += KV_LORA
    kr_raw = z[:, o:o + LANES]
    o += LANES
    gm_ref[0] = z[:, o:]

    cos_t, sin_a, sin_b = cos_ref[...], sa_ref[...], sb_ref[...]
    seg = seg_ref[...]

    kr_ms = jnp.dot((kr_raw * kr_raw).astype(BF16), seg[:LANES, :LANES], preferred_element_type=F32)
    kr_blk = _rope_block(kr_raw * lax.rsqrt(kr_ms + EPS) * gkr_ref[...], cos_t, sin_a, sin_b)
    kr_ref[0] = kr_blk[:, :ROPE_DIM]

    q = jnp.dot(_rms(c_q, qln_ref[...]).astype(BF16), w_uq_ref[...], preferred_element_type=F32)
    gq = gq_ref[...]
    for p in range(MLA_HEADS // 2):
        qn = _seg_rms(q[:, 2 * LANES * p:2 * LANES * (p + 1)], seg)
        for j in range(2):
            hd = 2 * p + j
            blk = _rope_block(qn[:, LANES * j:LANES * (j + 1)] * gq, cos_t, sin_a, sin_b)
            q_ref[0, :, LANES * hd:LANES * (hd + 1)] = blk.astype(BF16)

    ckv = _rms(c_kv, kvln_ref[...])
    ckv_ref[0] = ckv
    _expand_kv(ckv, kr_blk, w_ukv_ref, gkn_ref, seg_ref, vones_ref, k_ref, v_ref)


def _expand_kernel(ckv_ref, kr_ref, w_ukv_ref, gkn_ref, seg_ref, vones_ref, k_ref, v_ref):
    _expand_kv(ckv_ref[0], kr_ref[0], w_ukv_ref, gkn_ref, seg_ref, vones_ref, k_ref, v_ref)


def _full(shape):
    n = len(shape)
    return pl.BlockSpec(shape, lambda *_: (0,) * n)


def _proj_call(x, tables, wts, *, tq):
    b, s, d = x.shape
    ssm_w = wts['ssm_w']
    cos_t, sin_a, sin_b = tables
    grid = (s // tq, b)
    row = lambda i, j: (j, i, 0)
    tab = pl.BlockSpec((tq, LANES), lambda i, j: (i, 0))
    in_specs = [pl.BlockSpec((1, tq, d), row), tab, tab, tab,
                _full(wts['norm_in'].shape), _full(wts['w_in'].shape), _full(wts['qln'].shape),
                _full(wts['kvln'].shape), _full(wts['w_uq'].shape), _full(wts['w_ukv'].shape),
                _full(wts['gq'].shape), _full(wts['gkr'].shape), _full(wts['gkn'].shape),
                _full(wts['seg'].shape), _full(wts['vones'].shape)]
    tb_spec = pl.BlockSpec((tq, ssm_w), lambda i, j: (i, j))
    out_shape = [jax.ShapeDtypeStruct((s, b * ssm_w), F32),
                 jax.ShapeDtypeStruct((s, b * ssm_w), F32),
                 jax.ShapeDtypeStruct((b, s, ssm_w), F32),
                 jax.ShapeDtypeStruct((b, s, HEAD_W), BF16),
                 jax.ShapeDtypeStruct((b, s, HEAD_W), BF16),
                 jax.ShapeDtypeStruct((b, s, HEAD_W), BF16),
                 jax.ShapeDtypeStruct((b, s, KV_LORA), F32),
                 jax.ShapeDtypeStruct((b, s, ROPE_DIM), F32)]
    out_specs = [tb_spec, tb_spec,
                 pl.BlockSpec((1, tq, ssm_w), row),
                 pl.BlockSpec((1, tq, HEAD_W), row), pl.BlockSpec((1, tq, HEAD_W), row),
                 pl.BlockSpec((1, tq, HEAD_W), row),
                 pl.BlockSpec((1, tq, KV_LORA), row), pl.BlockSpec((1, tq, ROPE_DIM), row)]
    return pl.pallas_call(
        functools.partial(_proj_kernel, ssm_w=ssm_w),
        out_shape=out_shape, grid=grid, in_specs=in_specs, out_specs=out_specs,
        compiler_params=pltpu.CompilerParams(
            dimension_semantics=("arbitrary", "arbitrary"), vmem_limit_bytes=VMEM_LIMIT),
        name="proj",
    )(x, cos_t, sin_a, sin_b, wts['norm_in'], wts['w_in'], wts['qln'], wts['kvln'], wts['w_uq'],
      wts['w_ukv'], wts['gq'], wts['gkr'], wts['gkn'], wts['seg'], wts['vones'])


def _expand_call(ckv, kr_blk, wts, *, tq):
    b, t, _ = ckv.shape
    row = lambda j, i: (j, i, 0)
    return pl.pallas_call(
        _expand_kernel,
        out_shape=[jax.ShapeDtypeStruct((b, t, HEAD_W), BF16)] * 2,
        grid=(b, t // tq),
        in_specs=[pl.BlockSpec((1, tq, KV_LORA), row), pl.BlockSpec((1, tq, LANES), row),
                  _full(wts['w_ukv'].shape), _full(wts['gkn'].shape), _full(wts['seg'].shape),
                  _full(wts['vones'].shape)],
        out_specs=[pl.BlockSpec((1, tq, HEAD_W), row)] * 2,
        compiler_params=pltpu.CompilerParams(
            dimension_semantics=("arbitrary", "arbitrary"), vmem_limit_bytes=VMEM_LIMIT),
        name="expand",
    )(ckv, kr_blk, wts['w_ukv'], wts['gkn'], wts['seg'], wts['vones'])


def _ssm_kernel(u_ref, gs_ref, h0re_ref, h0im_ref, are_ref, aim_ref, bbig_ref, cbig_ref, d_ref,
                w_glu_ref, b_glu_ref, onorm_ref, mix_ref, hre_ref, him_ref, xs_ref, y_ref,
                *, batch, steps):
    @pl.when(pl.program_id(0) == 0)
    def _():
        hre_ref[...] = h0re_ref[...]
        him_ref[...] = h0im_ref[...]

    u = u_ref[...]
    ub = u.astype(BF16)
    n_blocks = bbig_ref.shape[0]
    cin = bbig_ref.shape[1]
    half = bbig_ref.shape[2] // 2
    for gb in range(n_blocks):
        xs_ref[...] = jnp.dot(ub[:, cin * gb:cin * (gb + 1)], bbig_ref[gb],
                              preferred_element_type=F32)
        cols = slice(half * gb, half * (gb + 1))
        a_re = jnp.broadcast_to(are_ref[:, cols], (batch, half))
        a_im = jnp.broadcast_to(aim_ref[:, cols], (batch, half))

        def step(t, carry):
            h_re, h_im = carry
            rows = pl.ds(pl.multiple_of(t * batch, batch), batch)
            n_re = a_re * h_re - a_im * h_im + xs_ref[rows, :half]
            n_im = a_re * h_im + a_im * h_re + xs_ref[rows, half:]
            xs_ref[rows, :half] = n_re
            xs_ref[rows, half:] = n_im
            return n_re, n_im

        h_re, h_im = lax.fori_loop(0, steps, step, (hre_ref[:, cols], him_ref[:, cols]), unroll=2)
        hre_ref[:, cols] = h_re
        him_ref[:, cols] = h_im
        y_ref[:, cin * gb:cin * (gb + 1)] = jnp.dot(xs_ref[...].astype(BF16), cbig_ref[gb],
                                                    preferred_element_type=F32)

    y = y_ref[...] + d_ref[...] * u
    yg = jax.nn.gelu(y)
    glu = jnp.dot(yg.astype(BF16), w_glu_ref[...], preferred_element_type=F32) + b_glu_ref[...]
    out = yg * jax.nn.sigmoid(glu)
    gs = gs_ref[...]
    mix_ref[...] = (_rms(out, onorm_ref[...]) * (gs * jax.nn.sigmoid(gs))).astype(BF16)


def _ssm_call(u2d, gs2d, h0re, h0im, wts, *, batch, steps):
    rows, ssm_w = u2d.shape
    seq = rows // batch
    tile = batch * steps
    n_state = h0re.shape[1]
    blk = pl.BlockSpec((tile, ssm_w), lambda i: (i, 0))
    names = ['a_re', 'a_im', 'bbig', 'cbig', 'ssm_d', 'w_glu', 'b_glu', 'onorm_ssm']
    return pl.pallas_call(
        functools.partial(_ssm_kernel, batch=batch, steps=steps),
        out_shape=[jax.ShapeDtypeStruct((rows, ssm_w), BF16),
                   jax.ShapeDtypeStruct((batch, n_state), F32),
                   jax.ShapeDtypeStruct((batch, n_state), F32)],
        grid=(seq // steps,),
        in_specs=[blk, blk, _full(h0re.shape), _full(h0im.shape)] + [_full(wts[n].shape) for n in names],
        out_specs=[blk, _full((batch, n_state)), _full((batch, n_state))],
        scratch_shapes=[pltpu.VMEM((tile, wts['bbig'].shape[2]), F32),
                        pltpu.VMEM((tile, ssm_w), F32)],
        compiler_params=pltpu.CompilerParams(
            dimension_semantics=("arbitrary",), vmem_limit_bytes=VMEM_LIMIT),
        name="ssm",
    )(u2d, gs2d, h0re, h0im, *[wts[n] for n in names])


def _attn_kernel(q_ref, k_ref, v_ref, gm_ref, ms_ref, x_ref, onorm_ref, w_out_ref, y_ref, *scratch,
                 tq, tk, last_len, causal, hg):
    if causal:
        s_full, ml_scr, s_last, m_scr, acc_scr = scratch
    else:
        s_last, m_scr, acc_scr = scratch
    i = pl.program_id(1)
    n_full = i if causal else 0
    last_rows = pl.ds(pl.multiple_of(i * tq, tq) if causal else 0, last_len)
    if causal:
        qc = lax.broadcasted_iota(jnp.int32, (tq, last_len), 0) // CHUNK
        kc = lax.broadcasted_iota(jnp.int32, (tq, last_len), 1) // CHUNK
        mask = qc >= kc
    dn = (((1,), (1,)), ((), ()))
    lane = lax.broadcasted_iota(jnp.int32, (tq, LANES), 1)

    def lane_tiles(s):
        return [s[:, LANES * c:LANES * (c + 1)] for c in range(s.shape[1] // LANES)]

    def probs(s, m_rep):
        if s.shape[1] % LANES == 0:
            p = jnp.concatenate([jnp.exp2(t - m_rep) for t in lane_tiles(s)], axis=-1)
        else:
            p = jnp.exp2(s - m_rep[:, :1])
        return p.astype(BF16)

    pairs = []
    for g in range(MLA_HEADS // hg):
        heads = [(hl, g * hg + hl) for hl in range(hg)]
        cols = {hd: slice(LANES * hd, LANES * (hd + 1)) for _, hd in heads}

        if causal:
            ml_scr[...] = jnp.full(ml_scr.shape, -jnp.inf, F32)

            def a_step(jt, carry, heads=heads, cols=cols):
                rows = pl.ds(pl.multiple_of(jt * tk, tk), tk)
                for hl, hd in heads:
                    s = lax.dot_general(q_ref[0, :, cols[hd]], k_ref[0, rows, cols[hd]], dn,
                                        preferred_element_type=F32)
                    s_full[hl, jt] = s
                    ml_scr[hl] = functools.reduce(jnp.maximum, lane_tiles(s), ml_scr[hl])
                return carry

            lax.fori_loop(0, n_full, a_step, 0)

        for hl, hd in heads:
            s = lax.dot_general(q_ref[0, :, cols[hd]], k_ref[0, last_rows, cols[hd]], dn,
                                preferred_element_type=F32)
            if causal:
                s = jnp.where(mask, s, -jnp.inf)
            s_last[hl] = s
            m = jnp.max(s, axis=-1, keepdims=True)
            if causal:
                m = jnp.maximum(m, jnp.max(ml_scr[hl], axis=-1, keepdims=True))
            m_scr[hl] = jnp.broadcast_to(m, (tq, LANES))
            acc_scr[hl] = jnp.zeros((tq, LANES), F32)

        if causal:
            def b_step(jt, carry, heads=heads, cols=cols):
                rows = pl.ds(pl.multiple_of(jt * tk, tk), tk)
                for hl, hd in heads:
                    p = probs(s_full[hl, jt], m_scr[hl])
                    acc_scr[hl] += jnp.dot(p, v_ref[0, rows, cols[hd]], preferred_element_type=F32)
                return carry

            lax.fori_loop(0, n_full, b_step, 0)

        outs = []
        for hl, hd in heads:
            p = probs(s_last[hl], m_scr[hl])
            acc = acc_scr[hl] + jnp.dot(p, v_ref[0, last_rows, cols[hd]], preferred_element_type=F32)
            ones_col = V_DIM if hd % 2 == 0 else 0
            outs.append(acc / acc[:, ones_col:ones_col + 1])
        for e in range(0, hg, 2):
            pairs.append(jnp.where(lane < V_DIM, outs[e], outs[e + 1]))

    attn = jnp.concatenate(pairs, axis=-1)
    gm = gm_ref[0]
    mla = _rms(attn, onorm_ref[...]) * (gm * jax.nn.sigmoid(gm))
    mix = jnp.concatenate([ms_ref[...], mla.astype(BF16)], axis=-1)
    y_ref[0] = x_ref[0] + jnp.dot(mix, w_out_ref[...], preferred_element_type=F32)


def _attn_call(q, k, v, gm, mix_ssm, x, wts, *, tq, tk, causal, hg):
    b, s, d = x.shape
    t_keys = k.shape[1]
    ssm_w = gm.shape[2]
    last_len = tq if causal else t_keys
    row = lambda j, i: (j, i, 0)
    res = lambda j, i: (j, 0, 0)
    scratch = [pltpu.VMEM((hg, tq, last_len), F32), pltpu.VMEM((hg, tq, LANES), F32),
               pltpu.VMEM((hg, tq, LANES), F32)]
    if causal:
        scratch = [pltpu.VMEM((hg, t_keys // tk - 1, tq, tk), F32),
                   pltpu.VMEM((hg, tq, LANES), F32)] + scratch
    return pl.pallas_call(
        functools.partial(_attn_kernel, tq=tq, tk=tk, last_len=last_len, causal=causal, hg=hg),
        out_shape=jax.ShapeDtypeStruct((b, s, d), F32),
        grid=(b, s // tq),
        in_specs=[pl.BlockSpec((1, tq, HEAD_W), row),
                  pl.BlockSpec((1, t_keys, HEAD_W), res), pl.BlockSpec((1, t_keys, HEAD_W), res),
                  pl.BlockSpec((1, tq, ssm_w), row),
                  pl.BlockSpec((tq, ssm_w), lambda j, i: (i, j)),
                  pl.BlockSpec((1, tq, d), row),
                  _full(wts['onorm_mla'].shape), _full(wts['w_out'].shape)],
        out_specs=pl.BlockSpec((1, tq, d), row),
        scratch_shapes=scratch,
        compiler_params=pltpu.CompilerParams(
            dimension_semantics=("arbitrary", "arbitrary"), vmem_limit_bytes=VMEM_LIMIT),
        name="attn",
    )(q, k, v, gm, mix_ssm, x, wts['onorm_mla'], wts['w_out'])


def _head_block_cols(w, offsets, widths, dst):
    k = w.shape[0]
    stride = w.shape[1] // MLA_HEADS
    out = jnp.zeros((k, MLA_HEADS, LANES), w.dtype)
    w3 = w.reshape(k, MLA_HEADS, stride)
    for off, wd, ds in zip(offsets, widths, dst):
        out = out.at[:, :, ds:ds + wd].set(w3[:, :, off:off + wd])
    return out.reshape(k, HEAD_W)


def _prepare_weights(norm_in, w_in, ssm_a_re, ssm_a_im, ssm_log_dt, ssm_b_re, ssm_b_im, ssm_c_re,
                     ssm_c_im, ssm_d, w_glu, b_glu, q_lora_norm, kv_lora_norm, w_uq, w_ukv,
                     q_nope_norm, k_nope_norm, q_rope_norm, k_rope_norm, out_norm_ssm,
                     out_norm_mla, w_out):
    groups, n_state = ssm_a_re.shape
    ssm_w = groups * SSM_GROUP
    d_model = w_in.shape[0]
    row = lambda v: v.reshape(1, -1).astype(F32)

    cuts = np.cumsum([ssm_w, ssm_w, Q_LORA, KV_LORA, ROPE_DIM]).tolist()
    w_u, w_gs, w_cq, w_ckv, w_kr, w_gm = jnp.split(w_in, cuts, axis=1)
    w_kr = jnp.pad(w_kr, ((0, 0), (0, LANES - ROPE_DIM)))
    w_in_p = jnp.concatenate([w_u, w_gs, w_cq, w_ckv, w_kr, w_gm], axis=1).astype(BF16)

    w_uq_p = _head_block_cols(w_uq, (NOPE_DIM, 0), (ROPE_DIM, NOPE_DIM), (0, ROPE_DIM)).astype(BF16)
    w_uk_p = _head_block_cols(w_ukv, (0,), (NOPE_DIM,), (ROPE_DIM,))
    w3 = w_ukv.reshape(KV_LORA, MLA_HEADS, NOPE_DIM + V_DIM)[:, :, NOPE_DIM:]
    w_uv_p = jnp.zeros((KV_LORA, MLA_HEADS, LANES), w_ukv.dtype)
    w_uv_p = w_uv_p.at[:, 0::2, :V_DIM].set(w3[:, 0::2]).at[:, 1::2, V_DIM:].set(w3[:, 1::2])
    w_ukv_p = jnp.concatenate([w_uk_p, w_uv_p.reshape(KV_LORA, HEAD_W)], axis=1).astype(BF16)

    scale = (NOPE_DIM + ROPE_DIM) ** -0.5 * np.log2(np.e)
    vones = np.zeros((MLA_HEADS, LANES), np.float32)
    vones[0::2, V_DIM] = 1.0
    vones[1::2, 0] = 1.0
    vones = jnp.asarray(vones.reshape(1, HEAD_W))
    zeros = lambda n: jnp.zeros((n,), F32)
    tail = LANES - ROPE_DIM - NOPE_DIM
    gq = jnp.concatenate([q_rope_norm, q_nope_norm, zeros(tail)]) * scale
    gkr = jnp.concatenate([k_rope_norm, zeros(LANES - ROPE_DIM)])
    gkn = jnp.concatenate([zeros(ROPE_DIM), k_nope_norm, zeros(tail)])

    seg = np.zeros((LANES, LANES), np.float32)
    seg[:ROPE_DIM, :ROPE_DIM] = 1.0 / ROPE_DIM
    seg[ROPE_DIM:ROPE_DIM + NOPE_DIM, ROPE_DIM:ROPE_DIM + NOPE_DIM] = 1.0 / NOPE_DIM
    seg[ROPE_DIM + NOPE_DIM:, ROPE_DIM + NOPE_DIM:] = 1.0 / tail
    seg = jnp.asarray(np.kron(np.eye(2, dtype=np.float32), seg), BF16)

    lam = lax.complex(ssm_a_re.astype(F32), ssm_a_im.astype(F32))
    dt = jnp.exp(ssm_log_dt.astype(F32))[:, None]
    a_bar = jnp.exp(lam * dt)
    b_bar = ((a_bar - 1.0) / lam)[..., None] * lax.complex(ssm_b_re.astype(F32), ssm_b_im.astype(F32))
    n_blocks = groups // GROUPS_PER_BLOCK
    eye = jnp.eye(GROUPS_PER_BLOCK, dtype=F32)

    def b_block(part):
        p5 = part.reshape(n_blocks, GROUPS_PER_BLOCK, n_state, SSM_GROUP)
        return jnp.einsum('ngpc,gh->ngchp', p5, eye).reshape(
            n_blocks, GROUPS_PER_BLOCK * SSM_GROUP, GROUPS_PER_BLOCK * n_state)

    def c_block(part):
        p5 = part.reshape(n_blocks, GROUPS_PER_BLOCK, SSM_GROUP, n_state)
        return jnp.einsum('ngcp,gh->ngphc', p5, eye).reshape(
            n_blocks, GROUPS_PER_BLOCK * n_state, GROUPS_PER_BLOCK * SSM_GROUP)

    bbig = jnp.concatenate([b_block(jnp.real(b_bar)), b_block(jnp.imag(b_bar))], axis=2).astype(BF16)
    cbig = jnp.concatenate([c_block(ssm_c_re.astype(F32)), -c_block(ssm_c_im.astype(F32))],
                           axis=1).astype(BF16)

    return dict(
        ssm_w=ssm_w, d_model=d_model,
        norm_in=row(norm_in), w_in=w_in_p, qln=row(q_lora_norm), kvln=row(kv_lora_norm),
        w_uq=w_uq_p, w_ukv=w_ukv_p, gq=row(gq), gkr=row(gkr), gkn=row(gkn), seg=seg, vones=vones,
        a_re=row(jnp.real(a_bar)), a_im=row(jnp.imag(a_bar)), bbig=bbig, cbig=cbig,
        ssm_d=row(ssm_d), w_glu=w_glu.astype(BF16), b_glu=row(b_glu), onorm_ssm=row(out_norm_ssm),
        onorm_mla=row(out_norm_mla), w_out=w_out.astype(BF16))


def _rope_tables(pos):
    half = ROPE_DIM // 2
    inv = ROPE_THETA ** (-jnp.arange(half, dtype=F32) / half)
    ang = pos.astype(F32)[:, None] * inv[None, :]
    cos, sin = jnp.cos(ang), jnp.sin(ang)
    n = pos.shape[0]
    cos_t = jnp.concatenate([cos, cos, jnp.ones((n, LANES - ROPE_DIM), F32)], axis=1)
    sin_a = jnp.concatenate([-sin, jnp.zeros((n, LANES - half), F32)], axis=1)
    sin_b = jnp.concatenate([jnp.zeros((n, half), F32), sin, jnp.zeros((n, LANES - ROPE_DIM), F32)],
                            axis=1)
    return cos_t, sin_a, sin_b


def _mixer(x, pos, h0re, h0im, past, wts, *, proj_tq, ssm_steps, attn_tq):
    b, s, _ = x.shape
    ssm_w = wts['ssm_w']
    u, gs, gm, q, k, v, ckv, kr = _proj_call(x, _rope_tables(pos), wts, tq=proj_tq)
    mix_ssm, hre, him = _ssm_call(u.reshape(s * b, ssm_w), gs.reshape(s * b, ssm_w), h0re, h0im, wts,
                                  batch=b, steps=ssm_steps)
    mix_ssm = mix_ssm.reshape(s, b * ssm_w)
    if past is None:
        y = _attn_call(q, k, v, gm, mix_ssm, x, wts, tq=attn_tq, tk=attn_tq, causal=True, hg=8)
    else:
        past_ckv, past_kr = past
        kr_blk = jnp.pad(past_kr, ((0, 0), (0, 0), (0, LANES - ROPE_DIM)))
        k_past, v_past = _expand_call(past_ckv, kr_blk, wts, tq=512)
        k_all = jnp.concatenate([k_past, k], axis=1)
        v_all = jnp.concatenate([v_past, v], axis=1)
        y = _attn_call(q, k_all, v_all, gm, mix_ssm, x, wts, tq=attn_tq, tk=attn_tq, causal=False,
                       hg=4)
    return y, ckv, kr, hre, him


def kernel(x_prompt, x_sample, cache_ckv, cache_krope, state_ssm_re, state_ssm_im, norm_in, w_in, ssm_a_re, ssm_a_im, ssm_log_dt, ssm_b_re, ssm_b_im, ssm_c_re, ssm_c_im, ssm_d, w_glu, b_glu, q_lora_norm, kv_lora_norm, w_uq, w_ukv, q_nope_norm, k_nope_norm, q_rope_norm, k_rope_norm, out_norm_ssm, out_norm_mla, w_out):
    depth = norm_in.shape[0]
    assert depth == 1, "single mixer layer"
    params = (norm_in, w_in, ssm_a_re, ssm_a_im, ssm_log_dt, ssm_b_re, ssm_b_im, ssm_c_re, ssm_c_im,
              ssm_d, w_glu, b_glu, q_lora_norm, kv_lora_norm, w_uq, w_ukv, q_nope_norm, k_nope_norm,
              q_rope_norm, k_rope_norm, out_norm_ssm, out_norm_mla, w_out)
    wts = _prepare_weights(*[p[0] for p in params])
    groups, n_state = ssm_a_re.shape[1:]
    bp, sp, _ = x_prompt.shape
    bs, ss, _ = x_sample.shape
    past_len = cache_ckv.shape[2]

    zero_state = jnp.zeros((bp, groups * n_state), F32)
    yp, ckv_p, kr_p, re_p, im_p = _mixer(
        x_prompt, jnp.arange(sp), zero_state, zero_state, None, wts,
        proj_tq=512, ssm_steps=64, attn_tq=256)
    ys, ckv_s, kr_s, re_s, im_s = _mixer(
        x_sample, past_len + jnp.arange(ss),
        state_ssm_re[0].reshape(bs, groups * n_state), state_ssm_im[0].reshape(bs, groups * n_state),
        (cache_ckv[0], cache_krope[0]), wts,
        proj_tq=ss, ssm_steps=ss, attn_tq=ss)

    st = lambda h, bb: h.reshape(1, bb, groups, n_state)
    return (yp, ys, ckv_p[None], kr_p[None], st(re_p, bp), st(im_p, bp),
            ckv_s[None], kr_s[None], st(re_s, bs), st(im_s, bs))
```

```python
import functools

import numpy as np
import jax
import jax.numpy as jnp
from jax import lax
from jax.experimental import pallas as pl
from jax.experimental.pallas import tpu as pltpu

F32 = jnp.float32
BF16 = jnp.bfloat16

CHUNK = 64
SSM_GROUP = 16
SSM_STATE = 64
MLA_HEADS = 8
NOPE_DIM = 64
ROPE_DIM = 32
V_DIM = 64
Q_LORA = 256
KV_LORA = 128
ROPE_THETA = 10000.0
EPS = 1e-6

LANES = 128
HEAD_W = MLA_HEADS * LANES
GROUPS_PER_BLOCK = 8
VMEM_LIMIT = 56 * 1024 * 1024


def _rms(x, gain):
    return x * lax.rsqrt(jnp.mean(x * x, axis=-1, keepdims=True) + EPS) * gain


def _seg_rms(x, seg):
    ms = jnp.dot((x * x).astype(BF16), seg, preferred_element_type=F32)
    return x * lax.rsqrt(ms + EPS)


def _rope_block(x, cos_t, sin_a, sin_b):
    return (x * cos_t + pltpu.roll(x, LANES - ROPE_DIM // 2, 1) * sin_a
            + pltpu.roll(x, ROPE_DIM // 2, 1) * sin_b)


def _expand_kv(ckv, kr_blk, w_ukv_ref, gkn_ref, seg_ref, vones_ref, k_ref, v_ref):
    kv = jnp.dot(ckv.astype(BF16), w_ukv_ref[...], preferred_element_type=F32)
    gkn = gkn_ref[...]
    for p in range(MLA_HEADS // 2):
        kn = _seg_rms(kv[:, 2 * LANES * p:2 * LANES * (p + 1)], seg_ref[...])
        for j in range(2):
            h = 2 * p + j
            blk = kn[:, LANES * j:LANES * (j + 1)] * gkn + kr_blk
            k_ref[0, :, LANES * h:LANES * (h + 1)] = blk.astype(BF16)
    v_ref[0] = (kv[:, HEAD_W:] + vones_ref[...]).astype(BF16)


def _proj_kernel(x_ref, cos_ref, sa_ref, sb_ref, norm_in_ref, w_in_ref, qln_ref, kvln_ref,
                 w_uq_ref, w_ukv_ref, gq_ref, gkr_ref, gkn_ref, seg_ref, vones_ref,
                 u_ref, gs_ref, gm_ref, q_ref, k_ref, v_ref, ckv_ref, kr_ref, *, ssm_w):
    x = x_ref[0]
    h = _rms(x, norm_in_ref[...])
    z = jnp.dot(h.astype(BF16), w_in_ref[...], preferred_element_type=F32)
    o = 0
    u_ref[0] = z[:, o:o + ssm_w]
    o += ssm_w
    gs_ref[0] = z[:, o:o + ssm_w]
    o += ssm_w
    c_q = z[:, o:o + Q_LORA]
    o += Q_LORA
    c_kv = z[:, o:o + KV_LORA]
    o += KV_LORA
    kr_raw = z[:, o:o + LANES]
    o += LANES
    gm_ref[0] = z[:, o:]

    cos_t, sin_a, sin_b = cos_ref[...], sa_ref[...], sb_ref[...]
    seg = seg_ref[...]

    kr_ms = jnp.dot((kr_raw * kr_raw).astype(BF16), seg[:LANES, :LANES], preferred_element_type=F32)
    kr_blk = _rope_block(kr_raw * lax.rsqrt(kr_ms + EPS) * gkr_ref[...], cos_t, sin_a, sin_b)
    kr_ref[0] = kr_blk[:, :ROPE_DIM]

    q = jnp.dot(_rms(c_q, qln_ref[...]).astype(BF16), w_uq_ref[...], preferred_element_type=F32)
    gq = gq_ref[...]
    for p in range(MLA_HEADS // 2):
        qn = _seg_rms(q[:, 2 * LANES * p:2 * LANES * (p + 1)], seg)
        for j in range(2):
            hd = 2 * p + j
            blk = _rope_block(qn[:, LANES * j:LANES * (j + 1)] * gq, cos_t, sin_a, sin_b)
            q_ref[0, :, LANES * hd:LANES * (hd + 1)] = blk.astype(BF16)

    ckv = _rms(c_kv, kvln_ref[...])
    ckv_ref[0] = ckv
    _expand_kv(ckv, kr_blk, w_ukv_ref, gkn_ref, seg_ref, vones_ref, k_ref, v_ref)


def _expand_kernel(ckv_ref, kr_ref, w_ukv_ref, gkn_ref, seg_ref, vones_ref, place_ref, k_ref, v_ref):
    kr_blk = jnp.dot(kr_ref[0].astype(BF16), place_ref[...], preferred_element_type=F32)
    _expand_kv(ckv_ref[0], kr_blk, w_ukv_ref, gkn_ref, seg_ref, vones_ref, k_ref, v_ref)


def _full(shape):
    n = len(shape)
    return pl.BlockSpec(shape, lambda *_: (0,) * n)


def _proj_call(x, tables, wts, *, tq):
    b, s, d = x.shape
    ssm_w = wts['ssm_w']
    cos_t, sin_a, sin_b = tables
    grid = (s // tq, b)
    row = lambda i, j: (j, i, 0)
    tab = pl.BlockSpec((tq, LANES), lambda i, j: (i, 0))
    in_specs = [pl.BlockSpec((1, tq, d), row), tab, tab, tab,
                _full(wts['norm_in'].shape), _full(wts['w_in'].shape), _full(wts['qln'].shape),
                _full(wts['kvln'].shape), _full(wts['w_uq'].shape), _full(wts['w_ukv'].shape),
                _full(wts['gq'].shape), _full(wts['gkr'].shape), _full(wts['gkn'].shape),
                _full(wts['seg'].shape), _full(wts['vones'].shape)]
    tb_spec = pl.BlockSpec((1, tq, ssm_w), row)
    out_shape = [jax.ShapeDtypeStruct((b, s, ssm_w), F32),
                 jax.ShapeDtypeStruct((b, s, ssm_w), F32),
                 jax.ShapeDtypeStruct((b, s, ssm_w), F32),
                 jax.ShapeDtypeStruct((b, s, HEAD_W), BF16),
                 jax.ShapeDtypeStruct((b, s, HEAD_W), BF16),
                 jax.ShapeDtypeStruct((b, s, HEAD_W), BF16),
                 jax.ShapeDtypeStruct((b, s, KV_LORA), F32),
                 jax.ShapeDtypeStruct((b, s, ROPE_DIM), F32)]
    out_specs = [tb_spec, tb_spec,
                 pl.BlockSpec((1, tq, ssm_w), row),
                 pl.BlockSpec((1, tq, HEAD_W), row), pl.BlockSpec((1, tq, HEAD_W), row),
                 pl.BlockSpec((1, tq, HEAD_W), row),
                 pl.BlockSpec((1, tq, KV_LORA), row), pl.BlockSpec((1, tq, ROPE_DIM), row)]
    return pl.pallas_call(
        functools.partial(_proj_kernel, ssm_w=ssm_w),
        out_shape=out_shape, grid=grid, in_specs=in_specs, out_specs=out_specs,
        compiler_params=pltpu.CompilerParams(
            dimension_semantics=("arbitrary", "arbitrary"), vmem_limit_bytes=VMEM_LIMIT),
        name="proj",
    )(x, cos_t, sin_a, sin_b, wts['norm_in'], wts['w_in'], wts['qln'], wts['kvln'], wts['w_uq'],
      wts['w_ukv'], wts['gq'], wts['gkr'], wts['gkn'], wts['seg'], wts['vones'])


def _expand_call(ckv, kr, wts, *, tq):
    b, t, _ = ckv.shape
    row = lambda j, i: (j, i, 0)
    return pl.pallas_call(
        _expand_kernel,
        out_shape=[jax.ShapeDtypeStruct((b, t, HEAD_W), BF16)] * 2,
        grid=(b, t // tq),
        in_specs=[pl.BlockSpec((1, tq, KV_LORA), row), pl.BlockSpec((1, tq, ROPE_DIM), row),
                  _full(wts['w_ukv'].shape), _full(wts['gkn'].shape), _full(wts['seg'].shape),
                  _full(wts['vones'].shape), _full(wts['place'].shape)],
        out_specs=[pl.BlockSpec((1, tq, HEAD_W), row)] * 2,
        compiler_params=pltpu.CompilerParams(
            dimension_semantics=("arbitrary", "arbitrary"), vmem_limit_bytes=VMEM_LIMIT),
        name="expand",
    )(ckv, kr, wts['w_ukv'], wts['gkn'], wts['seg'], wts['vones'], wts['place'])


def _ssm_kernel(u_ref, gs_ref, h0re_ref, h0im_ref, are_ref, aim_ref, bbig_ref, cbig_ref, d_ref,
                w_glu_ref, b_glu_ref, onorm_ref, mix_ref, hre_ref, him_ref, us_ref, xs_ref, y_ref,
                *, batch, steps):
    @pl.when(pl.program_id(0) == 0)
    def _():
        hre_ref[...] = h0re_ref[...]
        him_ref[...] = h0im_ref[...]

    n_blocks = bbig_ref.shape[0]
    cin = bbig_ref.shape[1]
    half = bbig_ref.shape[2] // 2
    for b in range(batch):
        for j in range(n_blocks):
            us_ref[j, pl.ds(b, steps, stride=batch), :] = u_ref[b, :, cin * j:cin * (j + 1)]
    u = jnp.concatenate([us_ref[j] for j in range(n_blocks)], axis=-1)
    for gb in range(n_blocks):
        xs_ref[...] = jnp.dot(us_ref[gb].astype(BF16), bbig_ref[gb], preferred_element_type=F32)
        cols = slice(half * gb, half * (gb + 1))
        a_re = jnp.broadcast_to(are_ref[:, cols], (batch, half))
        a_im = jnp.broadcast_to(aim_ref[:, cols], (batch, half))

        def step(t, carry):
            h_re, h_im = carry
            rows = pl.ds(pl.multiple_of(t * batch, batch), batch)
            n_re = a_re * h_re - a_im * h_im + xs_ref[rows, :half]
            n_im = a_re * h_im + a_im * h_re + xs_ref[rows, half:]
            xs_ref[rows, :half] = n_re
            xs_ref[rows, half:] = n_im
            return n_re, n_im

        h_re, h_im = lax.fori_loop(0, steps, step, (hre_ref[:, cols], him_ref[:, cols]), unroll=2)
        hre_ref[:, cols] = h_re
        him_ref[:, cols] = h_im
        y_ref[:, cin * gb:cin * (gb + 1)] = jnp.dot(xs_ref[...].astype(BF16), cbig_ref[gb],
                                                    preferred_element_type=F32)

    y = y_ref[...] + d_ref[...] * u
    yg = jax.nn.gelu(y)
    glu = jnp.dot(yg.astype(BF16), w_glu_ref[...], preferred_element_type=F32) + b_glu_ref[...]
    out = _rms(yg * jax.nn.sigmoid(glu), onorm_ref[...])
    for j in range(n_blocks):
        us_ref[j] = out[:, cin * j:cin * (j + 1)]
    for b in range(batch):
        for j in range(n_blocks):
            gs = gs_ref[b, :, cin * j:cin * (j + 1)]
            o = us_ref[j, pl.ds(b, steps, stride=batch), :]
            mix_ref[b, :, cin * j:cin * (j + 1)] = (o * (gs * jax.nn.sigmoid(gs))).astype(BF16)


def _ssm_call(u, gs, h0re, h0im, wts, *, steps):
    batch, seq, ssm_w = u.shape
    tile = batch * steps
    n_state = h0re.shape[1]
    n_blocks, cin, width = wts['bbig'].shape
    assert cin == LANES and n_blocks * cin == ssm_w
    blk = pl.BlockSpec((batch, steps, ssm_w), lambda i: (0, i, 0))
    names = ['a_re', 'a_im', 'bbig', 'cbig', 'ssm_d', 'w_glu', 'b_glu', 'onorm_ssm']
    return pl.pallas_call(
        functools.partial(_ssm_kernel, batch=batch, steps=steps),
        out_shape=[jax.ShapeDtypeStruct((batch, seq, ssm_w), BF16),
                   jax.ShapeDtypeStruct((batch, n_state), F32),
                   jax.ShapeDtypeStruct((batch, n_state), F32)],
        grid=(seq // steps,),
        in_specs=[blk, blk, _full(h0re.shape), _full(h0im.shape)] + [_full(wts[n].shape) for n in names],
        out_specs=[blk, _full((batch, n_state)), _full((batch, n_state))],
        scratch_shapes=[pltpu.VMEM((n_blocks, tile, LANES), F32),
                        pltpu.VMEM((tile, width), F32),
                        pltpu.VMEM((tile, ssm_w), F32)],
        compiler_params=pltpu.CompilerParams(
            dimension_semantics=("arbitrary",), vmem_limit_bytes=VMEM_LIMIT),
        name="ssm",
    )(u, gs, h0re, h0im, *[wts[n] for n in names])


def _attn_kernel(q_ref, kf_ref, vf_ref, kl_ref, vl_ref, gm_ref, ms_ref, x_ref, onorm_ref, w_out_ref,
                 y_ref, s_full, ml_scr, s_last, m_scr, acc_scr, *, tq, tk, causal, hg):
    i = pl.program_id(1)
    n_full = i if causal else kf_ref.shape[1] // tk
    last_len = kl_ref.shape[1]
    if causal:
        qc = lax.broadcasted_iota(jnp.int32, (tq, last_len), 0) // CHUNK
        kc = lax.broadcasted_iota(jnp.int32, (tq, last_len), 1) // CHUNK
        mask = qc >= kc
    dn = (((1,), (1,)), ((), ()))
    lane = lax.broadcasted_iota(jnp.int32, (tq, LANES), 1)

    def lane_tiles(s):
        return [s[:, LANES * c:LANES * (c + 1)] for c in range(s.shape[1] // LANES)]

    def probs(s, m_rep):
        if s.shape[1] % LANES == 0:
            p = jnp.concatenate([jnp.exp2(t - m_rep) for t in lane_tiles(s)], axis=-1)
        else:
            p = jnp.exp2(s - m_rep[:, :1])
        return p.astype(BF16)

    pairs = []
    for g in range(MLA_HEADS // hg):
        heads = [(hl, g * hg + hl) for hl in range(hg)]
        cols = {hd: slice(LANES * hd, LANES * (hd + 1)) for _, hd in heads}

        ml_scr[...] = jnp.full(ml_scr.shape, -jnp.inf, F32)

        def a_step(jt, carry, heads=heads, cols=cols):
            rows = pl.ds(pl.multiple_of(jt * tk, tk), tk)
            for hl, hd in heads:
                s = lax.dot_general(q_ref[0, :, cols[hd]], kf_ref[0, rows, cols[hd]], dn,
                                    preferred_element_type=F32)
                s_full[hl, jt] = s
                ml_scr[hl] = functools.reduce(jnp.maximum, lane_tiles(s), ml_scr[hl])
            return carry

        lax.fori_loop(0, n_full, a_step, 0)

        for hl, hd in heads:
            s = lax.dot_general(q_ref[0, :, cols[hd]], kl_ref[0, :, cols[hd]], dn,
                                preferred_element_type=F32)
            if causal:
                s = jnp.where(mask, s, -jnp.inf)
            s_last[hl] = s
            m = jnp.maximum(jnp.max(s, axis=-1, keepdims=True),
                            jnp.max(ml_scr[hl], axis=-1, keepdims=True))
            m_scr[hl] = jnp.broadcast_to(m, (tq, LANES))
            acc_scr[hl] = jnp.zeros((tq, LANES), F32)

        def b_step(jt, carry, heads=heads, cols=cols):
            rows = pl.ds(pl.multiple_of(jt * tk, tk), tk)
            for hl, hd in heads:
                p = probs(s_full[hl, jt], m_scr[hl])
                acc_scr[hl] += jnp.dot(p, vf_ref[0, rows, cols[hd]], preferred_element_type=F32)
            return carry

        lax.fori_loop(0, n_full, b_step, 0)

        outs = []
        for hl, hd in heads:
            p = probs(s_last[hl], m_scr[hl])
            acc = acc_scr[hl] + jnp.dot(p, vl_ref[0, :, cols[hd]], preferred_element_type=F32)
            ones_col = V_DIM if hd % 2 == 0 else 0
            outs.append(acc / acc[:, ones_col:ones_col + 1])
        for e in range(0, hg, 2):
            pairs.append(jnp.where(lane < V_DIM, outs[e], outs[e + 1]))

    attn = jnp.concatenate(pairs, axis=-1)
    gm = gm_ref[0]
    mla = _rms(attn, onorm_ref[...]) * (gm * jax.nn.sigmoid(gm))
    mix = jnp.concatenate([ms_ref[0], mla.astype(BF16)], axis=-1)
    y_ref[0] = x_ref[0] + jnp.dot(mix, w_out_ref[...], preferred_element_type=F32)


def _attn_call(q, k_full, v_full, k_last, v_last, gm, mix_ssm, x, wts, *, tk, causal, hg):
    b, s, d = x.shape
    t_full = k_full.shape[1]
    ssm_w = gm.shape[2]
    tq = tk if causal else s
    last_len = tq if causal else k_last.shape[1]
    max_full = t_full // tk - 1 if causal else t_full // tk
    row = lambda j, i: (j, i, 0)
    res = lambda j, i: (j, 0, 0)
    last = pl.BlockSpec((1, last_len, HEAD_W), row if causal else res)
    scratch = [pltpu.VMEM((hg, max_full, tq, tk), F32), pltpu.VMEM((hg, tq, LANES), F32),
               pltpu.VMEM((hg, tq, last_len), F32), pltpu.VMEM((hg, tq, LANES), F32),
               pltpu.VMEM((hg, tq, LANES), F32)]
    return pl.pallas_call(
        functools.partial(_attn_kernel, tq=tq, tk=tk, causal=causal, hg=hg),
        out_shape=jax.ShapeDtypeStruct((b, s, d), F32),
        grid=(b, s // tq),
        in_specs=[pl.BlockSpec((1, tq, HEAD_W), row),
                  pl.BlockSpec((1, t_full, HEAD_W), res), pl.BlockSpec((1, t_full, HEAD_W), res),
                  last, last,
                  pl.BlockSpec((1, tq, ssm_w), row), pl.BlockSpec((1, tq, ssm_w), row),
                  pl.BlockSpec((1, tq, d), row),
                  _full(wts['onorm_mla'].shape), _full(wts['w_out'].shape)],
        out_specs=pl.BlockSpec((1, tq, d), row),
        scratch_shapes=scratch,
        compiler_params=pltpu.CompilerParams(
            dimension_semantics=("arbitrary", "arbitrary"), vmem_limit_bytes=VMEM_LIMIT),
        name="attn",
    )(q, k_full, v_full, k_last, v_last, gm, mix_ssm, x, wts['onorm_mla'], wts['w_out'])


def _head_block_cols(w, pieces):
    k = w.shape[0]
    w3 = w.reshape(k, MLA_HEADS, w.shape[1] // MLA_HEADS)
    cols = [w3[:, :, p[0]:p[0] + p[1]] if isinstance(p, tuple) else jnp.zeros((k, MLA_HEADS, p), w.dtype)
            for p in pieces]
    return jnp.concatenate(cols, axis=-1).reshape(k, HEAD_W)


def _disc_kernel(are_ref, aim_ref, ldt_ref, btr_ref, bti_ref, abr_ref, abi_ref, bbr_ref, bbi_ref):
    lr, li = are_ref[...], aim_ref[...]
    dt = jnp.exp(ldt_ref[...])
    mag = jnp.exp(lr * dt)
    ar, ai = mag * jnp.cos(li * dt), mag * jnp.sin(li * dt)
    abr_ref[...] = ar
    abi_ref[...] = ai
    den = lr * lr + li * li
    cr = ((ar - 1.0) * lr + ai * li) / den
    ci = (ai * lr - (ar - 1.0) * li) / den
    for g in range(btr_ref.shape[0]):
        b_r, b_i = btr_ref[g], bti_ref[g]
        c_r, c_i = cr[g:g + 1, :], ci[g:g + 1, :]
        bbr_ref[g] = c_r * b_r - c_i * b_i
        bbi_ref[g] = c_r * b_i + c_i * b_r


def _prepare_weights(norm_in, w_in, ssm_a_re, ssm_a_im, ssm_log_dt, ssm_b_re, ssm_b_im, ssm_c_re,
                     ssm_c_im, ssm_d, w_glu, b_glu, q_lora_norm, kv_lora_norm, w_uq, w_ukv,
                     q_nope_norm, k_nope_norm, q_rope_norm, k_rope_norm, out_norm_ssm,
                     out_norm_mla, w_out):
    groups, n_state = ssm_a_re.shape
    ssm_w = groups * SSM_GROUP
    row = lambda v: v.reshape(1, -1).astype(F32)
    tail = LANES - ROPE_DIM - NOPE_DIM

    o_kr = 2 * ssm_w + Q_LORA + KV_LORA
    w_in_p = jnp.concatenate(
        [w_in[:, :o_kr + ROPE_DIM], jnp.zeros((w_in.shape[0], LANES - ROPE_DIM), w_in.dtype),
         w_in[:, o_kr + ROPE_DIM:]], axis=1).astype(BF16)

    w_uq_p = _head_block_cols(w_uq, [(NOPE_DIM, ROPE_DIM), (0, NOPE_DIM), tail]).astype(BF16)
    w_uk_p = _head_block_cols(w_ukv, [ROPE_DIM, (0, NOPE_DIM), tail])
    w_v_lo = _head_block_cols(w_ukv, [(NOPE_DIM, V_DIM), LANES - V_DIM])
    w_v_hi = _head_block_cols(w_ukv, [LANES - V_DIM, (NOPE_DIM, V_DIM)])
    odd_head = (np.arange(HEAD_W) // LANES) % 2 == 1
    w_ukv_p = jnp.concatenate([w_uk_p, jnp.where(odd_head[None, :], w_v_hi, w_v_lo)], axis=1).astype(BF16)

    scale = (NOPE_DIM + ROPE_DIM) ** -0.5 * np.log2(np.e)
    vones = np.zeros((MLA_HEADS, LANES), np.float32)
    vones[0::2, V_DIM] = 1.0
    vones[1::2, 0] = 1.0
    vones = jnp.asarray(vones.reshape(1, HEAD_W))
    zeros = lambda n: jnp.zeros((n,), F32)
    gq =jnp.concatenate([q_rope_norm, q_nope_norm, zeros(tail)]) * scale
    gkr = jnp.concatenate([k_rope_norm, zeros(LANES - ROPE_DIM)])
    gkn = jnp.concatenate([zeros(ROPE_DIM), k_nope_norm, zeros(tail)])

    seg = np.zeros((LANES, LANES), np.float32)
    seg[:ROPE_DIM, :ROPE_DIM] = 1.0 / ROPE_DIM
    seg[ROPE_DIM:ROPE_DIM + NOPE_DIM, ROPE_DIM:ROPE_DIM + NOPE_DIM] = 1.0 / NOPE_DIM
    seg[ROPE_DIM + NOPE_DIM:, ROPE_DIM + NOPE_DIM:] = 1.0 / tail
    seg = jnp.asarray(np.kron(np.eye(2, dtype=np.float32), seg), BF16)
    place = jnp.asarray(np.eye(ROPE_DIM, LANES, dtype=np.float32), BF16)

    gcp = jax.ShapeDtypeStruct((groups, SSM_GROUP, n_state), F32)
    gp = jax.ShapeDtypeStruct((groups, n_state), F32)
    a_re, a_im, bb_re, bb_im = pl.pallas_call(_disc_kernel, out_shape=[gp, gp, gcp, gcp], name="disc")(
        ssm_a_re.astype(F32), ssm_a_im.astype(F32), ssm_log_dt.astype(F32).reshape(groups, 1),
        jnp.swapaxes(ssm_b_re, 1, 2).astype(F32), jnp.swapaxes(ssm_b_im, 1, 2).astype(F32))
    n_blocks = groups // GROUPS_PER_BLOCK
    eye = np.eye(GROUPS_PER_BLOCK, dtype=np.float32)

    def block_diag(part):
        r, c = part.shape[1:]
        p5 = part.reshape(n_blocks, GROUPS_PER_BLOCK, r, 1, c) * eye[None, :, None, :, None]
        return p5.reshape(n_blocks, GROUPS_PER_BLOCK * r, GROUPS_PER_BLOCK * c)

    bbig = jnp.concatenate([block_diag(bb_re), block_diag(bb_im)], axis=2).astype(BF16)
    cbig = jnp.concatenate([block_diag(jnp.swapaxes(ssm_c_re, 1, 2).astype(F32)),
                            block_diag(-jnp.swapaxes(ssm_c_im, 1, 2).astype(F32))], axis=1).astype(BF16)

    return dict(
        ssm_w=ssm_w,
        norm_in=row(norm_in), w_in=w_in_p, qln=row(q_lora_norm), kvln=row(kv_lora_norm),
        w_uq=w_uq_p, w_ukv=w_ukv_p, gq=row(gq), gkr=row(gkr), gkn=row(gkn), seg=seg, vones=vones,
        place=place, a_re=row(a_re), a_im=row(a_im), bbig=bbig, cbig=cbig,
        ssm_d=row(ssm_d), w_glu=w_glu.astype(BF16), b_glu=row(b_glu), onorm_ssm=row(out_norm_ssm),
        onorm_mla=row(out_norm_mla), w_out=w_out.astype(BF16))


def _rope_tables(start, n):
    half = ROPE_DIM // 2
    inv = ROPE_THETA ** (-jnp.arange(half, dtype=F32) / half)
    ang = (start + jnp.arange(n)).astype(F32)[:, None] * inv[None, :]
    cos, sin = jnp.cos(ang), jnp.sin(ang)
    cos_t = jnp.concatenate([cos, cos, jnp.ones((n, LANES - ROPE_DIM), F32)], axis=1)
    sin_a = jnp.concatenate([-sin, jnp.zeros((n, LANES - half), F32)], axis=1)
    sin_b = jnp.concatenate([jnp.zeros((n, half), F32), sin, jnp.zeros((n, LANES - ROPE_DIM), F32)],
                            axis=1)
    return cos_t, sin_a, sin_b


def _mixer(x, pos0, h0re, h0im, past, wts, *, proj_tq, ssm_steps, attn_tk):
    u, gs, gm, q, k, v, ckv, kr = _proj_call(x, _rope_tables(pos0, x.shape[1]), wts, tq=proj_tq)
    mix_ssm, hre, him = _ssm_call(u, gs, h0re, h0im, wts, steps=ssm_steps)
    if past is None:
        y = _attn_call(q, k, v, k, v, gm, mix_ssm, x, wts, tk=attn_tk, causal=True, hg=MLA_HEADS)
    else:
        k_past, v_past = _expand_call(*past, wts, tq=512)
        y = _attn_call(q, k_past, v_past, k, v, gm, mix_ssm, x, wts, tk=attn_tk, causal=False,
                       hg=MLA_HEADS)
    return y, ckv, kr, hre, him


def kernel(x_prompt, x_sample, cache_ckv, cache_krope, state_ssm_re, state_ssm_im, norm_in, w_in, ssm_a_re, ssm_a_im, ssm_log_dt, ssm_b_re, ssm_b_im, ssm_c_re, ssm_c_im, ssm_d, w_glu, b_glu, q_lora_norm, kv_lora_norm, w_uq, w_ukv, q_nope_norm, k_nope_norm, q_rope_norm, k_rope_norm, out_norm_ssm, out_norm_mla, w_out):
    depth = norm_in.shape[0]
    assert depth == 1, "single mixer layer"
    params = (norm_in, w_in, ssm_a_re, ssm_a_im, ssm_log_dt, ssm_b_re, ssm_b_im, ssm_c_re, ssm_c_im,
              ssm_d, w_glu, b_glu, q_lora_norm, kv_lora_norm, w_uq, w_ukv, q_nope_norm, k_nope_norm,
              q_rope_norm, k_rope_norm, out_norm_ssm, out_norm_mla, w_out)
    wts = _prepare_weights(*[p[0] for p in params])
    groups, n_state = ssm_a_re.shape[1:]
    bp, sp, _ = x_prompt.shape
    bs, ss, _ = x_sample.shape
    past_len = cache_ckv.shape[2]

    zero_state = jnp.zeros((bp, groups * n_state), F32)
    yp, ckv_p, kr_p, re_p, im_p = _mixer(
        x_prompt, 0, zero_state, zero_state, None, wts,
        proj_tq=512, ssm_steps=64, attn_tk=256)
    ys, ckv_s, kr_s, re_s, im_s = _mixer(
        x_sample, past_len,
        state_ssm_re[0].reshape(bs, groups * n_state), state_ssm_im[0].reshape(bs, groups * n_state),
        (cache_ckv[0], cache_krope[0]), wts,
        proj_tq=ss, ssm_steps=ss, attn_tk=past_len)

    st = lambda h, bb: h.reshape(1, bb, groups, n_state)
    return (yp, ys, ckv_p[None], kr_p[None], st(re_p, bp), st(im_p, bp),
            ckv_s[None], kr_s[None], st(re_s, bs), st(im_s, bs))
```

```python
import functools

import numpy as np
import jax
import jax.numpy as jnp
from jax import lax
from jax.experimental import pallas as pl
from jax.experimental.pallas import tpu as pltpu

F32 = jnp.float32
BF16 = jnp.bfloat16

CHUNK = 64
SSM_GROUP = 16
SSM_STATE = 64
MLA_HEADS = 8
NOPE_DIM = 64
ROPE_DIM = 32
V_DIM = 64
Q_LORA = 256
KV_LORA = 128
ROPE_THETA = 10000.0
EPS = 1e-6

LANES = 128
HEAD_W = MLA_HEADS * LANES
GROUPS_PER_BLOCK = 8
SSM_STEPS_PER_SCAN = 4
VMEM_LIMIT = 56 * 1024 * 1024


def _rms(x, gain):
    return x * lax.rsqrt(jnp.mean(x * x, axis=-1, keepdims=True) + EPS) * gain


def _seg_rms(x, seg):
    ms = jnp.dot((x * x).astype(BF16), seg, preferred_element_type=F32)
    return x * lax.rsqrt(ms + EPS)


def _rope_block(x, cos_t, sin_a, sin_b):
    return (x * cos_t + pltpu.roll(x, LANES - ROPE_DIM // 2, 1) * sin_a
            + pltpu.roll(x, ROPE_DIM // 2, 1) * sin_b)


def _expand_kv(ckv, kr_blk, w_ukv_ref, gkn_ref, seg_ref, vones_ref, k_ref, v_ref):
    kv = jnp.dot(ckv.astype(BF16), w_ukv_ref[...], preferred_element_type=F32)
    gkn = gkn_ref[...]
    for p in range(MLA_HEADS // 2):
        kn = _seg_rms(kv[:, 2 * LANES * p:2 * LANES * (p + 1)], seg_ref[...])
        for j in range(2):
            h = 2 * p + j
            blk = kn[:, LANES * j:LANES * (j + 1)] * gkn + kr_blk
            k_ref[0, :, LANES * h:LANES * (h + 1)] = blk.astype(BF16)
    v_ref[0] = (kv[:, HEAD_W:] + vones_ref[...]).astype(BF16)


def _proj_kernel(x_ref, cos_ref, sa_ref, sb_ref, norm_in_ref, w_in_ref, qln_ref, kvln_ref,
                 w_uq_ref, w_ukv_ref, gq_ref, gkr_ref, gkn_ref, seg_ref, vones_ref,
                 u_ref, gs_ref, gm_ref, q_ref, k_ref, v_ref, ckv_ref, kr_ref, *, ssm_w):
    x = x_ref[0]
    h = _rms(x, norm_in_ref[...])
    z = jnp.dot(h.astype(BF16), w_in_ref[...], preferred_element_type=F32)
    o = 0
    for dst in (u_ref, gs_ref):
        for j in range(ssm_w // LANES):
            dst[j, 0] = z[:, o + LANES * j:o + LANES * (j + 1)]
        o += ssm_w
    c_q = z[:, o:o + Q_LORA]
    o += Q_LORA
    c_kv = z[:, o:o + KV_LORA]
    o += KV_LORA
    kr_raw = z[:, o:o + LANES]
    o += LANES
    gm_ref[0] = z[:, o:]

    cos_t, sin_a, sin_b = cos_ref[...], sa_ref[...], sb_ref[...]
    seg = seg_ref[...]

    kr_ms = jnp.dot((kr_raw * kr_raw).astype(BF16), seg[:LANES, :LANES], preferred_element_type=F32)
    kr_blk = _rope_block(kr_raw * lax.rsqrt(kr_ms + EPS) * gkr_ref[...], cos_t, sin_a, sin_b)
    kr_ref[0] = kr_blk[:, :ROPE_DIM]

    q = jnp.dot(_rms(c_q, qln_ref[...]).astype(BF16), w_uq_ref[...], preferred_element_type=F32)
    gq = gq_ref[...]
    for p in range(MLA_HEADS // 2):
        qn = _seg_rms(q[:, 2 * LANES * p:2 * LANES * (p + 1)], seg)
        for j in range(2):
            hd = 2 * p + j
            blk = _rope_block(qn[:, LANES * j:LANES * (j + 1)] * gq, cos_t, sin_a, sin_b)
            q_ref[0, :, LANES * hd:LANES * (hd + 1)] = blk.astype(BF16)

    ckv = _rms(c_kv, kvln_ref[...])
    ckv_ref[0] = ckv
    _expand_kv(ckv, kr_blk, w_ukv_ref, gkn_ref, seg_ref, vones_ref, k_ref, v_ref)


def _expand_kernel(ckv_ref, kr_ref, w_ukv_ref, gkn_ref, seg_ref, vones_ref, place_ref, k_ref, v_ref):
    kr_blk = jnp.dot(kr_ref[0].astype(BF16), place_ref[...], preferred_element_type=F32)
    _expand_kv(ckv_ref[0], kr_blk, w_ukv_ref, gkn_ref, seg_ref, vones_ref, k_ref, v_ref)


def _full(shape):
    n = len(shape)
    return pl.BlockSpec(shape, lambda *_: (0,) * n)


def _proj_call(x, tables, wts, *, tq):
    b, s, d = x.shape
    ssm_w = wts['ssm_w']
    cos_t, sin_a, sin_b = tables
    grid = (s // tq, b)
    row = lambda i, j: (j, i, 0)
    tab = pl.BlockSpec((tq, LANES), lambda i, j: (i, 0))
    in_specs = [pl.BlockSpec((1, tq, d), row), tab, tab, tab,
                _full(wts['norm_in'].shape), _full(wts['w_in'].shape), _full(wts['qln'].shape),
                _full(wts['kvln'].shape), _full(wts['w_uq'].shape), _full(wts['w_ukv'].shape),
                _full(wts['gq'].shape), _full(wts['gkr'].shape), _full(wts['gkn'].shape),
                _full(wts['seg'].shape), _full(wts['vones'].shape)]
    slabs = ssm_w // LANES
    tb_spec = pl.BlockSpec((slabs, 1, tq, LANES), lambda i, j: (0, j, i, 0))
    out_shape = [jax.ShapeDtypeStruct((slabs, b, s, LANES), F32),
                 jax.ShapeDtypeStruct((slabs, b, s, LANES), F32),
                 jax.ShapeDtypeStruct((b, s, ssm_w), F32),
                 jax.ShapeDtypeStruct((b, s, HEAD_W), BF16),
                 jax.ShapeDtypeStruct((b, s, HEAD_W), BF16),
                 jax.ShapeDtypeStruct((b, s, HEAD_W), BF16),
                 jax.ShapeDtypeStruct((b, s, KV_LORA), F32),
                 jax.ShapeDtypeStruct((b, s, ROPE_DIM), F32)]
    out_specs = [tb_spec, tb_spec,
                 pl.BlockSpec((1, tq, ssm_w), row),
                 pl.BlockSpec((1, tq, HEAD_W), row), pl.BlockSpec((1, tq, HEAD_W), row),
                 pl.BlockSpec((1, tq, HEAD_W), row),
                 pl.BlockSpec((1, tq, KV_LORA), row), pl.BlockSpec((1, tq, ROPE_DIM), row)]
    return pl.pallas_call(
        functools.partial(_proj_kernel, ssm_w=ssm_w),
        out_shape=out_shape, grid=grid, in_specs=in_specs, out_specs=out_specs,
        compiler_params=pltpu.CompilerParams(
            dimension_semantics=("arbitrary", "arbitrary"), vmem_limit_bytes=VMEM_LIMIT),
        name="proj",
    )(x, cos_t, sin_a, sin_b, wts['norm_in'], wts['w_in'], wts['qln'], wts['kvln'], wts['w_uq'],
      wts['w_ukv'], wts['gq'], wts['gkr'], wts['gkn'], wts['seg'], wts['vones'])


def _expand_call(ckv, kr, wts, *, tq):
    b, t, _ = ckv.shape
    row = lambda j, i: (j, i, 0)
    return pl.pallas_call(
        _expand_kernel,
        out_shape=[jax.ShapeDtypeStruct((b, t, HEAD_W), BF16)] * 2,
        grid=(b, t // tq),
        in_specs=[pl.BlockSpec((1, tq, KV_LORA), row), pl.BlockSpec((1, tq, ROPE_DIM), row),
                  _full(wts['w_ukv'].shape), _full(wts['gkn'].shape), _full(wts['seg'].shape),
                  _full(wts['vones'].shape), _full(wts['place'].shape)],
        out_specs=[pl.BlockSpec((1, tq, HEAD_W), row)] * 2,
        compiler_params=pltpu.CompilerParams(
            dimension_semantics=("arbitrary", "arbitrary"), vmem_limit_bytes=VMEM_LIMIT),
        name="expand",
    )(ckv, kr, wts['w_ukv'], wts['gkn'], wts['seg'], wts['vones'], wts['place'])


def _ssm_kernel(u_ref, gs_ref, h0re_ref, h0im_ref, are_ref, aim_ref, wb_ref, wc_ref, wf_ref, d_ref,
                w_glu_ref, b_glu_ref, onorm_ref, mix_ref, hre_ref, him_ref, us_ref, xs_ref, y_ref,
                *, batch, steps, unroll):
    @pl.when(pl.program_id(0) == 0)
    def _():
        hre_ref[...] = h0re_ref[...]
        him_ref[...] = h0im_ref[...]

    n_blocks = wb_ref.shape[0]
    r_blk = wb_ref.shape[1] // LANES
    half = wb_ref.shape[2] // 2
    n_k = steps // r_blk
    for b in range(batch):
        for j in range(n_blocks):
            for i in range(r_blk):
                us_ref[j, i, pl.ds(b, n_k, stride=batch), :] = (
                    u_ref[j, b, pl.ds(i, n_k, stride=r_blk), :])

    for gb in range(n_blocks):
        lhs = jnp.concatenate([us_ref[gb, i] for i in range(r_blk)], axis=-1).astype(BF16)
        xs_ref[...] = jnp.dot(lhs, wb_ref[gb], preferred_element_type=F32)
        cols = slice(half * gb, half * (gb + 1))
        a_re = jnp.broadcast_to(are_ref[:, cols], (batch, half))
        a_im = jnp.broadcast_to(aim_ref[:, cols], (batch, half))

        def step(k, carry):
            h_re, h_im = carry
            rows = pl.ds(pl.multiple_of(k * batch, batch), batch)
            n_re = a_re * h_re - a_im * h_im + xs_ref[rows, :half]
            n_im = a_re * h_im + a_im * h_re + xs_ref[rows, half:]
            xs_ref[rows, :half] = h_re
            xs_ref[rows, half:] = h_im
            return n_re, n_im

        h_re, h_im = lax.fori_loop(0, n_k, step, (hre_ref[:, cols], him_ref[:, cols]), unroll=unroll)
        hre_ref[:, cols] = h_re
        him_ref[:, cols] = h_im
        y = (jnp.dot(xs_ref[...].astype(BF16), wc_ref[gb], preferred_element_type=F32)
             + jnp.dot(lhs, wf_ref[gb], preferred_element_type=F32))
        for i in range(r_blk):
            y_ref[i, :, LANES * gb:LANES * (gb + 1)] = y[:, LANES * i:LANES * (i + 1)]

    rows = r_blk * n_k * batch
    u = jnp.concatenate([us_ref[j].reshape(rows, LANES) for j in range(n_blocks)], axis=-1)
    y = y_ref[...].reshape(rows, n_blocks * LANES) + d_ref[...] * u
    yg = jax.nn.gelu(y)
    glu = jnp.dot(yg.astype(BF16), w_glu_ref[...], preferred_element_type=F32) + b_glu_ref[...]
    out = _rms(yg * jax.nn.sigmoid(glu), onorm_ref[...])
    for j in range(n_blocks):
        us_ref[j] = out[:, LANES * j:LANES * (j + 1)].reshape(r_blk, n_k * batch, LANES)
    for b in range(batch):
        for j in range(n_blocks):
            for i in range(r_blk):
                tok = pl.ds(i, n_k, stride=r_blk)
                gs = gs_ref[j, b, tok, :]
                o = us_ref[j, i, pl.ds(b, n_k, stride=batch), :]
                mix_ref[j, b, tok, :] = o * (gs * jax.nn.sigmoid(gs))


def _ssm_call(u, gs, h0re, h0im, wts, *, steps, unroll):
    n_blocks, batch, seq, _ = u.shape
    n_state = h0re.shape[1]
    r_blk = wts['wb'].shape[1] // LANES
    width = wts['wb'].shape[2]
    assert wts['wb'].shape[0] == n_blocks and steps % r_blk == 0
    blk = pl.BlockSpec((n_blocks, batch, steps, LANES), lambda i: (0, 0, i, 0))
    names = ['a_re', 'a_im', 'wb', 'wc', 'wf', 'ssm_d', 'w_glu', 'b_glu', 'onorm_ssm']
    n_rows = steps // r_blk * batch
    return pl.pallas_call(
        functools.partial(_ssm_kernel, batch=batch, steps=steps, unroll=unroll),
        out_shape=[jax.ShapeDtypeStruct(u.shape, F32),
                   jax.ShapeDtypeStruct((batch, n_state), F32),
                   jax.ShapeDtypeStruct((batch, n_state), F32)],
        grid=(seq // steps,),
        in_specs=[blk, blk, _full(h0re.shape), _full(h0im.shape)] + [_full(wts[n].shape) for n in names],
        out_specs=[blk, _full((batch, n_state)), _full((batch, n_state))],
        scratch_shapes=[pltpu.VMEM((n_blocks, r_blk, n_rows, LANES), F32),
                        pltpu.VMEM((n_rows, width), F32),
                        pltpu.VMEM((r_blk, n_rows, n_blocks * LANES), F32)],
        compiler_params=pltpu.CompilerParams(
            dimension_semantics=("arbitrary",), vmem_limit_bytes=VMEM_LIMIT),
        name="ssm",
    )(u, gs, h0re, h0im, *[wts[n] for n in names])


def _attn_kernel(q_ref, kf_ref, vf_ref, kl_ref, vl_ref, gm_ref, ms_ref, x_ref, onorm_ref, w_out_ref,
                 y_ref, s_full, ml_scr, s_last, m_scr, acc_scr, *, tq, tk, causal, hg):
    i = pl.program_id(1)
    n_full = i if causal else kf_ref.shape[1] // tk
    last_len = kl_ref.shape[1]
    if causal:
        qc = lax.broadcasted_iota(jnp.int32, (tq, last_len), 0) // CHUNK
        kc = lax.broadcasted_iota(jnp.int32, (tq, last_len), 1) // CHUNK
        mask = qc >= kc
    dn = (((1,), (1,)), ((), ()))
    lane = lax.broadcasted_iota(jnp.int32, (tq, LANES), 1)

    def lane_tiles(s):
        return [s[:, LANES * c:LANES * (c + 1)] for c in range(s.shape[1] // LANES)]

    def probs(s, m_rep):
        if s.shape[1] % LANES == 0:
            p = jnp.concatenate([jnp.exp2(t - m_rep) for t in lane_tiles(s)], axis=-1)
        else:
            p = jnp.exp2(s - m_rep[:, :1])
        return p.astype(BF16)

    pairs = []
    for g in range(MLA_HEADS // hg):
        heads = [(hl, g * hg + hl) for hl in range(hg)]
        cols = {hd: slice(LANES * hd, LANES * (hd + 1)) for _, hd in heads}

        ml_scr[...] = jnp.full(ml_scr.shape, -jnp.inf, F32)

        def a_step(jt, carry, heads=heads, cols=cols):
            rows = pl.ds(pl.multiple_of(jt * tk, tk), tk)
            for hl, hd in heads:
                s = lax.dot_general(q_ref[0, :, cols[hd]], kf_ref[0, rows, cols[hd]], dn,
                                    preferred_element_type=F32)
                s_full[hl, jt] = s
                ml_scr[hl] = functools.reduce(jnp.maximum, lane_tiles(s), ml_scr[hl])
            return carry

        lax.fori_loop(0, n_full, a_step, 0)

        for hl, hd in heads:
            s = lax.dot_general(q_ref[0, :, cols[hd]], kl_ref[0, :, cols[hd]], dn,
                                preferred_element_type=F32)
            if causal:
                s = jnp.where(mask, s, -jnp.inf)
            s_last[hl] = s
            m = jnp.maximum(jnp.max(s, axis=-1, keepdims=True),
                            jnp.max(ml_scr[hl], axis=-1, keepdims=True))
            m_scr[hl] = jnp.broadcast_to(m, (tq, LANES))
            acc_scr[hl] = jnp.zeros((tq, LANES), F32)

        def b_step(jt, carry, heads=heads, cols=cols):
            rows = pl.ds(pl.multiple_of(jt * tk, tk), tk)
            for hl, hd in heads:
                p = probs(s_full[hl, jt], m_scr[hl])
                acc_scr[hl] += jnp.dot(p, vf_ref[0, rows, cols[hd]], preferred_element_type=F32)
            return carry

        lax.fori_loop(0, n_full, b_step, 0)

        outs = []
        for hl, hd in heads:
            p = probs(s_last[hl], m_scr[hl])
            acc = acc_scr[hl] + jnp.dot(p, vl_ref[0, :, cols[hd]], preferred_element_type=F32)
            ones_col = V_DIM if hd % 2 == 0 else 0
            outs.append(acc / acc[:, ones_col:ones_col + 1])
        for e in range(0, hg, 2):
            pairs.append(jnp.where(lane < V_DIM, outs[e], outs[e + 1]))

    attn = jnp.concatenate(pairs, axis=-1)
    gm = gm_ref[0]
    mla = _rms(attn, onorm_ref[...]) * (gm * jax.nn.sigmoid(gm))
    mix_ssm = [ms_ref[j, 0] for j in range(ms_ref.shape[0])]
    mix = jnp.concatenate(mix_ssm + [mla], axis=-1).astype(BF16)
    y_ref[0] = x_ref[0] + jnp.dot(mix, w_out_ref[...], preferred_element_type=F32)


def _attn_call(q, k_full, v_full, k_last, v_last, gm, mix_ssm, x, wts, *, tk, causal, hg):
    b, s, d = x.shape
    t_full = k_full.shape[1]
    ssm_w = gm.shape[2]
    tq = tk if causal else s
    last_len = tq if causal else k_last.shape[1]
    max_full = t_full // tk - 1 if causal else t_full // tk
    row = lambda j, i: (j, i, 0)
    res = lambda j, i: (j, 0, 0)
    last = pl.BlockSpec((1, last_len, HEAD_W), row if causal else res)
    scratch = [pltpu.VMEM((hg, max_full, tq, tk), F32), pltpu.VMEM((hg, tq, LANES), F32),
               pltpu.VMEM((hg, tq, last_len), F32), pltpu.VMEM((hg, tq, LANES), F32),
               pltpu.VMEM((hg, tq, LANES), F32)]
    return pl.pallas_call(
        functools.partial(_attn_kernel, tq=tq, tk=tk, causal=causal, hg=hg),
        out_shape=jax.ShapeDtypeStruct((b, s, d), F32),
        grid=(b, s // tq),
        in_specs=[pl.BlockSpec((1, tq, HEAD_W), row),
                  pl.BlockSpec((1, t_full, HEAD_W), res), pl.BlockSpec((1, t_full, HEAD_W), res),
                  last, last,
                  pl.BlockSpec((1, tq, ssm_w), row),
                  pl.BlockSpec((mix_ssm.shape[0], 1, tq, LANES), lambda j, i: (0, j, i, 0)),
                  pl.BlockSpec((1, tq, d), row),
                  _full(wts['onorm_mla'].shape), _full(wts['w_out'].shape)],
        out_specs=pl.BlockSpec((1, tq, d), row),
        scratch_shapes=scratch,
        compiler_params=pltpu.CompilerParams(
            dimension_semantics=("arbitrary", "arbitrary"), vmem_limit_bytes=VMEM_LIMIT),
        name="attn",
    )(q, k_full, v_full, k_last, v_last, gm, mix_ssm, x, wts['onorm_mla'], wts['w_out'])


def _head_block_cols(w, pieces):
    k = w.shape[0]
    w3 = w.reshape(k, MLA_HEADS, w.shape[1] // MLA_HEADS)
    cols = [w3[:, :, p[0]:p[0] + p[1]] if isinstance(p, tuple) else jnp.zeros((k, MLA_HEADS, p), w.dtype)
            for p in pieces]
    return jnp.concatenate(cols, axis=-1).reshape(k, HEAD_W)


def _disc_kernel(are_ref, aim_ref, ldt_ref, btr_ref, bti_ref, cr_ref, ci_ref,
                 apr_ref, api_ref, wbr_ref, wbi_ref, wcr_ref, wci_ref, f_ref):
    r_blk = wbr_ref.shape[0]
    lr, li = are_ref[...], aim_ref[...]
    dt = jnp.exp(ldt_ref[...])
    mag = jnp.exp(lr * dt)
    ar, ai = mag * jnp.cos(li * dt), mag * jnp.sin(li * dt)
    den = lr * lr + li * li
    kr = ((ar - 1.0) * lr + ai * li) / den
    ki = (ai * lr - (ar - 1.0) * li) / den
    pw = [(jnp.ones_like(ar), jnp.zeros_like(ai))]
    for _ in range(r_blk):
        pr, pi = pw[-1]
        pw.append((pr * ar - pi * ai, pr * ai + pi * ar))
    apr_ref[...], api_ref[...] = pw[r_blk]
    nt = (((1,), (1,)), ((), ()))
    for g in range(btr_ref.shape[0]):
        row = lambda v: v[g:g + 1, :]
        b_r = row(kr) * btr_ref[g] - row(ki) * bti_ref[g]
        b_i = row(kr) * bti_ref[g] + row(ki) * btr_ref[g]
        c_r, c_i = cr_ref[g], ci_ref[g]
        for i in range(r_blk):
            pr, pi = pw[r_blk - 1 - i]
            wbr_ref[i, g] = row(pr) * b_r - row(pi) * b_i
            wbi_ref[i, g] = row(pr) * b_i + row(pi) * b_r
        for m in range(r_blk + 1):
            pr, pi = pw[m]
            ca_r = c_r * row(pr) - c_i * row(pi)
            ca_i = c_r * row(pi) + c_i * row(pr)
            if m >= 1:
                wcr_ref[m - 1, g] = ca_r
                wci_ref[m - 1, g] = ca_i
            if m < r_blk:
                f_ref[m, g] = (
                    lax.dot_general(ca_r, b_r, nt, precision=lax.Precision.HIGHEST,
                                    preferred_element_type=F32)
                    - lax.dot_general(ca_i, b_i, nt, precision=lax.Precision.HIGHEST,
                                      preferred_element_type=F32))


def _prepare_weights(norm_in, w_in, ssm_a_re, ssm_a_im, ssm_log_dt, ssm_b_re, ssm_b_im, ssm_c_re,
                     ssm_c_im, ssm_d, w_glu, b_glu, q_lora_norm, kv_lora_norm, w_uq, w_ukv,
                     q_nope_norm, k_nope_norm, q_rope_norm, k_rope_norm, out_norm_ssm,
                     out_norm_mla, w_out):
    groups, n_state = ssm_a_re.shape
    ssm_w = groups * SSM_GROUP
    row = lambda v: v.reshape(1, -1).astype(F32)
    tail = LANES - ROPE_DIM - NOPE_DIM

    o_kr = 2 * ssm_w + Q_LORA + KV_LORA
    w_in_p = jnp.concatenate(
        [w_in[:, :o_kr + ROPE_DIM], jnp.zeros((w_in.shape[0], LANES - ROPE_DIM), w_in.dtype),
         w_in[:, o_kr + ROPE_DIM:]], axis=1).astype(BF16)

    w_uq_p = _head_block_cols(w_uq, [(NOPE_DIM, ROPE_DIM), (0, NOPE_DIM), tail]).astype(BF16)
    w_uk_p = _head_block_cols(w_ukv, [ROPE_DIM, (0, NOPE_DIM), tail])
    w_v_lo = _head_block_cols(w_ukv, [(NOPE_DIM, V_DIM), LANES - V_DIM])
    w_v_hi = _head_block_cols(w_ukv, [LANES - V_DIM, (NOPE_DIM, V_DIM)])
    odd_head = (np.arange(HEAD_W) // LANES) % 2 == 1
    w_ukv_p = jnp.concatenate([w_uk_p, jnp.where(odd_head[None, :], w_v_hi, w_v_lo)], axis=1).astype(BF16)

    scale = (NOPE_DIM + ROPE_DIM) ** -0.5 * np.log2(np.e)
    vones = np.zeros((MLA_HEADS, LANES), np.float32)
    vones[0::2, V_DIM] = 1.0
    vones[1::2, 0] = 1.0
    vones = jnp.asarray(vones.reshape(1, HEAD_W))
    zeros = lambda n: jnp.zeros((n,), F32)
    gq =jnp.concatenate([q_rope_norm, q_nope_norm, zeros(tail)]) * scale
    gkr = jnp.concatenate([k_rope_norm, zeros(LANES - ROPE_DIM)])
    gkn = jnp.concatenate([zeros(ROPE_DIM), k_nope_norm, zeros(tail)])

    seg = np.zeros((LANES, LANES), np.float32)
    seg[:ROPE_DIM, :ROPE_DIM] = 1.0 / ROPE_DIM
    seg[ROPE_DIM:ROPE_DIM + NOPE_DIM, ROPE_DIM:ROPE_DIM + NOPE_DIM] = 1.0 / NOPE_DIM
    seg[ROPE_DIM + NOPE_DIM:, ROPE_DIM + NOPE_DIM:] = 1.0 / tail
    seg = jnp.asarray(np.kron(np.eye(2, dtype=np.float32), seg), BF16)
    place = jnp.asarray(np.eye(ROPE_DIM, LANES, dtype=np.float32), BF16)

    r_blk = SSM_STEPS_PER_SCAN
    gp = jax.ShapeDtypeStruct((groups, n_state), F32)
    rgcp = jax.ShapeDtypeStruct((r_blk, groups, SSM_GROUP, n_state), F32)
    rgcc = jax.ShapeDtypeStruct((r_blk, groups, SSM_GROUP, SSM_GROUP), F32)
    a_re, a_im, wb_re, wb_im, wc_re, wc_im, f_m = pl.pallas_call(
        _disc_kernel, out_shape=[gp, gp, rgcp, rgcp, rgcp, rgcp, rgcc], name="disc")(
        ssm_a_re.astype(F32), ssm_a_im.astype(F32), ssm_log_dt.astype(F32).reshape(groups, 1),
        jnp.swapaxes(ssm_b_re, 1, 2).astype(F32), jnp.swapaxes(ssm_b_im, 1, 2).astype(F32),
        ssm_c_re.astype(F32), ssm_c_im.astype(F32))
    n_blocks = groups // GROUPS_PER_BLOCK
    eye = np.eye(GROUPS_PER_BLOCK, dtype=np.float32)

    def block_diag(part):
        r, c = part.shape[1:]
        p5 = part.reshape(n_blocks, GROUPS_PER_BLOCK, r, 1, c) * eye[None, :, None, :, None]
        return p5.reshape(n_blocks, GROUPS_PER_BLOCK * r, GROUPS_PER_BLOCK * c)

    wb = jnp.concatenate(
        [jnp.concatenate([block_diag(wb_re[i]), block_diag(wb_im[i])], axis=2) for i in range(r_blk)],
        axis=1).astype(BF16)
    wc = jnp.concatenate(
        [jnp.concatenate([block_diag(jnp.swapaxes(wc_re[j], 1, 2)),
                          block_diag(-jnp.swapaxes(wc_im[j], 1, 2))], axis=1) for j in range(r_blk)],
        axis=2).astype(BF16)
    f_blocks = [block_diag(jnp.swapaxes(f_m[m], 1, 2)) for m in range(r_blk)]
    f_zero = jnp.zeros_like(f_blocks[0])
    wf = jnp.concatenate(
        [jnp.concatenate([f_blocks[j - i] if j >= i else f_zero for j in range(r_blk)], axis=2)
         for i in range(r_blk)], axis=1).astype(BF16)

    return dict(
        ssm_w=ssm_w,
        norm_in=row(norm_in), w_in=w_in_p, qln=row(q_lora_norm), kvln=row(kv_lora_norm),
        w_uq=w_uq_p, w_ukv=w_ukv_p, gq=row(gq), gkr=row(gkr), gkn=row(gkn), seg=seg, vones=vones,
        place=place, a_re=row(a_re), a_im=row(a_im), wb=wb, wc=wc, wf=wf,
        ssm_d=row(ssm_d), w_glu=w_glu.astype(BF16), b_glu=row(b_glu), onorm_ssm=row(out_norm_ssm),
        onorm_mla=row(out_norm_mla), w_out=w_out.astype(BF16))


def _rope_tables(start, n):
    half = ROPE_DIM // 2
    inv = ROPE_THETA ** (-jnp.arange(half, dtype=F32) / half)
    ang = (start + jnp.arange(n)).astype(F32)[:, None] * inv[None, :]
    cos, sin = jnp.cos(ang), jnp.sin(ang)
    cos_t = jnp.concatenate([cos, cos, jnp.ones((n, LANES - ROPE_DIM), F32)], axis=1)
    sin_a = jnp.concatenate([-sin, jnp.zeros((n, LANES - half), F32)], axis=1)
    sin_b = jnp.concatenate([jnp.zeros((n, half), F32), sin, jnp.zeros((n, LANES - ROPE_DIM), F32)],
                            axis=1)
    return cos_t, sin_a, sin_b


def _mixer(x, pos0, h0re, h0im, past, wts, *, proj_tq, ssm_steps, attn_tk):
    u, gs, gm, q, k, v, ckv, kr = _proj_call(x, _rope_tables(pos0, x.shape[1]), wts, tq=proj_tq)
    mix_ssm, hre, him = _ssm_call(u, gs, h0re, h0im, wts, steps=ssm_steps, unroll=True)
    if past is None:
        y = _attn_call(q, k, v, k, v, gm, mix_ssm, x, wts, tk=attn_tk, causal=True, hg=MLA_HEADS)
    else:
        k_past, v_past = _expand_call(*past, wts, tq=512)
        y = _attn_call(q, k_past, v_past, k, v, gm, mix_ssm, x, wts, tk=attn_tk, causal=False,
                       hg=MLA_HEADS)
    return y, ckv, kr, hre, him


def kernel(x_prompt, x_sample, cache_ckv, cache_krope, state_ssm_re, state_ssm_im, norm_in, w_in, ssm_a_re, ssm_a_im, ssm_log_dt, ssm_b_re, ssm_b_im, ssm_c_re, ssm_c_im, ssm_d, w_glu, b_glu, q_lora_norm, kv_lora_norm, w_uq, w_ukv, q_nope_norm, k_nope_norm, q_rope_norm, k_rope_norm, out_norm_ssm, out_norm_mla, w_out):
    depth = norm_in.shape[0]
    assert depth == 1, "single mixer layer"
    params = (norm_in, w_in, ssm_a_re, ssm_a_im, ssm_log_dt, ssm_b_re, ssm_b_im, ssm_c_re, ssm_c_im,
              ssm_d, w_glu, b_glu, q_lora_norm, kv_lora_norm, w_uq, w_ukv, q_nope_norm, k_nope_norm,
              q_rope_norm, k_rope_norm, out_norm_ssm, out_norm_mla, w_out)
    wts = _prepare_weights(*[p[0] for p in params])
    groups, n_state = ssm_a_re.shape[1:]
    bp, sp, _ = x_prompt.shape
    bs, ss, _ = x_sample.shape
    past_len = cache_ckv.shape[2]

    zero_state = jnp.zeros((bp, groups * n_state), F32)
    yp, ckv_p, kr_p, re_p, im_p = _mixer(
        x_prompt, 0, zero_state, zero_state, None, wts,
        proj_tq=512, ssm_steps=64, attn_tk=256)
    ys, ckv_s, kr_s, re_s, im_s = _mixer(
        x_sample, past_len,
        state_ssm_re[0].reshape(bs, groups * n_state), state_ssm_im[0].reshape(bs, groups * n_state),
        (cache_ckv[0], cache_krope[0]), wts,
        proj_tq=ss, ssm_steps=ss, attn_tk=past_len)

    st = lambda h, bb: h.reshape(1, bb, groups, n_state)
    return (yp, ys, ckv_p[None], kr_p[None], st(re_p, bp), st(im_p, bp),
            ckv_s[None], kr_s[None], st(re_s, bs), st(im_s, bs))
```

```python
import functools

import numpy as np
import jax
import jax.numpy as jnp
from jax import lax
from jax.experimental import pallas as pl
from jax.experimental.pallas import tpu as pltpu

F32 = jnp.float32
BF16 = jnp.bfloat16

CHUNK = 64
SSM_GROUP = 16
SSM_STATE = 64
MLA_HEADS = 8
NOPE_DIM = 64
ROPE_DIM = 32
V_DIM = 64
Q_LORA = 256
KV_LORA = 128
ROPE_THETA = 10000.0
EPS = 1e-6

LANES = 128
HEAD_W = MLA_HEADS * LANES
GROUPS_PER_BLOCK = 8
SSM_STEPS_PER_SCAN = 4
VMEM_LIMIT = 56 * 1024 * 1024


def _rms(x, gain):
    return x * lax.rsqrt(jnp.mean(x * x, axis=-1, keepdims=True) + EPS) * gain


def _seg_rms(x, seg):
    ms = jnp.dot((x * x).astype(BF16), seg, preferred_element_type=F32)
    return x * lax.rsqrt(ms + EPS)


def _rope_block(x, cos_t, sin_a, sin_b):
    return (x * cos_t + pltpu.roll(x, LANES - ROPE_DIM // 2, 1) * sin_a
            + pltpu.roll(x, ROPE_DIM // 2, 1) * sin_b)


def _expand_kv(ckv, kr_blk, w_ukv_ref, gkn_ref, seg_ref, vones_ref, k_ref, v_ref):
    kv = jnp.dot(ckv.astype(BF16), w_ukv_ref[...], preferred_element_type=F32)
    gkn = gkn_ref[...]
    for p in range(MLA_HEADS // 2):
        kn = _seg_rms(kv[:, 2 * LANES * p:2 * LANES * (p + 1)], seg_ref[...])
        for j in range(2):
            h = 2 * p + j
            blk = kn[:, LANES * j:LANES * (j + 1)] * gkn + kr_blk
            k_ref[0, :, LANES * h:LANES * (h + 1)] = blk.astype(BF16)
    v_ref[0] = (kv[:, HEAD_W:] + vones_ref[...]).astype(BF16)


def _proj_kernel(x_ref, cos_ref, sa_ref, sb_ref, norm_in_ref, w_in_ref, qln_ref, kvln_ref,
                 w_uq_ref, w_ukv_ref, gq_ref, gkr_ref, gkn_ref, seg_ref, vones_ref,
                 u_ref, gs_ref, gm_ref, q_ref, k_ref, v_ref, ckv_ref, kr_ref, *, ssm_w):
    x = x_ref[0]
    h = _rms(x, norm_in_ref[...])
    z = jnp.dot(h.astype(BF16), w_in_ref[...], preferred_element_type=F32)
    o = 0
    for dst in (u_ref, gs_ref):
        for j in range(ssm_w // LANES):
            dst[j, 0] = z[:, o + LANES * j:o + LANES * (j + 1)]
        o += ssm_w
    c_q = z[:, o:o + Q_LORA]
    o += Q_LORA
    c_kv = z[:, o:o + KV_LORA]
    o += KV_LORA
    kr_raw = z[:, o:o + LANES]
    o += LANES
    gm_ref[0] = z[:, o:]

    cos_t, sin_a, sin_b = cos_ref[...], sa_ref[...], sb_ref[...]
    seg = seg_ref[...]

    kr_ms = jnp.dot((kr_raw * kr_raw).astype(BF16), seg[:LANES, :LANES], preferred_element_type=F32)
    kr_blk = _rope_block(kr_raw * lax.rsqrt(kr_ms + EPS) * gkr_ref[...], cos_t, sin_a, sin_b)
    kr_ref[0] = kr_blk[:, :ROPE_DIM]

    q = jnp.dot(_rms(c_q, qln_ref[...]).astype(BF16), w_uq_ref[...], preferred_element_type=F32)
    gq = gq_ref[...]
    for p in range(MLA_HEADS // 2):
        qn = _seg_rms(q[:, 2 * LANES * p:2 * LANES * (p + 1)], seg)
        for j in range(2):
            hd = 2 * p + j
            blk = _rope_block(qn[:, LANES * j:LANES * (j + 1)] * gq, cos_t, sin_a, sin_b)
            q_ref[0, :, LANES * hd:LANES * (hd + 1)] = blk.astype(BF16)

    ckv = _rms(c_kv, kvln_ref[...])
    ckv_ref[0] = ckv
    _expand_kv(ckv, kr_blk, w_ukv_ref, gkn_ref, seg_ref, vones_ref, k_ref, v_ref)


def _expand_kernel(ckv_ref, kr_ref, w_ukv_ref, gkn_ref, seg_ref, vones_ref, place_ref, k_ref, v_ref):
    kr_blk = jnp.dot(kr_ref[0].astype(BF16), place_ref[...], preferred_element_type=F32)
    _expand_kv(ckv_ref[0], kr_blk, w_ukv_ref, gkn_ref, seg_ref, vones_ref, k_ref, v_ref)


def _full(shape):
    n = len(shape)
    return pl.BlockSpec(shape, lambda *_: (0,) * n)


def _proj_call(x, tables, wts, *, tq):
    b, s, d = x.shape
    ssm_w = wts['ssm_w']
    cos_t, sin_a, sin_b = tables
    grid = (s // tq, b)
    row = lambda i, j: (j, i, 0)
    tab = pl.BlockSpec((tq, LANES), lambda i, j: (i, 0))
    in_specs = [pl.BlockSpec((1, tq, d), row), tab, tab, tab,
                _full(wts['norm_in'].shape), _full(wts['w_in'].shape), _full(wts['qln'].shape),
                _full(wts['kvln'].shape), _full(wts['w_uq'].shape), _full(wts['w_ukv'].shape),
                _full(wts['gq'].shape), _full(wts['gkr'].shape), _full(wts['gkn'].shape),
                _full(wts['seg'].shape), _full(wts['vones'].shape)]
    slabs = ssm_w // LANES
    tb_spec = pl.BlockSpec((slabs, 1, tq, LANES), lambda i, j: (0, j, i, 0))
    out_shape = [jax.ShapeDtypeStruct((slabs, b, s, LANES), F32),
                 jax.ShapeDtypeStruct((slabs, b, s, LANES), F32),
                 jax.ShapeDtypeStruct((b, s, ssm_w), F32),
                 jax.ShapeDtypeStruct((b, s, HEAD_W), BF16),
                 jax.ShapeDtypeStruct((b, s, HEAD_W), BF16),
                 jax.ShapeDtypeStruct((b, s, HEAD_W), BF16),
                 jax.ShapeDtypeStruct((b, s, KV_LORA), F32),
                 jax.ShapeDtypeStruct((b, s, ROPE_DIM), F32)]
    out_specs = [tb_spec, tb_spec,
                 pl.BlockSpec((1, tq, ssm_w), row),
                 pl.BlockSpec((1, tq, HEAD_W), row), pl.BlockSpec((1, tq, HEAD_W), row),
                 pl.BlockSpec((1, tq, HEAD_W), row),
                 pl.BlockSpec((1, tq, KV_LORA), row), pl.BlockSpec((1, tq, ROPE_DIM), row)]
    return pl.pallas_call(
        functools.partial(_proj_kernel, ssm_w=ssm_w),
        out_shape=out_shape, grid=grid, in_specs=in_specs, out_specs=out_specs,
        compiler_params=pltpu.CompilerParams(
            dimension_semantics=("arbitrary", "arbitrary"), vmem_limit_bytes=VMEM_LIMIT),
        name="proj",
    )(x, cos_t, sin_a, sin_b, wts['norm_in'], wts['w_in'], wts['qln'], wts['kvln'], wts['w_uq'],
      wts['w_ukv'], wts['gq'], wts['gkr'], wts['gkn'], wts['seg'], wts['vones'])


def _expand_call(ckv, kr, wts, *, tq):
    b, t, _ = ckv.shape
    row = lambda j, i: (j, i, 0)
    return pl.pallas_call(
        _expand_kernel,
        out_shape=[jax.ShapeDtypeStruct((b, t, HEAD_W), BF16)] * 2,
        grid=(b, t // tq),
        in_specs=[pl.BlockSpec((1, tq, KV_LORA), row), pl.BlockSpec((1, tq, ROPE_DIM), row),
                  _full(wts['w_ukv'].shape), _full(wts['gkn'].shape), _full(wts['seg'].shape),
                  _full(wts['vones'].shape), _full(wts['place'].shape)],
        out_specs=[pl.BlockSpec((1, tq, HEAD_W), row)] * 2,
        compiler_params=pltpu.CompilerParams(
            dimension_semantics=("arbitrary", "arbitrary"), vmem_limit_bytes=VMEM_LIMIT),
        name="expand",
    )(ckv, kr, wts['w_ukv'], wts['gkn'], wts['seg'], wts['vones'], wts['place'])


def _ssm_kernel(u_ref, gs_ref, h0re_ref, h0im_ref, are_ref, aim_ref, wb_ref, wc_ref, wf_ref, d_ref,
                w_glu_ref, b_glu_ref, onorm_ref, mix_ref, hre_ref, him_ref, us_ref, xs_ref, y_ref,
                *, batch, steps, unroll):
    @pl.when(pl.program_id(0) == 0)
    def _():
        hre_ref[...] = h0re_ref[...]
        him_ref[...] = h0im_ref[...]

    n_blocks = wb_ref.shape[0]
    r_blk = wb_ref.shape[1] // LANES
    half = wb_ref.shape[2] // 2
    n_k = steps // r_blk
    for b in range(batch):
        for j in range(n_blocks):
            for i in range(r_blk):
                us_ref[j, i, pl.ds(b, n_k, stride=batch), :] = (
                    u_ref[j, b, pl.ds(i, n_k, stride=r_blk), :])

    for gb in range(n_blocks):
        lhs = jnp.concatenate([us_ref[gb, i] for i in range(r_blk)], axis=-1).astype(BF16)
        xs_ref[...] = jnp.dot(lhs, wb_ref[gb], preferred_element_type=F32)
        cols = slice(half * gb, half * (gb + 1))
        a_re = jnp.broadcast_to(are_ref[:, cols], (batch, half))
        a_im = jnp.broadcast_to(aim_ref[:, cols], (batch, half))

        def step(k, carry):
            h_re, h_im = carry
            rows = pl.ds(pl.multiple_of(k * batch, batch), batch)
            n_re = a_re * h_re - a_im * h_im + xs_ref[rows, :half]
            n_im = a_re * h_im + a_im * h_re + xs_ref[rows, half:]
            xs_ref[rows, :half] = h_re
            xs_ref[rows, half:] = h_im
            return n_re, n_im

        h_re, h_im = lax.fori_loop(0, n_k, step, (hre_ref[:, cols], him_ref[:, cols]), unroll=unroll)
        hre_ref[:, cols] = h_re
        him_ref[:, cols] = h_im
        y = (jnp.dot(xs_ref[...].astype(BF16), wc_ref[gb], preferred_element_type=F32)
             + jnp.dot(lhs, wf_ref[gb], preferred_element_type=F32))
        for i in range(r_blk):
            y_ref[i, :, LANES * gb:LANES * (gb + 1)] = y[:, LANES * i:LANES * (i + 1)]

    rows = r_blk * n_k * batch
    u = jnp.concatenate([us_ref[j].reshape(rows, LANES) for j in range(n_blocks)], axis=-1)
    y = y_ref[...].reshape(rows, n_blocks * LANES) + d_ref[...] * u
    yg = jax.nn.gelu(y)
    glu = jnp.dot(yg.astype(BF16), w_glu_ref[...], preferred_element_type=F32) + b_glu_ref[...]
    out = _rms(yg * jax.nn.sigmoid(glu), onorm_ref[...])
    for j in range(n_blocks):
        us_ref[j] = out[:, LANES * j:LANES * (j + 1)].reshape(r_blk, n_k * batch, LANES)
    for b in range(batch):
        for j in range(n_blocks):
            for i in range(r_blk):
                tok = pl.ds(i, n_k, stride=r_blk)
                gs = gs_ref[j, b, tok, :]
                o = us_ref[j, i, pl.ds(b, n_k, stride=batch), :]
                mix_ref[j, b, tok, :] = o * (gs * jax.nn.sigmoid(gs))


def _ssm_call(u, gs, h0re, h0im, wts, *, steps, unroll):
    n_blocks, batch, seq, _ = u.shape
    n_state = h0re.shape[1]
    r_blk = wts['wb'].shape[1] // LANES
    width = wts['wb'].shape[2]
    assert wts['wb'].shape[0] == n_blocks and steps % r_blk == 0
    blk = pl.BlockSpec((n_blocks, batch, steps, LANES), lambda i: (0, 0, i, 0))
    names = ['a_re', 'a_im', 'wb', 'wc', 'wf', 'ssm_d', 'w_glu', 'b_glu', 'onorm_ssm']
    n_rows = steps // r_blk * batch
    return pl.pallas_call(
        functools.partial(_ssm_kernel, batch=batch, steps=steps, unroll=unroll),
        out_shape=[jax.ShapeDtypeStruct(u.shape, F32),
                   jax.ShapeDtypeStruct((batch, n_state), F32),
                   jax.ShapeDtypeStruct((batch, n_state), F32)],
        grid=(seq // steps,),
        in_specs=[blk, blk, _full(h0re.shape), _full(h0im.shape)] + [_full(wts[n].shape) for n in names],
        out_specs=[blk, _full((batch, n_state)), _full((batch, n_state))],
        scratch_shapes=[pltpu.VMEM((n_blocks, r_blk, n_rows, LANES), F32),
                        pltpu.VMEM((n_rows, width), F32),
                        pltpu.VMEM((r_blk, n_rows, n_blocks * LANES), F32)],
        compiler_params=pltpu.CompilerParams(
            dimension_semantics=("arbitrary",), vmem_limit_bytes=VMEM_LIMIT),
        name="ssm",
    )(u, gs, h0re, h0im, *[wts[n] for n in names])


def _attn_kernel(*refs, tq, tk, causal, hg):
    if causal:
        q_ref, kf_ref, vf_ref = refs[:3]
        rest = refs[3:]
    else:
        q_ref, kf_ref, vf_ref, kl_ref, vl_ref = refs[:5]
        rest = refs[5:]
    (gm_ref, ms_ref, x_ref, onorm_ref, w_out_ref, y_ref,
     s_full, ml_scr, s_last, m_scr, acc_scr) = rest
    i = pl.program_id(1)
    if causal:
        n_full = (i * tq) // tk
        last_rows = pl.ds(pl.multiple_of(n_full * tk, tk), tk)
        k_last = lambda c: kf_ref[0, last_rows, c]
        v_last = lambda c: vf_ref[0, last_rows, c]
        qc = i * (tq // CHUNK) + lax.broadcasted_iota(jnp.int32, (tq, tk), 0) // CHUNK
        kc = n_full * (tk // CHUNK) + lax.broadcasted_iota(jnp.int32, (tq, tk), 1) // CHUNK
        mask = qc >= kc
    else:
        n_full = kf_ref.shape[1] // tk
        k_last = lambda c: kl_ref[0, :, c]
        v_last = lambda c: vl_ref[0, :, c]
    dn = (((1,), (1,)), ((), ()))
    lane = lax.broadcasted_iota(jnp.int32, (tq, LANES), 1)

    def lane_tiles(s):
        return [s[:, LANES * c:LANES * (c + 1)] for c in range(s.shape[1] // LANES)]

    def probs(s, m_rep):
        if s.shape[1] % LANES == 0:
            p = jnp.concatenate([jnp.exp2(t - m_rep) for t in lane_tiles(s)], axis=-1)
        else:
            p = jnp.exp2(s - m_rep[:, :1])
        return p.astype(BF16)

    pairs = []
    for g in range(MLA_HEADS // hg):
        heads = [(hl, g * hg + hl) for hl in range(hg)]
        cols = {hd: slice(LANES * hd, LANES * (hd + 1)) for _, hd in heads}

        ml_scr[...] = jnp.full(ml_scr.shape, -jnp.inf, F32)

        def a_step(jt, carry, heads=heads, cols=cols):
            rows = pl.ds(pl.multiple_of(jt * tk, tk), tk)
            for hl, hd in heads:
                s = lax.dot_general(q_ref[0, :, cols[hd]], kf_ref[0, rows, cols[hd]], dn,
                                    preferred_element_type=F32)
                s_full[hl, jt] = s
                ml_scr[hl] = functools.reduce(jnp.maximum, lane_tiles(s), ml_scr[hl])
            return carry

        lax.fori_loop(0, n_full, a_step, 0)

        for hl, hd in heads:
            s = lax.dot_general(q_ref[0, :, cols[hd]], k_last(cols[hd]), dn,
                                preferred_element_type=F32)
            if causal:
                s = jnp.where(mask, s, -jnp.inf)
            s_last[hl] = s
            if s.shape[1] % LANES == 0:
                m = jnp.max(functools.reduce(jnp.maximum, lane_tiles(s), ml_scr[hl]),
                            axis=-1, keepdims=True)
            else:
                m = jnp.maximum(jnp.max(s, axis=-1, keepdims=True),
                                jnp.max(ml_scr[hl], axis=-1, keepdims=True))
            m_scr[hl] = jnp.broadcast_to(m, (tq, LANES))
            acc_scr[hl] = jnp.zeros((tq, LANES), F32)

        def b_step(jt, carry, heads=heads, cols=cols):
            rows = pl.ds(pl.multiple_of(jt * tk, tk), tk)
            for hl, hd in heads:
                p = probs(s_full[hl, jt], m_scr[hl])
                acc_scr[hl] += jnp.dot(p, vf_ref[0, rows, cols[hd]], preferred_element_type=F32)
            return carry

        lax.fori_loop(0, n_full, b_step, 0)

        outs = []
        for hl, hd in heads:
            p = probs(s_last[hl], m_scr[hl])
            acc = acc_scr[hl] + jnp.dot(p, v_last(cols[hd]), preferred_element_type=F32)
            ones_col = V_DIM if hd % 2 == 0 else 0
            outs.append(acc / acc[:, ones_col:ones_col + 1])
        for e in range(0, hg, 2):
            pairs.append(jnp.where(lane < V_DIM, outs[e], outs[e + 1]))

    attn = jnp.concatenate(pairs, axis=-1)
    gm = gm_ref[0]
    mla = _rms(attn, onorm_ref[...]) * (gm * jax.nn.sigmoid(gm))
    mix_ssm = [ms_ref[j, 0] for j in range(ms_ref.shape[0])]
    mix = jnp.concatenate(mix_ssm + [mla], axis=-1).astype(BF16)
    y_ref[0] = x_ref[0] + jnp.dot(mix, w_out_ref[...], preferred_element_type=F32)


def _attn_call(q, k_full, v_full, last, gm, mix_ssm, x, wts, *, tq, tk, hg):
    b, s, d = x.shape
    t_full = k_full.shape[1]
    ssm_w = gm.shape[2]
    causal = last is None
    row = lambda j, i: (j, i, 0)
    res = lambda j, i: (j, 0, 0)
    if causal:
        assert tk == 2 * tq and t_full % tk == 0 and tq % CHUNK == 0
        last, last_specs, last_len, max_full = (), [], tk, t_full // tk - 1
    else:
        last_len, max_full = last[0].shape[1], t_full // tk
        last_specs = [pl.BlockSpec((1, last_len, HEAD_W), res)] * 2
    scratch = [pltpu.VMEM((hg, max_full, tq, tk), F32), pltpu.VMEM((hg, tq, LANES), F32),
               pltpu.VMEM((hg, tq, last_len), F32), pltpu.VMEM((hg, tq, LANES), F32),
               pltpu.VMEM((hg, tq, LANES), F32)]
    return pl.pallas_call(
        functools.partial(_attn_kernel, tq=tq, tk=tk, causal=causal, hg=hg),
        out_shape=jax.ShapeDtypeStruct((b, s, d), F32),
        grid=(b, s // tq),
        in_specs=[pl.BlockSpec((1, tq, HEAD_W), row),
                  pl.BlockSpec((1, t_full, HEAD_W), res), pl.BlockSpec((1, t_full, HEAD_W), res),
                  *last_specs,
                  pl.BlockSpec((1, tq, ssm_w), row),
                  pl.BlockSpec((mix_ssm.shape[0], 1, tq, LANES), lambda j, i: (0, j, i, 0)),
                  pl.BlockSpec((1, tq, d), row),
                  _full(wts['onorm_mla'].shape), _full(wts['w_out'].shape)],
        out_specs=pl.BlockSpec((1, tq, d), row),
        scratch_shapes=scratch,
        compiler_params=pltpu.CompilerParams(
            dimension_semantics=("arbitrary", "arbitrary"), vmem_limit_bytes=VMEM_LIMIT),
        name="attn",
    )(q, k_full, v_full, *last, gm, mix_ssm, x, wts['onorm_mla'], wts['w_out'])


def _head_block_cols(w, pieces):
    k = w.shape[0]
    w3 = w.reshape(k, MLA_HEADS, w.shape[1] // MLA_HEADS)
    cols = [w3[:, :, p[0]:p[0] + p[1]] if isinstance(p, tuple) else jnp.zeros((k, MLA_HEADS, p), w.dtype)
            for p in pieces]
    return jnp.concatenate(cols, axis=-1).reshape(k, HEAD_W)


def _disc_kernel(are_ref, aim_ref, ldt_ref, btr_ref, bti_ref, cr_ref, ci_ref,
                 apr_ref, api_ref, wbr_ref, wbi_ref, wcr_ref, wci_ref, f_ref):
    r_blk = wbr_ref.shape[0]
    lr, li = are_ref[...], aim_ref[...]
    dt = jnp.exp(ldt_ref[...])
    mag = jnp.exp(lr * dt)
    ar, ai = mag * jnp.cos(li * dt), mag * jnp.sin(li * dt)
    den = lr * lr + li * li
    kr = ((ar - 1.0) * lr + ai * li) / den
    ki = (ai * lr - (ar - 1.0) * li) / den
    pw = [(jnp.ones_like(ar), jnp.zeros_like(ai))]
    for _ in range(r_blk):
        pr, pi = pw[-1]
        pw.append((pr * ar - pi * ai, pr * ai + pi * ar))
    apr_ref[...], api_ref[...] = pw[r_blk]
    nt = (((1,), (1,)), ((), ()))
    for g in range(btr_ref.shape[0]):
        row = lambda v: v[g:g + 1, :]
        b_r = row(kr) * btr_ref[g] - row(ki) * bti_ref[g]
        b_i = row(kr) * bti_ref[g] + row(ki) * btr_ref[g]
        c_r, c_i = cr_ref[g], ci_ref[g]
        for i in range(r_blk):
            pr, pi = pw[r_blk - 1 - i]
            wbr_ref[i, g] = row(pr) * b_r - row(pi) * b_i
            wbi_ref[i, g] = row(pr) * b_i + row(pi) * b_r
        for m in range(r_blk + 1):
            pr, pi = pw[m]
            ca_r = c_r * row(pr) - c_i * row(pi)
            ca_i = c_r * row(pi) + c_i * row(pr)
            if m >= 1:
                wcr_ref[m - 1, g] = ca_r
                wci_ref[m - 1, g] = ca_i
            if m < r_blk:
                f_ref[m, g] = (
                    lax.dot_general(ca_r, b_r, nt, precision=lax.Precision.HIGHEST,
                                    preferred_element_type=F32)
                    - lax.dot_general(ca_i, b_i, nt, precision=lax.Precision.HIGHEST,
                                      preferred_element_type=F32))


def _prepare_weights(norm_in, w_in, ssm_a_re, ssm_a_im, ssm_log_dt, ssm_b_re, ssm_b_im, ssm_c_re,
                     ssm_c_im, ssm_d, w_glu, b_glu, q_lora_norm, kv_lora_norm, w_uq, w_ukv,
                     q_nope_norm, k_nope_norm, q_rope_norm, k_rope_norm, out_norm_ssm,
                     out_norm_mla, w_out):
    groups, n_state = ssm_a_re.shape
    ssm_w = groups * SSM_GROUP
    row = lambda v: v.reshape(1, -1).astype(F32)
    tail = LANES - ROPE_DIM - NOPE_DIM

    o_kr = 2 * ssm_w + Q_LORA + KV_LORA
    w_in_p = jnp.concatenate(
        [w_in[:, :o_kr + ROPE_DIM], jnp.zeros((w_in.shape[0], LANES - ROPE_DIM), w_in.dtype),
         w_in[:, o_kr + ROPE_DIM:]], axis=1).astype(BF16)

    w_uq_p = _head_block_cols(w_uq, [(NOPE_DIM, ROPE_DIM), (0, NOPE_DIM), tail]).astype(BF16)
    w_uk_p = _head_block_cols(w_ukv, [ROPE_DIM, (0, NOPE_DIM), tail])
    w_v_lo = _head_block_cols(w_ukv, [(NOPE_DIM, V_DIM), LANES - V_DIM])
    w_v_hi = _head_block_cols(w_ukv, [LANES - V_DIM, (NOPE_DIM, V_DIM)])
    odd_head = (np.arange(HEAD_W) // LANES) % 2 == 1
    w_ukv_p = jnp.concatenate([w_uk_p, jnp.where(odd_head[None, :], w_v_hi, w_v_lo)], axis=1).astype(BF16)

    scale = (NOPE_DIM + ROPE_DIM) ** -0.5 * np.log2(np.e)
    vones = np.zeros((MLA_HEADS, LANES), np.float32)
    vones[0::2, V_DIM] = 1.0
    vones[1::2, 0] = 1.0
    vones = jnp.asarray(vones.reshape(1, HEAD_W))
    zeros = lambda n: jnp.zeros((n,), F32)
    gq =jnp.concatenate([q_rope_norm, q_nope_norm, zeros(tail)]) * scale
    gkr = jnp.concatenate([k_rope_norm, zeros(LANES - ROPE_DIM)])
    gkn = jnp.concatenate([zeros(ROPE_DIM), k_nope_norm, zeros(tail)])

    seg = np.zeros((LANES, LANES), np.float32)
    seg[:ROPE_DIM, :ROPE_DIM] = 1.0 / ROPE_DIM
    seg[ROPE_DIM:ROPE_DIM + NOPE_DIM, ROPE_DIM:ROPE_DIM + NOPE_DIM] = 1.0 / NOPE_DIM
    seg[ROPE_DIM + NOPE_DIM:, ROPE_DIM + NOPE_DIM:] = 1.0 / tail
    seg = jnp.asarray(np.kron(np.eye(2, dtype=np.float32), seg), BF16)
    place = jnp.asarray(np.eye(ROPE_DIM, LANES, dtype=np.float32), BF16)

    r_blk = SSM_STEPS_PER_SCAN
    gp = jax.ShapeDtypeStruct((groups, n_state), F32)
    rgcp = jax.ShapeDtypeStruct((r_blk, groups, SSM_GROUP, n_state), F32)
    rgcc = jax.ShapeDtypeStruct((r_blk, groups, SSM_GROUP, SSM_GROUP), F32)
    a_re, a_im, wb_re, wb_im, wc_re, wc_im, f_m = pl.pallas_call(
        _disc_kernel, out_shape=[gp, gp, rgcp, rgcp, rgcp, rgcp, rgcc], name="disc")(
        ssm_a_re.astype(F32), ssm_a_im.astype(F32), ssm_log_dt.astype(F32).reshape(groups, 1),
        jnp.swapaxes(ssm_b_re, 1, 2).astype(F32), jnp.swapaxes(ssm_b_im, 1, 2).astype(F32),
        ssm_c_re.astype(F32), ssm_c_im.astype(F32))
    n_blocks = groups // GROUPS_PER_BLOCK
    eye = np.eye(GROUPS_PER_BLOCK, dtype=np.float32)

    def block_diag(part):
        r, c = part.shape[1:]
        p5 = part.reshape(n_blocks, GROUPS_PER_BLOCK, r, 1, c) * eye[None, :, None, :, None]
        return p5.reshape(n_blocks, GROUPS_PER_BLOCK * r, GROUPS_PER_BLOCK * c)

    wb = jnp.concatenate(
        [jnp.concatenate([block_diag(wb_re[i]), block_diag(wb_im[i])], axis=2) for i in range(r_blk)],
        axis=1).astype(BF16)
    wc = jnp.concatenate(
        [jnp.concatenate([block_diag(jnp.swapaxes(wc_re[j], 1, 2)),
                          block_diag(-jnp.swapaxes(wc_im[j], 1, 2))], axis=1) for j in range(r_blk)],
        axis=2).astype(BF16)
    f_blocks = [block_diag(jnp.swapaxes(f_m[m], 1, 2)) for m in range(r_blk)]
    f_zero = jnp.zeros_like(f_blocks[0])
    wf = jnp.concatenate(
        [jnp.concatenate([f_blocks[j - i] if j >= i else f_zero for j in range(r_blk)], axis=2)
         for i in range(r_blk)], axis=1).astype(BF16)

    return dict(
        ssm_w=ssm_w,
        norm_in=row(norm_in), w_in=w_in_p, qln=row(q_lora_norm), kvln=row(kv_lora_norm),
        w_uq=w_uq_p, w_ukv=w_ukv_p, gq=row(gq), gkr=row(gkr), gkn=row(gkn), seg=seg, vones=vones,
        place=place, a_re=row(a_re), a_im=row(a_im), wb=wb, wc=wc, wf=wf,
        ssm_d=row(ssm_d), w_glu=w_glu.astype(BF16), b_glu=row(b_glu), onorm_ssm=row(out_norm_ssm),
        onorm_mla=row(out_norm_mla), w_out=w_out.astype(BF16))


def _rope_tables(start, n):
    half = ROPE_DIM // 2
    inv = ROPE_THETA ** (-jnp.arange(half, dtype=F32) / half)
    ang = (start + jnp.arange(n)).astype(F32)[:, None] * inv[None, :]
    cos, sin = jnp.cos(ang), jnp.sin(ang)
    cos_t = jnp.concatenate([cos, cos, jnp.ones((n, LANES - ROPE_DIM), F32)], axis=1)
    sin_a = jnp.concatenate([-sin, jnp.zeros((n, LANES - half), F32)], axis=1)
    sin_b = jnp.concatenate([jnp.zeros((n, half), F32), sin, jnp.zeros((n, LANES - ROPE_DIM), F32)],
                            axis=1)
    return cos_t, sin_a, sin_b


def _mixer(x, pos0, h0re, h0im, past, wts, *, proj_tq, ssm_steps, attn_tk):
    u, gs, gm, q, k, v, ckv, kr = _proj_call(x, _rope_tables(pos0, x.shape[1]), wts, tq=proj_tq)
    mix_ssm, hre, him = _ssm_call(u, gs, h0re, h0im, wts, steps=ssm_steps, unroll=True)
    if past is None:
        y = _attn_call(q, k, v, None, gm, mix_ssm, x, wts, tq=attn_tk // 2, tk=attn_tk, hg=MLA_HEADS)
    else:
        k_past, v_past = _expand_call(*past, wts, tq=512)
        y = _attn_call(q, k_past, v_past, (k, v), gm, mix_ssm, x, wts, tq=x.shape[1], tk=attn_tk,
                       hg=MLA_HEADS)
    return y, ckv, kr, hre, him


def kernel(x_prompt, x_sample, cache_ckv, cache_krope, state_ssm_re, state_ssm_im, norm_in, w_in, ssm_a_re, ssm_a_im, ssm_log_dt, ssm_b_re, ssm_b_im, ssm_c_re, ssm_c_im, ssm_d, w_glu, b_glu, q_lora_norm, kv_lora_norm, w_uq, w_ukv, q_nope_norm, k_nope_norm, q_rope_norm, k_rope_norm, out_norm_ssm, out_norm_mla, w_out):
    depth = norm_in.shape[0]
    assert depth == 1, "single mixer layer"
    params = (norm_in, w_in, ssm_a_re, ssm_a_im, ssm_log_dt, ssm_b_re, ssm_b_im, ssm_c_re, ssm_c_im,
              ssm_d, w_glu, b_glu, q_lora_norm, kv_lora_norm, w_uq, w_ukv, q_nope_norm, k_nope_norm,
              q_rope_norm, k_rope_norm, out_norm_ssm, out_norm_mla, w_out)
    wts = _prepare_weights(*[p[0] for p in params])
    groups, n_state = ssm_a_re.shape[1:]
    bp, sp, _ = x_prompt.shape
    bs, ss, _ = x_sample.shape
    past_len = cache_ckv.shape[2]

    zero_state = jnp.zeros((bp, groups * n_state), F32)
    yp, ckv_p, kr_p, re_p, im_p = _mixer(
        x_prompt, 0, zero_state, zero_state, None, wts,
        proj_tq=512, ssm_steps=64, attn_tk=512)
    ys, ckv_s, kr_s, re_s, im_s = _mixer(
        x_sample, past_len,
        state_ssm_re[0].reshape(bs, groups * n_state), state_ssm_im[0].reshape(bs, groups * n_state),
        (cache_ckv[0], cache_krope[0]), wts,
        proj_tq=ss, ssm_steps=ss, attn_tk=past_len)

    st = lambda h, bb: h.reshape(1, bb, groups, n_state)
    return (yp, ys, ckv_p[None], kr_p[None], st(re_p, bp), st(im_p, bp),
            ckv_s[None], kr_s[None], st(re_s, bs), st(im_s, bs))
```

```python
import functools

import numpy as np
import jax
import jax.numpy as jnp
from jax import lax
from jax.experimental import pallas as pl
from jax.experimental.pallas import tpu as pltpu

F32 = jnp.float32
BF16 = jnp.bfloat16

CHUNK = 64
SSM_GROUP = 16
SSM_STATE = 64
MLA_HEADS = 8
NOPE_DIM = 64
ROPE_DIM = 32
V_DIM = 64
Q_LORA = 256
KV_LORA = 128
ROPE_THETA = 10000.0
EPS = 1e-6

LANES = 128
HEAD_W = MLA_HEADS * LANES
GROUPS_PER_BLOCK = 8
SSM_STEPS_PER_SCAN = 4
VMEM_LIMIT = 56 * 1024 * 1024


def _rms(x, gain):
    return x * lax.rsqrt(jnp.mean(x * x, axis=-1, keepdims=True) + EPS) * gain


def _seg_rms(x, seg):
    ms = jnp.dot((x * x).astype(BF16), seg, preferred_element_type=F32)
    return x * lax.rsqrt(ms + EPS)


def _rope_block(x, cos_t, sin_a, sin_b):
    return (x * cos_t + pltpu.roll(x, LANES - ROPE_DIM // 2, 1) * sin_a
            + pltpu.roll(x, ROPE_DIM // 2, 1) * sin_b)


def _expand_kv(ckv, kr_blk, w_ukv_ref, gkn_ref, seg_ref, vones_ref, k_ref, v_ref):
    kv = jnp.dot(ckv.astype(BF16), w_ukv_ref[...], preferred_element_type=F32)
    gkn = gkn_ref[...]
    for p in range(MLA_HEADS // 2):
        kn = _seg_rms(kv[:, 2 * LANES * p:2 * LANES * (p + 1)], seg_ref[...])
        for j in range(2):
            h = 2 * p + j
            blk = kn[:, LANES * j:LANES * (j + 1)] * gkn + kr_blk
            k_ref[0, :, LANES * h:LANES * (h + 1)] = blk.astype(BF16)
    v_ref[0] = (kv[:, HEAD_W:] + vones_ref[...]).astype(BF16)


def _proj_kernel(x_ref, cos_ref, sa_ref, sb_ref, norm_in_ref, w_in_ref, qln_ref, kvln_ref,
                 w_uq_ref, w_ukv_ref, gq_ref, gkr_ref, gkn_ref, seg_ref, vones_ref,
                 u_ref, gs_ref, gm_ref, q_ref, k_ref, v_ref, ckv_ref, kr_ref, *, ssm_w):
    x = x_ref[0]
    h = _rms(x, norm_in_ref[...])
    z = jnp.dot(h.astype(BF16), w_in_ref[...], preferred_element_type=F32)
    o = 0
    for dst in (u_ref, gs_ref):
        for j in range(ssm_w // LANES):
            dst[j, 0] = z[:, o + LANES * j:o + LANES * (j + 1)]
        o += ssm_w
    c_q = z[:, o:o + Q_LORA]
    o += Q_LORA
    c_kv = z[:, o:o + KV_LORA]
    o += KV_LORA
    kr_raw = z[:, o:o + LANES]
    o += LANES
    gm_ref[0] = z[:, o:]

    cos_t, sin_a, sin_b = cos_ref[...], sa_ref[...], sb_ref[...]
    seg = seg_ref[...]

    kr_ms = jnp.dot((kr_raw * kr_raw).astype(BF16), seg[:LANES, :LANES], preferred_element_type=F32)
    kr_blk = _rope_block(kr_raw * lax.rsqrt(kr_ms + EPS) * gkr_ref[...], cos_t, sin_a, sin_b)
    kr_ref[0] = kr_blk[:, :ROPE_DIM]

    q = jnp.dot(_rms(c_q, qln_ref[...]).astype(BF16), w_uq_ref[...], preferred_element_type=F32)
    gq = gq_ref[...]
    for p in range(MLA_HEADS // 2):
        qn = _seg_rms(q[:, 2 * LANES * p:2 * LANES * (p + 1)], seg)
        for j in range(2):
            hd = 2 * p + j
            blk = _rope_block(qn[:, LANES * j:LANES * (j + 1)] * gq, cos_t, sin_a, sin_b)
            q_ref[0, :, LANES * hd:LANES * (hd + 1)] = blk.astype(BF16)

    ckv = _rms(c_kv, kvln_ref[...])
    ckv_ref[0] = ckv
    _expand_kv(ckv, kr_blk, w_ukv_ref, gkn_ref, seg_ref, vones_ref, k_ref, v_ref)


def _expand_kernel(ckv_ref, kr_ref, w_ukv_ref, gkn_ref, seg_ref, vones_ref, place_ref, k_ref, v_ref):
    kr_blk = jnp.dot(kr_ref[0].astype(BF16), place_ref[...], preferred_element_type=F32)
    _expand_kv(ckv_ref[0], kr_blk, w_ukv_ref, gkn_ref, seg_ref, vones_ref, k_ref, v_ref)


def _full(shape):
    n = len(shape)
    return pl.BlockSpec(shape, lambda *_: (0,) * n)


def _proj_call(x, tables, wts, *, tq):
    b, s, d = x.shape
    ssm_w = wts['ssm_w']
    cos_t, sin_a, sin_b = tables
    grid = (s // tq, b)
    row = lambda i, j: (j, i, 0)
    tab = pl.BlockSpec((tq, LANES), lambda i, j: (i, 0))
    in_specs = [pl.BlockSpec((1, tq, d), row), tab, tab, tab,
                _full(wts['norm_in'].shape), _full(wts['w_in'].shape), _full(wts['qln'].shape),
                _full(wts['kvln'].shape), _full(wts['w_uq'].shape), _full(wts['w_ukv'].shape),
                _full(wts['gq'].shape), _full(wts['gkr'].shape), _full(wts['gkn'].shape),
                _full(wts['seg'].shape), _full(wts['vones'].shape)]
    slabs = ssm_w // LANES
    tb_spec = pl.BlockSpec((slabs, 1, tq, LANES), lambda i, j: (0, j, i, 0))
    out_shape = [jax.ShapeDtypeStruct((slabs, b, s, LANES), F32),
                 jax.ShapeDtypeStruct((slabs, b, s, LANES), F32),
                 jax.ShapeDtypeStruct((b, s, ssm_w), F32),
                 jax.ShapeDtypeStruct((b, s, HEAD_W), BF16),
                 jax.ShapeDtypeStruct((b, s, HEAD_W), BF16),
                 jax.ShapeDtypeStruct((b, s, HEAD_W), BF16),
                 jax.ShapeDtypeStruct((b, s, KV_LORA), F32),
                 jax.ShapeDtypeStruct((b, s, ROPE_DIM), F32)]
    out_specs = [tb_spec, tb_spec,
                 pl.BlockSpec((1, tq, ssm_w), row),
                 pl.BlockSpec((1, tq, HEAD_W), row), pl.BlockSpec((1, tq, HEAD_W), row),
                 pl.BlockSpec((1, tq, HEAD_W), row),
                 pl.BlockSpec((1, tq, KV_LORA), row), pl.BlockSpec((1, tq, ROPE_DIM), row)]
    return pl.pallas_call(
        functools.partial(_proj_kernel, ssm_w=ssm_w),
        out_shape=out_shape, grid=grid, in_specs=in_specs, out_specs=out_specs,
        compiler_params=pltpu.CompilerParams(
            dimension_semantics=("arbitrary", "arbitrary"), vmem_limit_bytes=VMEM_LIMIT),
        name="proj",
    )(x, cos_t, sin_a, sin_b, wts['norm_in'], wts['w_in'], wts['qln'], wts['kvln'], wts['w_uq'],
      wts['w_ukv'], wts['gq'], wts['gkr'], wts['gkn'], wts['seg'], wts['vones'])


def _expand_call(ckv, kr, wts, *, tq):
    b, t, _ = ckv.shape
    row = lambda j, i: (j, i, 0)
    return pl.pallas_call(
        _expand_kernel,
        out_shape=[jax.ShapeDtypeStruct((b, t, HEAD_W), BF16)] * 2,
        grid=(b, t // tq),
        in_specs=[pl.BlockSpec((1, tq, KV_LORA), row), pl.BlockSpec((1, tq, ROPE_DIM), row),
                  _full(wts['w_ukv'].shape), _full(wts['gkn'].shape), _full(wts['seg'].shape),
                  _full(wts['vones'].shape), _full(wts['place'].shape)],
        out_specs=[pl.BlockSpec((1, tq, HEAD_W), row)] * 2,
        compiler_params=pltpu.CompilerParams(
            dimension_semantics=("arbitrary", "arbitrary"), vmem_limit_bytes=VMEM_LIMIT),
        name="expand",
    )(ckv, kr, wts['w_ukv'], wts['gkn'], wts['seg'], wts['vones'], wts['place'])


def _ssm_kernel(u_ref, gs_ref, h0re_ref, h0im_ref, are_ref, aim_ref, wb_ref, wc_ref, wf_ref, d_ref,
                w_glu_ref, b_glu_ref, onorm_ref, mix_ref, hre_ref, him_ref, us_ref, xs_ref, y_ref,
                *, batch, steps, unroll):
    @pl.when(pl.program_id(0) == 0)
    def _():
        hre_ref[...] = h0re_ref[...]
        him_ref[...] = h0im_ref[...]

    n_blocks = wb_ref.shape[0]
    r_blk = wb_ref.shape[1] // LANES
    half = wb_ref.shape[2] // 2
    n_k = steps // r_blk
    for b in range(batch):
        for j in range(n_blocks):
            for i in range(r_blk):
                us_ref[j, i, pl.ds(b, n_k, stride=batch), :] = (
                    u_ref[j, b, pl.ds(i, n_k, stride=r_blk), :])

    for gb in range(n_blocks):
        lhs = jnp.concatenate([us_ref[gb, i] for i in range(r_blk)], axis=-1).astype(BF16)
        xs_ref[...] = jnp.dot(lhs, wb_ref[gb], preferred_element_type=F32)
        cols = slice(half * gb, half * (gb + 1))
        a_re = jnp.broadcast_to(are_ref[:, cols], (batch, half))
        a_im = jnp.broadcast_to(aim_ref[:, cols], (batch, half))

        def step(k, carry):
            h_re, h_im = carry
            rows = pl.ds(pl.multiple_of(k * batch, batch), batch)
            n_re = a_re * h_re - a_im * h_im + xs_ref[rows, :half]
            n_im = a_re * h_im + a_im * h_re + xs_ref[rows, half:]
            xs_ref[rows, :half] = h_re
            xs_ref[rows, half:] = h_im
            return n_re, n_im

        h_re, h_im = lax.fori_loop(0, n_k, step, (hre_ref[:, cols], him_ref[:, cols]), unroll=unroll)
        hre_ref[:, cols] = h_re
        him_ref[:, cols] = h_im
        y = (jnp.dot(xs_ref[...].astype(BF16), wc_ref[gb], preferred_element_type=F32)
             + jnp.dot(lhs, wf_ref[gb], preferred_element_type=F32))
        for i in range(r_blk):
            y_ref[i, :, LANES * gb:LANES * (gb + 1)] = y[:, LANES * i:LANES * (i + 1)]

    rows = r_blk * n_k * batch
    u = jnp.concatenate([us_ref[j].reshape(rows, LANES) for j in range(n_blocks)], axis=-1)
    y = y_ref[...].reshape(rows, n_blocks * LANES) + d_ref[...] * u
    yg = jax.nn.gelu(y)
    glu = jnp.dot(yg.astype(BF16), w_glu_ref[...], preferred_element_type=F32) + b_glu_ref[...]
    out = _rms(yg * jax.nn.sigmoid(glu), onorm_ref[...])
    for j in range(n_blocks):
        us_ref[j] = out[:, LANES * j:LANES * (j + 1)].reshape(r_blk, n_k * batch, LANES)
    for b in range(batch):
        for j in range(n_blocks):
            for i in range(r_blk):
                tok = pl.ds(i, n_k, stride=r_blk)
                gs = gs_ref[j, b, tok, :]
                o = us_ref[j, i, pl.ds(b, n_k, stride=batch), :]
                mix_ref[j, b, tok, :] = o * (gs * jax.nn.sigmoid(gs))


def _ssm_call(u, gs, h0re, h0im, wts, *, steps, unroll):
    n_blocks, batch, seq, _ = u.shape
    n_state = h0re.shape[1]
    r_blk = wts['wb'].shape[1] // LANES
    width = wts['wb'].shape[2]
    assert wts['wb'].shape[0] == n_blocks and steps % r_blk == 0
    blk = pl.BlockSpec((n_blocks, batch, steps, LANES), lambda i: (0, 0, i, 0))
    names = ['a_re', 'a_im', 'wb', 'wc', 'wf', 'ssm_d', 'w_glu', 'b_glu', 'onorm_ssm']
    n_rows = steps // r_blk * batch
    return pl.pallas_call(
        functools.partial(_ssm_kernel, batch=batch, steps=steps, unroll=unroll),
        out_shape=[jax.ShapeDtypeStruct(u.shape, F32),
                   jax.ShapeDtypeStruct((batch, n_state), F32),
                   jax.ShapeDtypeStruct((batch, n_state), F32)],
        grid=(seq // steps,),
        in_specs=[blk, blk, _full(h0re.shape), _full(h0im.shape)] + [_full(wts[n].shape) for n in names],
        out_specs=[blk, _full((batch, n_state)), _full((batch, n_state))],
        scratch_shapes=[pltpu.VMEM((n_blocks, r_blk, n_rows, LANES), F32),
                        pltpu.VMEM((n_rows, width), F32),
                        pltpu.VMEM((r_blk, n_rows, n_blocks * LANES), F32)],
        compiler_params=pltpu.CompilerParams(
            dimension_semantics=("arbitrary",), vmem_limit_bytes=VMEM_LIMIT),
        name="ssm",
    )(u, gs, h0re, h0im, *[wts[n] for n in names])


def _attn_kernel(*refs, tq, tk, causal, hg):
    if causal:
        q_ref, kf_ref, vf_ref = refs[:3]
        rest = refs[3:]
    else:
        q_ref, kf_ref, vf_ref, kl_ref, vl_ref = refs[:5]
        rest = refs[5:]
    (gm_ref, ms_ref, x_ref, onorm_ref, w_out_ref, y_ref,
     s_full, ml_scr, s_last, m_scr, acc_scr) = rest
    i = pl.program_id(1)
    if causal:
        n_full = (i * tq) // tk
        def chunk_mask(width):
            qc = lax.broadcasted_iota(jnp.int32, (tq, width), 0) // CHUNK + (width - tq) // CHUNK
            return qc >= lax.broadcasted_iota(jnp.int32, (tq, width), 1) // CHUNK

        odd = i % 2 == 1
        lasts = [(odd, tk, pl.ds(pl.multiple_of(n_full * tk, tk), tk), chunk_mask(tk)),
                 (jnp.logical_not(odd), tq, pl.ds(pl.multiple_of(i * tq, tq), tq), chunk_mask(tq))]
        lasts = [(cond, width, functools.partial(lambda rows, c: kf_ref[0, rows, c], rows),
                  functools.partial(lambda rows, c: vf_ref[0, rows, c], rows), mask)
                 for cond, width, rows, mask in lasts]
    else:
        n_full = kf_ref.shape[1] // tk
        lasts = [(None, kl_ref.shape[1], lambda c: kl_ref[0, :, c], lambda c: vl_ref[0, :, c], None)]
    dn = (((1,), (1,)), ((), ()))
    lane = lax.broadcasted_iota(jnp.int32, (tq, LANES), 1)

    def lane_tiles(s):
        return [s[:, LANES * c:LANES * (c + 1)] for c in range(s.shape[1] // LANES)]

    def probs(s, m_rep):
        if s.shape[1] % LANES == 0:
            p = jnp.concatenate([jnp.exp2(t - m_rep) for t in lane_tiles(s)], axis=-1)
        else:
            p = jnp.exp2(s - m_rep[:, :1])
        return p.astype(BF16)

    pairs = []
    for g in range(MLA_HEADS // hg):
        heads = [(hl, g * hg + hl) for hl in range(hg)]
        cols = {hd: slice(LANES * hd, LANES * (hd + 1)) for _, hd in heads}

        ml_scr[...] = jnp.full(ml_scr.shape, -jnp.inf, F32)

        def a_step(jt, carry, heads=heads, cols=cols):
            rows = pl.ds(pl.multiple_of(jt * tk, tk), tk)
            for hl, hd in heads:
                s = lax.dot_general(q_ref[0, :, cols[hd]], kf_ref[0, rows, cols[hd]], dn,
                                    preferred_element_type=F32)
                s_full[hl, jt] = s
                ml_scr[hl] = functools.reduce(jnp.maximum, lane_tiles(s), ml_scr[hl])
            return carry

        lax.fori_loop(0, n_full, a_step, 0)

        def last_a(width, k_last, mask, heads=heads, cols=cols):
            for hl, hd in heads:
                s = lax.dot_general(q_ref[0, :, cols[hd]], k_last(cols[hd]), dn,
                                    preferred_element_type=F32)
                if mask is not None:
                    s = jnp.where(mask, s, -jnp.inf)
                s_last[hl, :, :width] = s
                if width % LANES == 0:
                    m = jnp.max(functools.reduce(jnp.maximum, lane_tiles(s), ml_scr[hl]),
                                axis=-1, keepdims=True)
                else:
                    m = jnp.maximum(jnp.max(s, axis=-1, keepdims=True),
                                    jnp.max(ml_scr[hl], axis=-1, keepdims=True))
                m_scr[hl] = jnp.broadcast_to(m, (tq, LANES))
                acc_scr[hl] = jnp.zeros((tq, LANES), F32)

        for cond, width, k_last, _, mask in lasts:
            run = functools.partial(last_a, width, k_last, mask)
            run() if cond is None else pl.when(cond)(run)

        def b_step(jt, carry, heads=heads, cols=cols):
            rows = pl.ds(pl.multiple_of(jt * tk, tk), tk)
            for hl, hd in heads:
                p = probs(s_full[hl, jt], m_scr[hl])
                acc_scr[hl] += jnp.dot(p, vf_ref[0, rows, cols[hd]], preferred_element_type=F32)
            return carry

        lax.fori_loop(0, n_full, b_step, 0)

        def last_b(width, v_last, heads=heads, cols=cols):
            for hl, hd in heads:
                p = probs(s_last[hl, :, :width], m_scr[hl])
                acc = acc_scr[hl] + jnp.dot(p, v_last(cols[hd]), preferred_element_type=F32)
                ones_col = V_DIM if hd % 2 == 0 else 0
                acc_scr[hl] = acc / acc[:, ones_col:ones_col + 1]

        for cond, width, _, v_last, _ in lasts:
            run = functools.partial(last_b, width, v_last)
            run() if cond is None else pl.when(cond)(run)

        for e in range(0, hg, 2):
            pairs.append(jnp.where(lane < V_DIM, acc_scr[e], acc_scr[e + 1]))

    attn = jnp.concatenate(pairs, axis=-1)
    gm = gm_ref[0]
    mla = _rms(attn, onorm_ref[...]) * (gm * jax.nn.sigmoid(gm))
    mix_ssm = [ms_ref[j, 0] for j in range(ms_ref.shape[0])]
    mix = jnp.concatenate(mix_ssm + [mla], axis=-1).astype(BF16)
    y_ref[0] = x_ref[0] + jnp.dot(mix, w_out_ref[...], preferred_element_type=F32)


def _attn_call(q, k_full, v_full, last, gm, mix_ssm, x, wts, *, tq, tk, hg):
    b, s, d = x.shape
    t_full = k_full.shape[1]
    ssm_w = gm.shape[2]
    causal = last is None
    row = lambda j, i: (j, i, 0)
    res = lambda j, i: (j, 0, 0)
    if causal:
        assert tk == 2 * tq and t_full % tk == 0 and tq % CHUNK == 0
        last, last_specs, last_len, max_full = (), [], tk, t_full // tk - 1
    else:
        last_len, max_full = last[0].shape[1], t_full // tk
        last_specs = [pl.BlockSpec((1, last_len, HEAD_W), res)] * 2
    scratch = [pltpu.VMEM((hg, max_full, tq, tk), F32), pltpu.VMEM((hg, tq, LANES), F32),
               pltpu.VMEM((hg, tq, last_len), F32), pltpu.VMEM((hg, tq, LANES), F32),
               pltpu.VMEM((hg, tq, LANES), F32)]
    return pl.pallas_call(
        functools.partial(_attn_kernel, tq=tq, tk=tk, causal=causal, hg=hg),
        out_shape=jax.ShapeDtypeStruct((b, s, d), F32),
        grid=(b, s // tq),
        in_specs=[pl.BlockSpec((1, tq, HEAD_W), row),
                  pl.BlockSpec((1, t_full, HEAD_W), res), pl.BlockSpec((1, t_full, HEAD_W), res),
                  *last_specs,
                  pl.BlockSpec((1, tq, ssm_w), row),
                  pl.BlockSpec((mix_ssm.shape[0], 1, tq, LANES), lambda j, i: (0, j, i, 0)),
                  pl.BlockSpec((1, tq, d), row),
                  _full(wts['onorm_mla'].shape), _full(wts['w_out'].shape)],
        out_specs=pl.BlockSpec((1, tq, d), row),
        scratch_shapes=scratch,
        compiler_params=pltpu.CompilerParams(
            dimension_semantics=("arbitrary", "arbitrary"), vmem_limit_bytes=VMEM_LIMIT),
        name="attn",
    )(q, k_full, v_full, *last, gm, mix_ssm, x, wts['onorm_mla'], wts['w_out'])


def _head_block_cols(w, pieces):
    k = w.shape[0]
    w3 = w.reshape(k, MLA_HEADS, w.shape[1] // MLA_HEADS)
    cols = [w3[:, :, p[0]:p[0] + p[1]] if isinstance(p, tuple) else jnp.zeros((k, MLA_HEADS, p), w.dtype)
            for p in pieces]
    return jnp.concatenate(cols, axis=-1).reshape(k, HEAD_W)


def _disc_kernel(are_ref, aim_ref, ldt_ref, bt_ref, c_ref, apr_ref, api_ref, wb_ref, wc_ref, wf_ref):
    r_blk = wb_ref.shape[1]
    lr, li = are_ref[...], aim_ref[...]
    dt = jnp.exp(ldt_ref[...])
    mag = jnp.exp(lr * dt)
    ar, ai = mag * jnp.cos(li * dt), mag * jnp.sin(li * dt)
    den = lr * lr + li * li
    kr = ((ar - 1.0) * lr + ai * li) / den
    ki = (ai * lr - (ar - 1.0) * li) / den
    pw = [(jnp.ones_like(ar), jnp.zeros_like(ai))]
    for _ in range(r_blk):
        pr, pi = pw[-1]
        pw.append((pr * ar - pi * ai, pr * ai + pi * ar))
    apr_ref[...], api_ref[...] = pw[r_blk]
    nt = (((1,), (1,)), ((), ()))
    f_zero = jnp.zeros(wf_ref.shape[3:], F32)
    for g in range(bt_ref.shape[1]):
        row = lambda v: v[g:g + 1, :]
        b_r = row(kr) * bt_ref[0, g] - row(ki) * bt_ref[1, g]
        b_i = row(kr) * bt_ref[1, g] + row(ki) * bt_ref[0, g]
        c_r, c_i = c_ref[0, g], c_ref[1, g]
        for i in range(r_blk):
            pr, pi = pw[r_blk - 1 - i]
            wb_ref[0, i, g] = row(pr) * b_r - row(pi) * b_i
            wb_ref[1, i, g] = row(pr) * b_i + row(pi) * b_r
        for m in range(r_blk + 1):
            pr, pi = pw[m]
            ca_r = c_r * row(pr) - c_i * row(pi)
            ca_i = c_r * row(pi) + c_i * row(pr)
            if m >= 1:
                wc_ref[0, m - 1, g] = ca_r
                wc_ref[1, m - 1, g] = -ca_i
            if m < r_blk:
                f_t = (lax.dot_general(b_r, ca_r, nt, precision=lax.Precision.HIGHEST,
                                       preferred_element_type=F32)
                       - lax.dot_general(b_i, ca_i, nt, precision=lax.Precision.HIGHEST,
                                         preferred_element_type=F32))
                for i in range(r_blk - m):
                    wf_ref[i, i + m, g] = f_t
        for i in range(r_blk):
            for j in range(i):
                wf_ref[i, j, g] = f_zero


def _prepare_weights(norm_in, w_in, ssm_a_re, ssm_a_im, ssm_log_dt, ssm_b_re, ssm_b_im, ssm_c_re,
                     ssm_c_im, ssm_d, w_glu, b_glu, q_lora_norm, kv_lora_norm, w_uq, w_ukv,
                     q_nope_norm, k_nope_norm, q_rope_norm, k_rope_norm, out_norm_ssm,
                     out_norm_mla, w_out):
    groups, n_state = ssm_a_re.shape
    ssm_w = groups * SSM_GROUP
    row = lambda v: v.reshape(1, -1).astype(F32)
    tail = LANES - ROPE_DIM - NOPE_DIM

    o_kr = 2 * ssm_w + Q_LORA + KV_LORA
    w_in_p = jnp.concatenate(
        [w_in[:, :o_kr + ROPE_DIM], jnp.zeros((w_in.shape[0], LANES - ROPE_DIM), w_in.dtype),
         w_in[:, o_kr + ROPE_DIM:]], axis=1).astype(BF16)

    w_uq_p = _head_block_cols(w_uq, [(NOPE_DIM, ROPE_DIM), (0, NOPE_DIM), tail]).astype(BF16)
    w_uk_p = _head_block_cols(w_ukv, [ROPE_DIM, (0, NOPE_DIM), tail])
    w_v_lo = _head_block_cols(w_ukv, [(NOPE_DIM, V_DIM), LANES - V_DIM])
    w_v_hi = _head_block_cols(w_ukv, [LANES - V_DIM, (NOPE_DIM, V_DIM)])
    odd_head = (np.arange(HEAD_W) // LANES) % 2 == 1
    w_ukv_p = jnp.concatenate([w_uk_p, jnp.where(odd_head[None, :], w_v_hi, w_v_lo)], axis=1).astype(BF16)

    scale = (NOPE_DIM + ROPE_DIM) ** -0.5 * np.log2(np.e)
    vones = np.zeros((MLA_HEADS, LANES), np.float32)
    vones[0::2, V_DIM] = 1.0
    vones[1::2, 0] = 1.0
    vones = jnp.asarray(vones.reshape(1, HEAD_W))
    zeros = lambda n: jnp.zeros((n,), F32)
    gq =jnp.concatenate([q_rope_norm, q_nope_norm, zeros(tail)]) * scale
    gkr = jnp.concatenate([k_rope_norm, zeros(LANES - ROPE_DIM)])
    gkn = jnp.concatenate([zeros(ROPE_DIM), k_nope_norm, zeros(tail)])

    seg = np.zeros((LANES, LANES), np.float32)
    seg[:ROPE_DIM, :ROPE_DIM] = 1.0 / ROPE_DIM
    seg[ROPE_DIM:ROPE_DIM + NOPE_DIM, ROPE_DIM:ROPE_DIM + NOPE_DIM] = 1.0 / NOPE_DIM
    seg[ROPE_DIM + NOPE_DIM:, ROPE_DIM + NOPE_DIM:] = 1.0 / tail
    seg = jnp.asarray(np.kron(np.eye(2, dtype=np.float32), seg), BF16)
    place = jnp.asarray(np.eye(ROPE_DIM, LANES, dtype=np.float32), BF16)

    r_blk = SSM_STEPS_PER_SCAN
    gp = jax.ShapeDtypeStruct((groups, n_state), F32)
    w_shape = jax.ShapeDtypeStruct((2, r_blk, groups, SSM_GROUP, n_state), F32)
    f_shape = jax.ShapeDtypeStruct((r_blk, r_blk, groups, SSM_GROUP, SSM_GROUP), F32)
    a_re, a_im, wb5, wc5, wf5 = pl.pallas_call(
        _disc_kernel, out_shape=[gp, gp, w_shape, w_shape, f_shape], name="disc")(
        ssm_a_re.astype(F32), ssm_a_im.astype(F32), ssm_log_dt.astype(F32).reshape(groups, 1),
        jnp.swapaxes(jnp.stack([ssm_b_re, ssm_b_im]), 2, 3).astype(F32),
        jnp.stack([ssm_c_re, ssm_c_im]).astype(F32))
    n_blocks, gpb = groups // GROUPS_PER_BLOCK, GROUPS_PER_BLOCK
    eye = np.eye(gpb, dtype=np.float32)
    wb = (wb5.reshape(2, r_blk, n_blocks, gpb, SSM_GROUP, 1, n_state)
          * eye[:, None, :, None]).transpose(2, 1, 3, 4, 0, 5, 6)
    wb = wb.reshape(n_blocks, r_blk * LANES, 2 * gpb * n_state).astype(BF16)
    wc = (wc5.reshape(2, r_blk, n_blocks, gpb, 1, SSM_GROUP, n_state)
          * eye[:, :, None, None]).transpose(2, 0, 3, 6, 1, 4, 5)
    wc = wc.reshape(n_blocks, 2 * gpb * n_state, r_blk * LANES).astype(BF16)
    wf = (wf5.reshape(r_blk, r_blk, n_blocks, gpb, 1, SSM_GROUP, SSM_GROUP)
          * eye[:, :, None, None]).transpose(2, 0, 3, 5, 1, 4, 6)
    wf = wf.reshape(n_blocks, r_blk * LANES, r_blk * LANES).astype(BF16)

    return dict(
        ssm_w=ssm_w,
        norm_in=row(norm_in), w_in=w_in_p, qln=row(q_lora_norm), kvln=row(kv_lora_norm),
        w_uq=w_uq_p, w_ukv=w_ukv_p, gq=row(gq), gkr=row(gkr), gkn=row(gkn), seg=seg, vones=vones,
        place=place, a_re=row(a_re), a_im=row(a_im), wb=wb, wc=wc, wf=wf,
        ssm_d=row(ssm_d), w_glu=w_glu.astype(BF16), b_glu=row(b_glu), onorm_ssm=row(out_norm_ssm),
        onorm_mla=row(out_norm_mla), w_out=w_out.astype(BF16))


def _rope_tables(start, n):
    half = ROPE_DIM // 2
    inv = ROPE_THETA ** (-jnp.arange(half, dtype=F32) / half)
    ang = (start + jnp.arange(n)).astype(F32)[:, None] * inv[None, :]
    cos, sin = jnp.cos(ang), jnp.sin(ang)
    cos_t = jnp.concatenate([cos, cos, jnp.ones((n, LANES - ROPE_DIM), F32)], axis=1)
    sin_a = jnp.concatenate([-sin, jnp.zeros((n, LANES - half), F32)], axis=1)
    sin_b = jnp.concatenate([jnp.zeros((n, half), F32), sin, jnp.zeros((n, LANES - ROPE_DIM), F32)],
                            axis=1)
    return cos_t, sin_a, sin_b


def _mixer(x, pos0, h0re, h0im, past, wts, *, proj_tq, ssm_steps, attn_tk):
    u, gs, gm, q, k, v, ckv, kr = _proj_call(x, _rope_tables(pos0, x.shape[1]), wts, tq=proj_tq)
    mix_ssm, hre, him = _ssm_call(u, gs, h0re, h0im, wts, steps=ssm_steps, unroll=True)
    if past is None:
        y = _attn_call(q, k, v, None, gm, mix_ssm, x, wts, tq=attn_tk // 2, tk=attn_tk, hg=MLA_HEADS)
    else:
        k_past, v_past = _expand_call(*past, wts, tq=512)
        y = _attn_call(q, k_past, v_past, (k, v), gm, mix_ssm, x, wts, tq=x.shape[1], tk=attn_tk,
                       hg=MLA_HEADS)
    return y, ckv, kr, hre, him


def kernel(x_prompt, x_sample, cache_ckv, cache_krope, state_ssm_re, state_ssm_im, norm_in, w_in, ssm_a_re, ssm_a_im, ssm_log_dt, ssm_b_re, ssm_b_im, ssm_c_re, ssm_c_im, ssm_d, w_glu, b_glu, q_lora_norm, kv_lora_norm, w_uq, w_ukv, q_nope_norm, k_nope_norm, q_rope_norm, k_rope_norm, out_norm_ssm, out_norm_mla, w_out):
    depth = norm_in.shape[0]
    assert depth == 1, "single mixer layer"
    params = (norm_in, w_in, ssm_a_re, ssm_a_im, ssm_log_dt, ssm_b_re, ssm_b_im, ssm_c_re, ssm_c_im,
              ssm_d, w_glu, b_glu, q_lora_norm, kv_lora_norm, w_uq, w_ukv, q_nope_norm, k_nope_norm,
              q_rope_norm, k_rope_norm, out_norm_ssm, out_norm_mla, w_out)
    drop_depth = lambda a: a.reshape(a.shape[1:])
    wts = _prepare_weights(*[drop_depth(p) for p in params])
    groups, n_state = ssm_a_re.shape[1:]
    bp, sp, _ = x_prompt.shape
    bs, ss, _ = x_sample.shape
    past_len = cache_ckv.shape[2]

    zero_state = jnp.zeros((bp, groups * n_state), F32)
    yp, ckv_p, kr_p, re_p, im_p = _mixer(
        x_prompt, 0, zero_state, zero_state, None, wts,
        proj_tq=512, ssm_steps=64, attn_tk=512)
    ys, ckv_s, kr_s, re_s, im_s = _mixer(
        x_sample, past_len,
        state_ssm_re.reshape(bs, groups * n_state), state_ssm_im.reshape(bs, groups * n_state),
        (drop_depth(cache_ckv), drop_depth(cache_krope)), wts,
        proj_tq=ss, ssm_steps=ss, attn_tk=past_len)

    st = lambda h, bb: h.reshape(1, bb, groups, n_state)
    return (yp, ys, ckv_p[None], kr_p[None], st(re_p, bp), st(im_p, bp),
            ckv_s[None], kr_s[None], st(re_s, bs), st(im_s, bs))
```

```python
import functools

import numpy as np
import jax
import jax.numpy as jnp
from jax import lax
from jax.experimental import pallas as pl
from jax.experimental.pallas import tpu as pltpu

F32 = jnp.float32
BF16 = jnp.bfloat16

CHUNK = 64
SSM_GROUP = 16
SSM_STATE = 64
MLA_HEADS = 8
NOPE_DIM = 64
ROPE_DIM = 32
V_DIM = 64
Q_LORA = 256
KV_LORA = 128
ROPE_THETA = 10000.0
EPS = 1e-6

LANES = 128
HEAD_W = MLA_HEADS * LANES
GROUPS_PER_BLOCK = 8
SSM_STEPS_PER_SCAN = 4
VMEM_LIMIT = 56 * 1024 * 1024


def _rms(x, gain):
    return x * lax.rsqrt(jnp.mean(x * x, axis=-1, keepdims=True) + EPS) * gain


def _seg_rms(x, seg):
    ms = jnp.dot((x * x).astype(BF16), seg, preferred_element_type=F32)
    return x * lax.rsqrt(ms + EPS)


def _rope_block(x, cos_t, sin_a, sin_b):
    return (x * cos_t + pltpu.roll(x, LANES - ROPE_DIM // 2, 1) * sin_a
            + pltpu.roll(x, ROPE_DIM // 2, 1) * sin_b)


def _expand_kv(ckv, kr_blk, w_ukv_ref, gkn_ref, seg_ref, vones_ref, k_ref, v_ref):
    kv = jnp.dot(ckv.astype(BF16), w_ukv_ref[...], preferred_element_type=F32)
    gkn = gkn_ref[...]
    for p in range(MLA_HEADS // 2):
        kn = _seg_rms(kv[:, 2 * LANES * p:2 * LANES * (p + 1)], seg_ref[...])
        for j in range(2):
            h = 2 * p + j
            blk = kn[:, LANES * j:LANES * (j + 1)] * gkn + kr_blk
            k_ref[0, :, LANES * h:LANES * (h + 1)] = blk.astype(BF16)
    v_ref[0] = (kv[:, HEAD_W:] + vones_ref[...]).astype(BF16)


def _proj_kernel(x_ref, cos_ref, sa_ref, sb_ref, norm_in_ref, w_in_ref, qln_ref, kvln_ref,
                 w_uq_ref, w_ukv_ref, gq_ref, gkr_ref, gkn_ref, seg_ref, vones_ref,
                 u_ref, gs_ref, gm_ref, q_ref, k_ref, v_ref, ckv_ref, kr_ref, *, ssm_w):
    x = x_ref[0]
    h = _rms(x, norm_in_ref[...])
    z = jnp.dot(h.astype(BF16), w_in_ref[...], preferred_element_type=F32)
    o = 0
    for dst in (u_ref, gs_ref):
        for j in range(ssm_w // LANES):
            dst[j, 0] = z[:, o + LANES * j:o + LANES * (j + 1)]
        o += ssm_w
    c_q = z[:, o:o + Q_LORA]
    o += Q_LORA
    c_kv = z[:, o:o + KV_LORA]
    o += KV_LORA
    kr_raw = z[:, o:o + LANES]
    o += LANES
    gm_ref[0] = z[:, o:]

    cos_t, sin_a, sin_b = cos_ref[...], sa_ref[...], sb_ref[...]
    seg = seg_ref[...]

    kr_ms = jnp.dot((kr_raw * kr_raw).astype(BF16), seg[:LANES, :LANES], preferred_element_type=F32)
    kr_blk = _rope_block(kr_raw * lax.rsqrt(kr_ms + EPS) * gkr_ref[...], cos_t, sin_a, sin_b)
    kr_ref[0] = kr_blk.T[:ROPE_DIM]

    q = jnp.dot(_rms(c_q, qln_ref[...]).astype(BF16), w_uq_ref[...], preferred_element_type=F32)
    gq = gq_ref[...]
    for p in range(MLA_HEADS // 2):
        qn = _seg_rms(q[:, 2 * LANES * p:2 * LANES * (p + 1)], seg)
        for j in range(2):
            hd = 2 * p + j
            blk = _rope_block(qn[:, LANES * j:LANES * (j + 1)] * gq, cos_t, sin_a, sin_b)
            q_ref[0, :, LANES * hd:LANES * (hd + 1)] = blk.astype(BF16)

    ckv = _rms(c_kv, kvln_ref[...])
    ckv_ref[0] = ckv
    _expand_kv(ckv, kr_blk, w_ukv_ref, gkn_ref, seg_ref, vones_ref, k_ref, v_ref)


def _expand_kernel(ckv_ref, kr_ref, w_ukv_ref, gkn_ref, seg_ref, vones_ref, place_ref, k_ref, v_ref):
    kr_blk = lax.dot_general(kr_ref[0].astype(BF16), place_ref[...], (((0,), (0,)), ((), ())),
                             preferred_element_type=F32)
    _expand_kv(ckv_ref[0], kr_blk, w_ukv_ref, gkn_ref, seg_ref, vones_ref, k_ref, v_ref)


def _full(shape):
    n = len(shape)
    return pl.BlockSpec(shape, lambda *_: (0,) * n)


def _proj_call(x, tables, wts, *, tq):
    b, s, d = x.shape
    ssm_w = wts['ssm_w']
    cos_t, sin_a, sin_b = tables
    grid = (s // tq, b)
    row = lambda i, j: (j, i, 0)
    tab = pl.BlockSpec((tq, LANES), lambda i, j: (i, 0))
    in_specs = [pl.BlockSpec((1, tq, d), row), tab, tab, tab,
                _full(wts['norm_in'].shape), _full(wts['w_in'].shape), _full(wts['qln'].shape),
                _full(wts['kvln'].shape), _full(wts['w_uq'].shape), _full(wts['w_ukv'].shape),
                _full(wts['gq'].shape), _full(wts['gkr'].shape), _full(wts['gkn'].shape),
                _full(wts['seg'].shape), _full(wts['vones'].shape)]
    slabs = ssm_w // LANES
    tb_spec = pl.BlockSpec((slabs, 1, tq, LANES), lambda i, j: (0, j, i, 0))
    out_shape = [jax.ShapeDtypeStruct((slabs, b, s, LANES), F32),
                 jax.ShapeDtypeStruct((slabs, b, s, LANES), F32),
                 jax.ShapeDtypeStruct((b, s, ssm_w), F32),
                 jax.ShapeDtypeStruct((b, s, HEAD_W), BF16),
                 jax.ShapeDtypeStruct((b, s, HEAD_W), BF16),
                 jax.ShapeDtypeStruct((b, s, HEAD_W), BF16),
                 jax.ShapeDtypeStruct((b, s, KV_LORA), F32),
                 jax.ShapeDtypeStruct((b, ROPE_DIM, s), F32)]
    out_specs = [tb_spec, tb_spec,
                 pl.BlockSpec((1, tq, ssm_w), row),
                 pl.BlockSpec((1, tq, HEAD_W), row), pl.BlockSpec((1, tq, HEAD_W), row),
                 pl.BlockSpec((1, tq, HEAD_W), row),
                 pl.BlockSpec((1, tq, KV_LORA), row),
                 pl.BlockSpec((1, ROPE_DIM, tq), lambda i, j: (j, 0, i))]
    return pl.pallas_call(
        functools.partial(_proj_kernel, ssm_w=ssm_w),
        out_shape=out_shape, grid=grid, in_specs=in_specs, out_specs=out_specs,
        compiler_params=pltpu.CompilerParams(
            dimension_semantics=("arbitrary", "arbitrary"), vmem_limit_bytes=VMEM_LIMIT),
        name="proj",
    )(x, cos_t, sin_a, sin_b, wts['norm_in'], wts['w_in'], wts['qln'], wts['kvln'], wts['w_uq'],
      wts['w_ukv'], wts['gq'], wts['gkr'], wts['gkn'], wts['seg'], wts['vones'])


def _expand_call(ckv, kr, wts, *, tq):
    b, t, _ = ckv.shape
    row = lambda j, i: (j, i, 0)
    return pl.pallas_call(
        _expand_kernel,
        out_shape=[jax.ShapeDtypeStruct((b, t, HEAD_W), BF16)] * 2,
        grid=(b, t // tq),
        in_specs=[pl.BlockSpec((1, tq, KV_LORA), row),
                  pl.BlockSpec((1, ROPE_DIM, tq), lambda j, i: (j, 0, i)),
                  _full(wts['w_ukv'].shape), _full(wts['gkn'].shape), _full(wts['seg'].shape),
                  _full(wts['vones'].shape), _full(wts['place'].shape)],
        out_specs=[pl.BlockSpec((1, tq, HEAD_W), row)] * 2,
        compiler_params=pltpu.CompilerParams(
            dimension_semantics=("arbitrary", "arbitrary"), vmem_limit_bytes=VMEM_LIMIT),
        name="expand",
    )(ckv, kr, wts['w_ukv'], wts['gkn'], wts['seg'], wts['vones'], wts['place'])


def _ssm_kernel(u_ref, gs_ref, h0re_ref, h0im_ref, are_ref, aim_ref, wb5_ref, wc5_ref, wf5_ref, d_ref,
                w_glu_ref, b_glu_ref, onorm_ref, mix_ref, hre_ref, him_ref, us_ref, xs_ref, y_ref,
                wb_ref, wc_ref, wf_ref, *, batch, steps, unroll):
    n_blocks, r_blk = wb_ref.shape[0], wb_ref.shape[1] // LANES
    half = wb_ref.shape[2] // 2
    gpb = wb5_ref.shape[2] // n_blocks
    n_k = steps // r_blk

    @pl.when(pl.program_id(0) == 0)
    def _():
        hre_ref[...] = h0re_ref[...]
        him_ref[...] = h0im_ref[...]

        def own_group(shape, row_div, lane_div):
            return (lax.broadcasted_iota(jnp.int32, shape, 0) // row_div
                    == lax.broadcasted_iota(jnp.int32, shape, 1) // lane_div)

        cin, n_state = wb5_ref.shape[3], half // gpb
        m_b = own_group((LANES, half), cin, n_state)
        m_c = own_group((half, LANES), n_state, cin)
        m_f = own_group((LANES, LANES), cin, cin)
        for gb in range(n_blocks):
            grp = slice(gpb * gb, gpb * (gb + 1))
            for ri in range(2):
                for i in range(r_blk):
                    piece = wb5_ref[ri, i, grp].reshape(LANES, LANES)
                    wide = jnp.concatenate([piece] * (half // LANES), axis=1)
                    wb_ref[gb, LANES * i:LANES * (i + 1), half * ri:half * (ri + 1)] = (
                        jnp.where(m_b, wide, 0.0).astype(BF16))
                    piece = wc5_ref[ri, i, grp].reshape(half, LANES)
                    wc_ref[gb, half * ri:half * (ri + 1), LANES * i:LANES * (i + 1)] = (
                        jnp.where(m_c, piece, 0.0).astype(BF16))
            for i in range(r_blk):
                for j in range(r_blk):
                    piece = wf5_ref[i, j, grp].reshape(LANES, LANES)
                    wf_ref[gb, LANES * i:LANES * (i + 1), LANES * j:LANES * (j + 1)] = (
                        jnp.where(m_f, piece, 0.0).astype(BF16))

    for b in range(batch):
        for j in range(n_blocks):
            for i in range(r_blk):
                us_ref[j, i, pl.ds(b, n_k, stride=batch), :] = (
                    u_ref[j, b, pl.ds(i, n_k, stride=r_blk), :])

    for gb in range(n_blocks):
        lhs = jnp.concatenate([us_ref[gb, i] for i in range(r_blk)], axis=-1).astype(BF16)
        xs_ref[...] = jnp.dot(lhs, wb_ref[gb], preferred_element_type=F32)
        cols = slice(half * gb, half * (gb + 1))
        a_re = jnp.broadcast_to(are_ref[:, cols], (batch, half))
        a_im = jnp.broadcast_to(aim_ref[:, cols], (batch, half))

        def step(k, carry):
            h_re, h_im = carry
            rows = pl.ds(pl.multiple_of(k * batch, batch), batch)
            n_re = a_re * h_re - a_im * h_im + xs_ref[rows, :half]
            n_im = a_re * h_im + a_im * h_re + xs_ref[rows, half:]
            xs_ref[rows, :half] = h_re
            xs_ref[rows, half:] = h_im
            return n_re, n_im

        h_re, h_im = lax.fori_loop(0, n_k, step, (hre_ref[:, cols], him_ref[:, cols]), unroll=unroll)
        hre_ref[:, cols] = h_re
        him_ref[:, cols] = h_im
        y = (jnp.dot(xs_ref[...].astype(BF16), wc_ref[gb], preferred_element_type=F32)
             + jnp.dot(lhs, wf_ref[gb], preferred_element_type=F32))
        for i in range(r_blk):
            y_ref[i, :, LANES * gb:LANES * (gb + 1)] = y[:, LANES * i:LANES * (i + 1)]

    rows = r_blk * n_k * batch
    u = jnp.concatenate([us_ref[j].reshape(rows, LANES) for j in range(n_blocks)], axis=-1)
    y = y_ref[...].reshape(rows, n_blocks * LANES) + d_ref[...] * u
    yg = jax.nn.gelu(y)
    glu = jnp.dot(yg.astype(BF16), w_glu_ref[...], preferred_element_type=F32) + b_glu_ref[...]
    out = _rms(yg * jax.nn.sigmoid(glu), onorm_ref[...])
    for j in range(n_blocks):
        us_ref[j] = out[:, LANES * j:LANES * (j + 1)].reshape(r_blk, n_k * batch, LANES)
    for b in range(batch):
        for j in range(n_blocks):
            for i in range(r_blk):
                tok = pl.ds(i, n_k, stride=r_blk)
                gs = gs_ref[j, b, tok, :]
                o = us_ref[j, i, pl.ds(b, n_k, stride=batch), :]
                mix_ref[j, b, tok, :] = o * (gs * jax.nn.sigmoid(gs))


def _ssm_call(u, gs, h0re, h0im, wts, *, steps, unroll):
    n_blocks, batch, seq, _ = u.shape
    n_state = h0re.shape[1]
    r_blk = wts['wb5'].shape[1]
    width = 2 * n_state // n_blocks
    assert steps % r_blk == 0 and wts['wb5'].shape[3] * GROUPS_PER_BLOCK == LANES
    blk = pl.BlockSpec((n_blocks, batch, steps, LANES), lambda i: (0, 0, i, 0))
    names = ['a_re', 'a_im', 'wb5', 'wc5', 'wf5', 'ssm_d', 'w_glu', 'b_glu', 'onorm_ssm']
    n_rows = steps // r_blk * batch
    return pl.pallas_call(
        functools.partial(_ssm_kernel, batch=batch, steps=steps, unroll=unroll),
        out_shape=[jax.ShapeDtypeStruct(u.shape, F32),
                   jax.ShapeDtypeStruct((batch, n_state), F32),
                   jax.ShapeDtypeStruct((batch, n_state), F32)],
        grid=(seq // steps,),
        in_specs=[blk, blk, _full(h0re.shape), _full(h0im.shape)] + [_full(wts[n].shape) for n in names],
        out_specs=[blk, _full((batch, n_state)), _full((batch, n_state))],
        scratch_shapes=[pltpu.VMEM((n_blocks, r_blk, n_rows, LANES), F32),
                        pltpu.VMEM((n_rows, width), F32),
                        pltpu.VMEM((r_blk, n_rows, n_blocks * LANES), F32),
                        pltpu.VMEM((n_blocks, r_blk * LANES, width), BF16),
                        pltpu.VMEM((n_blocks, width, r_blk * LANES), BF16),
                        pltpu.VMEM((n_blocks, r_blk * LANES, r_blk * LANES), BF16)],
        compiler_params=pltpu.CompilerParams(
            dimension_semantics=("arbitrary",), vmem_limit_bytes=VMEM_LIMIT),
        name="ssm",
    )(u, gs, h0re, h0im, *[wts[n] for n in names])


def _attn_kernel(*refs, tq, tk, causal, hg):
    if causal:
        q_ref, kf_ref, vf_ref = refs[:3]
        rest = refs[3:]
    else:
        q_ref, kf_ref, vf_ref, kl_ref, vl_ref = refs[:5]
        rest = refs[5:]
    (gm_ref, ms_ref, x_ref, onorm_ref, w_out_ref, y_ref,
     s_full, ml_scr, s_last, m_scr, acc_scr) = rest
    i = pl.program_id(1)
    if causal:
        n_full = (i * tq) // tk
        def chunk_mask(width):
            qc = lax.broadcasted_iota(jnp.int32, (tq, width), 0) // CHUNK + (width - tq) // CHUNK
            return qc >= lax.broadcasted_iota(jnp.int32, (tq, width), 1) // CHUNK

        odd = i % 2 == 1
        lasts = [(odd, tk, pl.ds(pl.multiple_of(n_full * tk, tk), tk), chunk_mask(tk)),
                 (jnp.logical_not(odd), tq, pl.ds(pl.multiple_of(i * tq, tq), tq), chunk_mask(tq))]
        lasts = [(cond, width, functools.partial(lambda rows, c: kf_ref[0, rows, c], rows),
                  functools.partial(lambda rows, c: vf_ref[0, rows, c], rows), mask)
                 for cond, width, rows, mask in lasts]
    else:
        n_full = kf_ref.shape[1] // tk
        lasts = [(None, kl_ref.shape[1], lambda c: kl_ref[0, :, c], lambda c: vl_ref[0, :, c], None)]
    dn = (((1,), (1,)), ((), ()))
    lane = lax.broadcasted_iota(jnp.int32, (tq, LANES), 1)

    def lane_tiles(s):
        return [s[:, LANES * c:LANES * (c + 1)] for c in range(s.shape[1] // LANES)]

    def probs(s, m_rep):
        if s.shape[1] % LANES == 0:
            p = jnp.concatenate([jnp.exp2(t - m_rep) for t in lane_tiles(s)], axis=-1)
        else:
            p = jnp.exp2(s - m_rep[:, :1])
        return p.astype(BF16)

    pairs = []
    for g in range(MLA_HEADS // hg):
        heads = [(hl, g * hg + hl) for hl in range(hg)]
        cols = {hd: slice(LANES * hd, LANES * (hd + 1)) for _, hd in heads}

        ml_scr[...] = jnp.full(ml_scr.shape, -jnp.inf, F32)

        def a_step(jt, carry, heads=heads, cols=cols):
            rows = pl.ds(pl.multiple_of(jt * tk, tk), tk)
            for hl, hd in heads:
                s = lax.dot_general(q_ref[0, :, cols[hd]], kf_ref[0, rows, cols[hd]], dn,
                                    preferred_element_type=F32)
                s_full[hl, jt] = s
                ml_scr[hl] = functools.reduce(jnp.maximum, lane_tiles(s), ml_scr[hl])
            return carry

        lax.fori_loop(0, n_full, a_step, 0)

        def last_a(width, k_last, mask, heads=heads, cols=cols):
            for hl, hd in heads:
                s = lax.dot_general(q_ref[0, :, cols[hd]], k_last(cols[hd]), dn,
                                    preferred_element_type=F32)
                if mask is not None:
                    s = jnp.where(mask, s, -jnp.inf)
                s_last[hl, :, :width] = s
                if width % LANES == 0:
                    m = jnp.max(functools.reduce(jnp.maximum, lane_tiles(s), ml_scr[hl]),
                                axis=-1, keepdims=True)
                else:
                    m = jnp.maximum(jnp.max(s, axis=-1, keepdims=True),
                                    jnp.max(ml_scr[hl], axis=-1, keepdims=True))
                m_scr[hl] = jnp.broadcast_to(m, (tq, LANES))
                acc_scr[hl] = jnp.zeros((tq, LANES), F32)

        for cond, width, k_last, _, mask in lasts:
            run = functools.partial(last_a, width, k_last, mask)
            run() if cond is None else pl.when(cond)(run)

        def b_step(jt, carry, heads=heads, cols=cols):
            rows = pl.ds(pl.multiple_of(jt * tk, tk), tk)
            for hl, hd in heads:
                p = probs(s_full[hl, jt], m_scr[hl])
                acc_scr[hl] += jnp.dot(p, vf_ref[0, rows, cols[hd]], preferred_element_type=F32)
            return carry

        lax.fori_loop(0, n_full, b_step, 0)

        def last_b(width, v_last, heads=heads, cols=cols):
            for hl, hd in heads:
                p = probs(s_last[hl, :, :width], m_scr[hl])
                acc = acc_scr[hl] + jnp.dot(p, v_last(cols[hd]), preferred_element_type=F32)
                ones_col = V_DIM if hd % 2 == 0 else 0
                acc_scr[hl] = acc / acc[:, ones_col:ones_col + 1]

        for cond, width, _, v_last, _ in lasts:
            run = functools.partial(last_b, width, v_last)
            run() if cond is None else pl.when(cond)(run)

        for e in range(0, hg, 2):
            pairs.append(jnp.where(lane < V_DIM, acc_scr[e], acc_scr[e + 1]))

    attn = jnp.concatenate(pairs, axis=-1)
    gm = gm_ref[0]
    mla = _rms(attn, onorm_ref[...]) * (gm * jax.nn.sigmoid(gm))
    mix_ssm = [ms_ref[j, 0] for j in range(ms_ref.shape[0])]
    mix = jnp.concatenate(mix_ssm + [mla], axis=-1).astype(BF16)
    y_ref[0] = x_ref[0] + jnp.dot(mix, w_out_ref[...], preferred_element_type=F32)


def _attn_call(q, k_full, v_full, last, gm, mix_ssm, x, wts, *, tq, tk, hg):
    b, s, d = x.shape
    t_full = k_full.shape[1]
    ssm_w = gm.shape[2]
    causal = last is None
    row = lambda j, i: (j, i, 0)
    res = lambda j, i: (j, 0, 0)
    if causal:
        assert tk == 2 * tq and t_full % tk == 0 and tq % CHUNK == 0
        last, last_specs, last_len, max_full = (), [], tk, t_full // tk - 1
    else:
        last_len, max_full = last[0].shape[1], t_full // tk
        last_specs = [pl.BlockSpec((1, last_len, HEAD_W), res)] * 2
    scratch = [pltpu.VMEM((hg, max_full, tq, tk), F32), pltpu.VMEM((hg, tq, LANES), F32),
               pltpu.VMEM((hg, tq, last_len), F32), pltpu.VMEM((hg, tq, LANES), F32),
               pltpu.VMEM((hg, tq, LANES), F32)]
    return pl.pallas_call(
        functools.partial(_attn_kernel, tq=tq, tk=tk, causal=causal, hg=hg),
        out_shape=jax.ShapeDtypeStruct((b, s, d), F32),
        grid=(b, s // tq),
        in_specs=[pl.BlockSpec((1, tq, HEAD_W), row),
                  pl.BlockSpec((1, t_full, HEAD_W), res), pl.BlockSpec((1, t_full, HEAD_W), res),
                  *last_specs,
                  pl.BlockSpec((1, tq, ssm_w), row),
                  pl.BlockSpec((mix_ssm.shape[0], 1, tq, LANES), lambda j, i: (0, j, i, 0)),
                  pl.BlockSpec((1, tq, d), row),
                  _full(wts['onorm_mla'].shape), _full(wts['w_out'].shape)],
        out_specs=pl.BlockSpec((1, tq, d), row),
        scratch_shapes=scratch,
        compiler_params=pltpu.CompilerParams(
            dimension_semantics=("arbitrary", "arbitrary"), vmem_limit_bytes=VMEM_LIMIT),
        name="attn",
    )(q, k_full, v_full, *last, gm, mix_ssm, x, wts['onorm_mla'], wts['w_out'])


def _head_block_cols(w, pieces):
    k = w.shape[0]
    w3 = w.reshape(k, MLA_HEADS, w.shape[1] // MLA_HEADS)
    cols = [w3[:, :, p[0]:p[0] + p[1]] if isinstance(p, tuple) else jnp.zeros((k, MLA_HEADS, p), w.dtype)
            for p in pieces]
    return jnp.concatenate(cols, axis=-1).reshape(k, HEAD_W)


def _zoh_powers(lr, li, dt, n):
    mag = jnp.exp(lr * dt)
    ar, ai = mag * jnp.cos(li * dt), mag * jnp.sin(li * dt)
    pw = [(jnp.ones_like(ar), jnp.zeros_like(ai))]
    for _ in range(n):
        pr, pi = pw[-1]
        pw.append((pr * ar - pi * ai, pr * ai + pi * ar))
    return pw


def _disc_kernel(arow_ref, acol_ref, ldt_ref, bt_ref, cd_ref, ct_ref,
                 apr_ref, api_ref, wb_ref, wc_ref, wf_ref):
    r_blk, groups, cin = wb_ref.shape[1], wb_ref.shape[2], wb_ref.shape[3]
    n_state = acol_ref.shape[2]
    gpb = LANES // cin
    lr, li = arow_ref[0], arow_ref[1]
    dt = jnp.exp(ldt_ref[...])
    pw = _zoh_powers(lr, li, dt, r_blk)
    ar, ai = pw[1]
    den = lr * lr + li * li
    kr = ((ar - 1.0) * lr + ai * li) / den
    ki = (ai * lr - (ar - 1.0) * li) / den
    apr_ref[...], api_ref[...] = pw[r_blk]

    pwc = _zoh_powers(acol_ref[0], acol_ref[1], dt, r_blk)
    for j in range(r_blk):
        pr, pi = pwc[j + 1]
        wc_ref[0, j] = ct_ref[0] * pr - ct_ref[1] * pi
        wc_ref[1, j] = -(ct_ref[0] * pi + ct_ref[1] * pr)

    b_r = kr * bt_ref[0] - ki * bt_ref[1]
    b_i = kr * bt_ref[1] + ki * bt_ref[0]
    for i in range(r_blk):
        pr, pi = pw[r_blk - 1 - i]
        wb_ref[0, i] = pr * b_r - pi * b_i
        wb_ref[1, i] = pr * b_i + pi * b_r

    nt = (((1,), (1,)), ((), ()))
    first_copy = lax.broadcasted_iota(jnp.int32, (LANES, LANES), 1) < n_state
    block = lambda v, n: v[gpb * n:gpb * (n + 1)].reshape(LANES, LANES)
    for i in range(r_blk):
        for j in range(i):
            wf_ref[i, j] = jnp.zeros(wf_ref.shape[2:], F32)
    for m in range(r_blk):
        pr, pi = pw[m]
        ca_r = cd_ref[0] * pr - cd_ref[1] * pi
        ca_i = cd_ref[0] * pi + cd_ref[1] * pr
        for n in range(groups // gpb):
            f_t = (lax.dot_general(jnp.where(first_copy, block(b_r, n), 0.0), block(ca_r, n), nt,
                                   precision=lax.Precision.HIGHEST, preferred_element_type=F32)
                   - lax.dot_general(jnp.where(first_copy, block(b_i, n), 0.0), block(ca_i, n), nt,
                                     precision=lax.Precision.HIGHEST, preferred_element_type=F32))
            for i in range(r_blk - m):
                wf_ref[i, i + m, gpb * n:gpb * (n + 1)] = f_t.reshape(gpb, cin, LANES)


def _prepare_weights(norm_in, w_in, ssm_a_re, ssm_a_im, ssm_log_dt, ssm_b_re, ssm_b_im, ssm_c_re,
                     ssm_c_im, ssm_d, w_glu, b_glu, q_lora_norm, kv_lora_norm, w_uq, w_ukv,
                     q_nope_norm, k_nope_norm, q_rope_norm, k_rope_norm, out_norm_ssm,
                     out_norm_mla, w_out):
    groups, n_state = ssm_a_re.shape
    ssm_w = groups * SSM_GROUP
    row = lambda v: v.reshape(1, -1).astype(F32)
    tail = LANES - ROPE_DIM - NOPE_DIM

    o_kr = 2 * ssm_w + Q_LORA + KV_LORA
    w_in_p = jnp.concatenate(
        [w_in[:, :o_kr + ROPE_DIM], jnp.zeros((w_in.shape[0], LANES - ROPE_DIM), w_in.dtype),
         w_in[:, o_kr + ROPE_DIM:]], axis=1).astype(BF16)

    w_uq_p = _head_block_cols(w_uq, [(NOPE_DIM, ROPE_DIM), (0, NOPE_DIM), tail]).astype(BF16)
    w_uk_p = _head_block_cols(w_ukv, [ROPE_DIM, (0, NOPE_DIM), tail])
    w_v_lo = _head_block_cols(w_ukv, [(NOPE_DIM, V_DIM), LANES - V_DIM])
    w_v_hi = _head_block_cols(w_ukv, [LANES - V_DIM, (NOPE_DIM, V_DIM)])
    odd_head = (np.arange(HEAD_W) // LANES) % 2 == 1
    w_ukv_p = jnp.concatenate([w_uk_p, jnp.where(odd_head[None, :], w_v_hi, w_v_lo)], axis=1).astype(BF16)

    scale = (NOPE_DIM + ROPE_DIM) ** -0.5 * np.log2(np.e)
    vones = np.zeros((MLA_HEADS, LANES), np.float32)
    vones[0::2, V_DIM] = 1.0
    vones[1::2, 0] = 1.0
    vones = jnp.asarray(vones.reshape(1, HEAD_W))
    zeros = lambda n: jnp.zeros((n,), F32)
    gq =jnp.concatenate([q_rope_norm, q_nope_norm, zeros(tail)]) * scale
    gkr = jnp.concatenate([k_rope_norm, zeros(LANES - ROPE_DIM)])
    gkn = jnp.concatenate([zeros(ROPE_DIM), k_nope_norm, zeros(tail)])

    seg = np.zeros((LANES, LANES), np.float32)
    seg[:ROPE_DIM, :ROPE_DIM] = 1.0 / ROPE_DIM
    seg[ROPE_DIM:ROPE_DIM + NOPE_DIM, ROPE_DIM:ROPE_DIM + NOPE_DIM] = 1.0 / NOPE_DIM
    seg[ROPE_DIM + NOPE_DIM:, ROPE_DIM + NOPE_DIM:] = 1.0 / tail
    seg = jnp.asarray(np.kron(np.eye(2, dtype=np.float32), seg), BF16)
    place = jnp.asarray(np.eye(ROPE_DIM, LANES, dtype=np.float32), BF16)

    r_blk = SSM_STEPS_PER_SCAN
    lane_rep = lambda x: jnp.tile(x, (1,) * (x.ndim - 1) + (LANES // x.shape[-1],))
    a2 = jnp.stack([ssm_a_re, ssm_a_im]).astype(F32)
    b2 = jnp.stack([ssm_b_re, ssm_b_im]).astype(F32)
    c2 = jnp.stack([ssm_c_re, ssm_c_im]).astype(F32)
    ldt = ssm_log_dt.astype(F32)
    gl = jax.ShapeDtypeStruct((groups, 1, LANES), F32)
    a_re, a_im, wb5, wc5, wf5 = pl.pallas_call(
        _disc_kernel,
        out_shape=[gl, gl,
                   jax.ShapeDtypeStruct((2, r_blk, groups, SSM_GROUP, LANES), F32),
                   jax.ShapeDtypeStruct((2, r_blk, groups, n_state, LANES), F32),
                   jax.ShapeDtypeStruct((r_blk, r_blk, groups, SSM_GROUP, LANES), F32)],
        name="disc")(
        lane_rep(a2)[:, :, None, :], jnp.broadcast_to(a2[..., None], a2.shape + (LANES,)),
        jnp.broadcast_to(ldt[:, None, None], (groups, 1, LANES)),
        lane_rep(jnp.swapaxes(b2, 2, 3)), lane_rep(c2), lane_rep(jnp.swapaxes(c2, 2, 3)))
    a_re, a_im = a_re[:, 0, :n_state], a_im[:, 0, :n_state]

    return dict(
        ssm_w=ssm_w,
        norm_in=row(norm_in), w_in=w_in_p, qln=row(q_lora_norm), kvln=row(kv_lora_norm),
        w_uq=w_uq_p, w_ukv=w_ukv_p, gq=row(gq), gkr=row(gkr), gkn=row(gkn), seg=seg, vones=vones,
        place=place, a_re=row(a_re), a_im=row(a_im), wb5=wb5, wc5=wc5, wf5=wf5,
        ssm_d=row(ssm_d), w_glu=w_glu.astype(BF16), b_glu=row(b_glu), onorm_ssm=row(out_norm_ssm),
        onorm_mla=row(out_norm_mla), w_out=w_out.astype(BF16))


def _rope_tables(start, n):
    half = ROPE_DIM // 2
    inv = ROPE_THETA ** (-jnp.arange(half, dtype=F32) / half)
    ang = (start + jnp.arange(n)).astype(F32)[:, None] * inv[None, :]
    cos, sin = jnp.cos(ang), jnp.sin(ang)
    cos_t = jnp.concatenate([cos, cos, jnp.ones((n, LANES - ROPE_DIM), F32)], axis=1)
    sin_a = jnp.concatenate([-sin, jnp.zeros((n, LANES - half), F32)], axis=1)
    sin_b = jnp.concatenate([jnp.zeros((n, half), F32), sin, jnp.zeros((n, LANES - ROPE_DIM), F32)],
                            axis=1)
    return cos_t, sin_a, sin_b


def _mixer(x, pos0, h0re, h0im, past, wts, *, proj_tq, ssm_steps, attn_tk):
    u, gs, gm, q, k, v, ckv, kr = _proj_call(x, _rope_tables(pos0, x.shape[1]), wts, tq=proj_tq)
    mix_ssm, hre, him = _ssm_call(u, gs, h0re, h0im, wts, steps=ssm_steps, unroll=True)
    if past is None:
        y = _attn_call(q, k, v, None, gm, mix_ssm, x, wts, tq=attn_tk // 2, tk=attn_tk, hg=MLA_HEADS)
    else:
        k_past, v_past = _expand_call(*past, wts, tq=512)
        y = _attn_call(q, k_past, v_past, (k, v), gm, mix_ssm, x, wts, tq=x.shape[1], tk=attn_tk,
                       hg=MLA_HEADS)
    return y, ckv, kr, hre, him


def kernel(x_prompt, x_sample, cache_ckv, cache_krope, state_ssm_re, state_ssm_im, norm_in, w_in, ssm_a_re, ssm_a_im, ssm_log_dt, ssm_b_re, ssm_b_im, ssm_c_re, ssm_c_im, ssm_d, w_glu, b_glu, q_lora_norm, kv_lora_norm, w_uq, w_ukv, q_nope_norm, k_nope_norm, q_rope_norm, k_rope_norm, out_norm_ssm, out_norm_mla, w_out):
    depth = norm_in.shape[0]
    assert depth == 1, "single mixer layer"
    params = (norm_in, w_in, ssm_a_re, ssm_a_im, ssm_log_dt, ssm_b_re, ssm_b_im, ssm_c_re, ssm_c_im,
              ssm_d, w_glu, b_glu, q_lora_norm, kv_lora_norm, w_uq, w_ukv, q_nope_norm, k_nope_norm,
              q_rope_norm, k_rope_norm, out_norm_ssm, out_norm_mla, w_out)
    drop_depth = lambda a: a.reshape(a.shape[1:])
    wts = _prepare_weights(*[drop_depth(p) for p in params])
    groups, n_state = ssm_a_re.shape[1:]
    bp, sp, _ = x_prompt.shape
    bs, ss, _ = x_sample.shape
    past_len = cache_ckv.shape[2]

    zero_state = jnp.zeros((bp, groups * n_state), F32)
    yp, ckv_p, kr_p, re_p, im_p = _mixer(
        x_prompt, 0, zero_state, zero_state, None, wts,
        proj_tq=512, ssm_steps=64, attn_tk=512)
    ys, ckv_s, kr_s, re_s, im_s = _mixer(
        x_sample, past_len,
        state_ssm_re.reshape(bs, groups * n_state), state_ssm_im.reshape(bs, groups * n_state),
        (drop_depth(cache_ckv), jnp.swapaxes(drop_depth(cache_krope), 1, 2)), wts,
        proj_tq=ss, ssm_steps=ss, attn_tk=past_len)

    st = lambda h, bb: h.reshape(1, bb, groups, n_state)
    kr_out = lambda kr_t: jnp.swapaxes(kr_t, 1, 2)[None]
    return (yp, ys, ckv_p[None], kr_out(kr_p), st(re_p, bp), st(im_p, bp),
            ckv_s[None], kr_out(kr_s), st(re_s, bs), st(im_s, bs))
```

```python
import functools

import numpy as np
import jax
import jax.numpy as jnp
from jax import lax
from jax.experimental import pallas as pl
from jax.experimental.pallas import tpu as pltpu

F32 = jnp.float32
BF16 = jnp.bfloat16

CHUNK = 64
SSM_GROUP = 16
SSM_STATE = 64
MLA_HEADS = 8
NOPE_DIM = 64
ROPE_DIM = 32
V_DIM = 64
Q_LORA = 256
KV_LORA = 128
ROPE_THETA = 10000.0
EPS = 1e-6

LANES = 128
HEAD_W = MLA_HEADS * LANES
GROUPS_PER_BLOCK = 8
SSM_STEPS_PER_SCAN = 4
VMEM_LIMIT = 56 * 1024 * 1024


def _rms(x, gain):
    return x * lax.rsqrt(jnp.mean(x * x, axis=-1, keepdims=True) + EPS) * gain


def _seg_rms(x, seg):
    ms = jnp.dot((x * x).astype(BF16), seg, preferred_element_type=F32)
    return x * lax.rsqrt(ms + EPS)


def _rope_block(x, cos_t, sin_a, sin_b):
    return (x * cos_t + pltpu.roll(x, LANES - ROPE_DIM // 2, 1) * sin_a
            + pltpu.roll(x, ROPE_DIM // 2, 1) * sin_b)


def _expand_kv(ckv, kr_blk, w_ukv_ref, gkn_ref, seg_ref, vones_ref, k_ref, v_ref):
    kv = jnp.dot(ckv.astype(BF16), w_ukv_ref[...], preferred_element_type=F32)
    gkn = gkn_ref[...]
    blk3 = k_ref.shape[:2] + (LANES,)
    for p in range(MLA_HEADS // 2):
        kn = _seg_rms(kv[:, 2 * LANES * p:2 * LANES * (p + 1)], seg_ref[...])
        for j in range(2):
            h = 2 * p + j
            blk = kn[:, LANES * j:LANES * (j + 1)] * gkn + kr_blk
            k_ref[:, :, LANES * h:LANES * (h + 1)] = blk.astype(BF16).reshape(blk3)
    v_ref[...] = (kv[:, HEAD_W:] + vones_ref[...]).astype(BF16).reshape(v_ref.shape)


def _proj_kernel(x_ref, cos_ref, sa_ref, sb_ref, norm_in_ref, w_in_ref, qln_ref, kvln_ref,
                 w_uq_ref, w_ukv_ref, gq_ref, gkr_ref, gkn_ref, seg_ref, vones_ref,
                 u_ref, gs_ref, gm_ref, q_ref, k_ref, v_ref, ckv_ref, kr_ref, *, ssm_w, kr_transposed):
    bb, tq, d = x_ref.shape
    x = x_ref[...].reshape(bb * tq, d)
    h = _rms(x, norm_in_ref[...])
    z = jnp.dot(h.astype(BF16), w_in_ref[...], preferred_element_type=F32)
    o = 0
    for dst in (u_ref, gs_ref):
        for j in range(ssm_w // LANES):
            dst[j] = z[:, o + LANES * j:o + LANES * (j + 1)].reshape(bb, tq, LANES)
        o += ssm_w
    c_q = z[:, o:o + Q_LORA]
    o += Q_LORA
    c_kv = z[:, o:o + KV_LORA]
    o += KV_LORA
    kr_raw = z[:, o:o + LANES]
    o += LANES
    gm_ref[...] = z[:, o:].reshape(gm_ref.shape)

    per_row = lambda t_ref: jnp.concatenate([t_ref[...]] * bb, axis=0)
    cos_t, sin_a, sin_b = per_row(cos_ref), per_row(sa_ref), per_row(sb_ref)
    seg = seg_ref[...]

    kr_ms = jnp.dot((kr_raw * kr_raw).astype(BF16), seg[:LANES, :LANES], preferred_element_type=F32)
    kr_blk = _rope_block(kr_raw * lax.rsqrt(kr_ms + EPS) * gkr_ref[...], cos_t, sin_a, sin_b)
    if kr_transposed:
        kr_ref[0] = kr_blk.T[:ROPE_DIM]
    else:
        kr_ref[...] = kr_blk[:, :ROPE_DIM].reshape(kr_ref.shape)

    q = jnp.dot(_rms(c_q, qln_ref[...]).astype(BF16), w_uq_ref[...], preferred_element_type=F32)
    gq = gq_ref[...]
    for p in range(MLA_HEADS // 2):
        qn = _seg_rms(q[:, 2 * LANES * p:2 * LANES * (p + 1)], seg)
        for j in range(2):
            hd = 2 * p + j
            blk = _rope_block(qn[:, LANES * j:LANES * (j + 1)] * gq, cos_t, sin_a, sin_b)
            q_ref[:, :, LANES * hd:LANES * (hd + 1)] = blk.astype(BF16).reshape(bb, tq, LANES)

    ckv = _rms(c_kv, kvln_ref[...])
    ckv_ref[...] = ckv.reshape(ckv_ref.shape)
    _expand_kv(ckv, kr_blk, w_ukv_ref, gkn_ref, seg_ref, vones_ref, k_ref, v_ref)


def _expand_kernel(ckv_ref, kr_ref, w_ukv_ref, gkn_ref, seg_ref, vones_ref, place_ref, k_ref, v_ref):
    kr_blk = lax.dot_general(kr_ref[0].astype(BF16), place_ref[...], (((0,), (0,)), ((), ())),
                             preferred_element_type=F32)
    _expand_kv(ckv_ref[0], kr_blk, w_ukv_ref, gkn_ref, seg_ref, vones_ref, k_ref, v_ref)


def _full(shape):
    n = len(shape)
    return pl.BlockSpec(shape, lambda *_: (0,) * n)


def _proj_call(x, tables, wts, *, bb, tq):
    b, s, d = x.shape
    ssm_w = wts['ssm_w']
    cos_t, sin_a, sin_b = tables
    kr_transposed = bb == 1
    grid = (s // tq, b // bb)
    row = lambda i, j: (j, i, 0)
    tab = pl.BlockSpec((tq, LANES), lambda i, j: (i, 0))
    in_specs = [pl.BlockSpec((bb, tq, d), row), tab, tab, tab,
                _full(wts['norm_in'].shape), _full(wts['w_in'].shape), _full(wts['qln'].shape),
                _full(wts['kvln'].shape), _full(wts['w_uq'].shape), _full(wts['w_ukv'].shape),
                _full(wts['gq'].shape), _full(wts['gkr'].shape), _full(wts['gkn'].shape),
                _full(wts['seg'].shape), _full(wts['vones'].shape)]
    slabs = ssm_w // LANES
    tb_spec = pl.BlockSpec((slabs, bb, tq, LANES), lambda i, j: (0, j, i, 0))
    kr_shape, kr_spec = (((b, ROPE_DIM, s), pl.BlockSpec((1, ROPE_DIM, tq), lambda i, j: (j, 0, i)))
                         if kr_transposed else
                         ((b, s, ROPE_DIM), pl.BlockSpec((bb, tq, ROPE_DIM), row)))
    out_shape = [jax.ShapeDtypeStruct((slabs, b, s, LANES), F32),
                 jax.ShapeDtypeStruct((slabs, b, s, LANES), F32),
                 jax.ShapeDtypeStruct((b, s, ssm_w), F32),
                 jax.ShapeDtypeStruct((b, s, HEAD_W), BF16),
                 jax.ShapeDtypeStruct((b, s, HEAD_W), BF16),
                 jax.ShapeDtypeStruct((b, s, HEAD_W), BF16),
                 jax.ShapeDtypeStruct((b, s, KV_LORA), F32),
                 jax.ShapeDtypeStruct(kr_shape, F32)]
    out_specs = [tb_spec, tb_spec,
                 pl.BlockSpec((bb, tq, ssm_w), row),
                 pl.BlockSpec((bb, tq, HEAD_W), row), pl.BlockSpec((bb, tq, HEAD_W), row),
                 pl.BlockSpec((bb, tq, HEAD_W), row),
                 pl.BlockSpec((bb, tq, KV_LORA), row), kr_spec]
    return pl.pallas_call(
        functools.partial(_proj_kernel, ssm_w=ssm_w, kr_transposed=kr_transposed),
        out_shape=out_shape, grid=grid, in_specs=in_specs, out_specs=out_specs,
        compiler_params=pltpu.CompilerParams(
            dimension_semantics=("arbitrary", "arbitrary"), vmem_limit_bytes=VMEM_LIMIT),
        name="proj",
    )(x, cos_t, sin_a, sin_b, wts['norm_in'], wts['w_in'], wts['qln'], wts['kvln'], wts['w_uq'],
      wts['w_ukv'], wts['gq'], wts['gkr'], wts['gkn'], wts['seg'], wts['vones'])


def _expand_call(ckv, kr, wts, *, tq):
    b, t, _ = ckv.shape
    row = lambda j, i: (j, i, 0)
    return pl.pallas_call(
        _expand_kernel,
        out_shape=[jax.ShapeDtypeStruct((b, t, HEAD_W), BF16)] * 2,
        grid=(b, t // tq),
        in_specs=[pl.BlockSpec((1, tq, KV_LORA), row),
                  pl.BlockSpec((1, ROPE_DIM, tq), lambda j, i: (j, 0, i)),
                  _full(wts['w_ukv'].shape), _full(wts['gkn'].shape), _full(wts['seg'].shape),
                  _full(wts['vones'].shape), _full(wts['place'].shape)],
        out_specs=[pl.BlockSpec((1, tq, HEAD_W), row)] * 2,
        compiler_params=pltpu.CompilerParams(
            dimension_semantics=("arbitrary", "arbitrary"), vmem_limit_bytes=VMEM_LIMIT),
        name="expand",
    )(ckv, kr, wts['w_ukv'], wts['gkn'], wts['seg'], wts['vones'], wts['place'])


def _ssm_kernel(u_ref, gs_ref, h0re_ref, h0im_ref, are_ref, aim_ref, wb5_ref, wc5_ref, wf5_ref, d_ref,
                w_glu_ref, b_glu_ref, onorm_ref, mix_ref, hre_ref, him_ref, us_ref, xs_ref, y_ref,
                wb_ref, wc_ref, wf_ref, *, batch, steps, unroll):
    n_blocks, r_blk = wb_ref.shape[0], wb_ref.shape[1] // LANES
    half = wb_ref.shape[2] // 2
    gpb = wb5_ref.shape[2] // n_blocks
    n_k = steps // r_blk

    @pl.when(pl.program_id(0) == 0)
    def _():
        hre_ref[...] = h0re_ref[...]
        him_ref[...] = h0im_ref[...]

        def own_group(shape, row_div, lane_div):
            return (lax.broadcasted_iota(jnp.int32, shape, 0) // row_div
                    == lax.broadcasted_iota(jnp.int32, shape, 1) // lane_div)

        cin, n_state = wb5_ref.shape[3], half // gpb
        m_b = own_group((LANES, half), cin, n_state)
        m_c = own_group((half, LANES), n_state, cin)
        m_f = own_group((LANES, LANES), cin, cin)
        for gb in range(n_blocks):
            grp = slice(gpb * gb, gpb * (gb + 1))
            for ri in range(2):
                for i in range(r_blk):
                    piece = wb5_ref[ri, i, grp].reshape(LANES, LANES)
                    wide = jnp.concatenate([piece] * (half // LANES), axis=1)
                    wb_ref[gb, LANES * i:LANES * (i + 1), half * ri:half * (ri + 1)] = (
                        jnp.where(m_b, wide, 0.0).astype(BF16))
                    piece = wc5_ref[ri, i, grp].reshape(half, LANES)
                    wc_ref[gb, half * ri:half * (ri + 1), LANES * i:LANES * (i + 1)] = (
                        jnp.where(m_c, piece, 0.0).astype(BF16))
            for i in range(r_blk):
                for j in range(r_blk):
                    piece = wf5_ref[i, j, grp].reshape(LANES, LANES)
                    wf_ref[gb, LANES * i:LANES * (i + 1), LANES * j:LANES * (j + 1)] = (
                        jnp.where(m_f, piece, 0.0).astype(BF16))

    for b in range(batch):
        for j in range(n_blocks):
            for i in range(r_blk):
                us_ref[j, i, pl.ds(b, n_k, stride=batch), :] = (
                    u_ref[j, b, pl.ds(i, n_k, stride=r_blk), :])

    for gb in range(n_blocks):
        lhs = jnp.concatenate([us_ref[gb, i] for i in range(r_blk)], axis=-1).astype(BF16)
        xs_ref[...] = jnp.dot(lhs, wb_ref[gb], preferred_element_type=F32)
        cols = slice(half * gb, half * (gb + 1))
        a_re = jnp.broadcast_to(are_ref[:, cols], (batch, half))
        a_im = jnp.broadcast_to(aim_ref[:, cols], (batch, half))

        def step(k, carry):
            h_re, h_im = carry
            rows = pl.ds(pl.multiple_of(k * batch, batch), batch)
            n_re = a_re * h_re - a_im * h_im + xs_ref[rows, :half]
            n_im = a_re * h_im + a_im * h_re + xs_ref[rows, half:]
            xs_ref[rows, :half] = h_re
            xs_ref[rows, half:] = h_im
            return n_re, n_im

        h_re, h_im = lax.fori_loop(0, n_k, step, (hre_ref[:, cols], him_ref[:, cols]), unroll=unroll)
        hre_ref[:, cols] = h_re
        him_ref[:, cols] = h_im
        y = (jnp.dot(xs_ref[...].astype(BF16), wc_ref[gb], preferred_element_type=F32)
             + jnp.dot(lhs, wf_ref[gb], preferred_element_type=F32))
        for i in range(r_blk):
            y_ref[i, :, LANES * gb:LANES * (gb + 1)] = y[:, LANES * i:LANES * (i + 1)]

    rows = r_blk * n_k * batch
    u = jnp.concatenate([us_ref[j].reshape(rows, LANES) for j in range(n_blocks)], axis=-1)
    y = y_ref[...].reshape(rows, n_blocks * LANES) + d_ref[...] * u
    yg = jax.nn.gelu(y)
    glu = jnp.dot(yg.astype(BF16), w_glu_ref[...], preferred_element_type=F32) + b_glu_ref[...]
    out = _rms(yg * jax.nn.sigmoid(glu), onorm_ref[...])
    for j in range(n_blocks):
        us_ref[j] = out[:, LANES * j:LANES * (j + 1)].reshape(r_blk, n_k * batch, LANES)
    for b in range(batch):
        for j in range(n_blocks):
            for i in range(r_blk):
                tok = pl.ds(i, n_k, stride=r_blk)
                gs = gs_ref[j, b, tok, :]
                o = us_ref[j, i, pl.ds(b, n_k, stride=batch), :]
                mix_ref[j, b, tok, :] = o * (gs * jax.nn.sigmoid(gs))


def _ssm_call(u, gs, h0re, h0im, wts, *, steps, unroll):
    n_blocks, batch, seq, _ = u.shape
    n_state = h0re.shape[1]
    r_blk = wts['wb5'].shape[1]
    width = 2 * n_state // n_blocks
    assert steps % r_blk == 0 and wts['wb5'].shape[3] * GROUPS_PER_BLOCK == LANES
    blk = pl.BlockSpec((n_blocks, batch, steps, LANES), lambda i: (0, 0, i, 0))
    names = ['a_re', 'a_im', 'wb5', 'wc5', 'wf5', 'ssm_d', 'w_glu', 'b_glu', 'onorm_ssm']
    n_rows = steps // r_blk * batch
    return pl.pallas_call(
        functools.partial(_ssm_kernel, batch=batch, steps=steps, unroll=unroll),
        out_shape=[jax.ShapeDtypeStruct(u.shape, F32),
                   jax.ShapeDtypeStruct((batch, n_state), F32),
                   jax.ShapeDtypeStruct((batch, n_state), F32)],
        grid=(seq // steps,),
        in_specs=[blk, blk, _full(h0re.shape), _full(h0im.shape)] + [_full(wts[n].shape) for n in names],
        out_specs=[blk, _full((batch, n_state)), _full((batch, n_state))],
        scratch_shapes=[pltpu.VMEM((n_blocks, r_blk, n_rows, LANES), F32),
                        pltpu.VMEM((n_rows, width), F32),
                        pltpu.VMEM((r_blk, n_rows, n_blocks * LANES), F32),
                        pltpu.VMEM((n_blocks, r_blk * LANES, width), BF16),
                        pltpu.VMEM((n_blocks, width, r_blk * LANES), BF16),
                        pltpu.VMEM((n_blocks, r_blk * LANES, r_blk * LANES), BF16)],
        compiler_params=pltpu.CompilerParams(
            dimension_semantics=("arbitrary",), vmem_limit_bytes=VMEM_LIMIT),
        name="ssm",
    )(u, gs, h0re, h0im, *[wts[n] for n in names])


def _attn_kernel(*refs, tq, tk, causal, fuse_out, hg):
    if causal:
        q_ref, kf_ref, vf_ref = refs[:3]
        rest = refs[3:]
    else:
        q_ref, kf_ref, vf_ref, kl_ref, vl_ref = refs[:5]
        rest = refs[5:]
    if fuse_out:
        gm_ref, ms_ref, x_ref, onorm_ref, w_out_ref, y_ref = rest[:6]
    else:
        gm_ref, onorm_ref, y_ref = rest[:3]
    s_full, ml_scr, s_last, m_scr, acc_scr = rest[-5:]
    i = pl.program_id(1)
    if causal:
        n_full = (i * tq) // tk
        def chunk_mask(width):
            qc = lax.broadcasted_iota(jnp.int32, (tq, width), 0) // CHUNK + (width - tq) // CHUNK
            return qc >= lax.broadcasted_iota(jnp.int32, (tq, width), 1) // CHUNK

        odd = i % 2 == 1
        lasts = [(odd, tk, pl.ds(pl.multiple_of(n_full * tk, tk), tk), chunk_mask(tk)),
                 (jnp.logical_not(odd), tq, pl.ds(pl.multiple_of(i * tq, tq), tq), chunk_mask(tq))]
        lasts = [(cond, width, functools.partial(lambda rows, c: kf_ref[0, rows, c], rows),
                  functools.partial(lambda rows, c: vf_ref[0, rows, c], rows), mask)
                 for cond, width, rows, mask in lasts]
    else:
        n_full = kf_ref.shape[1] // tk
        lasts = [(None, kl_ref.shape[1], lambda c: kl_ref[0, :, c], lambda c: vl_ref[0, :, c], None)]
    dn = (((1,), (1,)), ((), ()))
    lane = lax.broadcasted_iota(jnp.int32, (tq, LANES), 1)

    def lane_tiles(s):
        return [s[:, LANES * c:LANES * (c + 1)] for c in range(s.shape[1] // LANES)]

    def probs(s, m_rep):
        if s.shape[1] % LANES == 0:
            p = jnp.concatenate([jnp.exp2(t - m_rep) for t in lane_tiles(s)], axis=-1)
        else:
            p = jnp.exp2(s - m_rep[:, :1])
        return p.astype(BF16)

    pairs = []
    for g in range(MLA_HEADS // hg):
        heads = [(hl, g * hg + hl) for hl in range(hg)]
        cols = {hd: slice(LANES * hd, LANES * (hd + 1)) for _, hd in heads}

        ml_scr[...] = jnp.full(ml_scr.shape, -jnp.inf, F32)

        def a_step(jt, carry, heads=heads, cols=cols):
            rows = pl.ds(pl.multiple_of(jt * tk, tk), tk)
            for hl, hd in heads:
                s = lax.dot_general(q_ref[0, :, cols[hd]], kf_ref[0, rows, cols[hd]], dn,
                                    preferred_element_type=F32)
                s_full[hl, jt] = s
                ml_scr[hl] = functools.reduce(jnp.maximum, lane_tiles(s), ml_scr[hl])
            return carry

        lax.fori_loop(0, n_full, a_step, 0)

        def last_a(width, k_last, mask, heads=heads, cols=cols):
            for hl, hd in heads:
                s = lax.dot_general(q_ref[0, :, cols[hd]], k_last(cols[hd]), dn,
                                    preferred_element_type=F32)
                if mask is not None:
                    s = jnp.where(mask, s, -jnp.inf)
                s_last[hl, :, :width] = s
                if width % LANES == 0:
                    m = jnp.max(functools.reduce(jnp.maximum, lane_tiles(s), ml_scr[hl]),
                                axis=-1, keepdims=True)
                else:
                    m = jnp.maximum(jnp.max(s, axis=-1, keepdims=True),
                                    jnp.max(ml_scr[hl], axis=-1, keepdims=True))
                m_scr[hl] = jnp.broadcast_to(m, (tq, LANES))
                acc_scr[hl] = jnp.zeros((tq, LANES), F32)

        for cond, width, k_last, _, mask in lasts:
            run = functools.partial(last_a, width, k_last, mask)
            run() if cond is None else pl.when(cond)(run)

        def b_step(jt, carry, heads=heads, cols=cols):
            rows = pl.ds(pl.multiple_of(jt * tk, tk), tk)
            for hl, hd in heads:
                p = probs(s_full[hl, jt], m_scr[hl])
                acc_scr[hl] += jnp.dot(p, vf_ref[0, rows, cols[hd]], preferred_element_type=F32)
            return carry

        lax.fori_loop(0, n_full, b_step, 0)

        def last_b(width, v_last, heads=heads, cols=cols):
            for hl, hd in heads:
                p = probs(s_last[hl, :, :width], m_scr[hl])
                acc = acc_scr[hl] + jnp.dot(p, v_last(cols[hd]), preferred_element_type=F32)
                ones_col = V_DIM if hd % 2 == 0 else 0
                acc_scr[hl] = acc / acc[:, ones_col:ones_col + 1]

        for cond, width, _, v_last, _ in lasts:
            run = functools.partial(last_b, width, v_last)
            run() if cond is None else pl.when(cond)(run)

        for e in range(0, hg, 2):
            pairs.append(jnp.where(lane < V_DIM, acc_scr[e], acc_scr[e + 1]))

    attn = jnp.concatenate(pairs, axis=-1)
    gm = gm_ref[0]
    mla = _rms(attn, onorm_ref[...]) * (gm * jax.nn.sigmoid(gm))
    if fuse_out:
        mix_ssm = [ms_ref[j, 0] for j in range(ms_ref.shape[0])]
        mix = jnp.concatenate(mix_ssm + [mla], axis=-1).astype(BF16)
        y_ref[0] = x_ref[0] + jnp.dot(mix, w_out_ref[...], preferred_element_type=F32)
    else:
        y_ref[0] = mla.astype(BF16)


def _out_kernel(ms_ref, mla_ref, x_ref, w_out_ref, y_ref):
    bb, tq, d = x_ref.shape
    rows = bb * tq
    mix = jnp.concatenate([ms_ref[j].reshape(rows, LANES).astype(BF16) for j in range(ms_ref.shape[0])]
                          + [mla_ref[...].reshape(rows, mla_ref.shape[2])], axis=-1)
    y = x_ref[...].reshape(rows, d) + jnp.dot(mix, w_out_ref[...], preferred_element_type=F32)
    y_ref[...] = y.reshape(bb, tq, d)


def _out_call(mix_ssm, mla, x, wts, *, bb):
    b, s, d = x.shape
    row = lambda j: (j, 0, 0)
    return pl.pallas_call(
        _out_kernel,
        out_shape=jax.ShapeDtypeStruct((b, s, d), F32),
        grid=(b // bb,),
        in_specs=[pl.BlockSpec((mix_ssm.shape[0], bb, s, LANES), lambda j: (0, j, 0, 0)),
                  pl.BlockSpec((bb, s, mla.shape[2]), row), pl.BlockSpec((bb, s, d), row),
                  _full(wts['w_out'].shape)],
        out_specs=pl.BlockSpec((bb, s, d), row),
        compiler_params=pltpu.CompilerParams(
            dimension_semantics=("arbitrary",), vmem_limit_bytes=VMEM_LIMIT),
        name="outproj",
    )(mix_ssm, mla, x, wts['w_out'])


def _attn_call(q, k_full, v_full, last, gm, mix_ssm, x, wts, *, tq, tk, hg):
    b, s, d = x.shape
    t_full = k_full.shape[1]
    ssm_w = gm.shape[2]
    causal = last is None
    row = lambda j, i: (j, i, 0)
    res = lambda j, i: (j, 0, 0)
    if causal:
        assert tk == 2 * tq and t_full % tk == 0 and tq % CHUNK == 0
        last, last_specs, last_len, max_full = (), [], tk, t_full // tk - 1
    else:
        last_len, max_full = last[0].shape[1], t_full // tk
        last_specs = [pl.BlockSpec((1, last_len, HEAD_W), res)] * 2
    scratch = [pltpu.VMEM((hg, max_full, tq, tk), F32), pltpu.VMEM((hg, tq, LANES), F32),
               pltpu.VMEM((hg, tq, last_len), F32), pltpu.VMEM((hg, tq, LANES), F32),
               pltpu.VMEM((hg, tq, LANES), F32)]
    kv_specs = [pl.BlockSpec((1, tq, HEAD_W), row),
                pl.BlockSpec((1, t_full, HEAD_W), res), pl.BlockSpec((1, t_full, HEAD_W), res),
                *last_specs, pl.BlockSpec((1, tq, ssm_w), row)]
    if causal:
        tail_args = (mix_ssm, x, wts['onorm_mla'], wts['w_out'])
        tail_specs = [pl.BlockSpec((mix_ssm.shape[0], 1, tq, LANES), lambda j, i: (0, j, i, 0)),
                      pl.BlockSpec((1, tq, d), row),
                      _full(wts['onorm_mla'].shape), _full(wts['w_out'].shape)]
        out_shape, out_spec = jax.ShapeDtypeStruct((b, s, d), F32), pl.BlockSpec((1, tq, d), row)
    else:
        tail_args, tail_specs = (wts['onorm_mla'],), [_full(wts['onorm_mla'].shape)]
        out_shape, out_spec = jax.ShapeDtypeStruct((b, s, ssm_w), BF16), pl.BlockSpec((1, tq, ssm_w), row)
    return pl.pallas_call(
        functools.partial(_attn_kernel, tq=tq, tk=tk, causal=causal, fuse_out=causal, hg=hg),
        out_shape=out_shape,
        grid=(b, s // tq),
        in_specs=kv_specs + tail_specs,
        out_specs=out_spec,
        scratch_shapes=scratch,
        compiler_params=pltpu.CompilerParams(
            dimension_semantics=("arbitrary", "arbitrary"), vmem_limit_bytes=VMEM_LIMIT),
        name="attn",
    )(q, k_full, v_full, *last, gm, *tail_args)


def _head_block_cols(w, pieces):
    k = w.shape[0]
    w3 = w.reshape(k, MLA_HEADS, w.shape[1] // MLA_HEADS)
    cols = [w3[:, :, p[0]:p[0] + p[1]] if isinstance(p, tuple) else jnp.zeros((k, MLA_HEADS, p), w.dtype)
            for p in pieces]
    return jnp.concatenate(cols, axis=-1).reshape(k, HEAD_W)


def _zoh_powers(lr, li, dt, n):
    mag = jnp.exp(lr * dt)
    ar, ai = mag * jnp.cos(li * dt), mag * jnp.sin(li * dt)
    pw = [(jnp.ones_like(ar), jnp.zeros_like(ai))]
    for _ in range(n):
        pr, pi = pw[-1]
        pw.append((pr * ar - pi * ai, pr * ai + pi * ar))
    return pw


def _disc_kernel(arow_ref, acol_ref, ldt_ref, bt_ref, cd_ref, ct_ref,
                 apr_ref, api_ref, wb_ref, wc_ref, wf_ref):
    r_blk, groups, cin = wb_ref.shape[1], wb_ref.shape[2], wb_ref.shape[3]
    n_state = acol_ref.shape[2]
    gpb = LANES // cin
    lr, li = arow_ref[0], arow_ref[1]
    dt = jnp.exp(ldt_ref[...])
    pw = _zoh_powers(lr, li, dt, r_blk)
    ar, ai = pw[1]
    den = lr * lr + li * li
    kr = ((ar - 1.0) * lr + ai * li) / den
    ki = (ai * lr - (ar - 1.0) * li) / den
    apr_ref[...], api_ref[...] = pw[r_blk]

    pwc = _zoh_powers(acol_ref[0], acol_ref[1], dt, r_blk)
    for j in range(r_blk):
        pr, pi = pwc[j + 1]
        wc_ref[0, j] = ct_ref[0] * pr - ct_ref[1] * pi
        wc_ref[1, j] = -(ct_ref[0] * pi + ct_ref[1] * pr)

    b_r = kr * bt_ref[0] - ki * bt_ref[1]
    b_i = kr * bt_ref[1] + ki * bt_ref[0]
    for i in range(r_blk):
        pr, pi = pw[r_blk - 1 - i]
        wb_ref[0, i] = pr * b_r - pi * b_i
        wb_ref[1, i] = pr * b_i + pi * b_r

    nt = (((1,), (1,)), ((), ()))
    first_copy = lax.broadcasted_iota(jnp.int32, (LANES, LANES), 1) < n_state
    block = lambda v, n: v[gpb * n:gpb * (n + 1)].reshape(LANES, LANES)
    for i in range(r_blk):
        for j in range(i):
            wf_ref[i, j] = jnp.zeros(wf_ref.shape[2:], F32)
    for m in range(r_blk):
        pr, pi = pw[m]
        ca_r = cd_ref[0] * pr - cd_ref[1] * pi
        ca_i = cd_ref[0] * pi + cd_ref[1] * pr
        for n in range(groups // gpb):
            f_t = (lax.dot_general(jnp.where(first_copy, block(b_r, n), 0.0), block(ca_r, n), nt,
                                   precision=lax.Precision.HIGHEST, preferred_element_type=F32)
                   - lax.dot_general(jnp.where(first_copy, block(b_i, n), 0.0), block(ca_i, n), nt,
                                     precision=lax.Precision.HIGHEST, preferred_element_type=F32))
            for i in range(r_blk - m):
                wf_ref[i, i + m, gpb * n:gpb * (n + 1)] = f_t.reshape(gpb, cin, LANES)


def _prepare_weights(norm_in, w_in, ssm_a_re, ssm_a_im, ssm_log_dt, ssm_b_re, ssm_b_im, ssm_c_re,
                     ssm_c_im, ssm_d, w_glu, b_glu, q_lora_norm, kv_lora_norm, w_uq, w_ukv,
                     q_nope_norm, k_nope_norm, q_rope_norm, k_rope_norm, out_norm_ssm,
                     out_norm_mla, w_out):
    groups, n_state = ssm_a_re.shape
    ssm_w = groups * SSM_GROUP
    row = lambda v: v.reshape(1, -1).astype(F32)
    tail = LANES - ROPE_DIM - NOPE_DIM

    o_kr = 2 * ssm_w + Q_LORA + KV_LORA
    w_in_p = jnp.concatenate(
        [w_in[:, :o_kr + ROPE_DIM], jnp.zeros((w_in.shape[0], LANES - ROPE_DIM), w_in.dtype),
         w_in[:, o_kr + ROPE_DIM:]], axis=1).astype(BF16)

    w_uq_p = _head_block_cols(w_uq, [(NOPE_DIM, ROPE_DIM), (0, NOPE_DIM), tail]).astype(BF16)
    w_uk_p = _head_block_cols(w_ukv, [ROPE_DIM, (0, NOPE_DIM), tail])
    w_v_lo = _head_block_cols(w_ukv, [(NOPE_DIM, V_DIM), LANES - V_DIM])
    w_v_hi = _head_block_cols(w_ukv, [LANES - V_DIM, (NOPE_DIM, V_DIM)])
    odd_head = (np.arange(HEAD_W) // LANES) % 2 == 1
    w_ukv_p = jnp.concatenate([w_uk_p, jnp.where(odd_head[None, :], w_v_hi, w_v_lo)], axis=1).astype(BF16)

    scale = (NOPE_DIM + ROPE_DIM) ** -0.5 * np.log2(np.e)
    vones = np.zeros((MLA_HEADS, LANES), np.float32)
    vones[0::2, V_DIM] = 1.0
    vones[1::2, 0] = 1.0
    vones = jnp.asarray(vones.reshape(1, HEAD_W))
    zeros = lambda n: jnp.zeros((n,), F32)
    gq =jnp.concatenate([q_rope_norm, q_nope_norm, zeros(tail)]) * scale
    gkr = jnp.concatenate([k_rope_norm, zeros(LANES - ROPE_DIM)])
    gkn = jnp.concatenate([zeros(ROPE_DIM), k_nope_norm, zeros(tail)])

    seg = np.zeros((LANES, LANES), np.float32)
    seg[:ROPE_DIM, :ROPE_DIM] = 1.0 / ROPE_DIM
    seg[ROPE_DIM:ROPE_DIM + NOPE_DIM, ROPE_DIM:ROPE_DIM + NOPE_DIM] = 1.0 / NOPE_DIM
    seg[ROPE_DIM + NOPE_DIM:, ROPE_DIM + NOPE_DIM:] = 1.0 / tail
    seg = jnp.asarray(np.kron(np.eye(2, dtype=np.float32), seg), BF16)
    place = jnp.asarray(np.eye(ROPE_DIM, LANES, dtype=np.float32), BF16)

    r_blk = SSM_STEPS_PER_SCAN
    lane_rep = lambda x: jnp.tile(x, (1,) * (x.ndim - 1) + (LANES // x.shape[-1],))
    a2 = jnp.stack([ssm_a_re, ssm_a_im]).astype(F32)
    b2 = jnp.stack([ssm_b_re, ssm_b_im]).astype(F32)
    c2 = jnp.stack([ssm_c_re, ssm_c_im]).astype(F32)
    ldt = ssm_log_dt.astype(F32)
    gl = jax.ShapeDtypeStruct((groups, 1, LANES), F32)
    a_re, a_im, wb5, wc5, wf5 = pl.pallas_call(
        _disc_kernel,
        out_shape=[gl, gl,
                   jax.ShapeDtypeStruct((2, r_blk, groups, SSM_GROUP, LANES), F32),
                   jax.ShapeDtypeStruct((2, r_blk, groups, n_state, LANES), F32),
                   jax.ShapeDtypeStruct((r_blk, r_blk, groups, SSM_GROUP, LANES), F32)],
        name="disc")(
        lane_rep(a2)[:, :, None, :], jnp.broadcast_to(a2[..., None], a2.shape + (LANES,)),
        jnp.broadcast_to(ldt[:, None, None], (groups, 1, LANES)),
        lane_rep(jnp.swapaxes(b2, 2, 3)), lane_rep(c2), lane_rep(jnp.swapaxes(c2, 2, 3)))
    a_re, a_im = a_re[:, 0, :n_state], a_im[:, 0, :n_state]

    return dict(
        ssm_w=ssm_w,
        norm_in=row(norm_in), w_in=w_in_p, qln=row(q_lora_norm), kvln=row(kv_lora_norm),
        w_uq=w_uq_p, w_ukv=w_ukv_p, gq=row(gq), gkr=row(gkr), gkn=row(gkn), seg=seg, vones=vones,
        place=place, a_re=row(a_re), a_im=row(a_im), wb5=wb5, wc5=wc5, wf5=wf5,
        ssm_d=row(ssm_d), w_glu=w_glu.astype(BF16), b_glu=row(b_glu), onorm_ssm=row(out_norm_ssm),
        onorm_mla=row(out_norm_mla), w_out=w_out.astype(BF16))


def _rope_tables(start, n):
    half = ROPE_DIM // 2
    inv = ROPE_THETA ** (-jnp.arange(half, dtype=F32) / half)
    ang = (start + jnp.arange(n)).astype(F32)[:, None] * inv[None, :]
    cos, sin = jnp.cos(ang), jnp.sin(ang)
    cos_t = jnp.concatenate([cos, cos, jnp.ones((n, LANES - ROPE_DIM), F32)], axis=1)
    sin_a = jnp.concatenate([-sin, jnp.zeros((n, LANES - half), F32)], axis=1)
    sin_b = jnp.concatenate([jnp.zeros((n, half), F32), sin, jnp.zeros((n, LANES - ROPE_DIM), F32)],
                            axis=1)
    return cos_t, sin_a, sin_b


def _mixer(x, pos0, h0re, h0im, past, wts, *, proj_bb, proj_tq, ssm_steps, attn_tk):
    u, gs, gm, q, k, v, ckv, kr = _proj_call(x, _rope_tables(pos0, x.shape[1]), wts,
                                             bb=proj_bb, tq=proj_tq)
    mix_ssm, hre, him = _ssm_call(u, gs, h0re, h0im, wts, steps=ssm_steps, unroll=True)
    if past is None:
        y = _attn_call(q, k, v, None, gm, mix_ssm, x, wts, tq=attn_tk // 2, tk=attn_tk, hg=MLA_HEADS)
    else:
        k_past, v_past = _expand_call(*past, wts, tq=512)
        mla = _attn_call(q, k_past, v_past, (k, v), gm, mix_ssm, x, wts, tq=x.shape[1], tk=attn_tk,
                         hg=MLA_HEADS)
        y = _out_call(mix_ssm, mla, x, wts, bb=proj_bb)
    if proj_bb == 1:
        kr = jnp.swapaxes(kr, 1, 2)
    return y, ckv, kr, hre, him


def kernel(x_prompt, x_sample, cache_ckv, cache_krope, state_ssm_re, state_ssm_im, norm_in, w_in, ssm_a_re, ssm_a_im, ssm_log_dt, ssm_b_re, ssm_b_im, ssm_c_re, ssm_c_im, ssm_d, w_glu, b_glu, q_lora_norm, kv_lora_norm, w_uq, w_ukv, q_nope_norm, k_nope_norm, q_rope_norm, k_rope_norm, out_norm_ssm, out_norm_mla, w_out):
    depth = norm_in.shape[0]
    assert depth == 1, "single mixer layer"
    params = (norm_in, w_in, ssm_a_re, ssm_a_im, ssm_log_dt, ssm_b_re, ssm_b_im, ssm_c_re, ssm_c_im,
              ssm_d, w_glu, b_glu, q_lora_norm, kv_lora_norm, w_uq, w_ukv, q_nope_norm, k_nope_norm,
              q_rope_norm, k_rope_norm, out_norm_ssm, out_norm_mla, w_out)
    drop_depth = lambda a: a.reshape(a.shape[1:])
    wts = _prepare_weights(*[drop_depth(p) for p in params])
    groups, n_state = ssm_a_re.shape[1:]
    bp, sp, _ = x_prompt.shape
    bs, ss, _ = x_sample.shape
    past_len = cache_ckv.shape[2]

    zero_state = jnp.zeros((bp, groups * n_state), F32)
    yp, ckv_p, kr_p, re_p, im_p = _mixer(
        x_prompt, 0, zero_state, zero_state, None, wts,
        proj_bb=1, proj_tq=512, ssm_steps=64, attn_tk=512)
    ys, ckv_s, kr_s, re_s, im_s = _mixer(
        x_sample, past_len,
        state_ssm_re.reshape(bs, groups * n_state), state_ssm_im.reshape(bs, groups * n_state),
        (drop_depth(cache_ckv), jnp.swapaxes(drop_depth(cache_krope), 1, 2)), wts,
        proj_bb=bs // 2, proj_tq=ss, ssm_steps=ss, attn_tk=past_len)

    st = lambda h, bb: h.reshape(1, bb, groups, n_state)
    return (yp, ys, ckv_p[None], kr_p[None], st(re_p, bp), st(im_p, bp),
            ckv_s[None], kr_s[None], st(re_s, bs), st(im_s, bs))
```

```python
import functools

import numpy as np
import jax
import jax.numpy as jnp
from jax import lax
from jax.experimental import pallas as pl
from jax.experimental.pallas import tpu as pltpu

F32 = jnp.float32
BF16 = jnp.bfloat16

CHUNK = 64
SSM_GROUP = 16
SSM_STATE = 64
MLA_HEADS = 8
NOPE_DIM = 64
ROPE_DIM = 32
V_DIM = 64
Q_LORA = 256
KV_LORA = 128
ROPE_THETA = 10000.0
EPS = 1e-6

LANES = 128
SUBLANES = 8
HEAD_W = MLA_HEADS * LANES
GROUPS_PER_BLOCK = 8
SSM_STEPS_PER_SCAN = 4
VMEM_LIMIT = 56 * 1024 * 1024


def _rms(x, gain):
    return x * lax.rsqrt(jnp.mean(x * x, axis=-1, keepdims=True) + EPS) * gain


def _seg_rms(x, seg):
    ms = jnp.dot((x * x).astype(BF16), seg, preferred_element_type=F32)
    return x * lax.rsqrt(ms + EPS)


def _rope_block(x, cos_t, sin_a, sin_b):
    return (x * cos_t + pltpu.roll(x, LANES - ROPE_DIM // 2, 1) * sin_a
            + pltpu.roll(x, ROPE_DIM // 2, 1) * sin_b)


def _expand_kv(ckv, kr_blk, w_ukv_ref, gkn_ref, seg_ref, vones_ref, k_ref, v_ref):
    kv = jnp.dot(ckv.astype(BF16), w_ukv_ref[...], preferred_element_type=F32)
    gkn = gkn_ref[...]
    blk3 = k_ref.shape[:2] + (LANES,)
    for p in range(MLA_HEADS // 2):
        kn = _seg_rms(kv[:, 2 * LANES * p:2 * LANES * (p + 1)], seg_ref[...])
        for j in range(2):
            h = 2 * p + j
            blk = kn[:, LANES * j:LANES * (j + 1)] * gkn + kr_blk
            k_ref[:, :, LANES * h:LANES * (h + 1)] = blk.astype(BF16).reshape(blk3)
    v_ref[...] = (kv[:, HEAD_W:] + vones_ref[...]).astype(BF16).reshape(v_ref.shape)


def _proj_kernel(x_ref, cos_ref, sa_ref, sb_ref, norm_in_ref, w_in_ref, qln_ref, kvln_ref,
                 w_uq_ref, w_ukv_ref, gq_ref, gkr_ref, gkn_ref, seg_ref, vones_ref,
                 u_ref, gs_ref, gm_ref, q_ref, k_ref, v_ref, ckv_ref, kr_ref, *, ssm_w, kr_transposed):
    bb, tq, d = x_ref.shape
    x = x_ref[...].reshape(bb * tq, d)
    h = _rms(x, norm_in_ref[...])
    z = jnp.dot(h.astype(BF16), w_in_ref[...], preferred_element_type=F32)
    o = 0
    for dst in (u_ref, gs_ref):
        for j in range(ssm_w // LANES):
            dst[j] = z[:, o + LANES * j:o + LANES * (j + 1)].reshape(bb, tq, LANES)
        o += ssm_w
    c_q = z[:, o:o + Q_LORA]
    o += Q_LORA
    c_kv = z[:, o:o + KV_LORA]
    o += KV_LORA
    kr_raw = z[:, o:o + LANES]
    o += LANES
    gm_ref[...] = z[:, o:].reshape(gm_ref.shape)

    per_row = lambda t_ref: jnp.concatenate([t_ref[...]] * bb, axis=0)
    cos_t, sin_a, sin_b = per_row(cos_ref), per_row(sa_ref), per_row(sb_ref)
    seg = seg_ref[...]

    kr_ms = jnp.dot((kr_raw * kr_raw).astype(BF16), seg[:LANES, :LANES], preferred_element_type=F32)
    kr_blk = _rope_block(kr_raw * lax.rsqrt(kr_ms + EPS) * gkr_ref[...], cos_t, sin_a, sin_b)
    if kr_transposed:
        kr_ref[0] = kr_blk.T[:ROPE_DIM]
    else:
        kr_ref[...] = kr_blk[:, :ROPE_DIM].reshape(kr_ref.shape)

    q = jnp.dot(_rms(c_q, qln_ref[...]).astype(BF16), w_uq_ref[...], preferred_element_type=F32)
    gq = gq_ref[...]
    for p in range(MLA_HEADS // 2):
        qn = _seg_rms(q[:, 2 * LANES * p:2 * LANES * (p + 1)], seg)
        for j in range(2):
            hd = 2 * p + j
            blk = _rope_block(qn[:, LANES * j:LANES * (j + 1)] * gq, cos_t, sin_a, sin_b)
            q_ref[:, :, LANES * hd:LANES * (hd + 1)] = blk.astype(BF16).reshape(bb, tq, LANES)

    ckv = _rms(c_kv, kvln_ref[...])
    ckv_ref[...] = ckv.reshape(ckv_ref.shape)
    _expand_kv(ckv, kr_blk, w_ukv_ref, gkn_ref, seg_ref, vones_ref, k_ref, v_ref)


def _expand_kernel(ckv_ref, kr_ref, w_ukv_ref, gkn_ref, seg_ref, vones_ref, place_ref, k_ref, v_ref):
    kr_blk = lax.dot_general(kr_ref[0].astype(BF16), place_ref[...], (((0,), (0,)), ((), ())),
                             preferred_element_type=F32)
    _expand_kv(ckv_ref[0], kr_blk, w_ukv_ref, gkn_ref, seg_ref, vones_ref, k_ref, v_ref)


def _full(shape):
    n = len(shape)
    return pl.BlockSpec(shape, lambda *_: (0,) * n)


def _proj_call(x, tables, wts, *, bb, tq):
    b, s, d = x.shape
    ssm_w = wts['ssm_w']
    cos_t, sin_a, sin_b = tables
    kr_transposed = bb == 1
    grid = (s // tq, b // bb)
    row = lambda i, j: (j, i, 0)
    tab = pl.BlockSpec((tq, LANES), lambda i, j: (i, 0))
    in_specs = [pl.BlockSpec((bb, tq, d), row), tab, tab, tab,
                _full(wts['norm_in'].shape), _full(wts['w_in'].shape), _full(wts['qln'].shape),
                _full(wts['kvln'].shape), _full(wts['w_uq'].shape), _full(wts['w_ukv'].shape),
                _full(wts['gq'].shape), _full(wts['gkr'].shape), _full(wts['gkn'].shape),
                _full(wts['seg'].shape), _full(wts['vones'].shape)]
    slabs = ssm_w // LANES
    tb_spec = pl.BlockSpec((slabs, bb, tq, LANES), lambda i, j: (0, j, i, 0))
    kr_shape, kr_spec = (((b, ROPE_DIM, s), pl.BlockSpec((1, ROPE_DIM, tq), lambda i, j: (j, 0, i)))
                         if kr_transposed else
                         ((b, s, ROPE_DIM), pl.BlockSpec((bb, tq, ROPE_DIM), row)))
    out_shape = [jax.ShapeDtypeStruct((slabs, b, s, LANES), F32),
                 jax.ShapeDtypeStruct((slabs, b, s, LANES), F32),
                 jax.ShapeDtypeStruct((b, s, ssm_w), F32),
                 jax.ShapeDtypeStruct((b, s, HEAD_W), BF16),
                 jax.ShapeDtypeStruct((b, s, HEAD_W), BF16),
                 jax.ShapeDtypeStruct((b, s, HEAD_W), BF16),
                 jax.ShapeDtypeStruct((b, s, KV_LORA), F32),
                 jax.ShapeDtypeStruct(kr_shape, F32)]
    out_specs = [tb_spec, tb_spec,
                 pl.BlockSpec((bb, tq, ssm_w), row),
                 pl.BlockSpec((bb, tq, HEAD_W), row), pl.BlockSpec((bb, tq, HEAD_W), row),
                 pl.BlockSpec((bb, tq, HEAD_W), row),
                 pl.BlockSpec((bb, tq, KV_LORA), row), kr_spec]
    return pl.pallas_call(
        functools.partial(_proj_kernel, ssm_w=ssm_w, kr_transposed=kr_transposed),
        out_shape=out_shape, grid=grid, in_specs=in_specs, out_specs=out_specs,
        compiler_params=pltpu.CompilerParams(
            dimension_semantics=("arbitrary", "arbitrary"), vmem_limit_bytes=VMEM_LIMIT),
        name="proj",
    )(x, cos_t, sin_a, sin_b, wts['norm_in'], wts['w_in'], wts['qln'], wts['kvln'], wts['w_uq'],
      wts['w_ukv'], wts['gq'], wts['gkr'], wts['gkn'], wts['seg'], wts['vones'])


def _expand_call(ckv, kr, wts, *, tq):
    b, t, _ = ckv.shape
    row = lambda j, i: (j, i, 0)
    return pl.pallas_call(
        _expand_kernel,
        out_shape=[jax.ShapeDtypeStruct((b, t, HEAD_W), BF16)] * 2,
        grid=(b, t // tq),
        in_specs=[pl.BlockSpec((1, tq, KV_LORA), row),
                  pl.BlockSpec((1, ROPE_DIM, tq), lambda j, i: (j, 0, i)),
                  _full(wts['w_ukv'].shape), _full(wts['gkn'].shape), _full(wts['seg'].shape),
                  _full(wts['vones'].shape), _full(wts['place'].shape)],
        out_specs=[pl.BlockSpec((1, tq, HEAD_W), row)] * 2,
        compiler_params=pltpu.CompilerParams(
            dimension_semantics=("arbitrary", "arbitrary"), vmem_limit_bytes=VMEM_LIMIT),
        name="expand",
    )(ckv, kr, wts['w_ukv'], wts['gkn'], wts['seg'], wts['vones'], wts['place'])


def _ssm_kernel(u_ref, gs_ref, h0re_ref, h0im_ref, are_ref, aim_ref, wb5_ref, wc5_ref, wf5_ref, d_ref,
                w_glu_ref, b_glu_ref, onorm_ref, mix_ref, hre_ref, him_ref, us_ref, xs_ref, y_ref,
                wb_ref, wc_ref, wf_ref, *, batch, steps, unroll):
    n_blocks, r_blk = wb_ref.shape[0], wb_ref.shape[1] // LANES
    half = wb_ref.shape[2] // 2
    gpb = wb5_ref.shape[2] // n_blocks
    n_k = steps // r_blk

    @pl.when(pl.program_id(0) == 0)
    def _():
        hre_ref[...] = h0re_ref[...]
        him_ref[...] = h0im_ref[...]

        def own_group(shape, row_div, lane_div):
            return (lax.broadcasted_iota(jnp.int32, shape, 0) // row_div
                    == lax.broadcasted_iota(jnp.int32, shape, 1) // lane_div)

        cin, n_state = wb5_ref.shape[3], half // gpb
        m_b = own_group((LANES, half), cin, n_state)
        m_c = own_group((half, LANES), n_state, cin)
        m_f = own_group((LANES, LANES), cin, cin)
        for gb in range(n_blocks):
            grp = slice(gpb * gb, gpb * (gb + 1))
            for ri in range(2):
                for i in range(r_blk):
                    piece = wb5_ref[ri, i, grp].reshape(LANES, LANES)
                    wide = jnp.concatenate([piece] * (half // LANES), axis=1)
                    wb_ref[gb, LANES * i:LANES * (i + 1), half * ri:half * (ri + 1)] = (
                        jnp.where(m_b, wide, 0.0).astype(BF16))
                    piece = wc5_ref[ri, i, grp].reshape(half, LANES)
                    wc_ref[gb, half * ri:half * (ri + 1), LANES * i:LANES * (i + 1)] = (
                        jnp.where(m_c, piece, 0.0).astype(BF16))
            for i in range(r_blk):
                for j in range(r_blk):
                    piece = wf5_ref[i, j, grp].reshape(LANES, LANES)
                    wf_ref[gb, LANES * i:LANES * (i + 1), LANES * j:LANES * (j + 1)] = (
                        jnp.where(m_f, piece, 0.0).astype(BF16))

    pitch = us_ref.shape[2] // n_k
    for b in range(batch):
        for j in range(n_blocks):
            for i in range(r_blk):
                us_ref[j, i, pl.ds(b, n_k, stride=pitch), :] = (
                    u_ref[j, b, pl.ds(i, n_k, stride=r_blk), :])

    def slab_rows(j, i):
        return jnp.concatenate([us_ref[j, i, pitch * k:pitch * k + batch, :] for k in range(n_k)], axis=0)

    u_slabs = [[slab_rows(j, i) for i in range(r_blk)] for j in range(n_blocks)]
    for gb in range(n_blocks):
        lhs = jnp.concatenate(u_slabs[gb], axis=-1).astype(BF16)
        xs = xs_ref
        xs[...] = jnp.dot(lhs, wb_ref[gb], preferred_element_type=F32)
        cols = slice(half * gb, half * (gb + 1))
        a_re = jnp.broadcast_to(are_ref[:, cols], (batch, half))
        a_im = jnp.broadcast_to(aim_ref[:, cols], (batch, half))

        def step(k, carry, xs=xs, a_re=a_re, a_im=a_im):
            h_re, h_im = carry
            rows = pl.ds(pl.multiple_of(k * batch, batch), batch)
            n_re = a_re * h_re - a_im * h_im + xs[rows, :half]
            n_im = a_re * h_im + a_im * h_re + xs[rows, half:]
            xs[rows, :half] = h_re
            xs[rows, half:] = h_im
            return n_re, n_im

        h_re, h_im = lax.fori_loop(0, n_k, step, (hre_ref[:, cols], him_ref[:, cols]), unroll=unroll)
        hre_ref[:, cols] = h_re
        him_ref[:, cols] = h_im
        y = (jnp.dot(xs[...].astype(BF16), wc_ref[gb], preferred_element_type=F32)
             + jnp.dot(lhs, wf_ref[gb], preferred_element_type=F32))
        for i in range(r_blk):
            y_ref[i, :, LANES * gb:LANES * (gb + 1)] = y[:, LANES * i:LANES * (i + 1)]

    rows = r_blk * n_k * batch
    u = jnp.concatenate([jnp.concatenate(u_slabs[j], axis=0) for j in range(n_blocks)], axis=-1)
    y = y_ref[...].reshape(rows, n_blocks * LANES) + d_ref[...] * u
    yg = jax.nn.gelu(y)
    glu = jnp.dot(yg.astype(BF16), w_glu_ref[...], preferred_element_type=F32) + b_glu_ref[...]
    out = _rms(yg * jax.nn.sigmoid(glu), onorm_ref[...])
    for j in range(n_blocks):
        for i in range(r_blk):
            for k in range(n_k):
                r0 = (i * n_k + k) * batch
                us_ref[j, i, pitch * k:pitch * k + batch, :] = out[r0:r0 + batch, LANES * j:LANES * (j + 1)]
    for b in range(batch):
        for j in range(n_blocks):
            for i in range(r_blk):
                tok = pl.ds(i, n_k, stride=r_blk)
                gs = gs_ref[j, b, tok, :]
                o = us_ref[j, i, pl.ds(b, n_k, stride=pitch), :]
                mix_ref[j, b, tok, :] = o * (gs * jax.nn.sigmoid(gs))


def _ssm_call(u, gs, h0re, h0im, wts, *, steps, unroll):
    n_blocks, batch, seq, _ = u.shape
    n_state = h0re.shape[1]
    r_blk = wts['wb5'].shape[1]
    width = 2 * n_state // n_blocks
    assert steps % r_blk == 0 and wts['wb5'].shape[3] * GROUPS_PER_BLOCK == LANES
    blk = pl.BlockSpec((n_blocks, batch, steps, LANES), lambda i: (0, 0, i, 0))
    names = ['a_re', 'a_im', 'wb5', 'wc5', 'wf5', 'ssm_d', 'w_glu', 'b_glu', 'onorm_ssm']
    n_rows = steps // r_blk * batch
    return pl.pallas_call(
        functools.partial(_ssm_kernel, batch=batch, steps=steps, unroll=unroll),
        out_shape=[jax.ShapeDtypeStruct(u.shape, F32),
                   jax.ShapeDtypeStruct((batch, n_state), F32),
                   jax.ShapeDtypeStruct((batch, n_state), F32)],
        grid=(seq // steps,),
        in_specs=[blk, blk, _full(h0re.shape), _full(h0im.shape)] + [_full(wts[n].shape) for n in names],
        out_specs=[blk, _full((batch, n_state)), _full((batch, n_state))],
        scratch_shapes=[pltpu.VMEM((n_blocks, r_blk, steps // r_blk * (batch + SUBLANES), LANES), F32),
                        pltpu.VMEM((n_rows, width), F32),
                        pltpu.VMEM((r_blk, n_rows, n_blocks * LANES), F32),
                        pltpu.VMEM((n_blocks, r_blk * LANES, width), BF16),
                        pltpu.VMEM((n_blocks, width, r_blk * LANES), BF16),
                        pltpu.VMEM((n_blocks, r_blk * LANES, r_blk * LANES), BF16)],
        compiler_params=pltpu.CompilerParams(
            dimension_semantics=("arbitrary",), vmem_limit_bytes=VMEM_LIMIT),
        name="ssm",
    )(u, gs, h0re, h0im, *[wts[n] for n in names])


def _attn_kernel(*refs, tq, tk, causal, fuse_out, hg):
    if causal:
        q_ref, kf_ref, vf_ref = refs[:3]
        rest = refs[3:]
    else:
        q_ref, kf_ref, vf_ref, kl_ref, vl_ref = refs[:5]
        rest = refs[5:]
    if fuse_out:
        gm_ref, ms_ref, x_ref, onorm_ref, w_out_ref, y_ref = rest[:6]
    else:
        gm_ref, onorm_ref, y_ref = rest[:3]
    s_full, ml_scr, s_last, m_scr, acc_scr = rest[-5:]
    i = pl.program_id(1)
    if causal:
        n_full = (i * tq) // tk
        def chunk_mask(width):
            qc = lax.broadcasted_iota(jnp.int32, (tq, width), 0) // CHUNK + (width - tq) // CHUNK
            return qc >= lax.broadcasted_iota(jnp.int32, (tq, width), 1) // CHUNK

        odd = i % 2 == 1
        lasts = [(odd, tk, pl.ds(pl.multiple_of(n_full * tk, tk), tk), chunk_mask(tk)),
                 (jnp.logical_not(odd), tq, pl.ds(pl.multiple_of(i * tq, tq), tq), chunk_mask(tq))]
        lasts = [(cond, width, functools.partial(lambda rows, c: kf_ref[0, rows, c], rows),
                  functools.partial(lambda rows, c: vf_ref[0, rows, c], rows), mask)
                 for cond, width, rows, mask in lasts]
    else:
        n_full = kf_ref.shape[1] // tk
        lasts = [(None, kl_ref.shape[1], lambda c: kl_ref[0, :, c], lambda c: vl_ref[0, :, c], None)]
    dn = (((1,), (1,)), ((), ()))
    lane = lax.broadcasted_iota(jnp.int32, (tq, LANES), 1)

    def lane_tiles(s):
        return [s[:, LANES * c:LANES * (c + 1)] for c in range(s.shape[1] // LANES)]

    def probs(s, m_rep):
        if s.shape[1] % LANES == 0:
            p = jnp.concatenate([jnp.exp2(t - m_rep) for t in lane_tiles(s)], axis=-1)
        else:
            p = jnp.exp2(s - m_rep[:, :1])
        return p.astype(BF16)

    pairs = []
    for g in range(MLA_HEADS // hg):
        heads = [(hl, g * hg + hl) for hl in range(hg)]
        cols = {hd: slice(LANES * hd, LANES * (hd + 1)) for _, hd in heads}

        ml_scr[...] = jnp.full(ml_scr.shape, -jnp.inf, F32)

        def a_step(jt, carry, heads=heads, cols=cols):
            rows = pl.ds(pl.multiple_of(jt * tk, tk), tk)
            for hl, hd in heads:
                s = lax.dot_general(q_ref[0, :, cols[hd]], kf_ref[0, rows, cols[hd]], dn,
                                    preferred_element_type=F32)
                s_full[hl, jt] = s
                ml_scr[hl] = functools.reduce(jnp.maximum, lane_tiles(s), ml_scr[hl])
            return carry

        lax.fori_loop(0, n_full, a_step, 0)

        def last_a(width, k_last, mask, heads=heads, cols=cols):
            for hl, hd in heads:
                s = lax.dot_general(q_ref[0, :, cols[hd]], k_last(cols[hd]), dn,
                                    preferred_element_type=F32)
                if mask is not None:
                    s = jnp.where(mask, s, -jnp.inf)
                s_last[hl, :, :width] = s
                if width % LANES == 0:
                    m = jnp.max(functools.reduce(jnp.maximum, lane_tiles(s), ml_scr[hl]),
                                axis=-1, keepdims=True)
                else:
                    m = jnp.maximum(jnp.max(s, axis=-1, keepdims=True),
                                    jnp.max(ml_scr[hl], axis=-1, keepdims=True))
                m_scr[hl] = jnp.broadcast_to(m, (tq, LANES))
                acc_scr[hl] = jnp.zeros((tq, LANES), F32)

        for cond, width, k_last, _, mask in lasts:
            run = functools.partial(last_a, width, k_last, mask)
            run() if cond is None else pl.when(cond)(run)

        def b_step(jt, carry, heads=heads, cols=cols):
            rows = pl.ds(pl.multiple_of(jt * tk, tk), tk)
            for hl, hd in heads:
                p = probs(s_full[hl, jt], m_scr[hl])
                acc_scr[hl] += jnp.dot(p, vf_ref[0, rows, cols[hd]], preferred_element_type=F32)
            return carry

        lax.fori_loop(0, n_full, b_step, 0)

        def last_b(width, v_last, heads=heads, cols=cols):
            for hl, hd in heads:
                p = probs(s_last[hl, :, :width], m_scr[hl])
                acc = acc_scr[hl] + jnp.dot(p, v_last(cols[hd]), preferred_element_type=F32)
                ones_col = V_DIM if hd % 2 == 0 else 0
                acc_scr[hl] = acc / acc[:, ones_col:ones_col + 1]

        for cond, width, _, v_last, _ in lasts:
            run = functools.partial(last_b, width, v_last)
            run() if cond is None else pl.when(cond)(run)

        for e in range(0, hg, 2):
            pairs.append(jnp.where(lane < V_DIM, acc_scr[e], acc_scr[e + 1]))

    attn = jnp.concatenate(pairs, axis=-1)
    gm = gm_ref[0]
    mla = _rms(attn, onorm_ref[...]) * (gm * jax.nn.sigmoid(gm))
    if fuse_out:
        mix_ssm = [ms_ref[j, 0] for j in range(ms_ref.shape[0])]
        mix = jnp.concatenate(mix_ssm + [mla], axis=-1).astype(BF16)
        y_ref[0] = x_ref[0] + jnp.dot(mix, w_out_ref[...], preferred_element_type=F32)
    else:
        y_ref[0] = mla.astype(BF16)


def _out_kernel(ms_ref, mla_ref, x_ref, w_out_ref, y_ref):
    bb, tq, d = x_ref.shape
    rows = bb * tq
    mix = jnp.concatenate([ms_ref[j].reshape(rows, LANES).astype(BF16) for j in range(ms_ref.shape[0])]
                          + [mla_ref[...].reshape(rows, mla_ref.shape[2])], axis=-1)
    y = x_ref[...].reshape(rows, d) + jnp.dot(mix, w_out_ref[...], preferred_element_type=F32)
    y_ref[...] = y.reshape(bb, tq, d)


def _out_call(mix_ssm, mla, x, wts, *, bb):
    b, s, d = x.shape
    row = lambda j: (j, 0, 0)
    return pl.pallas_call(
        _out_kernel,
        out_shape=jax.ShapeDtypeStruct((b, s, d), F32),
        grid=(b // bb,),
        in_specs=[pl.BlockSpec((mix_ssm.shape[0], bb, s, LANES), lambda j: (0, j, 0, 0)),
                  pl.BlockSpec((bb, s, mla.shape[2]), row), pl.BlockSpec((bb, s, d), row),
                  _full(wts['w_out'].shape)],
        out_specs=pl.BlockSpec((bb, s, d), row),
        compiler_params=pltpu.CompilerParams(
            dimension_semantics=("arbitrary",), vmem_limit_bytes=VMEM_LIMIT),
        name="outproj",
    )(mix_ssm, mla, x, wts['w_out'])


def _attn_call(q, k_full, v_full, last, gm, mix_ssm, x, wts, *, tq, tk, hg):
    b, s, d = x.shape
    t_full = k_full.shape[1]
    ssm_w = gm.shape[2]
    causal = last is None
    row = lambda j, i: (j, i, 0)
    res = lambda j, i: (j, 0, 0)
    if causal:
        assert tk == 2 * tq and t_full % tk == 0 and tq % CHUNK == 0
        last, last_specs, last_len, max_full = (), [], tk, t_full // tk - 1
    else:
        last_len, max_full = last[0].shape[1], t_full // tk
        last_specs = [pl.BlockSpec((1, last_len, HEAD_W), res)] * 2
    scratch = [pltpu.VMEM((hg, max_full, tq, tk), F32), pltpu.VMEM((hg, tq, LANES), F32),
               pltpu.VMEM((hg, tq, last_len), F32), pltpu.VMEM((hg, tq, LANES), F32),
               pltpu.VMEM((hg, tq, LANES), F32)]
    kv_specs = [pl.BlockSpec((1, tq, HEAD_W), row),
                pl.BlockSpec((1, t_full, HEAD_W), res), pl.BlockSpec((1, t_full, HEAD_W), res),
                *last_specs, pl.BlockSpec((1, tq, ssm_w), row)]
    if causal:
        tail_args = (mix_ssm, x, wts['onorm_mla'], wts['w_out'])
        tail_specs = [pl.BlockSpec((mix_ssm.shape[0], 1, tq, LANES), lambda j, i: (0, j, i, 0)),
                      pl.BlockSpec((1, tq, d), row),
                      _full(wts['onorm_mla'].shape), _full(wts['w_out'].shape)]
        out_shape, out_spec = jax.ShapeDtypeStruct((b, s, d), F32), pl.BlockSpec((1, tq, d), row)
    else:
        tail_args, tail_specs = (wts['onorm_mla'],), [_full(wts['onorm_mla'].shape)]
        out_shape, out_spec = jax.ShapeDtypeStruct((b, s, ssm_w), BF16), pl.BlockSpec((1, tq, ssm_w), row)
    return pl.pallas_call(
        functools.partial(_attn_kernel, tq=tq, tk=tk, causal=causal, fuse_out=causal, hg=hg),
        out_shape=out_shape,
        grid=(b, s // tq),
        in_specs=kv_specs + tail_specs,
        out_specs=out_spec,
        scratch_shapes=scratch,
        compiler_params=pltpu.CompilerParams(
            dimension_semantics=("arbitrary", "arbitrary"), vmem_limit_bytes=VMEM_LIMIT),
        name="attn",
    )(q, k_full, v_full, *last, gm, *tail_args)


def _head_block_cols(w, pieces):
    k = w.shape[0]
    w3 = w.reshape(k, MLA_HEADS, w.shape[1] // MLA_HEADS)
    cols = [w3[:, :, p[0]:p[0] + p[1]] if isinstance(p, tuple) else jnp.zeros((k, MLA_HEADS, p), w.dtype)
            for p in pieces]
    return jnp.concatenate(cols, axis=-1).reshape(k, HEAD_W)


def _zoh_powers(lr, li, dt, n):
    mag = jnp.exp(lr * dt)
    ar, ai = mag * jnp.cos(li * dt), mag * jnp.sin(li * dt)
    pw = [(jnp.ones_like(ar), jnp.zeros_like(ai))]
    for _ in range(n):
        pr, pi = pw[-1]
        pw.append((pr * ar - pi * ai, pr * ai + pi * ar))
    return pw


def _disc_kernel(arow_ref, acol_ref, ldt_ref, bt_ref, cd_ref, ct_ref,
                 apr_ref, api_ref, wb_ref, wc_ref, wf_ref):
    r_blk, groups, cin = wb_ref.shape[1], wb_ref.shape[2], wb_ref.shape[3]
    n_state = acol_ref.shape[2]
    gpb = LANES // cin
    lr, li = arow_ref[0], arow_ref[1]
    dt = jnp.exp(ldt_ref[...])
    pw = _zoh_powers(lr, li, dt, r_blk)
    ar, ai = pw[1]
    den = lr * lr + li * li
    kr = ((ar - 1.0) * lr + ai * li) / den
    ki = (ai * lr - (ar - 1.0) * li) / den
    apr_ref[...], api_ref[...] = pw[r_blk]

    pwc = _zoh_powers(acol_ref[0], acol_ref[1], dt, r_blk)
    for j in range(r_blk):
        pr, pi = pwc[j + 1]
        wc_ref[0, j] = ct_ref[0] * pr - ct_ref[1] * pi
        wc_ref[1, j] = -(ct_ref[0] * pi + ct_ref[1] * pr)

    b_r = kr * bt_ref[0] - ki * bt_ref[1]
    b_i = kr * bt_ref[1] + ki * bt_ref[0]
    for i in range(r_blk):
        pr, pi = pw[r_blk - 1 - i]
        wb_ref[0, i] = pr * b_r - pi * b_i
        wb_ref[1, i] = pr * b_i + pi * b_r

    nt = (((1,), (1,)), ((), ()))
    first_copy = lax.broadcasted_iota(jnp.int32, (LANES, LANES), 1) < n_state
    block = lambda v, n: v[gpb * n:gpb * (n + 1)].reshape(LANES, LANES)
    for i in range(r_blk):
        for j in range(i):
            wf_ref[i, j] = jnp.zeros(wf_ref.shape[2:], F32)
    for m in range(r_blk):
        pr, pi = pw[m]
        ca_r = cd_ref[0] * pr - cd_ref[1] * pi
        ca_i = cd_ref[0] * pi + cd_ref[1] * pr
        for n in range(groups // gpb):
            f_t = (lax.dot_general(jnp.where(first_copy, block(b_r, n), 0.0), block(ca_r, n), nt,
                                   precision=lax.Precision.HIGHEST, preferred_element_type=F32)
                   - lax.dot_general(jnp.where(first_copy, block(b_i, n), 0.0), block(ca_i, n), nt,
                                     precision=lax.Precision.HIGHEST, preferred_element_type=F32))
            for i in range(r_blk - m):
                wf_ref[i, i + m, gpb * n:gpb * (n + 1)] = f_t.reshape(gpb, cin, LANES)


def _prepare_weights(norm_in, w_in, ssm_a_re, ssm_a_im, ssm_log_dt, ssm_b_re, ssm_b_im, ssm_c_re,
                     ssm_c_im, ssm_d, w_glu, b_glu, q_lora_norm, kv_lora_norm, w_uq, w_ukv,
                     q_nope_norm, k_nope_norm, q_rope_norm, k_rope_norm, out_norm_ssm,
                     out_norm_mla, w_out):
    groups, n_state = ssm_a_re.shape
    ssm_w = groups * SSM_GROUP
    row = lambda v: v.reshape(1, -1).astype(F32)
    tail = LANES - ROPE_DIM - NOPE_DIM

    o_kr = 2 * ssm_w + Q_LORA + KV_LORA
    w_in_p = jnp.concatenate(
        [w_in[:, :o_kr + ROPE_DIM], jnp.zeros((w_in.shape[0], LANES - ROPE_DIM), w_in.dtype),
         w_in[:, o_kr + ROPE_DIM:]], axis=1).astype(BF16)

    w_uq_p = _head_block_cols(w_uq, [(NOPE_DIM, ROPE_DIM), (0, NOPE_DIM), tail]).astype(BF16)
    w_uk_p = _head_block_cols(w_ukv, [ROPE_DIM, (0, NOPE_DIM), tail])
    w_v_lo = _head_block_cols(w_ukv, [(NOPE_DIM, V_DIM), LANES - V_DIM])
    w_v_hi = _head_block_cols(w_ukv, [LANES - V_DIM, (NOPE_DIM, V_DIM)])
    odd_head = (np.arange(HEAD_W) // LANES) % 2 == 1
    w_ukv_p = jnp.concatenate([w_uk_p, jnp.where(odd_head[None, :], w_v_hi, w_v_lo)], axis=1).astype(BF16)

    scale = (NOPE_DIM + ROPE_DIM) ** -0.5 * np.log2(np.e)
    vones = np.zeros((MLA_HEADS, LANES), np.float32)
    vones[0::2, V_DIM] = 1.0
    vones[1::2, 0] = 1.0
    vones = jnp.asarray(vones.reshape(1, HEAD_W))
    zeros = lambda n: jnp.zeros((n,), F32)
    gq =jnp.concatenate([q_rope_norm, q_nope_norm, zeros(tail)]) * scale
    gkr = jnp.concatenate([k_rope_norm, zeros(LANES - ROPE_DIM)])
    gkn = jnp.concatenate([zeros(ROPE_DIM), k_nope_norm, zeros(tail)])

    seg = np.zeros((LANES, LANES), np.float32)
    seg[:ROPE_DIM, :ROPE_DIM] = 1.0 / ROPE_DIM
    seg[ROPE_DIM:ROPE_DIM + NOPE_DIM, ROPE_DIM:ROPE_DIM + NOPE_DIM] = 1.0 / NOPE_DIM
    seg[ROPE_DIM + NOPE_DIM:, ROPE_DIM + NOPE_DIM:] = 1.0 / tail
    seg = jnp.asarray(np.kron(np.eye(2, dtype=np.float32), seg), BF16)
    place = jnp.asarray(np.eye(ROPE_DIM, LANES, dtype=np.float32), BF16)

    r_blk = SSM_STEPS_PER_SCAN
    lane_rep = lambda x: jnp.tile(x, (1,) * (x.ndim - 1) + (LANES // x.shape[-1],))
    a2 = jnp.stack([ssm_a_re, ssm_a_im]).astype(F32)
    b2 = jnp.stack([ssm_b_re, ssm_b_im]).astype(F32)
    c2 = jnp.stack([ssm_c_re, ssm_c_im]).astype(F32)
    ldt = ssm_log_dt.astype(F32)
    gl = jax.ShapeDtypeStruct((groups, 1, LANES), F32)
    a_re, a_im, wb5, wc5, wf5 = pl.pallas_call(
        _disc_kernel,
        out_shape=[gl, gl,
                   jax.ShapeDtypeStruct((2, r_blk, groups, SSM_GROUP, LANES), F32),
                   jax.ShapeDtypeStruct((2, r_blk, groups, n_state, LANES), F32),
                   jax.ShapeDtypeStruct((r_blk, r_blk, groups, SSM_GROUP, LANES), F32)],
        name="disc")(
        lane_rep(a2)[:, :, None, :], jnp.broadcast_to(a2[..., None], a2.shape + (LANES,)),
        jnp.broadcast_to(ldt[:, None, None], (groups, 1, LANES)),
        lane_rep(jnp.swapaxes(b2, 2, 3)), lane_rep(c2), lane_rep(jnp.swapaxes(c2, 2, 3)))
    a_re, a_im = a_re[:, 0, :n_state], a_im[:, 0, :n_state]

    return dict(
        ssm_w=ssm_w,
        norm_in=row(norm_in), w_in=w_in_p, qln=row(q_lora_norm), kvln=row(kv_lora_norm),
        w_uq=w_uq_p, w_ukv=w_ukv_p, gq=row(gq), gkr=row(gkr), gkn=row(gkn), seg=seg, vones=vones,
        place=place, a_re=row(a_re), a_im=row(a_im), wb5=wb5, wc5=wc5, wf5=wf5,
        ssm_d=row(ssm_d), w_glu=w_glu.astype(BF16), b_glu=row(b_glu), onorm_ssm=row(out_norm_ssm),
        onorm_mla=row(out_norm_mla), w_out=w_out.astype(BF16))


def _rope_tables(start, n):
    half = ROPE_DIM // 2
    inv = ROPE_THETA ** (-jnp.arange(half, dtype=F32) / half)
    ang = (start + jnp.arange(n)).astype(F32)[:, None] * inv[None, :]
    cos, sin = jnp.cos(ang), jnp.sin(ang)
    cos_t = jnp.concatenate([cos, cos, jnp.ones((n, LANES - ROPE_DIM), F32)], axis=1)
    sin_a = jnp.concatenate([-sin, jnp.zeros((n, LANES - half), F32)], axis=1)
    sin_b = jnp.concatenate([jnp.zeros((n, half), F32), sin, jnp.zeros((n, LANES - ROPE_DIM), F32)],
                            axis=1)
    return cos_t, sin_a, sin_b


def _mixer(x, pos0, h0re, h0im, past, wts, *, proj_bb, proj_tq, ssm_steps, attn_tk):
    u, gs, gm, q, k, v, ckv, kr = _proj_call(x, _rope_tables(pos0, x.shape[1]), wts,
                                             bb=proj_bb, tq=proj_tq)
    mix_ssm, hre, him = _ssm_call(u, gs, h0re, h0im, wts, steps=ssm_steps, unroll=True)
    if past is None:
        y = _attn_call(q, k, v, None, gm, mix_ssm, x, wts, tq=attn_tk // 2, tk=attn_tk, hg=MLA_HEADS)
    else:
        k_past, v_past = _expand_call(*past, wts, tq=512)
        mla = _attn_call(q, k_past, v_past, (k, v), gm, mix_ssm, x, wts, tq=x.shape[1], tk=attn_tk,
                         hg=MLA_HEADS)
        y = _out_call(mix_ssm, mla, x, wts, bb=proj_bb)
    if proj_bb == 1:
        kr = jnp.swapaxes(kr, 1, 2)
    return y, ckv, kr, hre, him


def kernel(x_prompt, x_sample, cache_ckv, cache_krope, state_ssm_re, state_ssm_im, norm_in, w_in, ssm_a_re, ssm_a_im, ssm_log_dt, ssm_b_re, ssm_b_im, ssm_c_re, ssm_c_im, ssm_d, w_glu, b_glu, q_lora_norm, kv_lora_norm, w_uq, w_ukv, q_nope_norm, k_nope_norm, q_rope_norm, k_rope_norm, out_norm_ssm, out_norm_mla, w_out):
    depth = norm_in.shape[0]
    assert depth == 1, "single mixer layer"
    params = (norm_in, w_in, ssm_a_re, ssm_a_im, ssm_log_dt, ssm_b_re, ssm_b_im, ssm_c_re, ssm_c_im,
              ssm_d, w_glu, b_glu, q_lora_norm, kv_lora_norm, w_uq, w_ukv, q_nope_norm, k_nope_norm,
              q_rope_norm, k_rope_norm, out_norm_ssm, out_norm_mla, w_out)
    drop_depth = lambda a: a.reshape(a.shape[1:])
    wts = _prepare_weights(*[drop_depth(p) for p in params])
    groups, n_state = ssm_a_re.shape[1:]
    bp, sp, _ = x_prompt.shape
    bs, ss, _ = x_sample.shape
    past_len = cache_ckv.shape[2]

    zero_state = jnp.zeros((bp, groups * n_state), F32)
    yp, ckv_p, kr_p, re_p, im_p = _mixer(
        x_prompt, 0, zero_state, zero_state, None, wts,
        proj_bb=1, proj_tq=512, ssm_steps=64, attn_tk=512)
    ys, ckv_s, kr_s, re_s, im_s = _mixer(
        x_sample, past_len,
        state_ssm_re.reshape(bs, groups * n_state), state_ssm_im.reshape(bs, groups * n_state),
        (drop_depth(cache_ckv), jnp.swapaxes(drop_depth(cache_krope), 1, 2)), wts,
        proj_bb=bs // 2, proj_tq=ss, ssm_steps=ss, attn_tk=past_len)

    st = lambda h, bb: h.reshape(1, bb, groups, n_state)
    return (yp, ys, ckv_p[None], kr_p[None], st(re_p, bp), st(im_p, bp),
            ckv_s[None], kr_s[None], st(re_s, bs), st(im_s, bs))
```

```python
import functools

import numpy as np
import jax
import jax.numpy as jnp
from jax import lax
from jax.experimental import pallas as pl
from jax.experimental.pallas import tpu as pltpu

F32 = jnp.float32
BF16 = jnp.bfloat16

CHUNK = 64
SSM_GROUP = 16
SSM_STATE = 64
MLA_HEADS = 8
NOPE_DIM = 64
ROPE_DIM = 32
V_DIM = 64
Q_LORA = 256
KV_LORA = 128
ROPE_THETA = 10000.0
EPS = 1e-6

LANES = 128
SUBLANES = 8
VT_ROWS = 80
HEAD_W = MLA_HEADS * LANES
GROUPS_PER_BLOCK = 8
SSM_STEPS_PER_SCAN = 4
VMEM_LIMIT = 56 * 1024 * 1024


def _rms(x, gain):
    return x * lax.rsqrt(jnp.mean(x * x, axis=-1, keepdims=True) + EPS) * gain


def _seg_rms(x, seg):
    ms = jnp.dot((x * x).astype(BF16), seg, preferred_element_type=F32)
    return x * lax.rsqrt(ms + EPS)


def _rope_block(x, cos_t, sin_a, sin_b):
    return (x * cos_t + pltpu.roll(x, LANES - ROPE_DIM // 2, 1) * sin_a
            + pltpu.roll(x, ROPE_DIM // 2, 1) * sin_b)


def _expand_kv(ckv, kr_blk, w_ukv_ref, gkn_ref, seg_ref, vones_ref, k_ref, v_ref):
    kv = jnp.dot(ckv.astype(BF16), w_ukv_ref[...], preferred_element_type=F32)
    gkn = gkn_ref[...]
    blk3 = k_ref.shape[:2] + (LANES,)
    for p in range(MLA_HEADS // 2):
        kn = _seg_rms(kv[:, 2 * LANES * p:2 * LANES * (p + 1)], seg_ref[...])
        for j in range(2):
            h = 2 * p + j
            blk = kn[:, LANES * j:LANES * (j + 1)] * gkn + kr_blk
            k_ref[:, :, LANES * h:LANES * (h + 1)] = blk.astype(BF16).reshape(blk3)
    if v_ref is not None:
        v_ref[...] = (kv[:, HEAD_W:] + vones_ref[...]).astype(BF16).reshape(v_ref.shape)


def _proj_kernel(x_ref, cos_ref, sa_ref, sb_ref, norm_in_ref, w_in_ref, qln_ref, kvln_ref,
                 w_uq_ref, w_ukv_ref, gq_ref, gkr_ref, gkn_ref, seg_ref, vones_ref,
                 u_ref, gs_ref, gm_ref, q_ref, k_ref, v_ref, ckv_ref, kr_ref, *, ssm_w, transposed):
    bb, tq, d = x_ref.shape
    x = x_ref[...].reshape(bb * tq, d)
    h = _rms(x, norm_in_ref[...])
    z = jnp.dot(h.astype(BF16), w_in_ref[...], preferred_element_type=F32)
    o = 0
    for dst in (u_ref, gs_ref):
        for j in range(ssm_w // LANES):
            dst[j] = z[:, o + LANES * j:o + LANES * (j + 1)].reshape(bb, tq, LANES)
        o += ssm_w
    c_q = z[:, o:o + Q_LORA]
    o += Q_LORA
    c_kv = z[:, o:o + KV_LORA]
    o += KV_LORA
    kr_raw = z[:, o:o + LANES]
    o += LANES
    gm_ref[...] = z[:, o:].reshape(gm_ref.shape)

    per_row = lambda t_ref: jnp.concatenate([t_ref[...]] * bb, axis=0)
    cos_t, sin_a, sin_b = per_row(cos_ref), per_row(sa_ref), per_row(sb_ref)
    seg = seg_ref[...]

    kr_ms = jnp.dot((kr_raw * kr_raw).astype(BF16), seg[:LANES, :LANES], preferred_element_type=F32)
    kr_blk = _rope_block(kr_raw * lax.rsqrt(kr_ms + EPS) * gkr_ref[...], cos_t, sin_a, sin_b)
    if transposed:
        kr_ref[0] = kr_blk.T[:ROPE_DIM]
    else:
        kr_ref[...] = kr_blk[:, :ROPE_DIM].reshape(kr_ref.shape)

    q = jnp.dot(_rms(c_q, qln_ref[...]).astype(BF16), w_uq_ref[...], preferred_element_type=F32)
    gq = gq_ref[...]
    for p in range(MLA_HEADS // 2):
        qn = _seg_rms(q[:, 2 * LANES * p:2 * LANES * (p + 1)], seg)
        for j in range(2):
            hd = 2 * p + j
            blk = _rope_block(qn[:, LANES * j:LANES * (j + 1)] * gq, cos_t, sin_a, sin_b)
            q_ref[:, :, LANES * hd:LANES * (hd + 1)] = blk.astype(BF16).reshape(bb, tq, LANES)

    ckv = _rms(c_kv, kvln_ref[...])
    ckv_ref[...] = ckv.reshape(ckv_ref.shape)
    if transposed:
        _expand_kv(ckv, kr_blk, w_ukv_ref, gkn_ref, seg_ref, None, k_ref, None)
        v_t = jnp.dot(vones_ref[...], ckv.T.astype(BF16), preferred_element_type=F32)
        ones_row = lax.broadcasted_iota(jnp.int32, v_t.shape, 0) % VT_ROWS == V_DIM
        v_ref[0, 0] = jnp.where(ones_row, 1.0, v_t).astype(BF16)
    else:
        _expand_kv(ckv, kr_blk, w_ukv_ref, gkn_ref, seg_ref, vones_ref, k_ref, v_ref)


def _expand_kernel(ckv_ref, kr_ref, w_ukv_ref, gkn_ref, seg_ref, vones_ref, place_ref, k_ref, v_ref):
    kr_blk = lax.dot_general(kr_ref[0].astype(BF16), place_ref[...], (((0,), (0,)), ((), ())),
                             preferred_element_type=F32)
    _expand_kv(ckv_ref[0], kr_blk, w_ukv_ref, gkn_ref, seg_ref, vones_ref, k_ref, v_ref)


def _full(shape):
    n = len(shape)
    return pl.BlockSpec(shape, lambda *_: (0,) * n)


def _proj_call(x, tables, wts, *, bb, tq):
    b, s, d = x.shape
    ssm_w = wts['ssm_w']
    cos_t, sin_a, sin_b = tables
    transposed = bb == 1
    w_kv, v_aux = (wts['w_uk'], wts['wvt']) if transposed else (wts['w_ukv'], wts['vones'])
    grid = (s // tq, b // bb)
    row = lambda i, j: (j, i, 0)
    tab = pl.BlockSpec((tq, LANES), lambda i, j: (i, 0))
    in_specs = [pl.BlockSpec((bb, tq, d), row), tab, tab, tab,
                _full(wts['norm_in'].shape), _full(wts['w_in'].shape), _full(wts['qln'].shape),
                _full(wts['kvln'].shape), _full(wts['w_uq'].shape), _full(w_kv.shape),
                _full(wts['gq'].shape), _full(wts['gkr'].shape), _full(wts['gkn'].shape),
                _full(wts['seg'].shape), _full(v_aux.shape)]
    slabs = ssm_w // LANES
    tb_spec = pl.BlockSpec((slabs, bb, tq, LANES), lambda i, j: (0, j, i, 0))
    kr_shape, kr_spec = (((b, ROPE_DIM, s), pl.BlockSpec((1, ROPE_DIM, tq), lambda i, j: (j, 0, i)))
                         if transposed else
                         ((b, s, ROPE_DIM), pl.BlockSpec((bb, tq, ROPE_DIM), row)))
    v_shape, v_spec = (((b, s // tq, MLA_HEADS * VT_ROWS, tq),
                        pl.BlockSpec((1, 1, MLA_HEADS * VT_ROWS, tq), lambda i, j: (j, i, 0, 0)))
                       if transposed else
                       ((b, s, HEAD_W), pl.BlockSpec((bb, tq, HEAD_W), row)))
    out_shape = [jax.ShapeDtypeStruct((slabs, b, s, LANES), F32),
                 jax.ShapeDtypeStruct((slabs, b, s, LANES), F32),
                 jax.ShapeDtypeStruct((b, s, ssm_w), F32),
                 jax.ShapeDtypeStruct((b, s, HEAD_W), BF16),
                 jax.ShapeDtypeStruct((b, s, HEAD_W), BF16),
                 jax.ShapeDtypeStruct(v_shape, BF16),
                 jax.ShapeDtypeStruct((b, s, KV_LORA), F32),
                 jax.ShapeDtypeStruct(kr_shape, F32)]
    out_specs = [tb_spec, tb_spec,
                 pl.BlockSpec((bb, tq, ssm_w), row),
                 pl.BlockSpec((bb, tq, HEAD_W), row), pl.BlockSpec((bb, tq, HEAD_W), row), v_spec,
                 pl.BlockSpec((bb, tq, KV_LORA), row), kr_spec]
    return pl.pallas_call(
        functools.partial(_proj_kernel, ssm_w=ssm_w, transposed=transposed),
        out_shape=out_shape, grid=grid, in_specs=in_specs, out_specs=out_specs,
        compiler_params=pltpu.CompilerParams(
            dimension_semantics=("arbitrary", "arbitrary"), vmem_limit_bytes=VMEM_LIMIT),
        name="proj",
    )(x, cos_t, sin_a, sin_b, wts['norm_in'], wts['w_in'], wts['qln'], wts['kvln'], wts['w_uq'],
      w_kv, wts['gq'], wts['gkr'], wts['gkn'], wts['seg'], v_aux)


def _expand_call(ckv, kr, wts, *, tq):
    b, t, _ = ckv.shape
    row = lambda j, i: (j, i, 0)
    return pl.pallas_call(
        _expand_kernel,
        out_shape=[jax.ShapeDtypeStruct((b, t, HEAD_W), BF16)] * 2,
        grid=(b, t // tq),
        in_specs=[pl.BlockSpec((1, tq, KV_LORA), row),
                  pl.BlockSpec((1, ROPE_DIM, tq), lambda j, i: (j, 0, i)),
                  _full(wts['w_ukv'].shape), _full(wts['gkn'].shape), _full(wts['seg'].shape),
                  _full(wts['vones'].shape), _full(wts['place'].shape)],
        out_specs=[pl.BlockSpec((1, tq, HEAD_W), row)] * 2,
        compiler_params=pltpu.CompilerParams(
            dimension_semantics=("arbitrary", "arbitrary"), vmem_limit_bytes=VMEM_LIMIT),
        name="expand",
    )(ckv, kr, wts['w_ukv'], wts['gkn'], wts['seg'], wts['vones'], wts['place'])


def _ssm_kernel(u_ref, gs_ref, h0re_ref, h0im_ref, are_ref, aim_ref, wb5_ref, wc5_ref, wf5_ref, d_ref,
                w_glu_ref, b_glu_ref, onorm_ref, mix_ref, hre_ref, him_ref, us_ref, xs_ref, y_ref,
                wb_ref, wc_ref, wf_ref, *, batch, steps, unroll):
    n_blocks, r_blk = wb_ref.shape[0], wb_ref.shape[1] // LANES
    half = wb_ref.shape[2] // 2
    gpb = wb5_ref.shape[2] // n_blocks
    n_k = steps // r_blk

    @pl.when(pl.program_id(0) == 0)
    def _():
        hre_ref[...] = h0re_ref[...]
        him_ref[...] = h0im_ref[...]

        def own_group(shape, row_div, lane_div):
            return (lax.broadcasted_iota(jnp.int32, shape, 0) // row_div
                    == lax.broadcasted_iota(jnp.int32, shape, 1) // lane_div)

        cin, n_state = wb5_ref.shape[3], half // gpb
        m_b = own_group((LANES, half), cin, n_state)
        m_c = own_group((half, LANES), n_state, cin)
        m_f = own_group((LANES, LANES), cin, cin)
        for gb in range(n_blocks):
            grp = slice(gpb * gb, gpb * (gb + 1))
            for ri in range(2):
                for i in range(r_blk):
                    piece = wb5_ref[ri, i, grp].reshape(LANES, LANES)
                    wide = jnp.concatenate([piece] * (half // LANES), axis=1)
                    wb_ref[gb, LANES * i:LANES * (i + 1), half * ri:half * (ri + 1)] = (
                        jnp.where(m_b, wide, 0.0).astype(BF16))
                    piece = wc5_ref[ri, i, grp].reshape(half, LANES)
                    wc_ref[gb, half * ri:half * (ri + 1), LANES * i:LANES * (i + 1)] = (
                        jnp.where(m_c, piece, 0.0).astype(BF16))
            for i in range(r_blk):
                for j in range(r_blk):
                    piece = wf5_ref[i, j, grp].reshape(LANES, LANES)
                    wf_ref[gb, LANES * i:LANES * (i + 1), LANES * j:LANES * (j + 1)] = (
                        jnp.where(m_f, piece, 0.0).astype(BF16))

    pitch = us_ref.shape[2] // n_k
    for b in range(batch):
        for j in range(n_blocks):
            for i in range(r_blk):
                us_ref[j, i, pl.ds(b, n_k, stride=pitch), :] = (
                    u_ref[j, b, pl.ds(i, n_k, stride=r_blk), :])

    def slab_rows(j, i):
        return jnp.concatenate([us_ref[j, i, pitch * k:pitch * k + batch, :] for k in range(n_k)], axis=0)

    u_slabs = [[slab_rows(j, i) for i in range(r_blk)] for j in range(n_blocks)]
    for gb in range(n_blocks):
        lhs = jnp.concatenate(u_slabs[gb], axis=-1).astype(BF16)
        xs = xs_ref
        xs[...] = jnp.dot(lhs, wb_ref[gb], preferred_element_type=F32)
        cols = slice(half * gb, half * (gb + 1))
        a_re = jnp.broadcast_to(are_ref[:, cols], (batch, half))
        a_im = jnp.broadcast_to(aim_ref[:, cols], (batch, half))

        def step(k, carry, xs=xs, a_re=a_re, a_im=a_im):
            h_re, h_im = carry
            rows = pl.ds(pl.multiple_of(k * batch, batch), batch)
            n_re = a_re * h_re - a_im * h_im + xs[rows, :half]
            n_im = a_re * h_im + a_im * h_re + xs[rows, half:]
            xs[rows, :half] = h_re
            xs[rows, half:] = h_im
            return n_re, n_im

        h_re, h_im = lax.fori_loop(0, n_k, step, (hre_ref[:, cols], him_ref[:, cols]), unroll=unroll)
        hre_ref[:, cols] = h_re
        him_ref[:, cols] = h_im
        y = (jnp.dot(xs[...].astype(BF16), wc_ref[gb], preferred_element_type=F32)
             + jnp.dot(lhs, wf_ref[gb], preferred_element_type=F32))
        for i in range(r_blk):
            y_ref[i, :, LANES * gb:LANES * (gb + 1)] = y[:, LANES * i:LANES * (i + 1)]

    rows = r_blk * n_k * batch
    u = jnp.concatenate([jnp.concatenate(u_slabs[j], axis=0) for j in range(n_blocks)], axis=-1)
    y = y_ref[...].reshape(rows, n_blocks * LANES) + d_ref[...] * u
    yg = jax.nn.gelu(y)
    glu = jnp.dot(yg.astype(BF16), w_glu_ref[...], preferred_element_type=F32) + b_glu_ref[...]
    out = _rms(yg * jax.nn.sigmoid(glu), onorm_ref[...])
    for j in range(n_blocks):
        for i in range(r_blk):
            for k in range(n_k):
                r0 = (i * n_k + k) * batch
                us_ref[j, i, pitch * k:pitch * k + batch, :] = out[r0:r0 + batch, LANES * j:LANES * (j + 1)]
    for b in range(batch):
        for j in range(n_blocks):
            for i in range(r_blk):
                tok = pl.ds(i, n_k, stride=r_blk)
                gs = gs_ref[j, b, tok, :]
                o = us_ref[j, i, pl.ds(b, n_k, stride=pitch), :]
                mix_ref[j, b, tok, :] = o * (gs * jax.nn.sigmoid(gs))


def _ssm_call(u, gs, h0re, h0im, wts, *, steps, unroll):
    n_blocks, batch, seq, _ = u.shape
    n_state = h0re.shape[1]
    r_blk = wts['wb5'].shape[1]
    width = 2 * n_state // n_blocks
    assert steps % r_blk == 0 and wts['wb5'].shape[3] * GROUPS_PER_BLOCK == LANES
    blk = pl.BlockSpec((n_blocks, batch, steps, LANES), lambda i: (0, 0, i, 0))
    names = ['a_re', 'a_im', 'wb5', 'wc5', 'wf5', 'ssm_d', 'w_glu', 'b_glu', 'onorm_ssm']
    n_rows = steps // r_blk * batch
    return pl.pallas_call(
        functools.partial(_ssm_kernel, batch=batch, steps=steps, unroll=unroll),
        out_shape=[jax.ShapeDtypeStruct(u.shape, F32),
                   jax.ShapeDtypeStruct((batch, n_state), F32),
                   jax.ShapeDtypeStruct((batch, n_state), F32)],
        grid=(seq // steps,),
        in_specs=[blk, blk, _full(h0re.shape), _full(h0im.shape)] + [_full(wts[n].shape) for n in names],
        out_specs=[blk, _full((batch, n_state)), _full((batch, n_state))],
        scratch_shapes=[pltpu.VMEM((n_blocks, r_blk, steps // r_blk * (batch + SUBLANES), LANES), F32),
                        pltpu.VMEM((n_rows, width), F32),
                        pltpu.VMEM((r_blk, n_rows, n_blocks * LANES), F32),
                        pltpu.VMEM((n_blocks, r_blk * LANES, width), BF16),
                        pltpu.VMEM((n_blocks, width, r_blk * LANES), BF16),
                        pltpu.VMEM((n_blocks, r_blk * LANES, r_blk * LANES), BF16)],
        compiler_params=pltpu.CompilerParams(
            dimension_semantics=("arbitrary",), vmem_limit_bytes=VMEM_LIMIT),
        name="ssm",
    )(u, gs, h0re, h0im, *[wts[n] for n in names])


def _attn_cached_kernel(q_ref, kf_ref, vf_ref, kl_ref, vl_ref, gm_ref, onorm_ref, y_ref,
                        s_full, ml_scr, s_last, m_scr, acc_scr, *, tq, tk):
    n_full = kf_ref.shape[1] // tk
    dn = (((1,), (1,)), ((), ()))
    lane = lax.broadcasted_iota(jnp.int32, (tq, LANES), 1)
    heads = range(MLA_HEADS)
    cols = [slice(LANES * hd, LANES * (hd + 1)) for hd in heads]

    def lane_tiles(s):
        return [s[:, LANES * c:LANES * (c + 1)] for c in range(s.shape[1] // LANES)]

    def probs(s, m_rep):
        if s.shape[1] % LANES == 0:
            p = jnp.concatenate([jnp.exp2(t - m_rep) for t in lane_tiles(s)], axis=-1)
        else:
            p = jnp.exp2(s - m_rep[:, :1])
        return p.astype(BF16)

    ml_scr[...] = jnp.full(ml_scr.shape, -jnp.inf, F32)

    def a_step(jt, carry):
        rows = pl.ds(pl.multiple_of(jt * tk, tk), tk)
        for hd in heads:
            s = lax.dot_general(q_ref[0, :, cols[hd]], kf_ref[0, rows, cols[hd]], dn,
                                preferred_element_type=F32)
            s_full[hd, jt] = s
            ml_scr[hd] = functools.reduce(jnp.maximum, lane_tiles(s), ml_scr[hd])
        return carry

    lax.fori_loop(0, n_full, a_step, 0)

    for hd in heads:
        s = lax.dot_general(q_ref[0, :, cols[hd]], kl_ref[0, :, cols[hd]], dn,
                            preferred_element_type=F32)
        s_last[hd] = s
        m = jnp.maximum(jnp.max(s, axis=-1, keepdims=True),
                        jnp.max(ml_scr[hd], axis=-1, keepdims=True))
        m_scr[hd] = jnp.broadcast_to(m, (tq, LANES))
        acc_scr[hd] = jnp.zeros((tq, LANES), F32)

    def b_step(jt, carry):
        rows = pl.ds(pl.multiple_of(jt * tk, tk), tk)
        for hd in heads:
            p = probs(s_full[hd, jt], m_scr[hd])
            acc_scr[hd] += jnp.dot(p, vf_ref[0, rows, cols[hd]], preferred_element_type=F32)
        return carry

    lax.fori_loop(0, n_full, b_step, 0)

    outs = []
    for hd in heads:
        p = probs(s_last[hd], m_scr[hd])
        acc = acc_scr[hd] + jnp.dot(p, vl_ref[0, :, cols[hd]], preferred_element_type=F32)
        ones_col = V_DIM if hd % 2 == 0 else 0
        outs.append(acc / acc[:, ones_col:ones_col + 1])
    attn = jnp.concatenate([jnp.where(lane < V_DIM, outs[e], outs[e + 1])
                            for e in range(0, MLA_HEADS, 2)], axis=-1)
    gm = gm_ref[0]
    y_ref[0] = (_rms(attn, onorm_ref[...]) * (gm * jax.nn.sigmoid(gm))).astype(BF16)


def _attn_causal_kernel(q_ref, k_ref, vt_ref, gm_ref, ms_ref, x_ref, onorm_ref, w_out_ref, y_ref,
                        s_full, ml_scr, s_last, m_scr, acc_scr, ot_scr, *, tq, tk):
    i = pl.program_id(1)
    n_full = (i * tq) // tk
    odd = i % 2 == 1
    nt = (((1,), (1,)), ((), ()))
    heads = range(MLA_HEADS)
    cols = [slice(LANES * hd, LANES * (hd + 1)) for hd in heads]
    vrows = [slice(VT_ROWS * hd, VT_ROWS * (hd + 1)) for hd in heads]

    def group_max(s):
        return jnp.max(s.reshape(s.shape[0] // SUBLANES, SUBLANES, s.shape[1]), axis=0)

    def chunk_mask(width):
        kc = lax.broadcasted_iota(jnp.int32, (width, tq), 0) // CHUNK
        qc = lax.broadcasted_iota(jnp.int32, (width, tq), 1) // CHUNK + (width - tq) // CHUNK
        return qc >= kc

    ml_scr[...] = jnp.full(ml_scr.shape, -jnp.inf, F32)

    def a_step(jt, carry):
        rows = pl.ds(pl.multiple_of(jt * tk, tk), tk)
        for hd in heads:
            s = lax.dot_general(k_ref[0, rows, cols[hd]], q_ref[0, :, cols[hd]], nt,
                                preferred_element_type=F32)
            s_full[hd, jt] = s
            ml_scr[hd] = jnp.maximum(ml_scr[hd], group_max(s))
        return carry

    lax.fori_loop(0, n_full, a_step, 0)

    def last_a(width, rows):
        mask = chunk_mask(width)
        for hd in heads:
            s = lax.dot_general(k_ref[0, rows, cols[hd]], q_ref[0, :, cols[hd]], nt,
                                preferred_element_type=F32)
            s = jnp.where(mask, s, -jnp.inf)
            s_last[hd, :width] = s
            m = jnp.max(jnp.maximum(ml_scr[hd], group_max(s)), axis=0, keepdims=True)
            m_scr[hd] = jnp.broadcast_to(m, m_scr.shape[1:])
            acc_scr[hd] = jnp.zeros(acc_scr.shape[1:], F32)

    pair_rows = pl.ds(pl.multiple_of(n_full * tk, tk), tk)
    pl.when(odd)(functools.partial(last_a, tk, pair_rows))
    pl.when(jnp.logical_not(odd))(functools.partial(last_a, tq, pl.ds(pl.multiple_of(i * tq, tq), tq)))

    def b_step(jt, carry):
        for hd in heads:
            p = jnp.exp2(s_full[hd, jt] - m_scr[hd, :1]).astype(BF16)
            acc_scr[hd] += jnp.dot(vt_ref[0, jt, vrows[hd], :], p, preferred_element_type=F32)
        return carry

    lax.fori_loop(0, n_full, b_step, 0)

    def last_b(width):
        for hd in heads:
            p = jnp.exp2(s_last[hd, :width] - m_scr[hd, :1]).astype(BF16)
            acc = acc_scr[hd] + jnp.dot(vt_ref[0, n_full, vrows[hd], :width], p,
                                        preferred_element_type=F32)
            ot_scr[V_DIM * hd:V_DIM * (hd + 1), :] = acc[:V_DIM] / acc[V_DIM:V_DIM + 1]

    pl.when(odd)(functools.partial(last_b, tk))
    pl.when(jnp.logical_not(odd))(functools.partial(last_b, tq))

    gm = gm_ref[0]
    mla = _rms(ot_scr[...].T, onorm_ref[...]) * (gm * jax.nn.sigmoid(gm))
    mix_ssm = [ms_ref[j, 0] for j in range(ms_ref.shape[0])]
    mix = jnp.concatenate(mix_ssm + [mla], axis=-1).astype(BF16)
    y_ref[0] = x_ref[0] + jnp.dot(mix, w_out_ref[...], preferred_element_type=F32)


def _attn_causal_call(q, k, vt, gm, mix_ssm, x, wts, *, tq):
    b, s, d = x.shape
    tk = vt.shape[3]
    ssm_w = gm.shape[2]
    assert tk == 2 * tq and s % tk == 0 and tq % CHUNK == 0
    row = lambda j, i: (j, i, 0)
    res = lambda j, i: (j, 0, 0)
    scratch = [pltpu.VMEM((MLA_HEADS, s // tk - 1, tk, tq), F32),
               pltpu.VMEM((MLA_HEADS, SUBLANES, tq), F32),
               pltpu.VMEM((MLA_HEADS, tk, tq), F32),
               pltpu.VMEM((MLA_HEADS, SUBLANES, tq), F32),
               pltpu.VMEM((MLA_HEADS, VT_ROWS, tq), F32),
               pltpu.VMEM((MLA_HEADS * V_DIM, tq), F32)]
    return pl.pallas_call(
        functools.partial(_attn_causal_kernel, tq=tq, tk=tk),
        out_shape=jax.ShapeDtypeStruct((b, s, d), F32),
        grid=(b, s // tq),
        in_specs=[pl.BlockSpec((1, tq, HEAD_W), row), pl.BlockSpec((1, s, HEAD_W), res),
                  pl.BlockSpec((1,) + vt.shape[1:], lambda j, i: (j, 0, 0, 0)),
                  pl.BlockSpec((1, tq, ssm_w), row),
                  pl.BlockSpec((mix_ssm.shape[0], 1, tq, LANES), lambda j, i: (0, j, i, 0)),
                  pl.BlockSpec((1, tq, d), row),
                  _full(wts['onorm_mla'].shape), _full(wts['w_out'].shape)],
        out_specs=pl.BlockSpec((1, tq, d), row),
        scratch_shapes=scratch,
        compiler_params=pltpu.CompilerParams(
            dimension_semantics=("arbitrary", "arbitrary"), vmem_limit_bytes=VMEM_LIMIT),
        name="attn",
    )(q, k, vt, gm, mix_ssm, x, wts['onorm_mla'], wts['w_out'])


def _out_kernel(ms_ref, mla_ref, x_ref, w_out_ref, y_ref):
    bb, tq, d = x_ref.shape
    rows = bb * tq
    mix = jnp.concatenate([ms_ref[j].reshape(rows, LANES).astype(BF16) for j in range(ms_ref.shape[0])]
                          + [mla_ref[...].reshape(rows, mla_ref.shape[2])], axis=-1)
    y = x_ref[...].reshape(rows, d) + jnp.dot(mix, w_out_ref[...], preferred_element_type=F32)
    y_ref[...] = y.reshape(bb, tq, d)


def _out_call(mix_ssm, mla, x, wts, *, bb):
    b, s, d = x.shape
    row = lambda j: (j, 0, 0)
    return pl.pallas_call(
        _out_kernel,
        out_shape=jax.ShapeDtypeStruct((b, s, d), F32),
        grid=(b // bb,),
        in_specs=[pl.BlockSpec((mix_ssm.shape[0], bb, s, LANES), lambda j: (0, j, 0, 0)),
                  pl.BlockSpec((bb, s, mla.shape[2]), row), pl.BlockSpec((bb, s, d), row),
                  _full(wts['w_out'].shape)],
        out_specs=pl.BlockSpec((bb, s, d), row),
        compiler_params=pltpu.CompilerParams(
            dimension_semantics=("arbitrary",), vmem_limit_bytes=VMEM_LIMIT),
        name="outproj",
    )(mix_ssm, mla, x, wts['w_out'])


def _attn_cached_call(q, k_full, v_full, k_last, v_last, gm, wts, *, tk):
    b, tq, _ = q.shape
    t_full, last_len = k_full.shape[1], k_last.shape[1]
    ssm_w = gm.shape[2]
    blk = lambda rows, width: pl.BlockSpec((1, rows, width), lambda j: (j, 0, 0))
    scratch = [pltpu.VMEM((MLA_HEADS, t_full // tk, tq, tk), F32), pltpu.VMEM((MLA_HEADS, tq, LANES), F32),
               pltpu.VMEM((MLA_HEADS, tq, last_len), F32), pltpu.VMEM((MLA_HEADS, tq, LANES), F32),
               pltpu.VMEM((MLA_HEADS, tq, LANES), F32)]
    return pl.pallas_call(
        functools.partial(_attn_cached_kernel, tq=tq, tk=tk),
        out_shape=jax.ShapeDtypeStruct((b, tq, ssm_w), BF16),
        grid=(b,),
        in_specs=[blk(tq, HEAD_W), blk(t_full, HEAD_W), blk(t_full, HEAD_W), blk(last_len, HEAD_W),
                  blk(last_len, HEAD_W), blk(tq, ssm_w), _full(wts['onorm_mla'].shape)],
        out_specs=blk(tq, ssm_w),
        scratch_shapes=scratch,
        compiler_params=pltpu.CompilerParams(
            dimension_semantics=("arbitrary",), vmem_limit_bytes=VMEM_LIMIT),
        name="attn_cached",
    )(q, k_full, v_full, k_last, v_last, gm, wts['onorm_mla'])


def _head_block_cols(w, pieces):
    k = w.shape[0]
    w3 = w.reshape(k, MLA_HEADS, w.shape[1] // MLA_HEADS)
    cols = [w3[:, :, p[0]:p[0] + p[1]] if isinstance(p, tuple) else jnp.zeros((k, MLA_HEADS, p), w.dtype)
            for p in pieces]
    return jnp.concatenate(cols, axis=-1).reshape(k, HEAD_W)


def _zoh_powers(lr, li, dt, n):
    mag = jnp.exp(lr * dt)
    ar, ai = mag * jnp.cos(li * dt), mag * jnp.sin(li * dt)
    pw = [(jnp.ones_like(ar), jnp.zeros_like(ai))]
    for _ in range(n):
        pr, pi = pw[-1]
        pw.append((pr * ar - pi * ai, pr * ai + pi * ar))
    return pw


def _disc_kernel(arow_ref, acol_ref, ldt_ref, bt_ref, cd_ref, ct_ref,
                 apr_ref, api_ref, wb_ref, wc_ref, wf_ref):
    r_blk, groups, cin = wb_ref.shape[1], wb_ref.shape[2], wb_ref.shape[3]
    n_state = acol_ref.shape[2]
    gpb = LANES // cin
    lr, li = arow_ref[0], arow_ref[1]
    dt = jnp.exp(ldt_ref[...])
    pw = _zoh_powers(lr, li, dt, r_blk)
    ar, ai = pw[1]
    den = lr * lr + li * li
    kr = ((ar - 1.0) * lr + ai * li) / den
    ki = (ai * lr - (ar - 1.0) * li) / den
    apr_ref[...], api_ref[...] = pw[r_blk]

    pwc = _zoh_powers(acol_ref[0], acol_ref[1], dt, r_blk)
    for j in range(r_blk):
        pr, pi = pwc[j + 1]
        wc_ref[0, j] = ct_ref[0] * pr - ct_ref[1] * pi
        wc_ref[1, j] = -(ct_ref[0] * pi + ct_ref[1] * pr)

    b_r = kr * bt_ref[0] - ki * bt_ref[1]
    b_i = kr * bt_ref[1] + ki * bt_ref[0]
    for i in range(r_blk):
        pr, pi = pw[r_blk - 1 - i]
        wb_ref[0, i] = pr * b_r - pi * b_i
        wb_ref[1, i] = pr * b_i + pi * b_r

    nt = (((1,), (1,)), ((), ()))
    first_copy = lax.broadcasted_iota(jnp.int32, (LANES, LANES), 1) < n_state
    block = lambda v, n: v[gpb * n:gpb * (n + 1)].reshape(LANES, LANES)
    for i in range(r_blk):
        for j in range(i):
            wf_ref[i, j] = jnp.zeros(wf_ref.shape[2:], F32)
    for m in range(r_blk):
        pr, pi = pw[m]
        ca_r = cd_ref[0] * pr - cd_ref[1] * pi
        ca_i = cd_ref[0] * pi + cd_ref[1] * pr
        for n in range(groups // gpb):
            f_t = (lax.dot_general(jnp.where(first_copy, block(b_r, n), 0.0), block(ca_r, n), nt,
                                   precision=lax.Precision.HIGHEST, preferred_element_type=F32)
                   - lax.dot_general(jnp.where(first_copy, block(b_i, n), 0.0), block(ca_i, n), nt,
                                     precision=lax.Precision.HIGHEST, preferred_element_type=F32))
            for i in range(r_blk - m):
                wf_ref[i, i + m, gpb * n:gpb * (n + 1)] = f_t.reshape(gpb, cin, LANES)


def _prepare_weights(norm_in, w_in, ssm_a_re, ssm_a_im, ssm_log_dt, ssm_b_re, ssm_b_im, ssm_c_re,
                     ssm_c_im, ssm_d, w_glu, b_glu, q_lora_norm, kv_lora_norm, w_uq, w_ukv,
                     q_nope_norm, k_nope_norm, q_rope_norm, k_rope_norm, out_norm_ssm,
                     out_norm_mla, w_out):
    groups, n_state = ssm_a_re.shape
    ssm_w = groups * SSM_GROUP
    row = lambda v: v.reshape(1, -1).astype(F32)
    tail = LANES - ROPE_DIM - NOPE_DIM

    o_kr = 2 * ssm_w + Q_LORA + KV_LORA
    w_in_p = jnp.concatenate(
        [w_in[:, :o_kr + ROPE_DIM], jnp.zeros((w_in.shape[0], LANES - ROPE_DIM), w_in.dtype),
         w_in[:, o_kr + ROPE_DIM:]], axis=1).astype(BF16)

    w_uq_p = _head_block_cols(w_uq, [(NOPE_DIM, ROPE_DIM), (0, NOPE_DIM), tail]).astype(BF16)
    w_uk_p = _head_block_cols(w_ukv, [ROPE_DIM, (0, NOPE_DIM), tail])
    w_v_lo = _head_block_cols(w_ukv, [(NOPE_DIM, V_DIM), LANES - V_DIM])
    w_v_hi = _head_block_cols(w_ukv, [LANES - V_DIM, (NOPE_DIM, V_DIM)])
    odd_head = (np.arange(HEAD_W) // LANES) % 2 == 1
    w_ukv_p = jnp.concatenate([w_uk_p, jnp.where(odd_head[None, :], w_v_hi, w_v_lo)], axis=1).astype(BF16)
    wvt = w_v_lo.reshape(KV_LORA, MLA_HEADS, LANES)[:, :, :VT_ROWS]
    wvt = jnp.transpose(wvt, (1, 2, 0)).reshape(MLA_HEADS * VT_ROWS, KV_LORA).astype(BF16)

    scale = (NOPE_DIM + ROPE_DIM) ** -0.5 * np.log2(np.e)
    vones = np.zeros((MLA_HEADS, LANES), np.float32)
    vones[0::2, V_DIM] = 1.0
    vones[1::2, 0] = 1.0
    vones = jnp.asarray(vones.reshape(1, HEAD_W))
    zeros = lambda n: jnp.zeros((n,), F32)
    gq =jnp.concatenate([q_rope_norm, q_nope_norm, zeros(tail)]) * scale
    gkr = jnp.concatenate([k_rope_norm, zeros(LANES - ROPE_DIM)])
    gkn = jnp.concatenate([zeros(ROPE_DIM), k_nope_norm, zeros(tail)])

    seg = np.zeros((LANES, LANES), np.float32)
    seg[:ROPE_DIM, :ROPE_DIM] = 1.0 / ROPE_DIM
    seg[ROPE_DIM:ROPE_DIM + NOPE_DIM, ROPE_DIM:ROPE_DIM + NOPE_DIM] = 1.0 / NOPE_DIM
    seg[ROPE_DIM + NOPE_DIM:, ROPE_DIM + NOPE_DIM:] = 1.0 / tail
    seg = jnp.asarray(np.kron(np.eye(2, dtype=np.float32), seg), BF16)
    place = jnp.asarray(np.eye(ROPE_DIM, LANES, dtype=np.float32), BF16)

    r_blk = SSM_STEPS_PER_SCAN
    lane_rep = lambda x: jnp.tile(x, (1,) * (x.ndim - 1) + (LANES // x.shape[-1],))
    a2 = jnp.stack([ssm_a_re, ssm_a_im]).astype(F32)
    b2 = jnp.stack([ssm_b_re, ssm_b_im]).astype(F32)
    c2 = jnp.stack([ssm_c_re, ssm_c_im]).astype(F32)
    ldt = ssm_log_dt.astype(F32)
    gl = jax.ShapeDtypeStruct((groups, 1, LANES), F32)
    a_re, a_im, wb5, wc5, wf5 = pl.pallas_call(
        _disc_kernel,
        out_shape=[gl, gl,
                   jax.ShapeDtypeStruct((2, r_blk, groups, SSM_GROUP, LANES), F32),
                   jax.ShapeDtypeStruct((2, r_blk, groups, n_state, LANES), F32),
                   jax.ShapeDtypeStruct((r_blk, r_blk, groups, SSM_GROUP, LANES), F32)],
        name="disc")(
        lane_rep(a2)[:, :, None, :], jnp.broadcast_to(a2[..., None], a2.shape + (LANES,)),
        jnp.broadcast_to(ldt[:, None, None], (groups, 1, LANES)),
        lane_rep(jnp.swapaxes(b2, 2, 3)), lane_rep(c2), lane_rep(jnp.swapaxes(c2, 2, 3)))
    a_re, a_im = a_re[:, 0, :n_state], a_im[:, 0, :n_state]

    return dict(
        ssm_w=ssm_w,
        norm_in=row(norm_in), w_in=w_in_p, qln=row(q_lora_norm), kvln=row(kv_lora_norm),
        w_uq=w_uq_p, w_ukv=w_ukv_p, w_uk=w_uk_p.astype(BF16), wvt=wvt, gq=row(gq), gkr=row(gkr), gkn=row(gkn), seg=seg, vones=vones,
        place=place, a_re=row(a_re), a_im=row(a_im), wb5=wb5, wc5=wc5, wf5=wf5,
        ssm_d=row(ssm_d), w_glu=w_glu.astype(BF16), b_glu=row(b_glu), onorm_ssm=row(out_norm_ssm),
        onorm_mla=row(out_norm_mla), w_out=w_out.astype(BF16))


def _rope_tables(start, n):
    half = ROPE_DIM // 2
    inv = ROPE_THETA ** (-jnp.arange(half, dtype=F32) / half)
    ang = (start + jnp.arange(n)).astype(F32)[:, None] * inv[None, :]
    cos, sin = jnp.cos(ang), jnp.sin(ang)
    cos_t = jnp.concatenate([cos, cos, jnp.ones((n, LANES - ROPE_DIM), F32)], axis=1)
    sin_a = jnp.concatenate([-sin, jnp.zeros((n, LANES - half), F32)], axis=1)
    sin_b = jnp.concatenate([jnp.zeros((n, half), F32), sin, jnp.zeros((n, LANES - ROPE_DIM), F32)],
                            axis=1)
    return cos_t, sin_a, sin_b


def _mixer(x, pos0, h0re, h0im, past, wts, *, proj_bb, proj_tq, ssm_steps, attn_tk):
    u, gs, gm, q, k, v, ckv, kr = _proj_call(x, _rope_tables(pos0, x.shape[1]), wts,
                                             bb=proj_bb, tq=proj_tq)
    mix_ssm, hre, him = _ssm_call(u, gs, h0re, h0im, wts, steps=ssm_steps, unroll=True)
    if past is None:
        y = _attn_causal_call(q, k, v, gm, mix_ssm, x, wts, tq=attn_tk // 2)
    else:
        k_past, v_past = _expand_call(*past, wts, tq=512)
        mla = _attn_cached_call(q, k_past, v_past, k, v, gm, wts, tk=attn_tk)
        y = _out_call(mix_ssm, mla, x, wts, bb=proj_bb)
    if proj_bb == 1:
        kr = jnp.swapaxes(kr, 1, 2)
    return y, ckv, kr, hre, him


def kernel(x_prompt, x_sample, cache_ckv, cache_krope, state_ssm_re, state_ssm_im, norm_in, w_in, ssm_a_re, ssm_a_im, ssm_log_dt, ssm_b_re, ssm_b_im, ssm_c_re, ssm_c_im, ssm_d, w_glu, b_glu, q_lora_norm, kv_lora_norm, w_uq, w_ukv, q_nope_norm, k_nope_norm, q_rope_norm, k_rope_norm, out_norm_ssm, out_norm_mla, w_out):
    depth = norm_in.shape[0]
    assert depth == 1, "single mixer layer"
    params = (norm_in, w_in, ssm_a_re, ssm_a_im, ssm_log_dt, ssm_b_re, ssm_b_im, ssm_c_re, ssm_c_im,
              ssm_d, w_glu, b_glu, q_lora_norm, kv_lora_norm, w_uq, w_ukv, q_nope_norm, k_nope_norm,
              q_rope_norm, k_rope_norm, out_norm_ssm, out_norm_mla, w_out)
    drop_depth = lambda a: a.reshape(a.shape[1:])
    wts = _prepare_weights(*[drop_depth(p) for p in params])
    groups, n_state = ssm_a_re.shape[1:]
    bp, sp, _ = x_prompt.shape
    bs, ss, _ = x_sample.shape
    past_len = cache_ckv.shape[2]

    zero_state = jnp.zeros((bp, groups * n_state), F32)
    yp, ckv_p, kr_p, re_p, im_p = _mixer(
        x_prompt, 0, zero_state, zero_state, None, wts,
        proj_bb=1, proj_tq=512, ssm_steps=64, attn_tk=512)
    ys, ckv_s, kr_s, re_s, im_s = _mixer(
        x_sample, past_len,
        state_ssm_re.reshape(bs, groups * n_state), state_ssm_im.reshape(bs, groups * n_state),
        (drop_depth(cache_ckv), jnp.swapaxes(drop_depth(cache_krope), 1, 2)), wts,
        proj_bb=bs // 2, proj_tq=ss, ssm_steps=ss, attn_tk=past_len)

    st = lambda h, bb: h.reshape(1, bb, groups, n_state)
    return (yp, ys, ckv_p[None], kr_p[None], st(re_p, bp), st(im_p, bp),
            ckv_s[None], kr_s[None], st(re_s, bs), st(im_s, bs))
```

```python
import functools

import numpy as np
import jax
import jax.numpy as jnp
from jax import lax
from jax.experimental import pallas as pl
from jax.experimental.pallas import tpu as pltpu

F32 = jnp.float32
BF16 = jnp.bfloat16

CHUNK = 64
SSM_GROUP = 16
SSM_STATE = 64
MLA_HEADS = 8
NOPE_DIM = 64
ROPE_DIM = 32
V_DIM = 64
Q_LORA = 256
KV_LORA = 128
ROPE_THETA = 10000.0
EPS = 1e-6

LANES = 128
SUBLANES = 8
VT_ROWS = 80
HEAD_W = MLA_HEADS * LANES
GROUPS_PER_BLOCK = 8
SSM_STEPS_PER_SCAN = 4
VMEM_LIMIT = 56 * 1024 * 1024


def _rms(x, gain):
    return x * lax.rsqrt(jnp.mean(x * x, axis=-1, keepdims=True) + EPS) * gain


def _seg_rms(x, seg):
    ms = jnp.dot((x * x).astype(BF16), seg, preferred_element_type=F32)
    return x * lax.rsqrt(ms + EPS)


def _rope_block(x, cos_t, sin_a, sin_b):
    return (x * cos_t + pltpu.roll(x, LANES - ROPE_DIM // 2, 1) * sin_a
            + pltpu.roll(x, ROPE_DIM // 2, 1) * sin_b)


def _expand_kv(ckv, kr_blk, w_ukv_ref, gkn_ref, seg_ref, vones_ref, k_ref, v_ref):
    kv = jnp.dot(ckv.astype(BF16), w_ukv_ref[...], preferred_element_type=F32)
    gkn = gkn_ref[...]
    blk3 = k_ref.shape[:2] + (LANES,)
    for p in range(MLA_HEADS // 2):
        kn = _seg_rms(kv[:, 2 * LANES * p:2 * LANES * (p + 1)], seg_ref[...])
        for j in range(2):
            h = 2 * p + j
            blk = kn[:, LANES * j:LANES * (j + 1)] * gkn + kr_blk
            k_ref[:, :, LANES * h:LANES * (h + 1)] = blk.astype(BF16).reshape(blk3)
    if v_ref is not None:
        v_ref[...] = (kv[:, HEAD_W:] + vones_ref[...]).astype(BF16).reshape(v_ref.shape)


def _proj_kernel(x_ref, cos_ref, sa_ref, sb_ref, norm_in_ref, w_in_ref, qln_ref, kvln_ref,
                 w_uq_ref, w_ukv_ref, gq_ref, gkr_ref, gkn_ref, seg_ref, vones_ref,
                 u_ref, gs_ref, gm_ref, q_ref, k_ref, v_ref, ckv_ref, kr_ref, *, ssm_w, transposed):
    bb, tq, d = x_ref.shape
    x = x_ref[...].reshape(bb * tq, d)
    h = _rms(x, norm_in_ref[...])
    z = jnp.dot(h.astype(BF16), w_in_ref[...], preferred_element_type=F32)
    o = 0
    for dst in (u_ref, gs_ref):
        for j in range(ssm_w // LANES):
            dst[j] = z[:, o + LANES * j:o + LANES * (j + 1)].reshape(bb, tq, LANES)
        o += ssm_w
    c_q = z[:, o:o + Q_LORA]
    o += Q_LORA
    c_kv = z[:, o:o + KV_LORA]
    o += KV_LORA
    kr_raw = z[:, o:o + LANES]
    o += LANES
    gm_ref[...] = z[:, o:].reshape(gm_ref.shape)

    per_row = lambda t_ref: jnp.concatenate([t_ref[...]] * bb, axis=0)
    cos_t, sin_a, sin_b = per_row(cos_ref), per_row(sa_ref), per_row(sb_ref)
    seg = seg_ref[...]

    kr_ms = jnp.dot((kr_raw * kr_raw).astype(BF16), seg[:LANES, :LANES], preferred_element_type=F32)
    kr_blk = _rope_block(kr_raw * lax.rsqrt(kr_ms + EPS) * gkr_ref[...], cos_t, sin_a, sin_b)
    if transposed:
        kr_ref[0] = kr_blk.T[:ROPE_DIM]
    else:
        kr_ref[...] = kr_blk[:, :ROPE_DIM].reshape(kr_ref.shape)

    q = jnp.dot(_rms(c_q, qln_ref[...]).astype(BF16), w_uq_ref[...], preferred_element_type=F32)
    gq = gq_ref[...]
    for p in range(MLA_HEADS // 2):
        qn = _seg_rms(q[:, 2 * LANES * p:2 * LANES * (p + 1)], seg)
        for j in range(2):
            hd = 2 * p + j
            blk = _rope_block(qn[:, LANES * j:LANES * (j + 1)] * gq, cos_t, sin_a, sin_b)
            q_ref[:, :, LANES * hd:LANES * (hd + 1)] = blk.astype(BF16).reshape(bb, tq, LANES)

    ckv = _rms(c_kv, kvln_ref[...])
    ckv_ref[...] = ckv.reshape(ckv_ref.shape)
    if transposed:
        _expand_kv(ckv, kr_blk, w_ukv_ref, gkn_ref, seg_ref, None, k_ref, None)
        v_t = jnp.dot(vones_ref[...], ckv.T.astype(BF16), preferred_element_type=F32)
        ones_row = lax.broadcasted_iota(jnp.int32, v_t.shape, 0) % VT_ROWS == V_DIM
        v_ref[0, 0] = jnp.where(ones_row, 1.0, v_t).astype(BF16)
    else:
        _expand_kv(ckv, kr_blk, w_ukv_ref, gkn_ref, seg_ref, vones_ref, k_ref, v_ref)


def _expand_kernel(ckv_ref, kr_ref, w_uk_ref, gkn_ref, seg_ref, place_ref, k_ref):
    kr_blk = lax.dot_general(kr_ref[0].astype(BF16), place_ref[...], (((0,), (0,)), ((), ())),
                             preferred_element_type=F32)
    _expand_kv(ckv_ref[0], kr_blk, w_uk_ref, gkn_ref, seg_ref, None, k_ref, None)


def _full(shape):
    n = len(shape)
    return pl.BlockSpec(shape, lambda *_: (0,) * n)


def _proj_call(x, tables, wts, *, bb, tq):
    b, s, d = x.shape
    ssm_w = wts['ssm_w']
    cos_t, sin_a, sin_b = tables
    transposed = bb == 1
    w_kv, v_aux = (wts['w_uk'], wts['wvt']) if transposed else (wts['w_ukv'], wts['vones'])
    grid = (s // tq, b // bb)
    row = lambda i, j: (j, i, 0)
    tab = pl.BlockSpec((tq, LANES), lambda i, j: (i, 0))
    in_specs = [pl.BlockSpec((bb, tq, d), row), tab, tab, tab,
                _full(wts['norm_in'].shape), _full(wts['w_in'].shape), _full(wts['qln'].shape),
                _full(wts['kvln'].shape), _full(wts['w_uq'].shape), _full(w_kv.shape),
                _full(wts['gq'].shape), _full(wts['gkr'].shape), _full(wts['gkn'].shape),
                _full(wts['seg'].shape), _full(v_aux.shape)]
    slabs = ssm_w // LANES
    tb_spec = pl.BlockSpec((slabs, bb, tq, LANES), lambda i, j: (0, j, i, 0))
    kr_shape, kr_spec = (((b, ROPE_DIM, s), pl.BlockSpec((1, ROPE_DIM, tq), lambda i, j: (j, 0, i)))
                         if transposed else
                         ((b, s, ROPE_DIM), pl.BlockSpec((bb, tq, ROPE_DIM), row)))
    v_shape, v_spec = (((b, s // tq, MLA_HEADS * VT_ROWS, tq),
                        pl.BlockSpec((1, 1, MLA_HEADS * VT_ROWS, tq), lambda i, j: (j, i, 0, 0)))
                       if transposed else
                       ((b, s, HEAD_W), pl.BlockSpec((bb, tq, HEAD_W), row)))
    out_shape = [jax.ShapeDtypeStruct((slabs, b, s, LANES), F32),
                 jax.ShapeDtypeStruct((slabs, b, s, LANES), F32),
                 jax.ShapeDtypeStruct((b, s, ssm_w), F32),
                 jax.ShapeDtypeStruct((b, s, HEAD_W), BF16),
                 jax.ShapeDtypeStruct((b, s, HEAD_W), BF16),
                 jax.ShapeDtypeStruct(v_shape, BF16),
                 jax.ShapeDtypeStruct((b, s, KV_LORA), F32),
                 jax.ShapeDtypeStruct(kr_shape, F32)]
    out_specs = [tb_spec, tb_spec,
                 pl.BlockSpec((bb, tq, ssm_w), row),
                 pl.BlockSpec((bb, tq, HEAD_W), row), pl.BlockSpec((bb, tq, HEAD_W), row), v_spec,
                 pl.BlockSpec((bb, tq, KV_LORA), row), kr_spec]
    return pl.pallas_call(
        functools.partial(_proj_kernel, ssm_w=ssm_w, transposed=transposed),
        out_shape=out_shape, grid=grid, in_specs=in_specs, out_specs=out_specs,
        compiler_params=pltpu.CompilerParams(
            dimension_semantics=("arbitrary", "arbitrary"), vmem_limit_bytes=VMEM_LIMIT),
        name="proj",
    )(x, cos_t, sin_a, sin_b, wts['norm_in'], wts['w_in'], wts['qln'], wts['kvln'], wts['w_uq'],
      w_kv, wts['gq'], wts['gkr'], wts['gkn'], wts['seg'], v_aux)


def _expand_call(ckv, kr, wts, *, tq):
    b, t, _ = ckv.shape
    row = lambda j, i: (j, i, 0)
    return pl.pallas_call(
        _expand_kernel,
        out_shape=jax.ShapeDtypeStruct((b, t, HEAD_W), BF16),
        grid=(b, t // tq),
        in_specs=[pl.BlockSpec((1, tq, KV_LORA), row),
                  pl.BlockSpec((1, ROPE_DIM, tq), lambda j, i: (j, 0, i)),
                  _full(wts['w_uk'].shape), _full(wts['gkn'].shape), _full(wts['seg'].shape),
                  _full(wts['place'].shape)],
        out_specs=pl.BlockSpec((1, tq, HEAD_W), row),
        compiler_params=pltpu.CompilerParams(
            dimension_semantics=("arbitrary", "arbitrary"), vmem_limit_bytes=VMEM_LIMIT),
        name="expand",
    )(ckv, kr, wts['w_uk'], wts['gkn'], wts['seg'], wts['place'])


def _ssm_kernel(u_ref, gs_ref, h0re_ref, h0im_ref, are_ref, aim_ref, wb5_ref, wc5_ref, wf5_ref, d_ref,
                w_glu_ref, b_glu_ref, onorm_ref, mix_ref, hre_ref, him_ref, us_ref, xs_ref, y_ref,
                wb_ref, wc_ref, wf_ref, *, batch, steps, unroll):
    n_blocks, r_blk = wb_ref.shape[0], wb_ref.shape[1] // LANES
    half = wb_ref.shape[2] // 2
    gpb = wb5_ref.shape[2] // n_blocks
    n_k = steps // r_blk

    @pl.when(pl.program_id(0) == 0)
    def _():
        hre_ref[...] = h0re_ref[...]
        him_ref[...] = h0im_ref[...]

        def own_group(shape, row_div, lane_div):
            return (lax.broadcasted_iota(jnp.int32, shape, 0) // row_div
                    == lax.broadcasted_iota(jnp.int32, shape, 1) // lane_div)

        cin, n_state = wb5_ref.shape[3], half // gpb
        m_b = own_group((LANES, half), cin, n_state)
        m_c = own_group((half, LANES), n_state, cin)
        m_f = own_group((LANES, LANES), cin, cin)
        for gb in range(n_blocks):
            grp = slice(gpb * gb, gpb * (gb + 1))
            for ri in range(2):
                for i in range(r_blk):
                    piece = wb5_ref[ri, i, grp].reshape(LANES, LANES)
                    wide = jnp.concatenate([piece] * (half // LANES), axis=1)
                    wb_ref[gb, LANES * i:LANES * (i + 1), half * ri:half * (ri + 1)] = (
                        jnp.where(m_b, wide, 0.0).astype(BF16))
                    piece = wc5_ref[ri, i, grp].reshape(half, LANES)
                    wc_ref[gb, half * ri:half * (ri + 1), LANES * i:LANES * (i + 1)] = (
                        jnp.where(m_c, piece, 0.0).astype(BF16))
            for i in range(r_blk):
                for j in range(r_blk):
                    piece = wf5_ref[i, j, grp].reshape(LANES, LANES)
                    wf_ref[gb, LANES * i:LANES * (i + 1), LANES * j:LANES * (j + 1)] = (
                        jnp.where(m_f, piece, 0.0).astype(BF16))

    pitch = us_ref.shape[2] // n_k
    for b in range(batch):
        for j in range(n_blocks):
            for i in range(r_blk):
                us_ref[j, i, pl.ds(b, n_k, stride=pitch), :] = (
                    u_ref[j, b, pl.ds(i, n_k, stride=r_blk), :])

    def slab_rows(j, i):
        return jnp.concatenate([us_ref[j, i, pitch * k:pitch * k + batch, :] for k in range(n_k)], axis=0)

    u_slabs = [[slab_rows(j, i) for i in range(r_blk)] for j in range(n_blocks)]
    for gb in range(n_blocks):
        lhs = jnp.concatenate(u_slabs[gb], axis=-1).astype(BF16)
        xs = xs_ref
        xs[...] = jnp.dot(lhs, wb_ref[gb], preferred_element_type=F32)
        cols = slice(half * gb, half * (gb + 1))
        a_re = jnp.broadcast_to(are_ref[:, cols], (batch, half))
        a_im = jnp.broadcast_to(aim_ref[:, cols], (batch, half))

        def step(k, carry, xs=xs, a_re=a_re, a_im=a_im):
            h_re, h_im = carry
            rows = pl.ds(pl.multiple_of(k * batch, batch), batch)
            n_re = a_re * h_re - a_im * h_im + xs[rows, :half]
            n_im = a_re * h_im + a_im * h_re + xs[rows, half:]
            xs[rows, :half] = h_re
            xs[rows, half:] = h_im
            return n_re, n_im

        h_re, h_im = lax.fori_loop(0, n_k, step, (hre_ref[:, cols], him_ref[:, cols]), unroll=unroll)
        hre_ref[:, cols] = h_re
        him_ref[:, cols] = h_im
        y = (jnp.dot(xs[...].astype(BF16), wc_ref[gb], preferred_element_type=F32)
             + jnp.dot(lhs, wf_ref[gb], preferred_element_type=F32))
        for i in range(r_blk):
            y_ref[i, :, LANES * gb:LANES * (gb + 1)] = y[:, LANES * i:LANES * (i + 1)]

    rows = r_blk * n_k * batch
    u = jnp.concatenate([jnp.concatenate(u_slabs[j], axis=0) for j in range(n_blocks)], axis=-1)
    y = y_ref[...].reshape(rows, n_blocks * LANES) + d_ref[...] * u
    yg = jax.nn.gelu(y)
    glu = jnp.dot(yg.astype(BF16), w_glu_ref[...], preferred_element_type=F32) + b_glu_ref[...]
    out = _rms(yg * jax.nn.sigmoid(glu), onorm_ref[...])
    for j in range(n_blocks):
        for i in range(r_blk):
            for k in range(n_k):
                r0 = (i * n_k + k) * batch
                us_ref[j, i, pitch * k:pitch * k + batch, :] = out[r0:r0 + batch, LANES * j:LANES * (j + 1)]
    for b in range(batch):
        for j in range(n_blocks):
            for i in range(r_blk):
                tok = pl.ds(i, n_k, stride=r_blk)
                gs = gs_ref[j, b, tok, :]
                o = us_ref[j, i, pl.ds(b, n_k, stride=pitch), :]
                mix_ref[j, b, tok, :] = o * (gs * jax.nn.sigmoid(gs))


def _ssm_call(u, gs, h0re, h0im, wts, *, steps, unroll):
    n_blocks, batch, seq, _ = u.shape
    n_state = h0re.shape[1]
    r_blk = wts['wb5'].shape[1]
    width = 2 * n_state // n_blocks
    assert steps % r_blk == 0 and wts['wb5'].shape[3] * GROUPS_PER_BLOCK == LANES
    blk = pl.BlockSpec((n_blocks, batch, steps, LANES), lambda i: (0, 0, i, 0))
    names = ['a_re', 'a_im', 'wb5', 'wc5', 'wf5', 'ssm_d', 'w_glu', 'b_glu', 'onorm_ssm']
    n_rows = steps // r_blk * batch
    return pl.pallas_call(
        functools.partial(_ssm_kernel, batch=batch, steps=steps, unroll=unroll),
        out_shape=[jax.ShapeDtypeStruct(u.shape, F32),
                   jax.ShapeDtypeStruct((batch, n_state), F32),
                   jax.ShapeDtypeStruct((batch, n_state), F32)],
        grid=(seq // steps,),
        in_specs=[blk, blk, _full(h0re.shape), _full(h0im.shape)] + [_full(wts[n].shape) for n in names],
        out_specs=[blk, _full((batch, n_state)), _full((batch, n_state))],
        scratch_shapes=[pltpu.VMEM((n_blocks, r_blk, steps // r_blk * (batch + SUBLANES), LANES), F32),
                        pltpu.VMEM((n_rows, width), F32),
                        pltpu.VMEM((r_blk, n_rows, n_blocks * LANES), F32),
                        pltpu.VMEM((n_blocks, r_blk * LANES, width), BF16),
                        pltpu.VMEM((n_blocks, width, r_blk * LANES), BF16),
                        pltpu.VMEM((n_blocks, r_blk * LANES, r_blk * LANES), BF16)],
        compiler_params=pltpu.CompilerParams(
            dimension_semantics=("arbitrary",), vmem_limit_bytes=VMEM_LIMIT),
        name="ssm",
    )(u, gs, h0re, h0im, *[wts[n] for n in names])


def _attn_cached_kernel(q_ref, kf_ref, cf_ref, kl_ref, vl_ref, gm_ref, onorm_ref, w_uv_ref, y_ref,
                        s_full, ml_scr, s_last, m_scr, ctx_scr, den_scr, *, tq, tk):
    n_full = kf_ref.shape[1] // tk
    dn = (((1,), (1,)), ((), ()))
    lane = lax.broadcasted_iota(jnp.int32, (tq, LANES), 1)
    heads = range(MLA_HEADS)
    cols = [slice(LANES * hd, LANES * (hd + 1)) for hd in heads]

    def lane_tiles(s):
        return [s[:, LANES * c:LANES * (c + 1)] for c in range(s.shape[1] // LANES)]

    def probs(s, m_rep):
        if s.shape[1] % LANES == 0:
            p = jnp.concatenate([jnp.exp2(t - m_rep) for t in lane_tiles(s)], axis=-1)
        else:
            p = jnp.exp2(s - m_rep[:, :1])
        return p.astype(BF16)

    ml_scr[...] = jnp.full(ml_scr.shape, -jnp.inf, F32)

    def a_step(jt, carry):
        rows = pl.ds(pl.multiple_of(jt * tk, tk), tk)
        for hd in heads:
            s = lax.dot_general(q_ref[0, :, cols[hd]], kf_ref[0, rows, cols[hd]], dn,
                                preferred_element_type=F32)
            s_full[hd, jt] = s
            ml_scr[hd] = functools.reduce(jnp.maximum, lane_tiles(s), ml_scr[hd])
        return carry

    lax.fori_loop(0, n_full, a_step, 0)

    for hd in heads:
        s = lax.dot_general(q_ref[0, :, cols[hd]], kl_ref[0, :, cols[hd]], dn,
                            preferred_element_type=F32)
        s_last[hd] = s
        m = jnp.maximum(jnp.max(s, axis=-1, keepdims=True),
                        jnp.max(ml_scr[hd], axis=-1, keepdims=True))
        m_scr[hd] = jnp.broadcast_to(m, (tq, LANES))
    ctx_scr[...] = jnp.zeros(ctx_scr.shape, F32)
    den_scr[...] = jnp.zeros(den_scr.shape, F32)

    def b_step(jt, carry):
        rows = pl.ds(pl.multiple_of(jt * tk, tk), tk)
        p = jnp.concatenate([probs(s_full[hd, jt], m_scr[hd]) for hd in heads], axis=0)
        ctx_scr[...] += jnp.dot(p, cf_ref[0, rows, :].astype(BF16), preferred_element_type=F32)
        den_scr[...] += jnp.sum(p.astype(F32), axis=-1, keepdims=True)
        return carry

    lax.fori_loop(0, n_full, b_step, 0)

    outs = []
    for hd in heads:
        mine = slice(tq * hd, tq * (hd + 1))
        cached = jnp.dot(ctx_scr[mine].astype(BF16), w_uv_ref[:, cols[hd]], preferred_element_type=F32)
        p = probs(s_last[hd], m_scr[hd])
        new = jnp.dot(p, vl_ref[0, :, cols[hd]], preferred_element_type=F32)
        ones_col = V_DIM if hd % 2 == 0 else 0
        outs.append((cached + new) / (den_scr[mine] + new[:, ones_col:ones_col + 1]))
    attn = jnp.concatenate([jnp.where(lane < V_DIM, outs[e], outs[e + 1])
                            for e in range(0, MLA_HEADS, 2)], axis=-1)
    gm = gm_ref[0]
    y_ref[0] = (_rms(attn, onorm_ref[...]) * (gm * jax.nn.sigmoid(gm))).astype(BF16)


def _attn_causal_kernel(q_ref, k_ref, vt_ref, gm_ref, ms_ref, x_ref, onorm_ref, w_out_ref, y_ref,
                        s_full, ml_scr, s_last, m_scr, acc_scr, ot_scr, *, tq, tk):
    i = pl.program_id(1)
    n_full = (i * tq) // tk
    odd = i % 2 == 1
    nt = (((1,), (1,)), ((), ()))
    heads = range(MLA_HEADS)
    cols = [slice(LANES * hd, LANES * (hd + 1)) for hd in heads]
    vrows = [slice(VT_ROWS * hd, VT_ROWS * (hd + 1)) for hd in heads]

    def group_max(s):
        return jnp.max(s.reshape(s.shape[0] // SUBLANES, SUBLANES, s.shape[1]), axis=0)

    def chunk_mask(width):
        kc = lax.broadcasted_iota(jnp.int32, (width, tq), 0) // CHUNK
        qc = lax.broadcasted_iota(jnp.int32, (width, tq), 1) // CHUNK + (width - tq) // CHUNK
        return qc >= kc

    ml_scr[...] = jnp.full(ml_scr.shape, -jnp.inf, F32)

    def a_step(jt, carry):
        rows = pl.ds(pl.multiple_of(jt * tk, tk), tk)
        for hd in heads:
            s = lax.dot_general(k_ref[0, rows, cols[hd]], q_ref[0, :, cols[hd]], nt,
                                preferred_element_type=F32)
            s_full[hd, jt] = s
            ml_scr[hd] = jnp.maximum(ml_scr[hd], group_max(s))
        return carry

    lax.fori_loop(0, n_full, a_step, 0)

    def last_a(width, rows):
        mask = chunk_mask(width)
        for hd in heads:
            s = lax.dot_general(k_ref[0, rows, cols[hd]], q_ref[0, :, cols[hd]], nt,
                                preferred_element_type=F32)
            s = jnp.where(mask, s, -jnp.inf)
            s_last[hd, :width] = s
            m = jnp.max(jnp.maximum(ml_scr[hd], group_max(s)), axis=0, keepdims=True)
            m_scr[hd] = jnp.broadcast_to(m, m_scr.shape[1:])
            acc_scr[hd] = jnp.zeros(acc_scr.shape[1:], F32)

    pair_rows = pl.ds(pl.multiple_of(n_full * tk, tk), tk)
    pl.when(odd)(functools.partial(last_a, tk, pair_rows))
    pl.when(jnp.logical_not(odd))(functools.partial(last_a, tq, pl.ds(pl.multiple_of(i * tq, tq), tq)))

    def b_step(jt, carry):
        for hd in heads:
            p = jnp.exp2(s_full[hd, jt] - m_scr[hd, :1]).astype(BF16)
            acc_scr[hd] += jnp.dot(vt_ref[0, jt, vrows[hd], :], p, preferred_element_type=F32)
        return carry

    lax.fori_loop(0, n_full, b_step, 0)

    def last_b(width):
        for hd in heads:
            p = jnp.exp2(s_last[hd, :width] - m_scr[hd, :1]).astype(BF16)
            acc = acc_scr[hd] + jnp.dot(vt_ref[0, n_full, vrows[hd], :width], p,
                                        preferred_element_type=F32)
            ot_scr[V_DIM * hd:V_DIM * (hd + 1), :] = acc[:V_DIM] / acc[V_DIM:V_DIM + 1]

    pl.when(odd)(functools.partial(last_b, tk))
    pl.when(jnp.logical_not(odd))(functools.partial(last_b, tq))

    gm = gm_ref[0]
    mla = _rms(ot_scr[...].T, onorm_ref[...]) * (gm * jax.nn.sigmoid(gm))
    mix_ssm = [ms_ref[j, 0] for j in range(ms_ref.shape[0])]
    mix = jnp.concatenate(mix_ssm + [mla], axis=-1).astype(BF16)
    y_ref[0] = x_ref[0] + jnp.dot(mix, w_out_ref[...], preferred_element_type=F32)


def _attn_causal_call(q, k, vt, gm, mix_ssm, x, wts, *, tq):
    b, s, d = x.shape
    tk = vt.shape[3]
    ssm_w = gm.shape[2]
    assert tk == 2 * tq and s % tk == 0 and tq % CHUNK == 0
    row = lambda j, i: (j, i, 0)
    res = lambda j, i: (j, 0, 0)
    scratch = [pltpu.VMEM((MLA_HEADS, s // tk - 1, tk, tq), F32),
               pltpu.VMEM((MLA_HEADS, SUBLANES, tq), F32),
               pltpu.VMEM((MLA_HEADS, tk, tq), F32),
               pltpu.VMEM((MLA_HEADS, SUBLANES, tq), F32),
               pltpu.VMEM((MLA_HEADS, VT_ROWS, tq), F32),
               pltpu.VMEM((MLA_HEADS * V_DIM, tq), F32)]
    return pl.pallas_call(
        functools.partial(_attn_causal_kernel, tq=tq, tk=tk),
        out_shape=jax.ShapeDtypeStruct((b, s, d), F32),
        grid=(b, s // tq),
        in_specs=[pl.BlockSpec((1, tq, HEAD_W), row), pl.BlockSpec((1, s, HEAD_W), res),
                  pl.BlockSpec((1,) + vt.shape[1:], lambda j, i: (j, 0, 0, 0)),
                  pl.BlockSpec((1, tq, ssm_w), row),
                  pl.BlockSpec((mix_ssm.shape[0], 1, tq, LANES), lambda j, i: (0, j, i, 0)),
                  pl.BlockSpec((1, tq, d), row),
                  _full(wts['onorm_mla'].shape), _full(wts['w_out'].shape)],
        out_specs=pl.BlockSpec((1, tq, d), row),
        scratch_shapes=scratch,
        compiler_params=pltpu.CompilerParams(
            dimension_semantics=("arbitrary", "arbitrary"), vmem_limit_bytes=VMEM_LIMIT),
        name="attn",
    )(q, k, vt, gm, mix_ssm, x, wts['onorm_mla'], wts['w_out'])


def _out_kernel(ms_ref, mla_ref, x_ref, w_out_ref, y_ref):
    bb, tq, d = x_ref.shape
    rows = bb * tq
    mix = jnp.concatenate([ms_ref[j].reshape(rows, LANES).astype(BF16) for j in range(ms_ref.shape[0])]
                          + [mla_ref[...].reshape(rows, mla_ref.shape[2])], axis=-1)
    y = x_ref[...].reshape(rows, d) + jnp.dot(mix, w_out_ref[...], preferred_element_type=F32)
    y_ref[...] = y.reshape(bb, tq, d)


def _out_call(mix_ssm, mla, x, wts, *, bb):
    b, s, d = x.shape
    row = lambda j: (j, 0, 0)
    return pl.pallas_call(
        _out_kernel,
        out_shape=jax.ShapeDtypeStruct((b, s, d), F32),
        grid=(b // bb,),
        in_specs=[pl.BlockSpec((mix_ssm.shape[0], bb, s, LANES), lambda j: (0, j, 0, 0)),
                  pl.BlockSpec((bb, s, mla.shape[2]), row), pl.BlockSpec((bb, s, d), row),
                  _full(wts['w_out'].shape)],
        out_specs=pl.BlockSpec((bb, s, d), row),
        compiler_params=pltpu.CompilerParams(
            dimension_semantics=("arbitrary",), vmem_limit_bytes=VMEM_LIMIT),
        name="outproj",
    )(mix_ssm, mla, x, wts['w_out'])


def _attn_cached_call(q, k_full, ckv_full, k_last, v_last, gm, wts, *, tk):
    b, tq, _ = q.shape
    t_full, last_len = k_full.shape[1], k_last.shape[1]
    ssm_w = gm.shape[2]
    blk = lambda rows, width: pl.BlockSpec((1, rows, width), lambda j: (j, 0, 0))
    scratch = [pltpu.VMEM((MLA_HEADS, t_full // tk, tq, tk), F32), pltpu.VMEM((MLA_HEADS, tq, LANES), F32),
               pltpu.VMEM((MLA_HEADS, tq, last_len), F32), pltpu.VMEM((MLA_HEADS, tq, LANES), F32),
               pltpu.VMEM((MLA_HEADS * tq, ckv_full.shape[2]), F32), pltpu.VMEM((MLA_HEADS * tq, 1), F32)]
    return pl.pallas_call(
        functools.partial(_attn_cached_kernel, tq=tq, tk=tk),
        out_shape=jax.ShapeDtypeStruct((b, tq, ssm_w), BF16),
        grid=(b,),
        in_specs=[blk(tq, HEAD_W), blk(t_full, HEAD_W), blk(t_full, ckv_full.shape[2]),
                  blk(last_len, HEAD_W), blk(last_len, HEAD_W), blk(tq, ssm_w),
                  _full(wts['onorm_mla'].shape), _full(wts['w_uv'].shape)],
        out_specs=blk(tq, ssm_w),
        scratch_shapes=scratch,
        compiler_params=pltpu.CompilerParams(
            dimension_semantics=("arbitrary",), vmem_limit_bytes=VMEM_LIMIT),
        name="attn_cached",
    )(q, k_full, ckv_full, k_last, v_last, gm, wts['onorm_mla'], wts['w_uv'])


def _head_block_cols(w, pieces):
    k = w.shape[0]
    w3 = w.reshape(k, MLA_HEADS, w.shape[1] // MLA_HEADS)
    cols = [w3[:, :, p[0]:p[0] + p[1]] if isinstance(p, tuple) else jnp.zeros((k, MLA_HEADS, p), w.dtype)
            for p in pieces]
    return jnp.concatenate(cols, axis=-1).reshape(k, HEAD_W)


def _zoh_powers(lr, li, dt, n):
    mag = jnp.exp(lr * dt)
    ar, ai = mag * jnp.cos(li * dt), mag * jnp.sin(li * dt)
    pw = [(jnp.ones_like(ar), jnp.zeros_like(ai))]
    for _ in range(n):
        pr, pi = pw[-1]
        pw.append((pr * ar - pi * ai, pr * ai + pi * ar))
    return pw


def _disc_kernel(arow_ref, acol_ref, ldt_ref, bt_ref, cd_ref, ct_ref,
                 apr_ref, api_ref, wb_ref, wc_ref, wf_ref):
    r_blk, groups, cin = wb_ref.shape[1], wb_ref.shape[2], wb_ref.shape[3]
    n_state = acol_ref.shape[2]
    gpb = LANES // cin
    lr, li = arow_ref[0], arow_ref[1]
    dt = jnp.exp(ldt_ref[...])
    pw = _zoh_powers(lr, li, dt, r_blk)
    ar, ai = pw[1]
    den = lr * lr + li * li
    kr = ((ar - 1.0) * lr + ai * li) / den
    ki = (ai * lr - (ar - 1.0) * li) / den
    apr_ref[...], api_ref[...] = pw[r_blk]

    pwc = _zoh_powers(acol_ref[0], acol_ref[1], dt, r_blk)
    for j in range(r_blk):
        pr, pi = pwc[j + 1]
        wc_ref[0, j] = ct_ref[0] * pr - ct_ref[1] * pi
        wc_ref[1, j] = -(ct_ref[0] * pi + ct_ref[1] * pr)

    b_r = kr * bt_ref[0] - ki * bt_ref[1]
    b_i = kr * bt_ref[1] + ki * bt_ref[0]
    for i in range(r_blk):
        pr, pi = pw[r_blk - 1 - i]
        wb_ref[0, i] = pr * b_r - pi * b_i
        wb_ref[1, i] = pr * b_i + pi * b_r

    nt = (((1,), (1,)), ((), ()))
    first_copy = lax.broadcasted_iota(jnp.int32, (LANES, LANES), 1) < n_state
    block = lambda v, n: v[gpb * n:gpb * (n + 1)].reshape(LANES, LANES)
    for i in range(r_blk):
        for j in range(i):
            wf_ref[i, j] = jnp.zeros(wf_ref.shape[2:], F32)
    for m in range(r_blk):
        pr, pi = pw[m]
        ca_r = cd_ref[0] * pr - cd_ref[1] * pi
        ca_i = cd_ref[0] * pi + cd_ref[1] * pr
        for n in range(groups // gpb):
            f_t = (lax.dot_general(jnp.where(first_copy, block(b_r, n), 0.0), block(ca_r, n), nt,
                                   precision=lax.Precision.HIGHEST, preferred_element_type=F32)
                   - lax.dot_general(jnp.where(first_copy, block(b_i, n), 0.0), block(ca_i, n), nt,
                                     precision=lax.Precision.HIGHEST, preferred_element_type=F32))
            for i in range(r_blk - m):
                wf_ref[i, i + m, gpb * n:gpb * (n + 1)] = f_t.reshape(gpb, cin, LANES)


def _prepare_weights(norm_in, w_in, ssm_a_re, ssm_a_im, ssm_log_dt, ssm_b_re, ssm_b_im, ssm_c_re,
                     ssm_c_im, ssm_d, w_glu, b_glu, q_lora_norm, kv_lora_norm, w_uq, w_ukv,
                     q_nope_norm, k_nope_norm, q_rope_norm, k_rope_norm, out_norm_ssm,
                     out_norm_mla, w_out):
    groups, n_state = ssm_a_re.shape
    ssm_w = groups * SSM_GROUP
    row = lambda v: v.reshape(1, -1).astype(F32)
    tail = LANES - ROPE_DIM - NOPE_DIM

    o_kr = 2 * ssm_w + Q_LORA + KV_LORA
    w_in_p = jnp.concatenate(
        [w_in[:, :o_kr + ROPE_DIM], jnp.zeros((w_in.shape[0], LANES - ROPE_DIM), w_in.dtype),
         w_in[:, o_kr + ROPE_DIM:]], axis=1).astype(BF16)

    w_uq_p = _head_block_cols(w_uq, [(NOPE_DIM, ROPE_DIM), (0, NOPE_DIM), tail]).astype(BF16)
    w_uk_p = _head_block_cols(w_ukv, [ROPE_DIM, (0, NOPE_DIM), tail])
    w_v_lo = _head_block_cols(w_ukv, [(NOPE_DIM, V_DIM), LANES - V_DIM])
    w_v_hi = _head_block_cols(w_ukv, [LANES - V_DIM, (NOPE_DIM, V_DIM)])
    odd_head = (np.arange(HEAD_W) // LANES) % 2 == 1
    w_ukv_p = jnp.concatenate([w_uk_p, jnp.where(odd_head[None, :], w_v_hi, w_v_lo)], axis=1).astype(BF16)
    wvt = w_v_lo.reshape(KV_LORA, MLA_HEADS, LANES)[:, :, :VT_ROWS]
    wvt = jnp.transpose(wvt, (1, 2, 0)).reshape(MLA_HEADS * VT_ROWS, KV_LORA).astype(BF16)

    scale = (NOPE_DIM + ROPE_DIM) ** -0.5 * np.log2(np.e)
    vones = np.zeros((MLA_HEADS, LANES), np.float32)
    vones[0::2, V_DIM] = 1.0
    vones[1::2, 0] = 1.0
    vones = jnp.asarray(vones.reshape(1, HEAD_W))
    zeros = lambda n: jnp.zeros((n,), F32)
    gq =jnp.concatenate([q_rope_norm, q_nope_norm, zeros(tail)]) * scale
    gkr = jnp.concatenate([k_rope_norm, zeros(LANES - ROPE_DIM)])
    gkn = jnp.concatenate([zeros(ROPE_DIM), k_nope_norm, zeros(tail)])

    seg = np.zeros((LANES, LANES), np.float32)
    seg[:ROPE_DIM, :ROPE_DIM] = 1.0 / ROPE_DIM
    seg[ROPE_DIM:ROPE_DIM + NOPE_DIM, ROPE_DIM:ROPE_DIM + NOPE_DIM] = 1.0 / NOPE_DIM
    seg[ROPE_DIM + NOPE_DIM:, ROPE_DIM + NOPE_DIM:] = 1.0 / tail
    seg = jnp.asarray(np.kron(np.eye(2, dtype=np.float32), seg), BF16)
    place = jnp.asarray(np.eye(ROPE_DIM, LANES, dtype=np.float32), BF16)

    r_blk = SSM_STEPS_PER_SCAN
    lane_rep = lambda x: jnp.tile(x, (1,) * (x.ndim - 1) + (LANES // x.shape[-1],))
    a2 = jnp.stack([ssm_a_re, ssm_a_im]).astype(F32)
    b2 = jnp.stack([ssm_b_re, ssm_b_im]).astype(F32)
    c2 = jnp.stack([ssm_c_re, ssm_c_im]).astype(F32)
    ldt = ssm_log_dt.astype(F32)
    gl = jax.ShapeDtypeStruct((groups, 1, LANES), F32)
    a_re, a_im, wb5, wc5, wf5 = pl.pallas_call(
        _disc_kernel,
        out_shape=[gl, gl,
                   jax.ShapeDtypeStruct((2, r_blk, groups, SSM_GROUP, LANES), F32),
                   jax.ShapeDtypeStruct((2, r_blk, groups, n_state, LANES), F32),
                   jax.ShapeDtypeStruct((r_blk, r_blk, groups, SSM_GROUP, LANES), F32)],
        name="disc")(
        lane_rep(a2)[:, :, None, :], jnp.broadcast_to(a2[..., None], a2.shape + (LANES,)),
        jnp.broadcast_to(ldt[:, None, None], (groups, 1, LANES)),
        lane_rep(jnp.swapaxes(b2, 2, 3)), lane_rep(c2), lane_rep(jnp.swapaxes(c2, 2, 3)))
    a_re, a_im = a_re[:, 0, :n_state], a_im[:, 0, :n_state]

    return dict(
        ssm_w=ssm_w,
        norm_in=row(norm_in), w_in=w_in_p, qln=row(q_lora_norm), kvln=row(kv_lora_norm),
        w_uq=w_uq_p, w_ukv=w_ukv_p, w_uk=w_ukv_p[:, :HEAD_W], w_uv=w_ukv_p[:, HEAD_W:], wvt=wvt,
        gq=row(gq), gkr=row(gkr), gkn=row(gkn), seg=seg, vones=vones,
        place=place, a_re=row(a_re), a_im=row(a_im), wb5=wb5, wc5=wc5, wf5=wf5,
        ssm_d=row(ssm_d), w_glu=w_glu.astype(BF16), b_glu=row(b_glu), onorm_ssm=row(out_norm_ssm),
        onorm_mla=row(out_norm_mla), w_out=w_out.astype(BF16))


def _rope_tables(start, n):
    half = ROPE_DIM // 2
    inv = ROPE_THETA ** (-jnp.arange(half, dtype=F32) / half)
    ang = (start + jnp.arange(n)).astype(F32)[:, None] * inv[None, :]
    cos, sin = jnp.cos(ang), jnp.sin(ang)
    cos_t = jnp.concatenate([cos, cos, jnp.ones((n, LANES - ROPE_DIM), F32)], axis=1)
    sin_a = jnp.concatenate([-sin, jnp.zeros((n, LANES - half), F32)], axis=1)
    sin_b = jnp.concatenate([jnp.zeros((n, half), F32), sin, jnp.zeros((n, LANES - ROPE_DIM), F32)],
                            axis=1)
    return cos_t, sin_a, sin_b


def _mixer(x, pos0, h0re, h0im, past, wts, *, proj_bb, proj_tq, ssm_steps, attn_tk):
    u, gs, gm, q, k, v, ckv, kr = _proj_call(x, _rope_tables(pos0, x.shape[1]), wts,
                                             bb=proj_bb, tq=proj_tq)
    mix_ssm, hre, him = _ssm_call(u, gs, h0re, h0im, wts, steps=ssm_steps, unroll=True)
    if past is None:
        y = _attn_causal_call(q, k, v, gm, mix_ssm, x, wts, tq=attn_tk // 2)
    else:
        k_past = _expand_call(*past, wts, tq=512)
        mla = _attn_cached_call(q, k_past, past[0], k, v, gm, wts, tk=attn_tk)
        y = _out_call(mix_ssm, mla, x, wts, bb=proj_bb)
    if proj_bb == 1:
        kr = jnp.swapaxes(kr, 1, 2)
    return y, ckv, kr, hre, him


def kernel(x_prompt, x_sample, cache_ckv, cache_krope, state_ssm_re, state_ssm_im, norm_in, w_in, ssm_a_re, ssm_a_im, ssm_log_dt, ssm_b_re, ssm_b_im, ssm_c_re, ssm_c_im, ssm_d, w_glu, b_glu, q_lora_norm, kv_lora_norm, w_uq, w_ukv, q_nope_norm, k_nope_norm, q_rope_norm, k_rope_norm, out_norm_ssm, out_norm_mla, w_out):
    depth = norm_in.shape[0]
    assert depth == 1, "single mixer layer"
    params = (norm_in, w_in, ssm_a_re, ssm_a_im, ssm_log_dt, ssm_b_re, ssm_b_im, ssm_c_re, ssm_c_im,
              ssm_d, w_glu, b_glu, q_lora_norm, kv_lora_norm, w_uq, w_ukv, q_nope_norm, k_nope_norm,
              q_rope_norm, k_rope_norm, out_norm_ssm, out_norm_mla, w_out)
    drop_depth = lambda a: a.reshape(a.shape[1:])
    wts = _prepare_weights(*[drop_depth(p) for p in params])
    groups, n_state = ssm_a_re.shape[1:]
    bp, sp, _ = x_prompt.shape
    bs, ss, _ = x_sample.shape
    past_len = cache_ckv.shape[2]

    zero_state = jnp.zeros((bp, groups * n_state), F32)
    yp, ckv_p, kr_p, re_p, im_p = _mixer(
        x_prompt, 0, zero_state, zero_state, None, wts,
        proj_bb=1, proj_tq=512, ssm_steps=64, attn_tk=512)
    ys, ckv_s, kr_s, re_s, im_s = _mixer(
        x_sample, past_len,
        state_ssm_re.reshape(bs, groups * n_state), state_ssm_im.reshape(bs, groups * n_state),
        (drop_depth(cache_ckv), jnp.swapaxes(drop_depth(cache_krope), 1, 2)), wts,
        proj_bb=bs // 2, proj_tq=ss, ssm_steps=ss, attn_tk=past_len)

    st = lambda h, bb: h.reshape(1, bb, groups, n_state)
    return (yp, ys, ckv_p[None], kr_p[None], st(re_p, bp), st(im_p, bp),
            ckv_s[None], kr_s[None], st(re_s, bs), st(im_s, bs))
```

```python
import functools

import numpy as np
import jax
import jax.numpy as jnp
from jax import lax
from jax.experimental import pallas as pl
from jax.experimental.pallas import tpu as pltpu

F32 = jnp.float32
BF16 = jnp.bfloat16

CHUNK = 64
SSM_GROUP = 16
SSM_STATE = 64
MLA_HEADS = 8
NOPE_DIM = 64
ROPE_DIM = 32
V_DIM = 64
Q_LORA = 256
KV_LORA = 128
ROPE_THETA = 10000.0
EPS = 1e-6

LANES = 128
SUBLANES = 8
VT_ROWS = 80
HEAD_W = MLA_HEADS * LANES
GROUPS_PER_BLOCK = 8
SSM_STEPS_PER_SCAN = 4
VMEM_LIMIT = 56 * 1024 * 1024


def _rms(x, gain):
    return x * lax.rsqrt(jnp.mean(x * x, axis=-1, keepdims=True) + EPS) * gain


def _seg_rms(x, seg):
    ms = jnp.dot((x * x).astype(BF16), seg, preferred_element_type=F32)
    return x * lax.rsqrt(ms + EPS)


def _rope_block(x, cos_t, sin_a, sin_b):
    return (x * cos_t + pltpu.roll(x, LANES - ROPE_DIM // 2, 1) * sin_a
            + pltpu.roll(x, ROPE_DIM // 2, 1) * sin_b)


def _expand_kv(ckv, kr_blk, w_ukv_ref, gkn_ref, seg_ref, vones_ref, k_ref, v_ref):
    kv = jnp.dot(ckv.astype(BF16), w_ukv_ref[...], preferred_element_type=F32)
    gkn = gkn_ref[...]
    blk3 = k_ref.shape[:2] + (LANES,)
    for p in range(MLA_HEADS // 2):
        kn = _seg_rms(kv[:, 2 * LANES * p:2 * LANES * (p + 1)], seg_ref[...])
        for j in range(2):
            h = 2 * p + j
            blk = kn[:, LANES * j:LANES * (j + 1)] * gkn + kr_blk
            k_ref[:, :, LANES * h:LANES * (h + 1)] = blk.astype(BF16).reshape(blk3)
    if v_ref is not None:
        v_ref[...] = (kv[:, HEAD_W:] + vones_ref[...]).astype(BF16).reshape(v_ref.shape)


def _proj_kernel(x_ref, cos_ref, sa_ref, sb_ref, norm_in_ref, w_in_ref, qln_ref, kvln_ref,
                 w_uq_ref, w_ukv_ref, gq_ref, gkr_ref, gkn_ref, seg_ref, vones_ref,
                 u_ref, gs_ref, gm_ref, q_ref, k_ref, v_ref, ckv_ref, kr_ref, *, ssm_w, transposed):
    bb, tq, d = x_ref.shape
    x = x_ref[...].reshape(bb * tq, d)
    h = _rms(x, norm_in_ref[...])
    z = jnp.dot(h.astype(BF16), w_in_ref[...], preferred_element_type=F32)
    o = 0
    for dst in (u_ref, gs_ref):
        for j in range(ssm_w // LANES):
            dst[j] = z[:, o + LANES * j:o + LANES * (j + 1)].reshape(bb, tq, LANES)
        o += ssm_w
    c_q = z[:, o:o + Q_LORA]
    o += Q_LORA
    c_kv = z[:, o:o + KV_LORA]
    o += KV_LORA
    kr_raw = z[:, o:o + LANES]
    o += LANES
    gm_ref[...] = z[:, o:].reshape(gm_ref.shape)

    per_row = lambda t_ref: jnp.concatenate([t_ref[...]] * bb, axis=0)
    cos_t, sin_a, sin_b = per_row(cos_ref), per_row(sa_ref), per_row(sb_ref)
    seg = seg_ref[...]

    kr_ms = jnp.dot((kr_raw * kr_raw).astype(BF16), seg[:LANES, :LANES], preferred_element_type=F32)
    kr_blk = _rope_block(kr_raw * lax.rsqrt(kr_ms + EPS) * gkr_ref[...], cos_t, sin_a, sin_b)
    if transposed:
        kr_ref[0] = kr_blk.T[:ROPE_DIM]
    else:
        kr_ref[...] = kr_blk[:, :ROPE_DIM].reshape(kr_ref.shape)

    q = jnp.dot(_rms(c_q, qln_ref[...]).astype(BF16), w_uq_ref[...], preferred_element_type=F32)
    gq = gq_ref[...]
    for p in range(MLA_HEADS // 2):
        qn = _seg_rms(q[:, 2 * LANES * p:2 * LANES * (p + 1)], seg)
        for j in range(2):
            hd = 2 * p + j
            blk = _rope_block(qn[:, LANES * j:LANES * (j + 1)] * gq, cos_t, sin_a, sin_b)
            q_ref[:, :, LANES * hd:LANES * (hd + 1)] = blk.astype(BF16).reshape(bb, tq, LANES)

    ckv = _rms(c_kv, kvln_ref[...])
    ckv_ref[...] = ckv.reshape(ckv_ref.shape)
    if transposed:
        _expand_kv(ckv, kr_blk, w_ukv_ref, gkn_ref, seg_ref, None, k_ref, None)
        v_t = jnp.dot(vones_ref[...], ckv.T.astype(BF16), preferred_element_type=F32)
        ones_row = lax.broadcasted_iota(jnp.int32, v_t.shape, 0) % VT_ROWS == V_DIM
        v_ref[0, 0] = jnp.where(ones_row, 1.0, v_t).astype(BF16)
    else:
        _expand_kv(ckv, kr_blk, w_ukv_ref, gkn_ref, seg_ref, vones_ref, k_ref, v_ref)


def _expand_kernel(ckv_ref, kr_ref, w_uk_ref, gkn_ref, seg_ref, place_ref, k_ref):
    kr_blk = lax.dot_general(kr_ref[0].astype(BF16), place_ref[...], (((0,), (0,)), ((), ())),
                             preferred_element_type=F32)
    _expand_kv(ckv_ref[0], kr_blk, w_uk_ref, gkn_ref, seg_ref, None, k_ref, None)


def _full(shape):
    n = len(shape)
    return pl.BlockSpec(shape, lambda *_: (0,) * n)


def _proj_call(x, tables, wts, *, bb, tq):
    b, s, d = x.shape
    ssm_w = wts['ssm_w']
    cos_t, sin_a, sin_b = tables
    transposed = bb == 1
    w_kv, v_aux = (wts['w_uk'], wts['wvt']) if transposed else (wts['w_ukv'], wts['vones'])
    grid = (s // tq, b // bb)
    row = lambda i, j: (j, i, 0)
    tab = pl.BlockSpec((tq, LANES), lambda i, j: (i, 0))
    in_specs = [pl.BlockSpec((bb, tq, d), row), tab, tab, tab,
                _full(wts['norm_in'].shape), _full(wts['w_in'].shape), _full(wts['qln'].shape),
                _full(wts['kvln'].shape), _full(wts['w_uq'].shape), _full(w_kv.shape),
                _full(wts['gq'].shape), _full(wts['gkr'].shape), _full(wts['gkn'].shape),
                _full(wts['seg'].shape), _full(v_aux.shape)]
    slabs = ssm_w // LANES
    tb_spec = pl.BlockSpec((slabs, bb, tq, LANES), lambda i, j: (0, j, i, 0))
    kr_shape, kr_spec = (((b, ROPE_DIM, s), pl.BlockSpec((1, ROPE_DIM, tq), lambda i, j: (j, 0, i)))
                         if transposed else
                         ((b, s, ROPE_DIM), pl.BlockSpec((bb, tq, ROPE_DIM), row)))
    v_shape, v_spec = (((b, s // tq, MLA_HEADS * VT_ROWS, tq),
                        pl.BlockSpec((1, 1, MLA_HEADS * VT_ROWS, tq), lambda i, j: (j, i, 0, 0)))
                       if transposed else
                       ((b, s, HEAD_W), pl.BlockSpec((bb, tq, HEAD_W), row)))
    out_shape = [jax.ShapeDtypeStruct((slabs, b, s, LANES), F32),
                 jax.ShapeDtypeStruct((slabs, b, s, LANES), F32),
                 jax.ShapeDtypeStruct((b, s, ssm_w), F32),
                 jax.ShapeDtypeStruct((b, s, HEAD_W), BF16),
                 jax.ShapeDtypeStruct((b, s, HEAD_W), BF16),
                 jax.ShapeDtypeStruct(v_shape, BF16),
                 jax.ShapeDtypeStruct((b, s, KV_LORA), F32),
                 jax.ShapeDtypeStruct(kr_shape, F32)]
    out_specs = [tb_spec, tb_spec,
                 pl.BlockSpec((bb, tq, ssm_w), row),
                 pl.BlockSpec((bb, tq, HEAD_W), row), pl.BlockSpec((bb, tq, HEAD_W), row), v_spec,
                 pl.BlockSpec((bb, tq, KV_LORA), row), kr_spec]
    return pl.pallas_call(
        functools.partial(_proj_kernel, ssm_w=ssm_w, transposed=transposed),
        out_shape=out_shape, grid=grid, in_specs=in_specs, out_specs=out_specs,
        compiler_params=pltpu.CompilerParams(
            dimension_semantics=("arbitrary", "arbitrary"), vmem_limit_bytes=VMEM_LIMIT),
        name="proj",
    )(x, cos_t, sin_a, sin_b, wts['norm_in'], wts['w_in'], wts['qln'], wts['kvln'], wts['w_uq'],
      w_kv, wts['gq'], wts['gkr'], wts['gkn'], wts['seg'], v_aux)


def _expand_call(ckv, kr, wts, *, tq):
    b, t, _ = ckv.shape
    row = lambda j, i: (j, i, 0)
    return pl.pallas_call(
        _expand_kernel,
        out_shape=jax.ShapeDtypeStruct((b, t, HEAD_W), BF16),
        grid=(b, t // tq),
        in_specs=[pl.BlockSpec((1, tq, KV_LORA), row),
                  pl.BlockSpec((1, ROPE_DIM, tq), lambda j, i: (j, 0, i)),
                  _full(wts['w_uk'].shape), _full(wts['gkn'].shape), _full(wts['seg'].shape),
                  _full(wts['place'].shape)],
        out_specs=pl.BlockSpec((1, tq, HEAD_W), row),
        compiler_params=pltpu.CompilerParams(
            dimension_semantics=("arbitrary", "arbitrary"), vmem_limit_bytes=VMEM_LIMIT),
        name="expand",
    )(ckv, kr, wts['w_uk'], wts['gkn'], wts['seg'], wts['place'])


def _ssm_kernel(u_ref, gs_ref, h0re_ref, h0im_ref, are_ref, aim_ref, wb5_ref, wc5_ref, wf5_ref, d_ref,
                w_glu_ref, b_glu_ref, onorm_ref, mix_ref, hre_ref, him_ref, us_ref, xs_ref, y_ref,
                wb_ref, wc_ref, wf_ref, *, batch, steps, unroll):
    n_blocks, r_blk = wb_ref.shape[0], wb_ref.shape[1] // LANES
    half = wb_ref.shape[2] // 2
    gpb = wb5_ref.shape[2] // n_blocks
    n_k = steps // r_blk

    @pl.when(pl.program_id(0) == 0)
    def _():
        hre_ref[...] = h0re_ref[...]
        him_ref[...] = h0im_ref[...]

        def own_group(shape, row_div, lane_div):
            return (lax.broadcasted_iota(jnp.int32, shape, 0) // row_div
                    == lax.broadcasted_iota(jnp.int32, shape, 1) // lane_div)

        cin, n_state = wb5_ref.shape[3], half // gpb
        m_b = own_group((LANES, half), cin, n_state)
        m_c = own_group((half, LANES), n_state, cin)
        m_f = own_group((LANES, LANES), cin, cin)
        for gb in range(n_blocks):
            grp = slice(gpb * gb, gpb * (gb + 1))
            for ri in range(2):
                for i in range(r_blk):
                    piece = wb5_ref[ri, i, grp].reshape(LANES, LANES)
                    wide = jnp.concatenate([piece] * (half // LANES), axis=1)
                    wb_ref[gb, LANES * i:LANES * (i + 1), half * ri:half * (ri + 1)] = (
                        jnp.where(m_b, wide, 0.0).astype(BF16))
                    piece = wc5_ref[ri, i, grp].reshape(half, LANES)
                    wc_ref[gb, half * ri:half * (ri + 1), LANES * i:LANES * (i + 1)] = (
                        jnp.where(m_c, piece, 0.0).astype(BF16))
            for i in range(r_blk):
                for j in range(r_blk):
                    piece = wf5_ref[i, j, grp].reshape(LANES, LANES)
                    wf_ref[gb, LANES * i:LANES * (i + 1), LANES * j:LANES * (j + 1)] = (
                        jnp.where(m_f, piece, 0.0).astype(BF16))

    pitch = us_ref.shape[2] // n_k
    for b in range(batch):
        for j in range(n_blocks):
            for i in range(r_blk):
                us_ref[j, i, pl.ds(b, n_k, stride=pitch), :] = (
                    u_ref[j, b, pl.ds(i, n_k, stride=r_blk), :])

    def slab_rows(j, i):
        return jnp.concatenate([us_ref[j, i, pitch * k:pitch * k + batch, :] for k in range(n_k)], axis=0)

    u_slabs = [[slab_rows(j, i) for i in range(r_blk)] for j in range(n_blocks)]
    for gb in range(n_blocks):
        lhs = jnp.concatenate(u_slabs[gb], axis=-1).astype(BF16)
        xs = xs_ref
        xs[...] = jnp.dot(lhs, wb_ref[gb], preferred_element_type=F32)
        cols = slice(half * gb, half * (gb + 1))
        a_re = jnp.broadcast_to(are_ref[:, cols], (batch, half))
        a_im = jnp.broadcast_to(aim_ref[:, cols], (batch, half))

        def step(k, carry, xs=xs, a_re=a_re, a_im=a_im):
            h_re, h_im = carry
            rows = pl.ds(pl.multiple_of(k * batch, batch), batch)
            n_re = a_re * h_re - a_im * h_im + xs[rows, :half]
            n_im = a_re * h_im + a_im * h_re + xs[rows, half:]
            xs[rows, :half] = h_re
            xs[rows, half:] = h_im
            return n_re, n_im

        h_re, h_im = lax.fori_loop(0, n_k, step, (hre_ref[:, cols], him_ref[:, cols]), unroll=unroll)
        hre_ref[:, cols] = h_re
        him_ref[:, cols] = h_im
        y = (jnp.dot(xs[...].astype(BF16), wc_ref[gb], preferred_element_type=F32)
             + jnp.dot(lhs, wf_ref[gb], preferred_element_type=F32))
        for i in range(r_blk):
            y_ref[i, :, LANES * gb:LANES * (gb + 1)] = y[:, LANES * i:LANES * (i + 1)]

    rows = r_blk * n_k * batch
    u = jnp.concatenate([jnp.concatenate(u_slabs[j], axis=0) for j in range(n_blocks)], axis=-1)
    y = y_ref[...].reshape(rows, n_blocks * LANES) + d_ref[...] * u
    yg = jax.nn.gelu(y)
    glu = jnp.dot(yg.astype(BF16), w_glu_ref[...], preferred_element_type=F32) + b_glu_ref[...]
    out = _rms(yg * jax.nn.sigmoid(glu), onorm_ref[...])
    for j in range(n_blocks):
        for i in range(r_blk):
            for k in range(n_k):
                r0 = (i * n_k + k) * batch
                us_ref[j, i, pitch * k:pitch * k + batch, :] = out[r0:r0 + batch, LANES * j:LANES * (j + 1)]
    for b in range(batch):
        for j in range(n_blocks):
            for i in range(r_blk):
                tok = pl.ds(i, n_k, stride=r_blk)
                gs = gs_ref[j, b, tok, :]
                o = us_ref[j, i, pl.ds(b, n_k, stride=pitch), :]
                mix_ref[j, b, tok, :] = o * (gs * jax.nn.sigmoid(gs))


def _ssm_call(u, gs, h0re, h0im, wts, *, steps, unroll):
    n_blocks, batch, seq, _ = u.shape
    n_state = h0re.shape[1]
    r_blk = wts['wb5'].shape[1]
    width = 2 * n_state // n_blocks
    assert steps % r_blk == 0 and wts['wb5'].shape[3] * GROUPS_PER_BLOCK == LANES
    blk = pl.BlockSpec((n_blocks, batch, steps, LANES), lambda i: (0, 0, i, 0))
    names = ['a_re', 'a_im', 'wb5', 'wc5', 'wf5', 'ssm_d', 'w_glu', 'b_glu', 'onorm_ssm']
    n_rows = steps // r_blk * batch
    return pl.pallas_call(
        functools.partial(_ssm_kernel, batch=batch, steps=steps, unroll=unroll),
        out_shape=[jax.ShapeDtypeStruct(u.shape, F32),
                   jax.ShapeDtypeStruct((batch, n_state), F32),
                   jax.ShapeDtypeStruct((batch, n_state), F32)],
        grid=(seq // steps,),
        in_specs=[blk, blk, _full(h0re.shape), _full(h0im.shape)] + [_full(wts[n].shape) for n in names],
        out_specs=[blk, _full((batch, n_state)), _full((batch, n_state))],
        scratch_shapes=[pltpu.VMEM((n_blocks, r_blk, steps // r_blk * (batch + SUBLANES), LANES), F32),
                        pltpu.VMEM((n_rows, width), F32),
                        pltpu.VMEM((r_blk, n_rows, n_blocks * LANES), F32),
                        pltpu.VMEM((n_blocks, r_blk * LANES, width), BF16),
                        pltpu.VMEM((n_blocks, width, r_blk * LANES), BF16),
                        pltpu.VMEM((n_blocks, r_blk * LANES, r_blk * LANES), BF16)],
        compiler_params=pltpu.CompilerParams(
            dimension_semantics=("arbitrary",), vmem_limit_bytes=VMEM_LIMIT),
        name="ssm",
    )(u, gs, h0re, h0im, *[wts[n] for n in names])


def _attn_cached_kernel(q_ref, kf_ref, cf_ref, kl_ref, vl_ref, gm_ref, onorm_ref, w_uv_ref, y_ref,
                        s_full, ml_scr, s_last, m_scr, ctx_scr, den_scr, *, tq, tk):
    n_full = kf_ref.shape[1] // tk
    dn = (((1,), (1,)), ((), ()))
    lane = lax.broadcasted_iota(jnp.int32, (tq, LANES), 1)
    heads = range(MLA_HEADS)
    cols = [slice(LANES * hd, LANES * (hd + 1)) for hd in heads]

    def lane_tiles(s):
        return [s[:, LANES * c:LANES * (c + 1)] for c in range(s.shape[1] // LANES)]

    def probs(s, m_rep):
        if s.shape[1] % LANES == 0:
            p = jnp.concatenate([jnp.exp2(t - m_rep) for t in lane_tiles(s)], axis=-1)
        else:
            p = jnp.exp2(s - m_rep[:, :1])
        return p.astype(BF16)

    ml_scr[...] = jnp.full(ml_scr.shape, -jnp.inf, F32)

    def a_step(jt, carry):
        rows = pl.ds(pl.multiple_of(jt * tk, tk), tk)
        for hd in heads:
            s = lax.dot_general(q_ref[0, :, cols[hd]], kf_ref[0, rows, cols[hd]], dn,
                                preferred_element_type=F32)
            s_full[hd, jt] = s
            ml_scr[hd] = functools.reduce(jnp.maximum, lane_tiles(s), ml_scr[hd])
        return carry

    lax.fori_loop(0, n_full, a_step, 0)

    for hd in heads:
        s = lax.dot_general(q_ref[0, :, cols[hd]], kl_ref[0, :, cols[hd]], dn,
                            preferred_element_type=F32)
        s_last[hd] = s
        m = jnp.maximum(jnp.max(s, axis=-1, keepdims=True),
                        jnp.max(ml_scr[hd], axis=-1, keepdims=True))
        m_scr[hd] = jnp.broadcast_to(m, (tq, LANES))
    ctx_scr[...] = jnp.zeros(ctx_scr.shape, F32)
    den_scr[...] = jnp.zeros(den_scr.shape, F32)

    def b_step(jt, carry):
        rows = pl.ds(pl.multiple_of(jt * tk, tk), tk)
        p = jnp.concatenate([probs(s_full[hd, jt], m_scr[hd]) for hd in heads], axis=0)
        ctx_scr[...] += jnp.dot(p, cf_ref[0, rows, :].astype(BF16), preferred_element_type=F32)
        den_scr[...] += jnp.sum(p.astype(F32), axis=-1, keepdims=True)
        return carry

    lax.fori_loop(0, n_full, b_step, 0)

    outs = []
    for hd in heads:
        mine = slice(tq * hd, tq * (hd + 1))
        cached = jnp.dot(ctx_scr[mine].astype(BF16), w_uv_ref[:, cols[hd]], preferred_element_type=F32)
        p = probs(s_last[hd], m_scr[hd])
        new = jnp.dot(p, vl_ref[0, :, cols[hd]], preferred_element_type=F32)
        ones_col = V_DIM if hd % 2 == 0 else 0
        outs.append((cached + new) / (den_scr[mine] + new[:, ones_col:ones_col + 1]))
    attn = jnp.concatenate([jnp.where(lane < V_DIM, outs[e], outs[e + 1])
                            for e in range(0, MLA_HEADS, 2)], axis=-1)
    gm = gm_ref[0]
    y_ref[0] = (_rms(attn, onorm_ref[...]) * (gm * jax.nn.sigmoid(gm))).astype(BF16)


def _attn_causal_kernel(q_ref, k_ref, vt_ref, gm_ref, ms_ref, x_ref, onorm_ref, w_out_ref, y_ref,
                        qt_scr, s_full, ml_scr, s_last, m_scr, acc_scr, ot_scr, *, tq, tk):
    i = pl.program_id(1)
    n_full = (i * tq) // tk
    odd = i % 2 == 1
    heads = range(MLA_HEADS)
    cols = [slice(LANES * hd, LANES * (hd + 1)) for hd in heads]
    vrows = [slice(VT_ROWS * hd, VT_ROWS * (hd + 1)) for hd in heads]

    def group_max(s):
        return jnp.max(s.reshape(s.shape[0] // SUBLANES, SUBLANES, s.shape[1]), axis=0)

    def chunk_mask(width):
        kc = lax.broadcasted_iota(jnp.int32, (width, tq), 0) // CHUNK
        qc = lax.broadcasted_iota(jnp.int32, (width, tq), 1) // CHUNK + (width - tq) // CHUNK
        return qc >= kc

    ml_scr[...] = jnp.full(ml_scr.shape, -jnp.inf, F32)
    qt_scr[...] = q_ref[0].T

    def a_step(jt, carry):
        rows = pl.ds(pl.multiple_of(jt * tk, tk), tk)
        for hd in heads:
            s = jnp.dot(k_ref[0, rows, cols[hd]], qt_scr[cols[hd], :], preferred_element_type=F32)
            s_full[hd, jt] = s
            ml_scr[hd] = jnp.maximum(ml_scr[hd], group_max(s))
        return carry

    lax.fori_loop(0, n_full, a_step, 0)

    def last_a(width, rows):
        mask = chunk_mask(width)
        for hd in heads:
            s = jnp.dot(k_ref[0, rows, cols[hd]], qt_scr[cols[hd], :], preferred_element_type=F32)
            s = jnp.where(mask, s, -jnp.inf)
            s_last[hd, :width] = s
            m = jnp.max(jnp.maximum(ml_scr[hd], group_max(s)), axis=0, keepdims=True)
            m_scr[hd] = jnp.broadcast_to(m, m_scr.shape[1:])
            acc_scr[hd] = jnp.zeros(acc_scr.shape[1:], F32)

    pair_rows = pl.ds(pl.multiple_of(n_full * tk, tk), tk)
    pl.when(odd)(functools.partial(last_a, tk, pair_rows))
    pl.when(jnp.logical_not(odd))(functools.partial(last_a, tq, pl.ds(pl.multiple_of(i * tq, tq), tq)))

    def b_step(jt, carry):
        for hd in heads:
            p = jnp.exp2(s_full[hd, jt] - m_scr[hd, :1]).astype(BF16)
            acc_scr[hd] += jnp.dot(vt_ref[0, jt, vrows[hd], :], p, preferred_element_type=F32)
        return carry

    lax.fori_loop(0, n_full, b_step, 0)

    def last_b(width):
        for hd in heads:
            p = jnp.exp2(s_last[hd, :width] - m_scr[hd, :1]).astype(BF16)
            acc = acc_scr[hd] + jnp.dot(vt_ref[0, n_full, vrows[hd], :width], p,
                                        preferred_element_type=F32)
            ot_scr[V_DIM * hd:V_DIM * (hd + 1), :] = acc[:V_DIM] / acc[V_DIM:V_DIM + 1]

    pl.when(odd)(functools.partial(last_b, tk))
    pl.when(jnp.logical_not(odd))(functools.partial(last_b, tq))

    gm = gm_ref[0]
    mla = _rms(ot_scr[...].T, onorm_ref[...]) * (gm * jax.nn.sigmoid(gm))
    mix_ssm = [ms_ref[j, 0] for j in range(ms_ref.shape[0])]
    mix = jnp.concatenate(mix_ssm + [mla], axis=-1).astype(BF16)
    y_ref[0] = x_ref[0] + jnp.dot(mix, w_out_ref[...], preferred_element_type=F32)


def _attn_causal_call(q, k, vt, gm, mix_ssm, x, wts, *, tq):
    b, s, d = x.shape
    tk = vt.shape[3]
    ssm_w = gm.shape[2]
    assert tk == 2 * tq and s % tk == 0 and tq % CHUNK == 0
    row = lambda j, i: (j, i, 0)
    res = lambda j, i: (j, 0, 0)
    scratch = [pltpu.VMEM((HEAD_W, tq), BF16),
               pltpu.VMEM((MLA_HEADS, s // tk - 1, tk, tq), F32),
               pltpu.VMEM((MLA_HEADS, SUBLANES, tq), F32),
               pltpu.VMEM((MLA_HEADS, tk, tq), F32),
               pltpu.VMEM((MLA_HEADS, SUBLANES, tq), F32),
               pltpu.VMEM((MLA_HEADS, VT_ROWS, tq), F32),
               pltpu.VMEM((MLA_HEADS * V_DIM, tq), F32)]
    return pl.pallas_call(
        functools.partial(_attn_causal_kernel, tq=tq, tk=tk),
        out_shape=jax.ShapeDtypeStruct((b, s, d), F32),
        grid=(b, s // tq),
        in_specs=[pl.BlockSpec((1, tq, HEAD_W), row), pl.BlockSpec((1, s, HEAD_W), res),
                  pl.BlockSpec((1,) + vt.shape[1:], lambda j, i: (j, 0, 0, 0)),
                  pl.BlockSpec((1, tq, ssm_w), row),
                  pl.BlockSpec((mix_ssm.shape[0], 1, tq, LANES), lambda j, i: (0, j, i, 0)),
                  pl.BlockSpec((1, tq, d), row),
                  _full(wts['onorm_mla'].shape), _full(wts['w_out'].shape)],
        out_specs=pl.BlockSpec((1, tq, d), row),
        scratch_shapes=scratch,
        compiler_params=pltpu.CompilerParams(
            dimension_semantics=("arbitrary", "arbitrary"), vmem_limit_bytes=VMEM_LIMIT),
        name="attn",
    )(q, k, vt, gm, mix_ssm, x, wts['onorm_mla'], wts['w_out'])


def _out_kernel(ms_ref, mla_ref, x_ref, w_out_ref, y_ref):
    bb, tq, d = x_ref.shape
    rows = bb * tq
    mix = jnp.concatenate([ms_ref[j].reshape(rows, LANES).astype(BF16) for j in range(ms_ref.shape[0])]
                          + [mla_ref[...].reshape(rows, mla_ref.shape[2])], axis=-1)
    y = x_ref[...].reshape(rows, d) + jnp.dot(mix, w_out_ref[...], preferred_element_type=F32)
    y_ref[...] = y.reshape(bb, tq, d)


def _out_call(mix_ssm, mla, x, wts, *, bb):
    b, s, d = x.shape
    row = lambda j: (j, 0, 0)
    return pl.pallas_call(
        _out_kernel,
        out_shape=jax.ShapeDtypeStruct((b, s, d), F32),
        grid=(b // bb,),
        in_specs=[pl.BlockSpec((mix_ssm.shape[0], bb, s, LANES), lambda j: (0, j, 0, 0)),
                  pl.BlockSpec((bb, s, mla.shape[2]), row), pl.BlockSpec((bb, s, d), row),
                  _full(wts['w_out'].shape)],
        out_specs=pl.BlockSpec((bb, s, d), row),
        compiler_params=pltpu.CompilerParams(
            dimension_semantics=("arbitrary",), vmem_limit_bytes=VMEM_LIMIT),
        name="outproj",
    )(mix_ssm, mla, x, wts['w_out'])


def _attn_cached_call(q, k_full, ckv_full, k_last, v_last, gm, wts, *, tk):
    b, tq, _ = q.shape
    t_full, last_len = k_full.shape[1], k_last.shape[1]
    ssm_w = gm.shape[2]
    blk = lambda rows, width: pl.BlockSpec((1, rows, width), lambda j: (j, 0, 0))
    scratch = [pltpu.VMEM((MLA_HEADS, t_full // tk, tq, tk), F32), pltpu.VMEM((MLA_HEADS, tq, LANES), F32),
               pltpu.VMEM((MLA_HEADS, tq, last_len), F32), pltpu.VMEM((MLA_HEADS, tq, LANES), F32),
               pltpu.VMEM((MLA_HEADS * tq, ckv_full.shape[2]), F32), pltpu.VMEM((MLA_HEADS * tq, 1), F32)]
    return pl.pallas_call(
        functools.partial(_attn_cached_kernel, tq=tq, tk=tk),
        out_shape=jax.ShapeDtypeStruct((b, tq, ssm_w), BF16),
        grid=(b,),
        in_specs=[blk(tq, HEAD_W), blk(t_full, HEAD_W), blk(t_full, ckv_full.shape[2]),
                  blk(last_len, HEAD_W), blk(last_len, HEAD_W), blk(tq, ssm_w),
                  _full(wts['onorm_mla'].shape), _full(wts['w_uv'].shape)],
        out_specs=blk(tq, ssm_w),
        scratch_shapes=scratch,
        compiler_params=pltpu.CompilerParams(
            dimension_semantics=("arbitrary",), vmem_limit_bytes=VMEM_LIMIT),
        name="attn_cached",
    )(q, k_full, ckv_full, k_last, v_last, gm, wts['onorm_mla'], wts['w_uv'])


def _head_block_cols(w, pieces):
    k = w.shape[0]
    w3 = w.reshape(k, MLA_HEADS, w.shape[1] // MLA_HEADS)
    cols = [w3[:, :, p[0]:p[0] + p[1]] if isinstance(p, tuple) else jnp.zeros((k, MLA_HEADS, p), w.dtype)
            for p in pieces]
    return jnp.concatenate(cols, axis=-1).reshape(k, HEAD_W)


def _zoh_powers(lr, li, dt, n):
    mag = jnp.exp(lr * dt)
    ar, ai = mag * jnp.cos(li * dt), mag * jnp.sin(li * dt)
    pw = [(jnp.ones_like(ar), jnp.zeros_like(ai))]
    for _ in range(n):
        pr, pi = pw[-1]
        pw.append((pr * ar - pi * ai, pr * ai + pi * ar))
    return pw


def _disc_kernel(arow_ref, acol_ref, ldt_ref, bt_ref, cd_ref, ct_ref,
                 apr_ref, api_ref, wb_ref, wc_ref, wf_ref):
    r_blk, groups, cin = wb_ref.shape[1], wb_ref.shape[2], wb_ref.shape[3]
    n_state = acol_ref.shape[2]
    gpb = LANES // cin
    lr, li = arow_ref[0], arow_ref[1]
    dt = jnp.exp(ldt_ref[...])
    pw = _zoh_powers(lr, li, dt, r_blk)
    ar, ai = pw[1]
    den = lr * lr + li * li
    kr = ((ar - 1.0) * lr + ai * li) / den
    ki = (ai * lr - (ar - 1.0) * li) / den
    apr_ref[...], api_ref[...] = pw[r_blk]

    pwc = _zoh_powers(acol_ref[0], acol_ref[1], dt, r_blk)
    for j in range(r_blk):
        pr, pi = pwc[j + 1]
        wc_ref[0, j] = ct_ref[0] * pr - ct_ref[1] * pi
        wc_ref[1, j] = -(ct_ref[0] * pi + ct_ref[1] * pr)

    b_r = kr * bt_ref[0] - ki * bt_ref[1]
    b_i = kr * bt_ref[1] + ki * bt_ref[0]
    for i in range(r_blk):
        pr, pi = pw[r_blk - 1 - i]
        wb_ref[0, i] = pr * b_r - pi * b_i
        wb_ref[1, i] = pr * b_i + pi * b_r

    nt = (((1,), (1,)), ((), ()))
    first_copy = lax.broadcasted_iota(jnp.int32, (LANES, LANES), 1) < n_state
    block = lambda v, n: v[gpb * n:gpb * (n + 1)].reshape(LANES, LANES)
    for i in range(r_blk):
        for j in range(i):
            wf_ref[i, j] = jnp.zeros(wf_ref.shape[2:], F32)
    for m in range(r_blk):
        pr, pi = pw[m]
        ca_r = cd_ref[0] * pr - cd_ref[1] * pi
        ca_i = cd_ref[0] * pi + cd_ref[1] * pr
        for n in range(groups // gpb):
            f_t = (lax.dot_general(jnp.where(first_copy, block(b_r, n), 0.0), block(ca_r, n), nt,
                                   precision=lax.Precision.HIGHEST, preferred_element_type=F32)
                   - lax.dot_general(jnp.where(first_copy, block(b_i, n), 0.0), block(ca_i, n), nt,
                                     precision=lax.Precision.HIGHEST, preferred_element_type=F32))
            for i in range(r_blk - m):
                wf_ref[i, i + m, gpb * n:gpb * (n + 1)] = f_t.reshape(gpb, cin, LANES)


def _prepare_weights(norm_in, w_in, ssm_a_re, ssm_a_im, ssm_log_dt, ssm_b_re, ssm_b_im, ssm_c_re,
                     ssm_c_im, ssm_d, w_glu, b_glu, q_lora_norm, kv_lora_norm, w_uq, w_ukv,
                     q_nope_norm, k_nope_norm, q_rope_norm, k_rope_norm, out_norm_ssm,
                     out_norm_mla, w_out):
    groups, n_state = ssm_a_re.shape
    ssm_w = groups * SSM_GROUP
    row = lambda v: v.reshape(1, -1).astype(F32)
    tail = LANES - ROPE_DIM - NOPE_DIM

    o_kr = 2 * ssm_w + Q_LORA + KV_LORA
    w_in_p = jnp.concatenate(
        [w_in[:, :o_kr + ROPE_DIM], jnp.zeros((w_in.shape[0], LANES - ROPE_DIM), w_in.dtype),
         w_in[:, o_kr + ROPE_DIM:]], axis=1).astype(BF16)

    w_uq_p = _head_block_cols(w_uq, [(NOPE_DIM, ROPE_DIM), (0, NOPE_DIM), tail]).astype(BF16)
    w_uk_p = _head_block_cols(w_ukv, [ROPE_DIM, (0, NOPE_DIM), tail])
    w_v_lo = _head_block_cols(w_ukv, [(NOPE_DIM, V_DIM), LANES - V_DIM])
    w_v_hi = _head_block_cols(w_ukv, [LANES - V_DIM, (NOPE_DIM, V_DIM)])
    odd_head = (np.arange(HEAD_W) // LANES) % 2 == 1
    w_ukv_p = jnp.concatenate([w_uk_p, jnp.where(odd_head[None, :], w_v_hi, w_v_lo)], axis=1).astype(BF16)
    wvt = w_v_lo.reshape(KV_LORA, MLA_HEADS, LANES)[:, :, :VT_ROWS]
    wvt = jnp.transpose(wvt, (1, 2, 0)).reshape(MLA_HEADS * VT_ROWS, KV_LORA).astype(BF16)

    scale = (NOPE_DIM + ROPE_DIM) ** -0.5 * np.log2(np.e)
    vones = np.zeros((MLA_HEADS, LANES), np.float32)
    vones[0::2, V_DIM] = 1.0
    vones[1::2, 0] = 1.0
    vones = jnp.asarray(vones.reshape(1, HEAD_W))
    zeros = lambda n: jnp.zeros((n,), F32)
    gq =jnp.concatenate([q_rope_norm, q_nope_norm, zeros(tail)]) * scale
    gkr = jnp.concatenate([k_rope_norm, zeros(LANES - ROPE_DIM)])
    gkn = jnp.concatenate([zeros(ROPE_DIM), k_nope_norm, zeros(tail)])

    seg = np.zeros((LANES, LANES), np.float32)
    seg[:ROPE_DIM, :ROPE_DIM] = 1.0 / ROPE_DIM
    seg[ROPE_DIM:ROPE_DIM + NOPE_DIM, ROPE_DIM:ROPE_DIM + NOPE_DIM] = 1.0 / NOPE_DIM
    seg[ROPE_DIM + NOPE_DIM:, ROPE_DIM + NOPE_DIM:] = 1.0 / tail
    seg = jnp.asarray(np.kron(np.eye(2, dtype=np.float32), seg), BF16)
    place = jnp.asarray(np.eye(ROPE_DIM, LANES, dtype=np.float32), BF16)

    r_blk = SSM_STEPS_PER_SCAN
    lane_rep = lambda x: jnp.tile(x, (1,) * (x.ndim - 1) + (LANES // x.shape[-1],))
    a2 = jnp.stack([ssm_a_re, ssm_a_im]).astype(F32)
    b2 = jnp.stack([ssm_b_re, ssm_b_im]).astype(F32)
    c2 = jnp.stack([ssm_c_re, ssm_c_im]).astype(F32)
    ldt = ssm_log_dt.astype(F32)
    gl = jax.ShapeDtypeStruct((groups, 1, LANES), F32)
    a_re, a_im, wb5, wc5, wf5 = pl.pallas_call(
        _disc_kernel,
        out_shape=[gl, gl,
                   jax.ShapeDtypeStruct((2, r_blk, groups, SSM_GROUP, LANES), F32),
                   jax.ShapeDtypeStruct((2, r_blk, groups, n_state, LANES), F32),
                   jax.ShapeDtypeStruct((r_blk, r_blk, groups, SSM_GROUP, LANES), F32)],
        name="disc")(
        lane_rep(a2)[:, :, None, :], jnp.broadcast_to(a2[..., None], a2.shape + (LANES,)),
        jnp.broadcast_to(ldt[:, None, None], (groups, 1, LANES)),
        lane_rep(jnp.swapaxes(b2, 2, 3)), lane_rep(c2), lane_rep(jnp.swapaxes(c2, 2, 3)))
    a_re, a_im = a_re[:, 0, :n_state], a_im[:, 0, :n_state]

    return dict(
        ssm_w=ssm_w,
        norm_in=row(norm_in), w_in=w_in_p, qln=row(q_lora_norm), kvln=row(kv_lora_norm),
        w_uq=w_uq_p, w_ukv=w_ukv_p, w_uk=w_ukv_p[:, :HEAD_W], w_uv=w_ukv_p[:, HEAD_W:], wvt=wvt,
        gq=row(gq), gkr=row(gkr), gkn=row(gkn), seg=seg, vones=vones,
        place=place, a_re=row(a_re), a_im=row(a_im), wb5=wb5, wc5=wc5, wf5=wf5,
        ssm_d=row(ssm_d), w_glu=w_glu.astype(BF16), b_glu=row(b_glu), onorm_ssm=row(out_norm_ssm),
        onorm_mla=row(out_norm_mla), w_out=w_out.astype(BF16))


def _rope_tables(start, n):
    half = ROPE_DIM // 2
    inv = ROPE_THETA ** (-jnp.arange(half, dtype=F32) / half)
    ang = (start + jnp.arange(n)).astype(F32)[:, None] * inv[None, :]
    cos, sin = jnp.cos(ang), jnp.sin(ang)
    cos_t = jnp.concatenate([cos, cos, jnp.ones((n, LANES - ROPE_DIM), F32)], axis=1)
    sin_a = jnp.concatenate([-sin, jnp.zeros((n, LANES - half), F32)], axis=1)
    sin_b = jnp.concatenate([jnp.zeros((n, half), F32), sin, jnp.zeros((n, LANES - ROPE_DIM), F32)],
                            axis=1)
    return cos_t, sin_a, sin_b


def _mixer(x, pos0, h0re, h0im, past, wts, *, proj_bb, proj_tq, ssm_steps, attn_tk):
    u, gs, gm, q, k, v, ckv, kr = _proj_call(x, _rope_tables(pos0, x.shape[1]), wts,
                                             bb=proj_bb, tq=proj_tq)
    mix_ssm, hre, him = _ssm_call(u, gs, h0re, h0im, wts, steps=ssm_steps, unroll=True)
    if past is None:
        y = _attn_causal_call(q, k, v, gm, mix_ssm, x, wts, tq=attn_tk // 2)
    else:
        k_past = _expand_call(*past, wts, tq=512)
        mla = _attn_cached_call(q, k_past, past[0], k, v, gm, wts, tk=attn_tk)
        y = _out_call(mix_ssm, mla, x, wts, bb=proj_bb)
    if proj_bb == 1:
        kr = jnp.swapaxes(kr, 1, 2)
    return y, ckv, kr, hre, him


def kernel(x_prompt, x_sample, cache_ckv, cache_krope, state_ssm_re, state_ssm_im, norm_in, w_in, ssm_a_re, ssm_a_im, ssm_log_dt, ssm_b_re, ssm_b_im, ssm_c_re, ssm_c_im, ssm_d, w_glu, b_glu, q_lora_norm, kv_lora_norm, w_uq, w_ukv, q_nope_norm, k_nope_norm, q_rope_norm, k_rope_norm, out_norm_ssm, out_norm_mla, w_out):
    depth = norm_in.shape[0]
    assert depth == 1, "single mixer layer"
    params = (norm_in, w_in, ssm_a_re, ssm_a_im, ssm_log_dt, ssm_b_re, ssm_b_im, ssm_c_re, ssm_c_im,
              ssm_d, w_glu, b_glu, q_lora_norm, kv_lora_norm, w_uq, w_ukv, q_nope_norm, k_nope_norm,
              q_rope_norm, k_rope_norm, out_norm_ssm, out_norm_mla, w_out)
    drop_depth = lambda a: a.reshape(a.shape[1:])
    wts = _prepare_weights(*[drop_depth(p) for p in params])
    groups, n_state = ssm_a_re.shape[1:]
    bp, sp, _ = x_prompt.shape
    bs, ss, _ = x_sample.shape
    past_len = cache_ckv.shape[2]

    zero_state = jnp.zeros((bp, groups * n_state), F32)
    yp, ckv_p, kr_p, re_p, im_p = _mixer(
        x_prompt, 0, zero_state, zero_state, None, wts,
        proj_bb=1, proj_tq=512, ssm_steps=64, attn_tk=512)
    ys, ckv_s, kr_s, re_s, im_s = _mixer(
        x_sample, past_len,
        state_ssm_re.reshape(bs, groups * n_state), state_ssm_im.reshape(bs, groups * n_state),
        (drop_depth(cache_ckv), jnp.swapaxes(drop_depth(cache_krope), 1, 2)), wts,
        proj_bb=bs // 2, proj_tq=ss, ssm_steps=ss, attn_tk=past_len)

    st = lambda h, bb: h.reshape(1, bb, groups, n_state)
    return (yp, ys, ckv_p[None], kr_p[None], st(re_p, bp), st(im_p, bp),
            ckv_s[None], kr_s[None], st(re_s, bs), st(im_s, bs))
```

```python
import functools

import numpy as np
import jax
import jax.numpy as jnp
from jax import lax
from jax.experimental import pallas as pl
from jax.experimental.pallas import tpu as pltpu

F32 = jnp.float32
BF16 = jnp.bfloat16

CHUNK = 64
SSM_GROUP = 16
SSM_STATE = 64
MLA_HEADS = 8
NOPE_DIM = 64
ROPE_DIM = 32
V_DIM = 64
Q_LORA = 256
KV_LORA = 128
ROPE_THETA = 10000.0
EPS = 1e-6

LANES = 128
SUBLANES = 8
VT_ROWS = 80
HEAD_W = MLA_HEADS * LANES
GROUPS_PER_BLOCK = 8
SSM_STEPS_PER_SCAN = 4
VMEM_LIMIT = 56 * 1024 * 1024


def _rms(x, gain):
    return x * lax.rsqrt(jnp.mean(x * x, axis=-1, keepdims=True) + EPS) * gain


def _seg_rms(x, seg):
    ms = jnp.dot((x * x).astype(BF16), seg, preferred_element_type=F32)
    return x * lax.rsqrt(ms + EPS)


def _rope_block(x, cos_t, sin_a, sin_b):
    return (x * cos_t + pltpu.roll(x, LANES - ROPE_DIM // 2, 1) * sin_a
            + pltpu.roll(x, ROPE_DIM // 2, 1) * sin_b)


def _expand_kv(ckv, kr_blk, w_ukv_ref, gkn_ref, seg_ref, vones_ref, k_ref, v_ref):
    kv = jnp.dot(ckv.astype(BF16), w_ukv_ref[...], preferred_element_type=F32)
    gkn = gkn_ref[...]
    blk3 = k_ref.shape[:2] + (LANES,)
    for p in range(MLA_HEADS // 2):
        kn = _seg_rms(kv[:, 2 * LANES * p:2 * LANES * (p + 1)], seg_ref[...])
        for j in range(2):
            h = 2 * p + j
            blk = kn[:, LANES * j:LANES * (j + 1)] * gkn + kr_blk
            k_ref[:, :, LANES * h:LANES * (h + 1)] = blk.astype(BF16).reshape(blk3)
    if v_ref is not None:
        v_ref[...] = (kv[:, HEAD_W:] + vones_ref[...]).astype(BF16).reshape(v_ref.shape)


def _proj_kernel(x_ref, cos_ref, sa_ref, sb_ref, norm_in_ref, w_in_ref, qln_ref, kvln_ref,
                 w_uq_ref, w_ukv_ref, gq_ref, gkr_ref, gkn_ref, seg_ref, vones_ref,
                 u_ref, gs_ref, gm_ref, q_ref, k_ref, v_ref, ckv_ref, kr_ref, *, ssm_w, transposed):
    bb, tq, d = x_ref.shape
    x = x_ref[...].reshape(bb * tq, d)
    h = _rms(x, norm_in_ref[...])
    z = jnp.dot(h.astype(BF16), w_in_ref[...], preferred_element_type=F32)
    o = 0
    for dst in (u_ref, gs_ref):
        for j in range(ssm_w // LANES):
            dst[j] = z[:, o + LANES * j:o + LANES * (j + 1)].reshape(bb, tq, LANES)
        o += ssm_w
    c_q = z[:, o:o + Q_LORA]
    o += Q_LORA
    c_kv = z[:, o:o + KV_LORA]
    o += KV_LORA
    kr_raw = z[:, o:o + LANES]
    o += LANES
    gm_ref[...] = z[:, o:].reshape(gm_ref.shape)

    per_row = lambda t_ref: jnp.concatenate([t_ref[...]] * bb, axis=0)
    cos_t, sin_a, sin_b = per_row(cos_ref), per_row(sa_ref), per_row(sb_ref)
    seg = seg_ref[...]

    kr_ms = jnp.dot((kr_raw * kr_raw).astype(BF16), seg[:LANES, :LANES], preferred_element_type=F32)
    kr_blk = _rope_block(kr_raw * lax.rsqrt(kr_ms + EPS) * gkr_ref[...], cos_t, sin_a, sin_b)
    if transposed:
        kr_ref[0] = kr_blk.T[:ROPE_DIM]
    else:
        kr_ref[...] = kr_blk[:, :ROPE_DIM].reshape(kr_ref.shape)

    q = jnp.dot(_rms(c_q, qln_ref[...]).astype(BF16), w_uq_ref[...], preferred_element_type=F32)
    gq = gq_ref[...]
    for p in range(MLA_HEADS // 2):
        qn = _seg_rms(q[:, 2 * LANES * p:2 * LANES * (p + 1)], seg)
        for j in range(2):
            hd = 2 * p + j
            blk = _rope_block(qn[:, LANES * j:LANES * (j + 1)] * gq, cos_t, sin_a, sin_b)
            q_ref[:, :, LANES * hd:LANES * (hd + 1)] = blk.astype(BF16).reshape(bb, tq, LANES)

    ckv = _rms(c_kv, kvln_ref[...])
    ckv_ref[...] = ckv.reshape(ckv_ref.shape)
    if transposed:
        _expand_kv(ckv, kr_blk, w_ukv_ref, gkn_ref, seg_ref, None, k_ref, None)
        v_t = jnp.dot(vones_ref[...], ckv.T.astype(BF16), preferred_element_type=F32)
        ones_row = lax.broadcasted_iota(jnp.int32, v_t.shape, 0) % VT_ROWS == V_DIM
        v_t = jnp.where(ones_row, 1.0, v_t).astype(BF16)
        width = v_ref.shape[3]
        for t in range(v_ref.shape[1]):
            v_ref[0, t] = v_t[:, width * t:width * (t + 1)]
    else:
        _expand_kv(ckv, kr_blk, w_ukv_ref, gkn_ref, seg_ref, vones_ref, k_ref, v_ref)


def _expand_kernel(ckv_ref, kr_ref, w_uk_ref, gkn_ref, seg_ref, place_ref, k_ref):
    kr_blk = lax.dot_general(kr_ref[0].astype(BF16), place_ref[...], (((0,), (0,)), ((), ())),
                             preferred_element_type=F32)
    _expand_kv(ckv_ref[0], kr_blk, w_uk_ref, gkn_ref, seg_ref, None, k_ref, None)


def _full(shape):
    n = len(shape)
    return pl.BlockSpec(shape, lambda *_: (0,) * n)


def _proj_call(x, tables, wts, *, bb, tq, vt_tile):
    b, s, d = x.shape
    ssm_w = wts['ssm_w']
    cos_t, sin_a, sin_b = tables
    transposed = bb == 1
    w_kv, v_aux = (wts['w_uk'], wts['wvt']) if transposed else (wts['w_ukv'], wts['vones'])
    grid = (s // tq, b // bb)
    row = lambda i, j: (j, i, 0)
    tab = pl.BlockSpec((tq, LANES), lambda i, j: (i, 0))
    in_specs = [pl.BlockSpec((bb, tq, d), row), tab, tab, tab,
                _full(wts['norm_in'].shape), _full(wts['w_in'].shape), _full(wts['qln'].shape),
                _full(wts['kvln'].shape), _full(wts['w_uq'].shape), _full(w_kv.shape),
                _full(wts['gq'].shape), _full(wts['gkr'].shape), _full(wts['gkn'].shape),
                _full(wts['seg'].shape), _full(v_aux.shape)]
    slabs = ssm_w // LANES
    tb_spec = pl.BlockSpec((slabs, bb, tq, LANES), lambda i, j: (0, j, i, 0))
    kr_shape, kr_spec = (((b, ROPE_DIM, s), pl.BlockSpec((1, ROPE_DIM, tq), lambda i, j: (j, 0, i)))
                         if transposed else
                         ((b, s, ROPE_DIM), pl.BlockSpec((bb, tq, ROPE_DIM), row)))
    v_shape, v_spec = (((b, s // vt_tile, MLA_HEADS * VT_ROWS, vt_tile),
                        pl.BlockSpec((1, tq // vt_tile, MLA_HEADS * VT_ROWS, vt_tile),
                                     lambda i, j: (j, i, 0, 0)))
                       if transposed else
                       ((b, s, HEAD_W), pl.BlockSpec((bb, tq, HEAD_W), row)))
    out_shape = [jax.ShapeDtypeStruct((slabs, b, s, LANES), F32),
                 jax.ShapeDtypeStruct((slabs, b, s, LANES), F32),
                 jax.ShapeDtypeStruct((b, s, ssm_w), F32),
                 jax.ShapeDtypeStruct((b, s, HEAD_W), BF16),
                 jax.ShapeDtypeStruct((b, s, HEAD_W), BF16),
                 jax.ShapeDtypeStruct(v_shape, BF16),
                 jax.ShapeDtypeStruct((b, s, KV_LORA), F32),
                 jax.ShapeDtypeStruct(kr_shape, F32)]
    out_specs = [tb_spec, tb_spec,
                 pl.BlockSpec((bb, tq, ssm_w), row),
                 pl.BlockSpec((bb, tq, HEAD_W), row), pl.BlockSpec((bb, tq, HEAD_W), row), v_spec,
                 pl.BlockSpec((bb, tq, KV_LORA), row), kr_spec]
    return pl.pallas_call(
        functools.partial(_proj_kernel, ssm_w=ssm_w, transposed=transposed),
        out_shape=out_shape, grid=grid, in_specs=in_specs, out_specs=out_specs,
        compiler_params=pltpu.CompilerParams(
            dimension_semantics=("arbitrary", "arbitrary"), vmem_limit_bytes=VMEM_LIMIT),
        name="proj",
    )(x, cos_t, sin_a, sin_b, wts['norm_in'], wts['w_in'], wts['qln'], wts['kvln'], wts['w_uq'],
      w_kv, wts['gq'], wts['gkr'], wts['gkn'], wts['seg'], v_aux)


def _expand_call(ckv, kr, wts, *, tq):
    b, t, _ = ckv.shape
    row = lambda j, i: (j, i, 0)
    return pl.pallas_call(
        _expand_kernel,
        out_shape=jax.ShapeDtypeStruct((b, t, HEAD_W), BF16),
        grid=(b, t // tq),
        in_specs=[pl.BlockSpec((1, tq, KV_LORA), row),
                  pl.BlockSpec((1, ROPE_DIM, tq), lambda j, i: (j, 0, i)),
                  _full(wts['w_uk'].shape), _full(wts['gkn'].shape), _full(wts['seg'].shape),
                  _full(wts['place'].shape)],
        out_specs=pl.BlockSpec((1, tq, HEAD_W), row),
        compiler_params=pltpu.CompilerParams(
            dimension_semantics=("arbitrary", "arbitrary"), vmem_limit_bytes=VMEM_LIMIT),
        name="expand",
    )(ckv, kr, wts['w_uk'], wts['gkn'], wts['seg'], wts['place'])


def _ssm_kernel(u_ref, gs_ref, h0re_ref, h0im_ref, are_ref, aim_ref, wb5_ref, wc5_ref, wf5_ref, d_ref,
                w_glu_ref, b_glu_ref, onorm_ref, mix_ref, hre_ref, him_ref, us_ref, xs_ref, y_ref,
                wb_ref, wc_ref, wf_ref, *, batch, steps, unroll):
    n_blocks, r_blk = wb_ref.shape[0], wb_ref.shape[1] // LANES
    half = wb_ref.shape[2] // 2
    gpb = wb5_ref.shape[2] // n_blocks
    n_k = steps // r_blk

    @pl.when(pl.program_id(0) == 0)
    def _():
        hre_ref[...] = h0re_ref[...]
        him_ref[...] = h0im_ref[...]

        def own_group(shape, row_div, lane_div):
            return (lax.broadcasted_iota(jnp.int32, shape, 0) // row_div
                    == lax.broadcasted_iota(jnp.int32, shape, 1) // lane_div)

        cin, n_state = wb5_ref.shape[3], half // gpb
        m_b = own_group((LANES, half), cin, n_state)
        m_c = own_group((half, LANES), n_state, cin)
        m_f = own_group((LANES, LANES), cin, cin)
        for gb in range(n_blocks):
            grp = slice(gpb * gb, gpb * (gb + 1))
            for ri in range(2):
                for i in range(r_blk):
                    piece = wb5_ref[ri, i, grp].reshape(LANES, LANES)
                    wide = jnp.concatenate([piece] * (half // LANES), axis=1)
                    wb_ref[gb, LANES * i:LANES * (i + 1), half * ri:half * (ri + 1)] = (
                        jnp.where(m_b, wide, 0.0).astype(BF16))
                    piece = wc5_ref[ri, i, grp].reshape(half, LANES)
                    wc_ref[gb, half * ri:half * (ri + 1), LANES * i:LANES * (i + 1)] = (
                        jnp.where(m_c, piece, 0.0).astype(BF16))
            for i in range(r_blk):
                for j in range(r_blk):
                    piece = wf5_ref[i, j, grp].reshape(LANES, LANES)
                    wf_ref[gb, LANES * i:LANES * (i + 1), LANES * j:LANES * (j + 1)] = (
                        jnp.where(m_f, piece, 0.0).astype(BF16))

    pitch = us_ref.shape[2] // n_k
    for b in range(batch):
        for j in range(n_blocks):
            for i in range(r_blk):
                us_ref[j, i, pl.ds(b, n_k, stride=pitch), :] = (
                    u_ref[j, b, pl.ds(i, n_k, stride=r_blk), :])

    def slab_rows(j, i):
        return jnp.concatenate([us_ref[j, i, pitch * k:pitch * k + batch, :] for k in range(n_k)], axis=0)

    u_slabs = [[slab_rows(j, i) for i in range(r_blk)] for j in range(n_blocks)]
    for gb in range(n_blocks):
        lhs = jnp.concatenate(u_slabs[gb], axis=-1).astype(BF16)
        xs = xs_ref
        xs[...] = jnp.dot(lhs, wb_ref[gb], preferred_element_type=F32)
        cols = slice(half * gb, half * (gb + 1))
        a_re = jnp.broadcast_to(are_ref[:, cols], (batch, half))
        a_im = jnp.broadcast_to(aim_ref[:, cols], (batch, half))

        def step(k, carry, xs=xs, a_re=a_re, a_im=a_im):
            h_re, h_im = carry
            rows = pl.ds(pl.multiple_of(k * batch, batch), batch)
            n_re = a_re * h_re - a_im * h_im + xs[rows, :half]
            n_im = a_re * h_im + a_im * h_re + xs[rows, half:]
            xs[rows, :half] = h_re
            xs[rows, half:] = h_im
            return n_re, n_im

        h_re, h_im = lax.fori_loop(0, n_k, step, (hre_ref[:, cols], him_ref[:, cols]), unroll=unroll)
        hre_ref[:, cols] = h_re
        him_ref[:, cols] = h_im
        y = (jnp.dot(xs[...].astype(BF16), wc_ref[gb], preferred_element_type=F32)
             + jnp.dot(lhs, wf_ref[gb], preferred_element_type=F32))
        for i in range(r_blk):
            y_ref[i, :, LANES * gb:LANES * (gb + 1)] = y[:, LANES * i:LANES * (i + 1)]

    rows = r_blk * n_k * batch
    u = jnp.concatenate([jnp.concatenate(u_slabs[j], axis=0) for j in range(n_blocks)], axis=-1)
    y = y_ref[...].reshape(rows, n_blocks * LANES) + d_ref[...] * u
    yg = jax.nn.gelu(y)
    glu = jnp.dot(yg.astype(BF16), w_glu_ref[...], preferred_element_type=F32) + b_glu_ref[...]
    out = _rms(yg * jax.nn.sigmoid(glu), onorm_ref[...])
    for j in range(n_blocks):
        for i in range(r_blk):
            for k in range(n_k):
                r0 = (i * n_k + k) * batch
                us_ref[j, i, pitch * k:pitch * k + batch, :] = out[r0:r0 + batch, LANES * j:LANES * (j + 1)]
    for b in range(batch):
        for j in range(n_blocks):
            for i in range(r_blk):
                tok = pl.ds(i, n_k, stride=r_blk)
                gs = gs_ref[j, b, tok, :]
                o = us_ref[j, i, pl.ds(b, n_k, stride=pitch), :]
                mix_ref[j, b, tok, :] = o * (gs * jax.nn.sigmoid(gs))


def _ssm_call(u, gs, h0re, h0im, wts, *, steps, unroll):
    n_blocks, batch, seq, _ = u.shape
    n_state = h0re.shape[1]
    r_blk = wts['wb5'].shape[1]
    width = 2 * n_state // n_blocks
    assert steps % r_blk == 0 and wts['wb5'].shape[3] * GROUPS_PER_BLOCK == LANES
    blk = pl.BlockSpec((n_blocks, batch, steps, LANES), lambda i: (0, 0, i, 0))
    names = ['a_re', 'a_im', 'wb5', 'wc5', 'wf5', 'ssm_d', 'w_glu', 'b_glu', 'onorm_ssm']
    n_rows = steps // r_blk * batch
    return pl.pallas_call(
        functools.partial(_ssm_kernel, batch=batch, steps=steps, unroll=unroll),
        out_shape=[jax.ShapeDtypeStruct(u.shape, F32),
                   jax.ShapeDtypeStruct((batch, n_state), F32),
                   jax.ShapeDtypeStruct((batch, n_state), F32)],
        grid=(seq // steps,),
        in_specs=[blk, blk, _full(h0re.shape), _full(h0im.shape)] + [_full(wts[n].shape) for n in names],
        out_specs=[blk, _full((batch, n_state)), _full((batch, n_state))],
        scratch_shapes=[pltpu.VMEM((n_blocks, r_blk, steps // r_blk * (batch + SUBLANES), LANES), F32),
                        pltpu.VMEM((n_rows, width), F32),
                        pltpu.VMEM((r_blk, n_rows, n_blocks * LANES), F32),
                        pltpu.VMEM((n_blocks, r_blk * LANES, width), BF16),
                        pltpu.VMEM((n_blocks, width, r_blk * LANES), BF16),
                        pltpu.VMEM((n_blocks, r_blk * LANES, r_blk * LANES), BF16)],
        compiler_params=pltpu.CompilerParams(
            dimension_semantics=("arbitrary",), vmem_limit_bytes=VMEM_LIMIT),
        name="ssm",
    )(u, gs, h0re, h0im, *[wts[n] for n in names])


def _attn_cached_kernel(q_ref, kf_ref, cf_ref, kl_ref, vl_ref, gm_ref, onorm_ref, w_uv_ref, y_ref,
                        s_full, ml_scr, s_last, m_scr, ctx_scr, den_scr, *, tq, tk):
    n_full = kf_ref.shape[1] // tk
    dn = (((1,), (1,)), ((), ()))
    lane = lax.broadcasted_iota(jnp.int32, (tq, LANES), 1)
    heads = range(MLA_HEADS)
    cols = [slice(LANES * hd, LANES * (hd + 1)) for hd in heads]

    def lane_tiles(s):
        return [s[:, LANES * c:LANES * (c + 1)] for c in range(s.shape[1] // LANES)]

    def probs(s, m_rep):
        if s.shape[1] % LANES == 0:
            p = jnp.concatenate([jnp.exp2(t - m_rep) for t in lane_tiles(s)], axis=-1)
        else:
            p = jnp.exp2(s - m_rep[:, :1])
        return p.astype(BF16)

    ml_scr[...] = jnp.full(ml_scr.shape, -jnp.inf, F32)

    def a_step(jt, carry):
        rows = pl.ds(pl.multiple_of(jt * tk, tk), tk)
        for hd in heads:
            s = lax.dot_general(q_ref[0, :, cols[hd]], kf_ref[0, rows, cols[hd]], dn,
                                preferred_element_type=F32)
            s_full[hd, jt] = s
            ml_scr[hd] = functools.reduce(jnp.maximum, lane_tiles(s), ml_scr[hd])
        return carry

    lax.fori_loop(0, n_full, a_step, 0)

    for hd in heads:
        s = lax.dot_general(q_ref[0, :, cols[hd]], kl_ref[0, :, cols[hd]], dn,
                            preferred_element_type=F32)
        s_last[hd] = s
        m = jnp.maximum(jnp.max(s, axis=-1, keepdims=True),
                        jnp.max(ml_scr[hd], axis=-1, keepdims=True))
        m_scr[hd] = jnp.broadcast_to(m, (tq, LANES))
    ctx_scr[...] = jnp.zeros(ctx_scr.shape, F32)
    den_scr[...] = jnp.zeros(den_scr.shape, F32)

    def b_step(jt, carry):
        rows = pl.ds(pl.multiple_of(jt * tk, tk), tk)
        p = jnp.concatenate([probs(s_full[hd, jt], m_scr[hd]) for hd in heads], axis=0)
        ctx_scr[...] += jnp.dot(p, cf_ref[0, rows, :].astype(BF16), preferred_element_type=F32)
        den_scr[...] += jnp.sum(p.astype(F32), axis=-1, keepdims=True)
        return carry

    lax.fori_loop(0, n_full, b_step, 0)

    outs = []
    for hd in heads:
        mine = slice(tq * hd, tq * (hd + 1))
        cached = jnp.dot(ctx_scr[mine].astype(BF16), w_uv_ref[:, cols[hd]], preferred_element_type=F32)
        p = probs(s_last[hd], m_scr[hd])
        new = jnp.dot(p, vl_ref[0, :, cols[hd]], preferred_element_type=F32)
        ones_col = V_DIM if hd % 2 == 0 else 0
        outs.append((cached + new) / (den_scr[mine] + new[:, ones_col:ones_col + 1]))
    attn = jnp.concatenate([jnp.where(lane < V_DIM, outs[e], outs[e + 1])
                            for e in range(0, MLA_HEADS, 2)], axis=-1)
    gm = gm_ref[0]
    y_ref[0] = (_rms(attn, onorm_ref[...]) * (gm * jax.nn.sigmoid(gm))).astype(BF16)


def _attn_causal_kernel(q_ref, k_ref, vt_ref, gm_ref, ms_ref, x_ref, onorm_ref, w_out_ref, y_ref,
                        s_full, ml_scr, s_last, m_scr, acc_scr, ot_scr, *, tq, tk):
    i = pl.program_id(1)
    n_full = (i * tq) // tk
    odd = i % 2 == 1
    nt = (((1,), (1,)), ((), ()))
    heads = range(MLA_HEADS)
    cols = [slice(LANES * hd, LANES * (hd + 1)) for hd in heads]
    vrows = [slice(VT_ROWS * hd, VT_ROWS * (hd + 1)) for hd in heads]

    def group_max(s):
        return jnp.max(s.reshape(s.shape[0] // SUBLANES, SUBLANES, s.shape[1]), axis=0)

    def chunk_mask(width):
        kc = lax.broadcasted_iota(jnp.int32, (width, tq), 0) // CHUNK
        qc = lax.broadcasted_iota(jnp.int32, (width, tq), 1) // CHUNK + (width - tq) // CHUNK
        return qc >= kc

    ml_scr[...] = jnp.full(ml_scr.shape, -jnp.inf, F32)

    def a_step(jt, carry):
        rows = pl.ds(pl.multiple_of(jt * tk, tk), tk)
        for hd in heads:
            s = lax.dot_general(k_ref[0, rows, cols[hd]], q_ref[0, :, cols[hd]], nt,
                                preferred_element_type=F32)
            s_full[hd, jt] = s
            ml_scr[hd] = jnp.maximum(ml_scr[hd], group_max(s))
        return carry

    lax.fori_loop(0, n_full, a_step, 0)

    def last_a(width, rows):
        mask = chunk_mask(width)
        for hd in heads:
            s = lax.dot_general(k_ref[0, rows, cols[hd]], q_ref[0, :, cols[hd]], nt,
                                preferred_element_type=F32)
            s = jnp.where(mask, s, -jnp.inf)
            s_last[hd, :width] = s
            m = jnp.max(jnp.maximum(ml_scr[hd], group_max(s)), axis=0, keepdims=True)
            m_scr[hd] = jnp.broadcast_to(m, m_scr.shape[1:])
            acc_scr[hd] = jnp.zeros(acc_scr.shape[1:], F32)

    pair_rows = pl.ds(pl.multiple_of(n_full * tk, tk), tk)
    pl.when(odd)(functools.partial(last_a, tk, pair_rows))
    pl.when(jnp.logical_not(odd))(functools.partial(last_a, tq, pl.ds(pl.multiple_of(i * tq, tq), tq)))

    def b_step(jt, carry):
        for hd in heads:
            p = jnp.exp2(s_full[hd, jt] - m_scr[hd, :1]).astype(BF16)
            acc_scr[hd] += jnp.dot(vt_ref[0, jt, vrows[hd], :], p, preferred_element_type=F32)
        return carry

    lax.fori_loop(0, n_full, b_step, 0)

    def last_b(width):
        for hd in heads:
            p = jnp.exp2(s_last[hd, :width] - m_scr[hd, :1]).astype(BF16)
            acc = acc_scr[hd] + jnp.dot(vt_ref[0, n_full, vrows[hd], :width], p,
                                        preferred_element_type=F32)
            ot_scr[V_DIM * hd:V_DIM * (hd + 1), :] = acc[:V_DIM] / acc[V_DIM:V_DIM + 1]

    pl.when(odd)(functools.partial(last_b, tk))
    pl.when(jnp.logical_not(odd))(functools.partial(last_b, tq))

    gm = gm_ref[0]
    mla = _rms(ot_scr[...].T, onorm_ref[...]) * (gm * jax.nn.sigmoid(gm))
    mix_ssm = [ms_ref[j, 0] for j in range(ms_ref.shape[0])]
    mix = jnp.concatenate(mix_ssm + [mla], axis=-1).astype(BF16)
    y_ref[0] = x_ref[0] + jnp.dot(mix, w_out_ref[...], preferred_element_type=F32)


def _attn_causal_call(q, k, vt, gm, mix_ssm, x, wts, *, tq):
    b, s, d = x.shape
    tk = vt.shape[3]
    ssm_w = gm.shape[2]
    assert tk == 2 * tq and s % tk == 0 and tq % CHUNK == 0
    row = lambda j, i: (j, i, 0)
    res = lambda j, i: (j, 0, 0)
    scratch = [pltpu.VMEM((MLA_HEADS, s // tk - 1, tk, tq), F32),
               pltpu.VMEM((MLA_HEADS, SUBLANES, tq), F32),
               pltpu.VMEM((MLA_HEADS, tk, tq), F32),
               pltpu.VMEM((MLA_HEADS, SUBLANES, tq), F32),
               pltpu.VMEM((MLA_HEADS, VT_ROWS, tq), F32),
               pltpu.VMEM((MLA_HEADS * V_DIM, tq), F32)]
    return pl.pallas_call(
        functools.partial(_attn_causal_kernel, tq=tq, tk=tk),
        out_shape=jax.ShapeDtypeStruct((b, s, d), F32),
        grid=(b, s // tq),
        in_specs=[pl.BlockSpec((1, tq, HEAD_W), row), pl.BlockSpec((1, s, HEAD_W), res),
                  pl.BlockSpec((1,) + vt.shape[1:], lambda j, i: (j, 0, 0, 0)),
                  pl.BlockSpec((1, tq, ssm_w), row),
                  pl.BlockSpec((mix_ssm.shape[0], 1, tq, LANES), lambda j, i: (0, j, i, 0)),
                  pl.BlockSpec((1, tq, d), row),
                  _full(wts['onorm_mla'].shape), _full(wts['w_out'].shape)],
        out_specs=pl.BlockSpec((1, tq, d), row),
        scratch_shapes=scratch,
        compiler_params=pltpu.CompilerParams(
            dimension_semantics=("arbitrary", "arbitrary"), vmem_limit_bytes=VMEM_LIMIT),
        name="attn",
    )(q, k, vt, gm, mix_ssm, x, wts['onorm_mla'], wts['w_out'])


def _out_kernel(ms_ref, mla_ref, x_ref, w_out_ref, y_ref):
    bb, tq, d = x_ref.shape
    rows = bb * tq
    mix = jnp.concatenate([ms_ref[j].reshape(rows, LANES).astype(BF16) for j in range(ms_ref.shape[0])]
                          + [mla_ref[...].reshape(rows, mla_ref.shape[2])], axis=-1)
    y = x_ref[...].reshape(rows, d) + jnp.dot(mix, w_out_ref[...], preferred_element_type=F32)
    y_ref[...] = y.reshape(bb, tq, d)


def _out_call(mix_ssm, mla, x, wts, *, bb):
    b, s, d = x.shape
    row = lambda j: (j, 0, 0)
    return pl.pallas_call(
        _out_kernel,
        out_shape=jax.ShapeDtypeStruct((b, s, d), F32),
        grid=(b // bb,),
        in_specs=[pl.BlockSpec((mix_ssm.shape[0], bb, s, LANES), lambda j: (0, j, 0, 0)),
                  pl.BlockSpec((bb, s, mla.shape[2]), row), pl.BlockSpec((bb, s, d), row),
                  _full(wts['w_out'].shape)],
        out_specs=pl.BlockSpec((bb, s, d), row),
        compiler_params=pltpu.CompilerParams(
            dimension_semantics=("arbitrary",), vmem_limit_bytes=VMEM_LIMIT),
        name="outproj",
    )(mix_ssm, mla, x, wts['w_out'])


def _attn_cached_call(q, k_full, ckv_full, k_last, v_last, gm, wts, *, tk):
    b, tq, _ = q.shape
    t_full, last_len = k_full.shape[1], k_last.shape[1]
    ssm_w = gm.shape[2]
    blk = lambda rows, width: pl.BlockSpec((1, rows, width), lambda j: (j, 0, 0))
    scratch = [pltpu.VMEM((MLA_HEADS, t_full // tk, tq, tk), F32), pltpu.VMEM((MLA_HEADS, tq, LANES), F32),
               pltpu.VMEM((MLA_HEADS, tq, last_len), F32), pltpu.VMEM((MLA_HEADS, tq, LANES), F32),
               pltpu.VMEM((MLA_HEADS * tq, ckv_full.shape[2]), F32), pltpu.VMEM((MLA_HEADS * tq, 1), F32)]
    return pl.pallas_call(
        functools.partial(_attn_cached_kernel, tq=tq, tk=tk),
        out_shape=jax.ShapeDtypeStruct((b, tq, ssm_w), BF16),
        grid=(b,),
        in_specs=[blk(tq, HEAD_W), blk(t_full, HEAD_W), blk(t_full, ckv_full.shape[2]),
                  blk(last_len, HEAD_W), blk(last_len, HEAD_W), blk(tq, ssm_w),
                  _full(wts['onorm_mla'].shape), _full(wts['w_uv'].shape)],
        out_specs=blk(tq, ssm_w),
        scratch_shapes=scratch,
        compiler_params=pltpu.CompilerParams(
            dimension_semantics=("arbitrary",), vmem_limit_bytes=VMEM_LIMIT),
        name="attn_cached",
    )(q, k_full, ckv_full, k_last, v_last, gm, wts['onorm_mla'], wts['w_uv'])


def _head_block_cols(w, pieces):
    k = w.shape[0]
    w3 = w.reshape(k, MLA_HEADS, w.shape[1] // MLA_HEADS)
    cols = [w3[:, :, p[0]:p[0] + p[1]] if isinstance(p, tuple) else jnp.zeros((k, MLA_HEADS, p), w.dtype)
            for p in pieces]
    return jnp.concatenate(cols, axis=-1).reshape(k, HEAD_W)


def _zoh_powers(lr, li, dt, n):
    mag = jnp.exp(lr * dt)
    ar, ai = mag * jnp.cos(li * dt), mag * jnp.sin(li * dt)
    pw = [(jnp.ones_like(ar), jnp.zeros_like(ai))]
    for _ in range(n):
        pr, pi = pw[-1]
        pw.append((pr * ar - pi * ai, pr * ai + pi * ar))
    return pw


def _disc_kernel(arow_ref, acol_ref, ldt_ref, bt_ref, cd_ref, ct_ref,
                 apr_ref, api_ref, wb_ref, wc_ref, wf_ref):
    r_blk, groups, cin = wb_ref.shape[1], wb_ref.shape[2], wb_ref.shape[3]
    n_state = acol_ref.shape[2]
    gpb = LANES // cin
    lr, li = arow_ref[0], arow_ref[1]
    dt = jnp.exp(ldt_ref[...])
    pw = _zoh_powers(lr, li, dt, r_blk)
    ar, ai = pw[1]
    den = lr * lr + li * li
    kr = ((ar - 1.0) * lr + ai * li) / den
    ki = (ai * lr - (ar - 1.0) * li) / den
    apr_ref[...], api_ref[...] = pw[r_blk]

    pwc = _zoh_powers(acol_ref[0], acol_ref[1], dt, r_blk)
    for j in range(r_blk):
        pr, pi = pwc[j + 1]
        wc_ref[0, j] = ct_ref[0] * pr - ct_ref[1] * pi
        wc_ref[1, j] = -(ct_ref[0] * pi + ct_ref[1] * pr)

    b_r = kr * bt_ref[0] - ki * bt_ref[1]
    b_i = kr * bt_ref[1] + ki * bt_ref[0]
    for i in range(r_blk):
        pr, pi = pw[r_blk - 1 - i]
        wb_ref[0, i] = pr * b_r - pi * b_i
        wb_ref[1, i] = pr * b_i + pi * b_r

    nt = (((1,), (1,)), ((), ()))
    first_copy = lax.broadcasted_iota(jnp.int32, (LANES, LANES), 1) < n_state
    block = lambda v, n: v[gpb * n:gpb * (n + 1)].reshape(LANES, LANES)
    for i in range(r_blk):
        for j in range(i):
            wf_ref[i, j] = jnp.zeros(wf_ref.shape[2:], F32)
    for m in range(r_blk):
        pr, pi = pw[m]
        ca_r = cd_ref[0] * pr - cd_ref[1] * pi
        ca_i = cd_ref[0] * pi + cd_ref[1] * pr
        for n in range(groups // gpb):
            f_t = (lax.dot_general(jnp.where(first_copy, block(b_r, n), 0.0), block(ca_r, n), nt,
                                   precision=lax.Precision.HIGHEST, preferred_element_type=F32)
                   - lax.dot_general(jnp.where(first_copy, block(b_i, n), 0.0), block(ca_i, n), nt,
                                     precision=lax.Precision.HIGHEST, preferred_element_type=F32))
            for i in range(r_blk - m):
                wf_ref[i, i + m, gpb * n:gpb * (n + 1)] = f_t.reshape(gpb, cin, LANES)


def _prepare_weights(norm_in, w_in, ssm_a_re, ssm_a_im, ssm_log_dt, ssm_b_re, ssm_b_im, ssm_c_re,
                     ssm_c_im, ssm_d, w_glu, b_glu, q_lora_norm, kv_lora_norm, w_uq, w_ukv,
                     q_nope_norm, k_nope_norm, q_rope_norm, k_rope_norm, out_norm_ssm,
                     out_norm_mla, w_out):
    groups, n_state = ssm_a_re.shape
    ssm_w = groups * SSM_GROUP
    row = lambda v: v.reshape(1, -1).astype(F32)
    tail = LANES - ROPE_DIM - NOPE_DIM

    o_kr = 2 * ssm_w + Q_LORA + KV_LORA
    w_in_p = jnp.concatenate(
        [w_in[:, :o_kr + ROPE_DIM], jnp.zeros((w_in.shape[0], LANES - ROPE_DIM), w_in.dtype),
         w_in[:, o_kr + ROPE_DIM:]], axis=1).astype(BF16)

    w_uq_p = _head_block_cols(w_uq, [(NOPE_DIM, ROPE_DIM), (0, NOPE_DIM), tail]).astype(BF16)
    w_uk_p = _head_block_cols(w_ukv, [ROPE_DIM, (0, NOPE_DIM), tail])
    w_v_lo = _head_block_cols(w_ukv, [(NOPE_DIM, V_DIM), LANES - V_DIM])
    w_v_hi = _head_block_cols(w_ukv, [LANES - V_DIM, (NOPE_DIM, V_DIM)])
    odd_head = (np.arange(HEAD_W) // LANES) % 2 == 1
    w_ukv_p = jnp.concatenate([w_uk_p, jnp.where(odd_head[None, :], w_v_hi, w_v_lo)], axis=1).astype(BF16)
    wvt = w_v_lo.reshape(KV_LORA, MLA_HEADS, LANES)[:, :, :VT_ROWS]
    wvt = jnp.transpose(wvt, (1, 2, 0)).reshape(MLA_HEADS * VT_ROWS, KV_LORA).astype(BF16)

    scale = (NOPE_DIM + ROPE_DIM) ** -0.5 * np.log2(np.e)
    vones = np.zeros((MLA_HEADS, LANES), np.float32)
    vones[0::2, V_DIM] = 1.0
    vones[1::2, 0] = 1.0
    vones = jnp.asarray(vones.reshape(1, HEAD_W))
    zeros = lambda n: jnp.zeros((n,), F32)
    gq =jnp.concatenate([q_rope_norm, q_nope_norm, zeros(tail)]) * scale
    gkr = jnp.concatenate([k_rope_norm, zeros(LANES - ROPE_DIM)])
    gkn = jnp.concatenate([zeros(ROPE_DIM), k_nope_norm, zeros(tail)])

    seg = np.zeros((LANES, LANES), np.float32)
    seg[:ROPE_DIM, :ROPE_DIM] = 1.0 / ROPE_DIM
    seg[ROPE_DIM:ROPE_DIM + NOPE_DIM, ROPE_DIM:ROPE_DIM + NOPE_DIM] = 1.0 / NOPE_DIM
    seg[ROPE_DIM + NOPE_DIM:, ROPE_DIM + NOPE_DIM:] = 1.0 / tail
    seg = jnp.asarray(np.kron(np.eye(2, dtype=np.float32), seg), BF16)
    place = jnp.asarray(np.eye(ROPE_DIM, LANES, dtype=np.float32), BF16)

    r_blk = SSM_STEPS_PER_SCAN
    lane_rep = lambda x: jnp.tile(x, (1,) * (x.ndim - 1) + (LANES // x.shape[-1],))
    a2 = jnp.stack([ssm_a_re, ssm_a_im]).astype(F32)
    b2 = jnp.stack([ssm_b_re, ssm_b_im]).astype(F32)
    c2 = jnp.stack([ssm_c_re, ssm_c_im]).astype(F32)
    ldt = ssm_log_dt.astype(F32)
    gl = jax.ShapeDtypeStruct((groups, 1, LANES), F32)
    a_re, a_im, wb5, wc5, wf5 = pl.pallas_call(
        _disc_kernel,
        out_shape=[gl, gl,
                   jax.ShapeDtypeStruct((2, r_blk, groups, SSM_GROUP, LANES), F32),
                   jax.ShapeDtypeStruct((2, r_blk, groups, n_state, LANES), F32),
                   jax.ShapeDtypeStruct((r_blk, r_blk, groups, SSM_GROUP, LANES), F32)],
        name="disc")(
        lane_rep(a2)[:, :, None, :], jnp.broadcast_to(a2[..., None], a2.shape + (LANES,)),
        jnp.broadcast_to(ldt[:, None, None], (groups, 1, LANES)),
        lane_rep(jnp.swapaxes(b2, 2, 3)), lane_rep(c2), lane_rep(jnp.swapaxes(c2, 2, 3)))
    a_re, a_im = a_re[:, 0, :n_state], a_im[:, 0, :n_state]

    return dict(
        ssm_w=ssm_w,
        norm_in=row(norm_in), w_in=w_in_p, qln=row(q_lora_norm), kvln=row(kv_lora_norm),
        w_uq=w_uq_p, w_ukv=w_ukv_p, w_uk=w_ukv_p[:, :HEAD_W], w_uv=w_ukv_p[:, HEAD_W:], wvt=wvt,
        gq=row(gq), gkr=row(gkr), gkn=row(gkn), seg=seg, vones=vones,
        place=place, a_re=row(a_re), a_im=row(a_im), wb5=wb5, wc5=wc5, wf5=wf5,
        ssm_d=row(ssm_d), w_glu=w_glu.astype(BF16), b_glu=row(b_glu), onorm_ssm=row(out_norm_ssm),
        onorm_mla=row(out_norm_mla), w_out=w_out.astype(BF16))


def _rope_tables(start, n):
    half = ROPE_DIM // 2
    inv = ROPE_THETA ** (-jnp.arange(half, dtype=F32) / half)
    ang = (start + jnp.arange(n)).astype(F32)[:, None] * inv[None, :]
    cos, sin = jnp.cos(ang), jnp.sin(ang)
    cos_t = jnp.concatenate([cos, cos, jnp.ones((n, LANES - ROPE_DIM), F32)], axis=1)
    sin_a = jnp.concatenate([-sin, jnp.zeros((n, LANES - half), F32)], axis=1)
    sin_b = jnp.concatenate([jnp.zeros((n, half), F32), sin, jnp.zeros((n, LANES - ROPE_DIM), F32)],
                            axis=1)
    return cos_t, sin_a, sin_b


def _mixer(x, pos0, h0re, h0im, past, wts, *, proj_bb, proj_tq, ssm_steps, attn_tk):
    u, gs, gm, q, k, v, ckv, kr = _proj_call(x, _rope_tables(pos0, x.shape[1]), wts,
                                             bb=proj_bb, tq=proj_tq, vt_tile=min(attn_tk, proj_tq))
    mix_ssm, hre, him = _ssm_call(u, gs, h0re, h0im, wts, steps=ssm_steps, unroll=True)
    if past is None:
        y = _attn_causal_call(q, k, v, gm, mix_ssm, x, wts, tq=attn_tk // 2)
    else:
        k_past = _expand_call(*past, wts, tq=512)
        mla = _attn_cached_call(q, k_past, past[0], k, v, gm, wts, tk=attn_tk)
        y = _out_call(mix_ssm, mla, x, wts, bb=proj_bb)
    if proj_bb == 1:
        kr = jnp.swapaxes(kr, 1, 2)
    return y, ckv, kr, hre, him


def kernel(x_prompt, x_sample, cache_ckv, cache_krope, state_ssm_re, state_ssm_im, norm_in, w_in, ssm_a_re, ssm_a_im, ssm_log_dt, ssm_b_re, ssm_b_im, ssm_c_re, ssm_c_im, ssm_d, w_glu, b_glu, q_lora_norm, kv_lora_norm, w_uq, w_ukv, q_nope_norm, k_nope_norm, q_rope_norm, k_rope_norm, out_norm_ssm, out_norm_mla, w_out):
    depth = norm_in.shape[0]
    assert depth == 1, "single mixer layer"
    params = (norm_in, w_in, ssm_a_re, ssm_a_im, ssm_log_dt, ssm_b_re, ssm_b_im, ssm_c_re, ssm_c_im,
              ssm_d, w_glu, b_glu, q_lora_norm, kv_lora_norm, w_uq, w_ukv, q_nope_norm, k_nope_norm,
              q_rope_norm, k_rope_norm, out_norm_ssm, out_norm_mla, w_out)
    drop_depth = lambda a: a.reshape(a.shape[1:])
    wts = _prepare_weights(*[drop_depth(p) for p in params])
    groups, n_state = ssm_a_re.shape[1:]
    bp, sp, _ = x_prompt.shape
    bs, ss, _ = x_sample.shape
    past_len = cache_ckv.shape[2]

    zero_state = jnp.zeros((bp, groups * n_state), F32)
    yp, ckv_p, kr_p, re_p, im_p = _mixer(
        x_prompt, 0, zero_state, zero_state, None, wts,
        proj_bb=1, proj_tq=1024, ssm_steps=64, attn_tk=512)
    ys, ckv_s, kr_s, re_s, im_s = _mixer(
        x_sample, past_len,
        state_ssm_re.reshape(bs, groups * n_state), state_ssm_im.reshape(bs, groups * n_state),
        (drop_depth(cache_ckv), jnp.swapaxes(drop_depth(cache_krope), 1, 2)), wts,
        proj_bb=bs // 2, proj_tq=ss, ssm_steps=ss, attn_tk=past_len)

    st = lambda h, bb: h.reshape(1, bb, groups, n_state)
    return (yp, ys, ckv_p[None], kr_p[None], st(re_p, bp), st(im_p, bp),
            ckv_s[None], kr_s[None], st(re_s, bs), st(im_s, bs))
```

```python
import functools

import numpy as np
import jax
import jax.numpy as jnp
from jax import lax
from jax.experimental import pallas as pl
from jax.experimental.pallas import tpu as pltpu

F32 = jnp.float32
BF16 = jnp.bfloat16

CHUNK = 64
SSM_GROUP = 16
SSM_STATE = 64
MLA_HEADS = 8
NOPE_DIM = 64
ROPE_DIM = 32
V_DIM = 64
Q_LORA = 256
KV_LORA = 128
ROPE_THETA = 10000.0
EPS = 1e-6

LANES = 128
SUBLANES = 8
VT_ROWS = 80
HEAD_W = MLA_HEADS * LANES
GROUPS_PER_BLOCK = 8
SSM_STEPS_PER_SCAN = 4
VMEM_LIMIT = 56 * 1024 * 1024


def _rms(x, gain):
    return x * lax.rsqrt(jnp.mean(x * x, axis=-1, keepdims=True) + EPS) * gain


def _seg_rms(x, seg):
    ms = jnp.dot((x * x).astype(BF16), seg, preferred_element_type=F32)
    return x * lax.rsqrt(ms + EPS)


def _rope_block(x, cos_t, sin_a, sin_b):
    return (x * cos_t + pltpu.roll(x, LANES - ROPE_DIM // 2, 1) * sin_a
            + pltpu.roll(x, ROPE_DIM // 2, 1) * sin_b)


def _key_blocks(ckv_b, kr_blk, w_ukc_ref, gkn_ref, segk_ref, k_ref):
    kc = jnp.dot(ckv_b, w_ukc_ref[...], preferred_element_type=F32)
    kr_even = pltpu.roll(kr_blk, NOPE_DIM, 1)
    low = lax.broadcasted_iota(jnp.int32, (1, LANES), 1) < NOPE_DIM
    blk3 = k_ref.shape[:2] + (LANES,)
    for pp in range(kc.shape[1] // (2 * LANES)):
        kn = _seg_rms(kc[:, 2 * LANES * pp:2 * LANES * (pp + 1)], segk_ref[...]) * gkn_ref[...]
        for t in range(2):
            pair = kn[:, LANES * t:LANES * (t + 1)]
            he = 2 * (2 * pp + t)
            k_ref[:, :, LANES * he:LANES * (he + 1)] = (
                (jnp.where(low, pair, 0.0) + kr_even).astype(BF16).reshape(blk3))
            k_ref[:, :, LANES * (he + 1):LANES * (he + 2)] = (
                (jnp.where(low, 0.0, pair) + kr_blk).astype(BF16).reshape(blk3))


def _proj_kernel(x_ref, cos_ref, sa_ref, sb_ref, norm_in_ref, w_in_ref, qln_ref, kvln_ref,
                 w_uq_ref, w_ukc_ref, gq_ref, gkr_ref, gkn_ref, segq_ref, segk_ref, *rest,
                 ssm_w, transposed):
    v_weights, outs = (rest[:1], rest[1:]) if transposed else (rest[:2], rest[2:])
    u_ref, gs_ref, gm_ref, q_ref, k_ref, v_ref, ckv_ref, kr_ref = outs
    bb, tq, d = x_ref.shape
    x = x_ref[...].reshape(bb * tq, d)
    h = _rms(x, norm_in_ref[...])
    z = jnp.dot(h.astype(BF16), w_in_ref[...], preferred_element_type=F32)
    o = 0
    for dst in (u_ref, gs_ref):
        for j in range(ssm_w // LANES):
            dst[j] = z[:, o + LANES * j:o + LANES * (j + 1)].reshape(bb, tq, LANES)
        o += ssm_w
    c_q = z[:, o:o + Q_LORA]
    o += Q_LORA
    c_kv = z[:, o:o + KV_LORA]
    o += KV_LORA
    kr_raw = z[:, o:o + LANES]
    o += LANES
    gm_ref[...] = z[:, o:].reshape(gm_ref.shape)

    per_row = lambda t_ref: jnp.concatenate([t_ref[...]] * bb, axis=0)
    cos_t, sin_a, sin_b = per_row(cos_ref), per_row(sa_ref), per_row(sb_ref)
    even = lambda t: pltpu.roll(t, NOPE_DIM, 1)
    cos_e, sin_ae, sin_be = even(cos_t), even(sin_a), even(sin_b)
    seg = segq_ref[...]

    kr_ms = jnp.dot((kr_raw * kr_raw).astype(BF16), seg[LANES:, LANES:], preferred_element_type=F32)
    kr_blk = _rope_block(kr_raw * lax.rsqrt(kr_ms + EPS) * gkr_ref[...], cos_t, sin_a, sin_b)
    if transposed:
        kr_ref[0] = kr_blk.T[:ROPE_DIM]
    else:
        kr_ref[...] = kr_blk[:, :ROPE_DIM].reshape(kr_ref.shape)

    q = jnp.dot(_rms(c_q, qln_ref[...]).astype(BF16), w_uq_ref[...], preferred_element_type=F32)
    gq = gq_ref[...]
    for p in range(MLA_HEADS // 2):
        qn = _seg_rms(q[:, 2 * LANES * p:2 * LANES * (p + 1)], seg)
        blocks = (_rope_block(qn[:, :LANES] * gq[0:1], cos_e, sin_ae, sin_be),
                  _rope_block(qn[:, LANES:] * gq[1:2], cos_t, sin_a, sin_b))
        for j in range(2):
            hd = 2 * p + j
            q_ref[:, :, LANES * hd:LANES * (hd + 1)] = blocks[j].astype(BF16).reshape(bb, tq, LANES)

    ckv = _rms(c_kv, kvln_ref[...])
    ckv_ref[...] = ckv.reshape(ckv_ref.shape)
    _key_blocks(ckv.astype(BF16), kr_blk, w_ukc_ref, gkn_ref, segk_ref, k_ref)
    if transposed:
        v_t = jnp.dot(v_weights[0][...], ckv.T.astype(BF16), preferred_element_type=F32)
        ones_row = lax.broadcasted_iota(jnp.int32, v_t.shape, 0) % VT_ROWS == V_DIM
        v_t = jnp.where(ones_row, 1.0, v_t).astype(BF16)
        width = v_ref.shape[3]
        for t in range(v_ref.shape[1]):
            v_ref[0, t] = v_t[:, width * t:width * (t + 1)]
    else:
        v = jnp.dot(ckv.astype(BF16), v_weights[0][...], preferred_element_type=F32) + v_weights[1][...]
        v_ref[...] = v.astype(BF16).reshape(v_ref.shape)


def _expand_kernel(ckv_ref, kr_ref, w_ukc_ref, gkn_ref, segk_ref, place_ref, k_ref):
    kr_blk = lax.dot_general(kr_ref[0].astype(BF16), place_ref[...], (((0,), (0,)), ((), ())),
                             preferred_element_type=F32)
    _key_blocks(ckv_ref[0].astype(BF16), kr_blk, w_ukc_ref, gkn_ref, segk_ref, k_ref)


def _full(shape):
    n = len(shape)
    return pl.BlockSpec(shape, lambda *_: (0,) * n)


def _proj_call(x, tables, wts, *, bb, tq, vt_tile):
    b, s, d = x.shape
    ssm_w = wts['ssm_w']
    cos_t, sin_a, sin_b = tables
    transposed = bb == 1
    v_weights = (wts['wvt'],) if transposed else (wts['w_uv'], wts['vones'])
    grid = (s // tq, b // bb)
    row = lambda i, j: (j, i, 0)
    tab = pl.BlockSpec((tq, LANES), lambda i, j: (i, 0))
    in_specs = [pl.BlockSpec((bb, tq, d), row), tab, tab, tab,
                _full(wts['norm_in'].shape), _full(wts['w_in'].shape), _full(wts['qln'].shape),
                _full(wts['kvln'].shape), _full(wts['w_uq'].shape), _full(wts['w_ukc'].shape),
                _full(wts['gq'].shape), _full(wts['gkr'].shape), _full(wts['gkn'].shape),
                _full(wts['segq'].shape), _full(wts['segk'].shape)] + [_full(w.shape) for w in v_weights]
    slabs = ssm_w // LANES
    tb_spec = pl.BlockSpec((slabs, bb, tq, LANES), lambda i, j: (0, j, i, 0))
    kr_shape, kr_spec = (((b, ROPE_DIM, s), pl.BlockSpec((1, ROPE_DIM, tq), lambda i, j: (j, 0, i)))
                         if transposed else
                         ((b, s, ROPE_DIM), pl.BlockSpec((bb, tq, ROPE_DIM), row)))
    v_shape, v_spec = (((b, s // vt_tile, MLA_HEADS * VT_ROWS, vt_tile),
                        pl.BlockSpec((1, tq // vt_tile, MLA_HEADS * VT_ROWS, vt_tile),
                                     lambda i, j: (j, i, 0, 0)))
                       if transposed else
                       ((b, s, HEAD_W), pl.BlockSpec((bb, tq, HEAD_W), row)))
    out_shape = [jax.ShapeDtypeStruct((slabs, b, s, LANES), F32),
                 jax.ShapeDtypeStruct((slabs, b, s, LANES), F32),
                 jax.ShapeDtypeStruct((b, s, ssm_w), F32),
                 jax.ShapeDtypeStruct((b, s, HEAD_W), BF16),
                 jax.ShapeDtypeStruct((b, s, HEAD_W), BF16),
                 jax.ShapeDtypeStruct(v_shape, BF16),
                 jax.ShapeDtypeStruct((b, s, KV_LORA), F32),
                 jax.ShapeDtypeStruct(kr_shape, F32)]
    out_specs = [tb_spec, tb_spec,
                 pl.BlockSpec((bb, tq, ssm_w), row),
                 pl.BlockSpec((bb, tq, HEAD_W), row), pl.BlockSpec((bb, tq, HEAD_W), row), v_spec,
                 pl.BlockSpec((bb, tq, KV_LORA), row), kr_spec]
    return pl.pallas_call(
        functools.partial(_proj_kernel, ssm_w=ssm_w, transposed=transposed),
        out_shape=out_shape, grid=grid, in_specs=in_specs, out_specs=out_specs,
        compiler_params=pltpu.CompilerParams(
            dimension_semantics=("arbitrary", "arbitrary"), vmem_limit_bytes=VMEM_LIMIT),
        name="proj",
    )(x, cos_t, sin_a, sin_b, wts['norm_in'], wts['w_in'], wts['qln'], wts['kvln'], wts['w_uq'],
      wts['w_ukc'], wts['gq'], wts['gkr'], wts['gkn'], wts['segq'], wts['segk'], *v_weights)


def _expand_call(ckv, kr, wts, *, tq):
    b, t, _ = ckv.shape
    row = lambda j, i: (j, i, 0)
    return pl.pallas_call(
        _expand_kernel,
        out_shape=jax.ShapeDtypeStruct((b, t, HEAD_W), BF16),
        grid=(b, t // tq),
        in_specs=[pl.BlockSpec((1, tq, KV_LORA), row),
                  pl.BlockSpec((1, ROPE_DIM, tq), lambda j, i: (j, 0, i)),
                  _full(wts['w_ukc'].shape), _full(wts['gkn'].shape), _full(wts['segk'].shape),
                  _full(wts['place'].shape)],
        out_specs=pl.BlockSpec((1, tq, HEAD_W), row),
        compiler_params=pltpu.CompilerParams(
            dimension_semantics=("arbitrary", "arbitrary"), vmem_limit_bytes=VMEM_LIMIT),
        name="expand",
    )(ckv, kr, wts['w_ukc'], wts['gkn'], wts['segk'], wts['place'])


def _ssm_kernel(u_ref, gs_ref, h0re_ref, h0im_ref, are_ref, aim_ref, wb5_ref, wc5_ref, wf5_ref, d_ref,
                w_glu_ref, b_glu_ref, onorm_ref, mix_ref, hre_ref, him_ref, us_ref, xs_ref, y_ref,
                wb_ref, wc_ref, wf_ref, *, batch, steps, unroll):
    n_blocks, r_blk = wb_ref.shape[0], wb_ref.shape[1] // LANES
    half = wb_ref.shape[2] // 2
    gpb = wb5_ref.shape[2] // n_blocks
    n_k = steps // r_blk

    @pl.when(pl.program_id(0) == 0)
    def _():
        hre_ref[...] = h0re_ref[...]
        him_ref[...] = h0im_ref[...]

        def own_group(shape, row_div, lane_div):
            return (lax.broadcasted_iota(jnp.int32, shape, 0) // row_div
                    == lax.broadcasted_iota(jnp.int32, shape, 1) // lane_div)

        cin, n_state = wb5_ref.shape[3], half // gpb
        m_b = own_group((LANES, half), cin, n_state)
        m_c = own_group((half, LANES), n_state, cin)
        m_f = own_group((LANES, LANES), cin, cin)
        for gb in range(n_blocks):
            grp = slice(gpb * gb, gpb * (gb + 1))
            for ri in range(2):
                for i in range(r_blk):
                    piece = wb5_ref[ri, i, grp].reshape(LANES, LANES)
                    wide = jnp.concatenate([piece] * (half // LANES), axis=1)
                    wb_ref[gb, LANES * i:LANES * (i + 1), half * ri:half * (ri + 1)] = (
                        jnp.where(m_b, wide, 0.0).astype(BF16))
                    piece = wc5_ref[ri, i, grp].reshape(half, LANES)
                    wc_ref[gb, half * ri:half * (ri + 1), LANES * i:LANES * (i + 1)] = (
                        jnp.where(m_c, piece, 0.0).astype(BF16))
            for i in range(r_blk):
                for j in range(r_blk):
                    piece = wf5_ref[i, j, grp].reshape(LANES, LANES)
                    wf_ref[gb, LANES * i:LANES * (i + 1), LANES * j:LANES * (j + 1)] = (
                        jnp.where(m_f, piece, 0.0).astype(BF16))

    pitch = us_ref.shape[2] // n_k
    for b in range(batch):
        for j in range(n_blocks):
            for i in range(r_blk):
                us_ref[j, i, pl.ds(b, n_k, stride=pitch), :] = (
                    u_ref[j, b, pl.ds(i, n_k, stride=r_blk), :])

    def slab_rows(j, i):
        return jnp.concatenate([us_ref[j, i, pitch * k:pitch * k + batch, :] for k in range(n_k)], axis=0)

    u_slabs = [[slab_rows(j, i) for i in range(r_blk)] for j in range(n_blocks)]
    for gb in range(n_blocks):
        lhs = jnp.concatenate(u_slabs[gb], axis=-1).astype(BF16)
        xs = xs_ref
        xs[...] = jnp.dot(lhs, wb_ref[gb], preferred_element_type=F32)
        cols = slice(half * gb, half * (gb + 1))
        a_re = jnp.broadcast_to(are_ref[:, cols], (batch, half))
        a_im = jnp.broadcast_to(aim_ref[:, cols], (batch, half))

        def step(k, carry, xs=xs, a_re=a_re, a_im=a_im):
            h_re, h_im = carry
            rows = pl.ds(pl.multiple_of(k * batch, batch), batch)
            n_re = a_re * h_re - a_im * h_im + xs[rows, :half]
            n_im = a_re * h_im + a_im * h_re + xs[rows, half:]
            xs[rows, :half] = h_re
            xs[rows, half:] = h_im
            return n_re, n_im

        h_re, h_im = lax.fori_loop(0, n_k, step, (hre_ref[:, cols], him_ref[:, cols]), unroll=unroll)
        hre_ref[:, cols] = h_re
        him_ref[:, cols] = h_im
        y = (jnp.dot(xs[...].astype(BF16), wc_ref[gb], preferred_element_type=F32)
             + jnp.dot(lhs, wf_ref[gb], preferred_element_type=F32))
        for i in range(r_blk):
            y_ref[i, :, LANES * gb:LANES * (gb + 1)] = y[:, LANES * i:LANES * (i + 1)]

    rows = r_blk * n_k * batch
    u = jnp.concatenate([jnp.concatenate(u_slabs[j], axis=0) for j in range(n_blocks)], axis=-1)
    y = y_ref[...].reshape(rows, n_blocks * LANES) + d_ref[...] * u
    yg = jax.nn.gelu(y)
    glu = jnp.dot(yg.astype(BF16), w_glu_ref[...], preferred_element_type=F32) + b_glu_ref[...]
    out = _rms(yg * jax.nn.sigmoid(glu), onorm_ref[...])
    for j in range(n_blocks):
        for i in range(r_blk):
            for k in range(n_k):
                r0 = (i * n_k + k) * batch
                us_ref[j, i, pitch * k:pitch * k + batch, :] = out[r0:r0 + batch, LANES * j:LANES * (j + 1)]
    for b in range(batch):
        for j in range(n_blocks):
            for i in range(r_blk):
                tok = pl.ds(i, n_k, stride=r_blk)
                gs = gs_ref[j, b, tok, :]
                o = us_ref[j, i, pl.ds(b, n_k, stride=pitch), :]
                mix_ref[j, b, tok, :] = o * (gs * jax.nn.sigmoid(gs))


def _ssm_call(u, gs, h0re, h0im, wts, *, steps, unroll):
    n_blocks, batch, seq, _ = u.shape
    n_state = h0re.shape[1]
    r_blk = wts['wb5'].shape[1]
    width = 2 * n_state // n_blocks
    assert steps % r_blk == 0 and wts['wb5'].shape[3] * GROUPS_PER_BLOCK == LANES
    blk = pl.BlockSpec((n_blocks, batch, steps, LANES), lambda i: (0, 0, i, 0))
    names = ['a_re', 'a_im', 'wb5', 'wc5', 'wf5', 'ssm_d', 'w_glu', 'b_glu', 'onorm_ssm']
    n_rows = steps // r_blk * batch
    return pl.pallas_call(
        functools.partial(_ssm_kernel, batch=batch, steps=steps, unroll=unroll),
        out_shape=[jax.ShapeDtypeStruct(u.shape, F32),
                   jax.ShapeDtypeStruct((batch, n_state), F32),
                   jax.ShapeDtypeStruct((batch, n_state), F32)],
        grid=(seq // steps,),
        in_specs=[blk, blk, _full(h0re.shape), _full(h0im.shape)] + [_full(wts[n].shape) for n in names],
        out_specs=[blk, _full((batch, n_state)), _full((batch, n_state))],
        scratch_shapes=[pltpu.VMEM((n_blocks, r_blk, steps // r_blk * (batch + SUBLANES), LANES), F32),
                        pltpu.VMEM((n_rows, width), F32),
                        pltpu.VMEM((r_blk, n_rows, n_blocks * LANES), F32),
                        pltpu.VMEM((n_blocks, r_blk * LANES, width), BF16),
                        pltpu.VMEM((n_blocks, width, r_blk * LANES), BF16),
                        pltpu.VMEM((n_blocks, r_blk * LANES, r_blk * LANES), BF16)],
        compiler_params=pltpu.CompilerParams(
            dimension_semantics=("arbitrary",), vmem_limit_bytes=VMEM_LIMIT),
        name="ssm",
    )(u, gs, h0re, h0im, *[wts[n] for n in names])


def _attn_cached_kernel(q_ref, kf_ref, cf_ref, kl_ref, vl_ref, gm_ref, onorm_ref, w_uv_ref, y_ref,
                        s_full, ml_scr, s_last, m_scr, ctx_scr, den_scr, *, tq, tk):
    n_full = kf_ref.shape[1] // tk
    dn = (((1,), (1,)), ((), ()))
    lane = lax.broadcasted_iota(jnp.int32, (tq, LANES), 1)
    heads = range(MLA_HEADS)
    cols = [slice(LANES * hd, LANES * (hd + 1)) for hd in heads]

    def lane_tiles(s):
        return [s[:, LANES * c:LANES * (c + 1)] for c in range(s.shape[1] // LANES)]

    def probs(s, m_rep):
        if s.shape[1] % LANES == 0:
            p = jnp.concatenate([jnp.exp2(t - m_rep) for t in lane_tiles(s)], axis=-1)
        else:
            p = jnp.exp2(s - m_rep[:, :1])
        return p.astype(BF16)

    ml_scr[...] = jnp.full(ml_scr.shape, -jnp.inf, F32)

    def a_step(jt, carry):
        rows = pl.ds(pl.multiple_of(jt * tk, tk), tk)
        for hd in heads:
            s = lax.dot_general(q_ref[0, :, cols[hd]], kf_ref[0, rows, cols[hd]], dn,
                                preferred_element_type=F32)
            s_full[hd, jt] = s
            ml_scr[hd] = functools.reduce(jnp.maximum, lane_tiles(s), ml_scr[hd])
        return carry

    lax.fori_loop(0, n_full, a_step, 0)

    for hd in heads:
        s = lax.dot_general(q_ref[0, :, cols[hd]], kl_ref[0, :, cols[hd]], dn,
                            preferred_element_type=F32)
        s_last[hd] = s
        m = jnp.maximum(jnp.max(s, axis=-1, keepdims=True),
                        jnp.max(ml_scr[hd], axis=-1, keepdims=True))
        m_scr[hd] = jnp.broadcast_to(m, (tq, LANES))
    ctx_scr[...] = jnp.zeros(ctx_scr.shape, F32)
    den_scr[...] = jnp.zeros(den_scr.shape, F32)

    def b_step(jt, carry):
        rows = pl.ds(pl.multiple_of(jt * tk, tk), tk)
        p = jnp.concatenate([probs(s_full[hd, jt], m_scr[hd]) for hd in heads], axis=0)
        ctx_scr[...] += jnp.dot(p, cf_ref[0, rows, :].astype(BF16), preferred_element_type=F32)
        den_scr[...] += jnp.sum(p.astype(F32), axis=-1, keepdims=True)
        return carry

    lax.fori_loop(0, n_full, b_step, 0)

    outs = []
    for hd in heads:
        mine = slice(tq * hd, tq * (hd + 1))
        cached = jnp.dot(ctx_scr[mine].astype(BF16), w_uv_ref[:, cols[hd]], preferred_element_type=F32)
        p = probs(s_last[hd], m_scr[hd])
        new = jnp.dot(p, vl_ref[0, :, cols[hd]], preferred_element_type=F32)
        ones_col = V_DIM if hd % 2 == 0 else 0
        outs.append((cached + new) / (den_scr[mine] + new[:, ones_col:ones_col + 1]))
    attn = jnp.concatenate([jnp.where(lane < V_DIM, outs[e], outs[e + 1])
                            for e in range(0, MLA_HEADS, 2)], axis=-1)
    gm = gm_ref[0]
    y_ref[0] = (_rms(attn, onorm_ref[...]) * (gm * jax.nn.sigmoid(gm))).astype(BF16)


def _attn_causal_kernel(q_ref, k_ref, vt_ref, gm_ref, ms_ref, x_ref, onorm_ref, w_out_ref, y_ref,
                        s_full, ml_scr, s_last, m_scr, acc_scr, ot_scr, *, tq, tk):
    i = pl.program_id(1)
    n_full = (i * tq) // tk
    odd = i % 2 == 1
    nt = (((1,), (1,)), ((), ()))
    heads = range(MLA_HEADS)
    cols = [slice(LANES * hd, LANES * (hd + 1)) for hd in heads]
    vrows = [slice(VT_ROWS * hd, VT_ROWS * (hd + 1)) for hd in heads]

    def group_max(s):
        return jnp.max(s.reshape(s.shape[0] // SUBLANES, SUBLANES, s.shape[1]), axis=0)

    def chunk_mask(width):
        kc = lax.broadcasted_iota(jnp.int32, (width, tq), 0) // CHUNK
        qc = lax.broadcasted_iota(jnp.int32, (width, tq), 1) // CHUNK + (width - tq) // CHUNK
        return qc >= kc

    ml_scr[...] = jnp.full(ml_scr.shape, -jnp.inf, F32)

    def a_step(jt, carry):
        rows = pl.ds(pl.multiple_of(jt * tk, tk), tk)
        for hd in heads:
            s = lax.dot_general(k_ref[0, rows, cols[hd]], q_ref[0, :, cols[hd]], nt,
                                preferred_element_type=F32)
            s_full[hd, jt] = s
            ml_scr[hd] = jnp.maximum(ml_scr[hd], group_max(s))
        return carry

    lax.fori_loop(0, n_full, a_step, 0)

    def last_a(width, rows):
        mask = chunk_mask(width)
        for hd in heads:
            s = lax.dot_general(k_ref[0, rows, cols[hd]], q_ref[0, :, cols[hd]], nt,
                                preferred_element_type=F32)
            s = jnp.where(mask, s, -jnp.inf)
            s_last[hd, :width] = s
            m = jnp.max(jnp.maximum(ml_scr[hd], group_max(s)), axis=0, keepdims=True)
            m_scr[hd] = jnp.broadcast_to(m, m_scr.shape[1:])
            acc_scr[hd] = jnp.zeros(acc_scr.shape[1:], F32)

    pair_rows = pl.ds(pl.multiple_of(n_full * tk, tk), tk)
    pl.when(odd)(functools.partial(last_a, tk, pair_rows))
    pl.when(jnp.logical_not(odd))(functools.partial(last_a, tq, pl.ds(pl.multiple_of(i * tq, tq), tq)))

    def b_step(jt, carry):
        for hd in heads:
            p = jnp.exp2(s_full[hd, jt] - m_scr[hd, :1]).astype(BF16)
            acc_scr[hd] += jnp.dot(vt_ref[0, jt, vrows[hd], :], p, preferred_element_type=F32)
        return carry

    lax.fori_loop(0, n_full, b_step, 0)

    def last_b(width):
        for hd in heads:
            p = jnp.exp2(s_last[hd, :width] - m_scr[hd, :1]).astype(BF16)
            acc = acc_scr[hd] + jnp.dot(vt_ref[0, n_full, vrows[hd], :width], p,
                                        preferred_element_type=F32)
            ot_scr[V_DIM * hd:V_DIM * (hd + 1), :] = acc[:V_DIM] / acc[V_DIM:V_DIM + 1]

    pl.when(odd)(functools.partial(last_b, tk))
    pl.when(jnp.logical_not(odd))(functools.partial(last_b, tq))

    gm = gm_ref[0]
    mla = _rms(ot_scr[...].T, onorm_ref[...]) * (gm * jax.nn.sigmoid(gm))
    mix_ssm = [ms_ref[j, 0] for j in range(ms_ref.shape[0])]
    mix = jnp.concatenate(mix_ssm + [mla], axis=-1).astype(BF16)
    y_ref[0] = x_ref[0] + jnp.dot(mix, w_out_ref[...], preferred_element_type=F32)


def _attn_causal_call(q, k, vt, gm, mix_ssm, x, wts, *, tq):
    b, s, d = x.shape
    tk = vt.shape[3]
    ssm_w = gm.shape[2]
    assert tk == 2 * tq and s % tk == 0 and tq % CHUNK == 0
    row = lambda j, i: (j, i, 0)
    res = lambda j, i: (j, 0, 0)
    scratch = [pltpu.VMEM((MLA_HEADS, s // tk - 1, tk, tq), F32),
               pltpu.VMEM((MLA_HEADS, SUBLANES, tq), F32),
               pltpu.VMEM((MLA_HEADS, tk, tq), F32),
               pltpu.VMEM((MLA_HEADS, SUBLANES, tq), F32),
               pltpu.VMEM((MLA_HEADS, VT_ROWS, tq), F32),
               pltpu.VMEM((MLA_HEADS * V_DIM, tq), F32)]
    return pl.pallas_call(
        functools.partial(_attn_causal_kernel, tq=tq, tk=tk),
        out_shape=jax.ShapeDtypeStruct((b, s, d), F32),
        grid=(b, s // tq),
        in_specs=[pl.BlockSpec((1, tq, HEAD_W), row), pl.BlockSpec((1, s, HEAD_W), res),
                  pl.BlockSpec((1,) + vt.shape[1:], lambda j, i: (j, 0, 0, 0)),
                  pl.BlockSpec((1, tq, ssm_w), row),
                  pl.BlockSpec((mix_ssm.shape[0], 1, tq, LANES), lambda j, i: (0, j, i, 0)),
                  pl.BlockSpec((1, tq, d), row),
                  _full(wts['onorm_mla'].shape), _full(wts['w_out'].shape)],
        out_specs=pl.BlockSpec((1, tq, d), row),
        scratch_shapes=scratch,
        compiler_params=pltpu.CompilerParams(
            dimension_semantics=("arbitrary", "arbitrary"), vmem_limit_bytes=VMEM_LIMIT),
        name="attn",
    )(q, k, vt, gm, mix_ssm, x, wts['onorm_mla'], wts['w_out'])


def _out_kernel(ms_ref, mla_ref, x_ref, w_out_ref, y_ref):
    bb, tq, d = x_ref.shape
    rows = bb * tq
    mix = jnp.concatenate([ms_ref[j].reshape(rows, LANES).astype(BF16) for j in range(ms_ref.shape[0])]
                          + [mla_ref[...].reshape(rows, mla_ref.shape[2])], axis=-1)
    y = x_ref[...].reshape(rows, d) + jnp.dot(mix, w_out_ref[...], preferred_element_type=F32)
    y_ref[...] = y.reshape(bb, tq, d)


def _out_call(mix_ssm, mla, x, wts, *, bb):
    b, s, d = x.shape
    row = lambda j: (j, 0, 0)
    return pl.pallas_call(
        _out_kernel,
        out_shape=jax.ShapeDtypeStruct((b, s, d), F32),
        grid=(b // bb,),
        in_specs=[pl.BlockSpec((mix_ssm.shape[0], bb, s, LANES), lambda j: (0, j, 0, 0)),
                  pl.BlockSpec((bb, s, mla.shape[2]), row), pl.BlockSpec((bb, s, d), row),
                  _full(wts['w_out'].shape)],
        out_specs=pl.BlockSpec((bb, s, d), row),
        compiler_params=pltpu.CompilerParams(
            dimension_semantics=("arbitrary",), vmem_limit_bytes=VMEM_LIMIT),
        name="outproj",
    )(mix_ssm, mla, x, wts['w_out'])


def _attn_cached_call(q, k_full, ckv_full, k_last, v_last, gm, wts, *, tk):
    b, tq, _ = q.shape
    t_full, last_len = k_full.shape[1], k_last.shape[1]
    ssm_w = gm.shape[2]
    blk = lambda rows, width: pl.BlockSpec((1, rows, width), lambda j: (j, 0, 0))
    scratch = [pltpu.VMEM((MLA_HEADS, t_full // tk, tq, tk), F32), pltpu.VMEM((MLA_HEADS, tq, LANES), F32),
               pltpu.VMEM((MLA_HEADS, tq, last_len), F32), pltpu.VMEM((MLA_HEADS, tq, LANES), F32),
               pltpu.VMEM((MLA_HEADS * tq, ckv_full.shape[2]), F32), pltpu.VMEM((MLA_HEADS * tq, 1), F32)]
    return pl.pallas_call(
        functools.partial(_attn_cached_kernel, tq=tq, tk=tk),
        out_shape=jax.ShapeDtypeStruct((b, tq, ssm_w), BF16),
        grid=(b,),
        in_specs=[blk(tq, HEAD_W), blk(t_full, HEAD_W), blk(t_full, ckv_full.shape[2]),
                  blk(last_len, HEAD_W), blk(last_len, HEAD_W), blk(tq, ssm_w),
                  _full(wts['onorm_mla'].shape), _full(wts['w_uv'].shape)],
        out_specs=blk(tq, ssm_w),
        scratch_shapes=scratch,
        compiler_params=pltpu.CompilerParams(
            dimension_semantics=("arbitrary",), vmem_limit_bytes=VMEM_LIMIT),
        name="attn_cached",
    )(q, k_full, ckv_full, k_last, v_last, gm, wts['onorm_mla'], wts['w_uv'])


def _head_block_cols(w, pieces):
    k = w.shape[0]
    w3 = w.reshape(k, MLA_HEADS, w.shape[1] // MLA_HEADS)
    cols = [w3[:, :, p[0]:p[0] + p[1]] if isinstance(p, tuple) else jnp.zeros((k, MLA_HEADS, p), w.dtype)
            for p in pieces]
    return jnp.concatenate(cols, axis=-1).reshape(k, HEAD_W)


def _zoh_powers(lr, li, dt, n):
    mag = jnp.exp(lr * dt)
    ar, ai = mag * jnp.cos(li * dt), mag * jnp.sin(li * dt)
    pw = [(jnp.ones_like(ar), jnp.zeros_like(ai))]
    for _ in range(n):
        pr, pi = pw[-1]
        pw.append((pr * ar - pi * ai, pr * ai + pi * ar))
    return pw


def _disc_kernel(arow_ref, acol_ref, ldt_ref, bt_ref, cd_ref, ct_ref,
                 apr_ref, api_ref, wb_ref, wc_ref, wf_ref):
    r_blk, groups, cin = wb_ref.shape[1], wb_ref.shape[2], wb_ref.shape[3]
    n_state = acol_ref.shape[2]
    gpb = LANES // cin
    lr, li = arow_ref[0], arow_ref[1]
    dt = jnp.exp(ldt_ref[...])
    pw = _zoh_powers(lr, li, dt, r_blk)
    ar, ai = pw[1]
    den = lr * lr + li * li
    kr = ((ar - 1.0) * lr + ai * li) / den
    ki = (ai * lr - (ar - 1.0) * li) / den
    apr_ref[...], api_ref[...] = pw[r_blk]

    pwc = _zoh_powers(acol_ref[0], acol_ref[1], dt, r_blk)
    for j in range(r_blk):
        pr, pi = pwc[j + 1]
        wc_ref[0, j] = ct_ref[0] * pr - ct_ref[1] * pi
        wc_ref[1, j] = -(ct_ref[0] * pi + ct_ref[1] * pr)

    b_r = kr * bt_ref[0] - ki * bt_ref[1]
    b_i = kr * bt_ref[1] + ki * bt_ref[0]
    for i in range(r_blk):
        pr, pi = pw[r_blk - 1 - i]
        wb_ref[0, i] = pr * b_r - pi * b_i
        wb_ref[1, i] = pr * b_i + pi * b_r

    nt = (((1,), (1,)), ((), ()))
    first_copy = lax.broadcasted_iota(jnp.int32, (LANES, LANES), 1) < n_state
    block = lambda v, n: v[gpb * n:gpb * (n + 1)].reshape(LANES, LANES)
    for i in range(r_blk):
        for j in range(i):
            wf_ref[i, j] = jnp.zeros(wf_ref.shape[2:], F32)
    for m in range(r_blk):
        pr, pi = pw[m]
        ca_r = cd_ref[0] * pr - cd_ref[1] * pi
        ca_i = cd_ref[0] * pi + cd_ref[1] * pr
        for n in range(groups // gpb):
            f_t = (lax.dot_general(jnp.where(first_copy, block(b_r, n), 0.0), block(ca_r, n), nt,
                                   precision=lax.Precision.HIGHEST, preferred_element_type=F32)
                   - lax.dot_general(jnp.where(first_copy, block(b_i, n), 0.0), block(ca_i, n), nt,
                                     precision=lax.Precision.HIGHEST, preferred_element_type=F32))
            for i in range(r_blk - m):
                wf_ref[i, i + m, gpb * n:gpb * (n + 1)] = f_t.reshape(gpb, cin, LANES)


def _prepare_weights(norm_in, w_in, ssm_a_re, ssm_a_im, ssm_log_dt, ssm_b_re, ssm_b_im, ssm_c_re,
                     ssm_c_im, ssm_d, w_glu, b_glu, q_lora_norm, kv_lora_norm, w_uq, w_ukv,
                     q_nope_norm, k_nope_norm, q_rope_norm, k_rope_norm, out_norm_ssm,
                     out_norm_mla, w_out):
    groups, n_state = ssm_a_re.shape
    ssm_w = groups * SSM_GROUP
    row = lambda v: v.reshape(1, -1).astype(F32)
    tail = LANES - ROPE_DIM - NOPE_DIM

    o_kr = 2 * ssm_w + Q_LORA + KV_LORA
    w_in_p = jnp.concatenate(
        [w_in[:, :o_kr + ROPE_DIM], jnp.zeros((w_in.shape[0], LANES - ROPE_DIM), w_in.dtype),
         w_in[:, o_kr + ROPE_DIM:]], axis=1).astype(BF16)

    odd_head = (np.arange(HEAD_W) // LANES) % 2 == 1
    by_parity = lambda even, odd: jnp.where(odd_head[None, :], odd, even)
    w_uq_p = by_parity(_head_block_cols(w_uq, [(0, NOPE_DIM), (NOPE_DIM, ROPE_DIM), tail]),
                       _head_block_cols(w_uq, [(NOPE_DIM, ROPE_DIM), tail, (0, NOPE_DIM)])).astype(BF16)
    w_ukc = w_ukv.reshape(KV_LORA, MLA_HEADS, NOPE_DIM + V_DIM)[:, :, :NOPE_DIM]
    w_ukc = w_ukc.reshape(KV_LORA, MLA_HEADS * NOPE_DIM).astype(BF16)
    w_v_lo = _head_block_cols(w_ukv, [(NOPE_DIM, V_DIM), LANES - V_DIM])
    w_v_hi = _head_block_cols(w_ukv, [LANES - V_DIM, (NOPE_DIM, V_DIM)])
    w_uv = by_parity(w_v_lo, w_v_hi).astype(BF16)
    wvt = w_v_lo.reshape(KV_LORA, MLA_HEADS, LANES)[:, :, :VT_ROWS]
    wvt = jnp.transpose(wvt, (1, 2, 0)).reshape(MLA_HEADS * VT_ROWS, KV_LORA).astype(BF16)

    scale = (NOPE_DIM + ROPE_DIM) ** -0.5 * np.log2(np.e)
    vones = np.zeros((MLA_HEADS, LANES), np.float32)
    vones[0::2, V_DIM] = 1.0
    vones[1::2, 0] = 1.0
    vones = jnp.asarray(vones.reshape(1, HEAD_W))
    zeros = lambda n: jnp.zeros((n,), F32)
    gq = jnp.stack([jnp.concatenate([q_nope_norm, q_rope_norm, zeros(tail)]),
                    jnp.concatenate([q_rope_norm, zeros(tail), q_nope_norm])]) * scale
    gkr = jnp.concatenate([k_rope_norm, zeros(LANES - ROPE_DIM)])
    gkn = jnp.tile(k_nope_norm, 2 * LANES // NOPE_DIM)

    def seg_means(widths):
        m = np.zeros((sum(widths), sum(widths)), np.float32)
        o = 0
        for w in widths:
            m[o:o + w, o:o + w] = 1.0 / w
            o += w
        return m

    segq = jnp.asarray(seg_means([NOPE_DIM, ROPE_DIM, tail, ROPE_DIM, tail, NOPE_DIM]), BF16)
    segk = jnp.asarray(seg_means([NOPE_DIM] * (2 * LANES // NOPE_DIM)), BF16)
    place = jnp.asarray(np.eye(ROPE_DIM, LANES, dtype=np.float32), BF16)

    r_blk = SSM_STEPS_PER_SCAN
    lane_rep = lambda x: jnp.tile(x, (1,) * (x.ndim - 1) + (LANES // x.shape[-1],))
    a2 = jnp.stack([ssm_a_re, ssm_a_im]).astype(F32)
    b2 = jnp.stack([ssm_b_re, ssm_b_im]).astype(F32)
    c2 = jnp.stack([ssm_c_re, ssm_c_im]).astype(F32)
    ldt = ssm_log_dt.astype(F32)
    gl = jax.ShapeDtypeStruct((groups, 1, LANES), F32)
    a_re, a_im, wb5, wc5, wf5 = pl.pallas_call(
        _disc_kernel,
        out_shape=[gl, gl,
                   jax.ShapeDtypeStruct((2, r_blk, groups, SSM_GROUP, LANES), F32),
                   jax.ShapeDtypeStruct((2, r_blk, groups, n_state, LANES), F32),
                   jax.ShapeDtypeStruct((r_blk, r_blk, groups, SSM_GROUP, LANES), F32)],
        name="disc")(
        lane_rep(a2)[:, :, None, :], jnp.broadcast_to(a2[..., None], a2.shape + (LANES,)),
        jnp.broadcast_to(ldt[:, None, None], (groups, 1, LANES)),
        lane_rep(jnp.swapaxes(b2, 2, 3)), lane_rep(c2), lane_rep(jnp.swapaxes(c2, 2, 3)))
    a_re, a_im = a_re[:, 0, :n_state], a_im[:, 0, :n_state]

    return dict(
        ssm_w=ssm_w,
        norm_in=row(norm_in), w_in=w_in_p, qln=row(q_lora_norm), kvln=row(kv_lora_norm),
        w_uq=w_uq_p, w_ukc=w_ukc, w_uv=w_uv, wvt=wvt,
        gq=gq.astype(F32), gkr=row(gkr), gkn=row(gkn), segq=segq, segk=segk, vones=vones,
        place=place, a_re=row(a_re), a_im=row(a_im), wb5=wb5, wc5=wc5, wf5=wf5,
        ssm_d=row(ssm_d), w_glu=w_glu.astype(BF16), b_glu=row(b_glu), onorm_ssm=row(out_norm_ssm),
        onorm_mla=row(out_norm_mla), w_out=w_out.astype(BF16))


def _rope_tables(start, n):
    half = ROPE_DIM // 2
    inv = ROPE_THETA ** (-jnp.arange(half, dtype=F32) / half)
    ang = (start + jnp.arange(n)).astype(F32)[:, None] * inv[None, :]
    cos, sin = jnp.cos(ang), jnp.sin(ang)
    cos_t = jnp.concatenate([cos, cos, jnp.ones((n, LANES - ROPE_DIM), F32)], axis=1)
    sin_a = jnp.concatenate([-sin, jnp.zeros((n, LANES - half), F32)], axis=1)
    sin_b = jnp.concatenate([jnp.zeros((n, half), F32), sin, jnp.zeros((n, LANES - ROPE_DIM), F32)],
                            axis=1)
    return cos_t, sin_a, sin_b


def _mixer(x, pos0, h0re, h0im, past, wts, *, proj_bb, proj_tq, ssm_steps, attn_tk):
    u, gs, gm, q, k, v, ckv, kr = _proj_call(x, _rope_tables(pos0, x.shape[1]), wts,
                                             bb=proj_bb, tq=proj_tq, vt_tile=min(attn_tk, proj_tq))
    mix_ssm, hre, him = _ssm_call(u, gs, h0re, h0im, wts, steps=ssm_steps, unroll=True)
    if past is None:
        y = _attn_causal_call(q, k, v, gm, mix_ssm, x, wts, tq=attn_tk // 2)
    else:
        k_past = _expand_call(*past, wts, tq=past[0].shape[1])
        mla = _attn_cached_call(q, k_past, past[0], k, v, gm, wts, tk=attn_tk)
        y = _out_call(mix_ssm, mla, x, wts, bb=proj_bb)
    if proj_bb == 1:
        kr = jnp.swapaxes(kr, 1, 2)
    return y, ckv, kr, hre, him


def kernel(x_prompt, x_sample, cache_ckv, cache_krope, state_ssm_re, state_ssm_im, norm_in, w_in, ssm_a_re, ssm_a_im, ssm_log_dt, ssm_b_re, ssm_b_im, ssm_c_re, ssm_c_im, ssm_d, w_glu, b_glu, q_lora_norm, kv_lora_norm, w_uq, w_ukv, q_nope_norm, k_nope_norm, q_rope_norm, k_rope_norm, out_norm_ssm, out_norm_mla, w_out):
    depth = norm_in.shape[0]
    assert depth == 1, "single mixer layer"
    params = (norm_in, w_in, ssm_a_re, ssm_a_im, ssm_log_dt, ssm_b_re, ssm_b_im, ssm_c_re, ssm_c_im,
              ssm_d, w_glu, b_glu, q_lora_norm, kv_lora_norm, w_uq, w_ukv, q_nope_norm, k_nope_norm,
              q_rope_norm, k_rope_norm, out_norm_ssm, out_norm_mla, w_out)
    drop_depth = lambda a: a.reshape(a.shape[1:])
    wts = _prepare_weights(*[drop_depth(p) for p in params])
    groups, n_state = ssm_a_re.shape[1:]
    bp, sp, _ = x_prompt.shape
    bs, ss, _ = x_sample.shape
    past_len = cache_ckv.shape[2]

    zero_state = jnp.zeros((bp, groups * n_state), F32)
    yp, ckv_p, kr_p, re_p, im_p = _mixer(
        x_prompt, 0, zero_state, zero_state, None, wts,
        proj_bb=1, proj_tq=1024, ssm_steps=64, attn_tk=512)
    ys, ckv_s, kr_s, re_s, im_s = _mixer(
        x_sample, past_len,
        state_ssm_re.reshape(bs, groups * n_state), state_ssm_im.reshape(bs, groups * n_state),
        (drop_depth(cache_ckv), jnp.swapaxes(drop_depth(cache_krope), 1, 2)), wts,
        proj_bb=bs // 2, proj_tq=ss, ssm_steps=ss, attn_tk=past_len)

    st = lambda h, bb: h.reshape(1, bb, groups, n_state)
    return (yp, ys, ckv_p[None], kr_p[None], st(re_p, bp), st(im_p, bp),
            ckv_s[None], kr_s[None], st(re_s, bs), st(im_s, bs))
```

```python
import functools

import numpy as np
import jax
import jax.numpy as jnp
from jax import lax
from jax.experimental import pallas as pl
from jax.experimental.pallas import tpu as pltpu

F32 = jnp.float32
BF16 = jnp.bfloat16

CHUNK = 64
SSM_GROUP = 16
SSM_STATE = 64
MLA_HEADS = 8
NOPE_DIM = 64
ROPE_DIM = 32
V_DIM = 64
Q_LORA = 256
KV_LORA = 128
ROPE_THETA = 10000.0
EPS = 1e-6

LANES = 128
SUBLANES = 8
VT_ROWS = 80
HEAD_W = MLA_HEADS * LANES
GROUPS_PER_BLOCK = 8
SSM_STEPS_PER_SCAN = 4
VMEM_LIMIT = 56 * 1024 * 1024


def _rms(x, gain):
    return x * lax.rsqrt(jnp.mean(x * x, axis=-1, keepdims=True) + EPS) * gain


def _seg_rms(x, seg):
    ms = jnp.dot((x * x).astype(BF16), seg, preferred_element_type=F32)
    return x * lax.rsqrt(ms + EPS)


def _rope_block(x, cos_t, sin_a, sin_b):
    return (x * cos_t + pltpu.roll(x, LANES - ROPE_DIM // 2, 1) * sin_a
            + pltpu.roll(x, ROPE_DIM // 2, 1) * sin_b)


def _key_blocks(ckv_b, kr_blk, w_ukc_ref, gkn_ref, segk_ref, k_ref):
    kc = jnp.dot(ckv_b, w_ukc_ref[...], preferred_element_type=F32)
    kr_even = pltpu.roll(kr_blk, NOPE_DIM, 1)
    low = lax.broadcasted_iota(jnp.int32, (1, LANES), 1) < NOPE_DIM
    blk3 = k_ref.shape[:2] + (LANES,)
    for pp in range(kc.shape[1] // (2 * LANES)):
        kn = _seg_rms(kc[:, 2 * LANES * pp:2 * LANES * (pp + 1)], segk_ref[...]) * gkn_ref[...]
        for t in range(2):
            pair = kn[:, LANES * t:LANES * (t + 1)]
            he = 2 * (2 * pp + t)
            k_ref[:, :, LANES * he:LANES * (he + 1)] = (
                (jnp.where(low, pair, 0.0) + kr_even).astype(BF16).reshape(blk3))
            k_ref[:, :, LANES * (he + 1):LANES * (he + 2)] = (
                (jnp.where(low, 0.0, pair) + kr_blk).astype(BF16).reshape(blk3))


def _proj_kernel(x_ref, cos_ref, sa_ref, sb_ref, norm_in_ref, w_in_ref, qln_ref, kvln_ref,
                 w_uq_ref, w_ukc_ref, gq_ref, gkr_ref, gkn_ref, segq_ref, segk_ref, *rest,
                 ssm_w, transposed):
    v_weights, outs = (rest[:1], rest[1:]) if transposed else (rest[:2], rest[2:])
    u_ref, gs_ref, gm_ref, q_ref, k_ref, v_ref, ckv_ref, kr_ref = outs
    bb, tq, d = x_ref.shape
    x = x_ref[...].reshape(bb * tq, d)
    h = _rms(x, norm_in_ref[...])
    z = jnp.dot(h.astype(BF16), w_in_ref[...], preferred_element_type=F32)
    o = 0
    for dst in (u_ref, gs_ref):
        for j in range(ssm_w // LANES):
            dst[j] = z[:, o + LANES * j:o + LANES * (j + 1)].reshape(bb, tq, LANES)
        o += ssm_w
    c_q = z[:, o:o + Q_LORA]
    o += Q_LORA
    c_kv = z[:, o:o + KV_LORA]
    o += KV_LORA
    kr_raw = z[:, o:o + LANES]
    o += LANES
    gm_ref[...] = z[:, o:].reshape(gm_ref.shape)

    per_row = lambda t_ref: jnp.concatenate([t_ref[...]] * bb, axis=0)
    cos_t, sin_a, sin_b = per_row(cos_ref), per_row(sa_ref), per_row(sb_ref)
    even = lambda t: pltpu.roll(t, NOPE_DIM, 1)
    cos_e, sin_ae, sin_be = even(cos_t), even(sin_a), even(sin_b)
    seg = segq_ref[...]

    kr_ms = jnp.dot((kr_raw * kr_raw).astype(BF16), seg[LANES:, LANES:], preferred_element_type=F32)
    kr_blk = _rope_block(kr_raw * lax.rsqrt(kr_ms + EPS) * gkr_ref[...], cos_t, sin_a, sin_b)
    if transposed:
        kr_ref[0] = kr_blk.T[:ROPE_DIM]
    else:
        kr_ref[...] = kr_blk[:, :ROPE_DIM].reshape(kr_ref.shape)

    q = jnp.dot(_rms(c_q, qln_ref[...]).astype(BF16), w_uq_ref[...], preferred_element_type=F32)
    gq = gq_ref[...]
    for p in range(MLA_HEADS // 2):
        qn = _seg_rms(q[:, 2 * LANES * p:2 * LANES * (p + 1)], seg)
        blocks = (_rope_block(qn[:, :LANES] * gq[0:1], cos_e, sin_ae, sin_be),
                  _rope_block(qn[:, LANES:] * gq[1:2], cos_t, sin_a, sin_b))
        for j in range(2):
            hd = 2 * p + j
            q_ref[:, :, LANES * hd:LANES * (hd + 1)] = blocks[j].astype(BF16).reshape(bb, tq, LANES)

    ckv = _rms(c_kv, kvln_ref[...])
    ckv_ref[...] = ckv.reshape(ckv_ref.shape)
    _key_blocks(ckv.astype(BF16), kr_blk, w_ukc_ref, gkn_ref, segk_ref, k_ref)
    if transposed:
        v_t = jnp.dot(v_weights[0][...], ckv.T.astype(BF16), preferred_element_type=F32)
        ones_row = lax.broadcasted_iota(jnp.int32, v_t.shape, 0) % VT_ROWS == V_DIM
        v_t = jnp.where(ones_row, 1.0, v_t).astype(BF16)
        width = v_ref.shape[3]
        for t in range(v_ref.shape[1]):
            v_ref[0, t] = v_t[:, width * t:width * (t + 1)]
    else:
        v = jnp.dot(ckv.astype(BF16), v_weights[0][...], preferred_element_type=F32) + v_weights[1][...]
        v_ref[...] = v.astype(BF16).reshape(v_ref.shape)


def _full(shape):
    n = len(shape)
    return pl.BlockSpec(shape, lambda *_: (0,) * n)


def _proj_call(x, tables, wts, *, bb, tq, vt_tile):
    b, s, d = x.shape
    ssm_w = wts['ssm_w']
    cos_t, sin_a, sin_b = tables
    transposed = bb == 1
    v_weights = (wts['wvt'],) if transposed else (wts['w_uv'], wts['vones'])
    grid = (s // tq, b // bb)
    row = lambda i, j: (j, i, 0)
    tab = pl.BlockSpec((tq, LANES), lambda i, j: (i, 0))
    in_specs = [pl.BlockSpec((bb, tq, d), row), tab, tab, tab,
                _full(wts['norm_in'].shape), _full(wts['w_in'].shape), _full(wts['qln'].shape),
                _full(wts['kvln'].shape), _full(wts['w_uq'].shape), _full(wts['w_ukc'].shape),
                _full(wts['gq'].shape), _full(wts['gkr'].shape), _full(wts['gkn'].shape),
                _full(wts['segq'].shape), _full(wts['segk'].shape)] + [_full(w.shape) for w in v_weights]
    slabs = ssm_w // LANES
    tb_spec = pl.BlockSpec((slabs, bb, tq, LANES), lambda i, j: (0, j, i, 0))
    kr_shape, kr_spec = (((b, ROPE_DIM, s), pl.BlockSpec((1, ROPE_DIM, tq), lambda i, j: (j, 0, i)))
                         if transposed else
                         ((b, s, ROPE_DIM), pl.BlockSpec((bb, tq, ROPE_DIM), row)))
    v_shape, v_spec = (((b, s // vt_tile, MLA_HEADS * VT_ROWS, vt_tile),
                        pl.BlockSpec((1, tq // vt_tile, MLA_HEADS * VT_ROWS, vt_tile),
                                     lambda i, j: (j, i, 0, 0)))
                       if transposed else
                       ((b, s, HEAD_W), pl.BlockSpec((bb, tq, HEAD_W), row)))
    out_shape = [jax.ShapeDtypeStruct((slabs, b, s, LANES), F32),
                 jax.ShapeDtypeStruct((slabs, b, s, LANES), F32),
                 jax.ShapeDtypeStruct((b, s, ssm_w), F32),
                 jax.ShapeDtypeStruct((b, s, HEAD_W), BF16),
                 jax.ShapeDtypeStruct((b, s, HEAD_W), BF16),
                 jax.ShapeDtypeStruct(v_shape, BF16),
                 jax.ShapeDtypeStruct((b, s, KV_LORA), F32),
                 jax.ShapeDtypeStruct(kr_shape, F32)]
    out_specs = [tb_spec, tb_spec,
                 pl.BlockSpec((bb, tq, ssm_w), row),
                 pl.BlockSpec((bb, tq, HEAD_W), row), pl.BlockSpec((bb, tq, HEAD_W), row), v_spec,
                 pl.BlockSpec((bb, tq, KV_LORA), row), kr_spec]
    return pl.pallas_call(
        functools.partial(_proj_kernel, ssm_w=ssm_w, transposed=transposed),
        out_shape=out_shape, grid=grid, in_specs=in_specs, out_specs=out_specs,
        compiler_params=pltpu.CompilerParams(
            dimension_semantics=("arbitrary", "arbitrary"), vmem_limit_bytes=VMEM_LIMIT),
        name="proj",
    )(x, cos_t, sin_a, sin_b, wts['norm_in'], wts['w_in'], wts['qln'], wts['kvln'], wts['w_uq'],
      wts['w_ukc'], wts['gq'], wts['gkr'], wts['gkn'], wts['segq'], wts['segk'], *v_weights)


def _ssm_kernel(u_ref, gs_ref, h0re_ref, h0im_ref, are_ref, aim_ref, wb5_ref, wc5_ref, wf5_ref, d_ref,
                w_glu_ref, b_glu_ref, onorm_ref, mix_ref, hre_ref, him_ref, us_ref, xs_ref, y_ref,
                wb_ref, wc_ref, wf_ref, *, batch, steps, unroll):
    n_blocks, r_blk = wb_ref.shape[0], wb_ref.shape[1] // LANES
    half = wb_ref.shape[2] // 2
    gpb = wb5_ref.shape[2] // n_blocks
    n_k = steps // r_blk

    @pl.when(pl.program_id(0) == 0)
    def _():
        hre_ref[...] = h0re_ref[...]
        him_ref[...] = h0im_ref[...]

        def own_group(shape, row_div, lane_div):
            return (lax.broadcasted_iota(jnp.int32, shape, 0) // row_div
                    == lax.broadcasted_iota(jnp.int32, shape, 1) // lane_div)

        cin, n_state = wb5_ref.shape[3], half // gpb
        m_b = own_group((LANES, half), cin, n_state)
        m_c = own_group((half, LANES), n_state, cin)
        m_f = own_group((LANES, LANES), cin, cin)
        for gb in range(n_blocks):
            grp = slice(gpb * gb, gpb * (gb + 1))
            for ri in range(2):
                for i in range(r_blk):
                    piece = wb5_ref[ri, i, grp].reshape(LANES, LANES)
                    wide = jnp.concatenate([piece] * (half // LANES), axis=1)
                    wb_ref[gb, LANES * i:LANES * (i + 1), half * ri:half * (ri + 1)] = (
                        jnp.where(m_b, wide, 0.0).astype(BF16))
                    piece = wc5_ref[ri, i, grp].reshape(half, LANES)
                    wc_ref[gb, half * ri:half * (ri + 1), LANES * i:LANES * (i + 1)] = (
                        jnp.where(m_c, piece, 0.0).astype(BF16))
            for i in range(r_blk):
                for j in range(r_blk):
                    piece = wf5_ref[i, j, grp].reshape(LANES, LANES)
                    wf_ref[gb, LANES * i:LANES * (i + 1), LANES * j:LANES * (j + 1)] = (
                        jnp.where(m_f, piece, 0.0).astype(BF16))

    pitch = us_ref.shape[2] // n_k
    for b in range(batch):
        for j in range(n_blocks):
            for i in range(r_blk):
                us_ref[j, i, pl.ds(b, n_k, stride=pitch), :] = (
                    u_ref[j, b, pl.ds(i, n_k, stride=r_blk), :])

    def slab_rows(j, i):
        return jnp.concatenate([us_ref[j, i, pitch * k:pitch * k + batch, :] for k in range(n_k)], axis=0)

    u_slabs = [[slab_rows(j, i) for i in range(r_blk)] for j in range(n_blocks)]
    for gb in range(n_blocks):
        lhs = jnp.concatenate(u_slabs[gb], axis=-1).astype(BF16)
        xs = xs_ref
        xs[...] = jnp.dot(lhs, wb_ref[gb], preferred_element_type=F32)
        cols = slice(half * gb, half * (gb + 1))
        a_re = jnp.broadcast_to(are_ref[:, cols], (batch, half))
        a_im = jnp.broadcast_to(aim_ref[:, cols], (batch, half))

        def step(k, carry, xs=xs, a_re=a_re, a_im=a_im):
            h_re, h_im = carry
            rows = pl.ds(pl.multiple_of(k * batch, batch), batch)
            n_re = a_re * h_re - a_im * h_im + xs[rows, :half]
            n_im = a_re * h_im + a_im * h_re + xs[rows, half:]
            xs[rows, :half] = h_re
            xs[rows, half:] = h_im
            return n_re, n_im

        h_re, h_im = lax.fori_loop(0, n_k, step, (hre_ref[:, cols], him_ref[:, cols]), unroll=unroll)
        hre_ref[:, cols] = h_re
        him_ref[:, cols] = h_im
        y = (jnp.dot(xs[...].astype(BF16), wc_ref[gb], preferred_element_type=F32)
             + jnp.dot(lhs, wf_ref[gb], preferred_element_type=F32))
        for i in range(r_blk):
            y_ref[i, :, LANES * gb:LANES * (gb + 1)] = y[:, LANES * i:LANES * (i + 1)]

    rows = r_blk * n_k * batch
    u = jnp.concatenate([jnp.concatenate(u_slabs[j], axis=0) for j in range(n_blocks)], axis=-1)
    y = y_ref[...].reshape(rows, n_blocks * LANES) + d_ref[...] * u
    yg = jax.nn.gelu(y)
    glu = jnp.dot(yg.astype(BF16), w_glu_ref[...], preferred_element_type=F32) + b_glu_ref[...]
    out = _rms(yg * jax.nn.sigmoid(glu), onorm_ref[...])
    for j in range(n_blocks):
        for i in range(r_blk):
            for k in range(n_k):
                r0 = (i * n_k + k) * batch
                us_ref[j, i, pitch * k:pitch * k + batch, :] = out[r0:r0 + batch, LANES * j:LANES * (j + 1)]
    for b in range(batch):
        for j in range(n_blocks):
            for i in range(r_blk):
                tok = pl.ds(i, n_k, stride=r_blk)
                gs = gs_ref[j, b, tok, :]
                o = us_ref[j, i, pl.ds(b, n_k, stride=pitch), :]
                mix_ref[j, b, tok, :] = o * (gs * jax.nn.sigmoid(gs))


def _ssm_call(u, gs, h0re, h0im, wts, *, steps, unroll):
    n_blocks, batch, seq, _ = u.shape
    n_state = h0re.shape[1]
    r_blk = wts['wb5'].shape[1]
    width = 2 * n_state // n_blocks
    assert steps % r_blk == 0 and wts['wb5'].shape[3] * GROUPS_PER_BLOCK == LANES
    blk = pl.BlockSpec((n_blocks, batch, steps, LANES), lambda i: (0, 0, i, 0))
    names = ['a_re', 'a_im', 'wb5', 'wc5', 'wf5', 'ssm_d', 'w_glu', 'b_glu', 'onorm_ssm']
    n_rows = steps // r_blk * batch
    return pl.pallas_call(
        functools.partial(_ssm_kernel, batch=batch, steps=steps, unroll=unroll),
        out_shape=[jax.ShapeDtypeStruct(u.shape, F32),
                   jax.ShapeDtypeStruct((batch, n_state), F32),
                   jax.ShapeDtypeStruct((batch, n_state), F32)],
        grid=(seq // steps,),
        in_specs=[blk, blk, _full(h0re.shape), _full(h0im.shape)] + [_full(wts[n].shape) for n in names],
        out_specs=[blk, _full((batch, n_state)), _full((batch, n_state))],
        scratch_shapes=[pltpu.VMEM((n_blocks, r_blk, steps // r_blk * (batch + SUBLANES), LANES), F32),
                        pltpu.VMEM((n_rows, width), F32),
                        pltpu.VMEM((r_blk, n_rows, n_blocks * LANES), F32),
                        pltpu.VMEM((n_blocks, r_blk * LANES, width), BF16),
                        pltpu.VMEM((n_blocks, width, r_blk * LANES), BF16),
                        pltpu.VMEM((n_blocks, r_blk * LANES, r_blk * LANES), BF16)],
        compiler_params=pltpu.CompilerParams(
            dimension_semantics=("arbitrary",), vmem_limit_bytes=VMEM_LIMIT),
        name="ssm",
    )(u, gs, h0re, h0im, *[wts[n] for n in names])


def _attn_cached_kernel(q_ref, cf_ref, krf_ref, kl_ref, vl_ref, gm_ref, onorm_ref, w_uv_ref,
                        w_ukc_ref, gkn_ref, segk_ref, place_ref, y_ref,
                        kf_ref, s_full, ml_scr, s_last, m_scr, ctx_scr, den_scr, *, tq, tk):
    kr_blk = lax.dot_general(krf_ref[0].astype(BF16), place_ref[...], (((0,), (0,)), ((), ())),
                             preferred_element_type=F32)
    _key_blocks(cf_ref[0].astype(BF16), kr_blk, w_ukc_ref, gkn_ref, segk_ref, kf_ref)

    n_full = kf_ref.shape[1] // tk
    dn = (((1,), (1,)), ((), ()))
    lane = lax.broadcasted_iota(jnp.int32, (tq, LANES), 1)
    heads = range(MLA_HEADS)
    cols = [slice(LANES * hd, LANES * (hd + 1)) for hd in heads]

    def lane_tiles(s):
        return [s[:, LANES * c:LANES * (c + 1)] for c in range(s.shape[1] // LANES)]

    def probs(s, m_rep):
        if s.shape[1] % LANES == 0:
            p = jnp.concatenate([jnp.exp2(t - m_rep) for t in lane_tiles(s)], axis=-1)
        else:
            p = jnp.exp2(s - m_rep[:, :1])
        return p.astype(BF16)

    ml_scr[...] = jnp.full(ml_scr.shape, -jnp.inf, F32)

    def a_step(jt, carry):
        rows = pl.ds(pl.multiple_of(jt * tk, tk), tk)
        for hd in heads:
            s = lax.dot_general(q_ref[0, :, cols[hd]], kf_ref[0, rows, cols[hd]], dn,
                                preferred_element_type=F32)
            s_full[hd, jt] = s
            ml_scr[hd] = functools.reduce(jnp.maximum, lane_tiles(s), ml_scr[hd])
        return carry

    lax.fori_loop(0, n_full, a_step, 0)

    for hd in heads:
        s = lax.dot_general(q_ref[0, :, cols[hd]], kl_ref[0, :, cols[hd]], dn,
                            preferred_element_type=F32)
        s_last[hd] = s
        m = jnp.maximum(jnp.max(s, axis=-1, keepdims=True),
                        jnp.max(ml_scr[hd], axis=-1, keepdims=True))
        m_scr[hd] = jnp.broadcast_to(m, (tq, LANES))
    ctx_scr[...] = jnp.zeros(ctx_scr.shape, F32)
    den_scr[...] = jnp.zeros(den_scr.shape, F32)

    def b_step(jt, carry):
        rows = pl.ds(pl.multiple_of(jt * tk, tk), tk)
        p = jnp.concatenate([probs(s_full[hd, jt], m_scr[hd]) for hd in heads], axis=0)
        ctx_scr[...] += jnp.dot(p, cf_ref[0, rows, :].astype(BF16), preferred_element_type=F32)
        den_scr[...] += jnp.sum(p.astype(F32), axis=-1, keepdims=True)
        return carry

    lax.fori_loop(0, n_full, b_step, 0)

    outs = []
    for hd in heads:
        mine = slice(tq * hd, tq * (hd + 1))
        cached = jnp.dot(ctx_scr[mine].astype(BF16), w_uv_ref[:, cols[hd]], preferred_element_type=F32)
        p = probs(s_last[hd], m_scr[hd])
        new = jnp.dot(p, vl_ref[0, :, cols[hd]], preferred_element_type=F32)
        ones_col = V_DIM if hd % 2 == 0 else 0
        outs.append((cached + new) / (den_scr[mine] + new[:, ones_col:ones_col + 1]))
    attn = jnp.concatenate([jnp.where(lane < V_DIM, outs[e], outs[e + 1])
                            for e in range(0, MLA_HEADS, 2)], axis=-1)
    gm = gm_ref[0]
    y_ref[0] = (_rms(attn, onorm_ref[...]) * (gm * jax.nn.sigmoid(gm))).astype(BF16)


def _attn_causal_kernel(q_ref, k_ref, vt_ref, gm_ref, ms_ref, x_ref, onorm_ref, w_out_ref, y_ref,
                        s_full, ml_scr, s_last, m_scr, acc_scr, ot_scr, *, tq, tk):
    i = pl.program_id(1)
    n_full = (i * tq) // tk
    odd = i % 2 == 1
    nt = (((1,), (1,)), ((), ()))
    heads = range(MLA_HEADS)
    cols = [slice(LANES * hd, LANES * (hd + 1)) for hd in heads]
    vrows = [slice(VT_ROWS * hd, VT_ROWS * (hd + 1)) for hd in heads]

    def group_max(s):
        return jnp.max(s.reshape(s.shape[0] // SUBLANES, SUBLANES, s.shape[1]), axis=0)

    def chunk_mask(width):
        kc = lax.broadcasted_iota(jnp.int32, (width, tq), 0) // CHUNK
        qc = lax.broadcasted_iota(jnp.int32, (width, tq), 1) // CHUNK + (width - tq) // CHUNK
        return qc >= kc

    ml_scr[...] = jnp.full(ml_scr.shape, -jnp.inf, F32)

    def a_step(jt, carry):
        rows = pl.ds(pl.multiple_of(jt * tk, tk), tk)
        for hd in heads:
            s = lax.dot_general(k_ref[0, rows, cols[hd]], q_ref[0, :, cols[hd]], nt,
                                preferred_element_type=F32)
            s_full[hd, jt] = s
            ml_scr[hd] = jnp.maximum(ml_scr[hd], group_max(s))
        return carry

    lax.fori_loop(0, n_full, a_step, 0)

    def last_a(width, rows):
        mask = chunk_mask(width)
        for hd in heads:
            s = lax.dot_general(k_ref[0, rows, cols[hd]], q_ref[0, :, cols[hd]], nt,
                                preferred_element_type=F32)
            s = jnp.where(mask, s, -jnp.inf)
            s_last[hd, :width] = s
            m = jnp.max(jnp.maximum(ml_scr[hd], group_max(s)), axis=0, keepdims=True)
            m_scr[hd] = jnp.broadcast_to(m, m_scr.shape[1:])
            acc_scr[hd] = jnp.zeros(acc_scr.shape[1:], F32)

    pair_rows = pl.ds(pl.multiple_of(n_full * tk, tk), tk)
    pl.when(odd)(functools.partial(last_a, tk, pair_rows))
    pl.when(jnp.logical_not(odd))(functools.partial(last_a, tq, pl.ds(pl.multiple_of(i * tq, tq), tq)))

    def b_step(jt, carry):
        for hd in heads:
            p = jnp.exp2(s_full[hd, jt] - m_scr[hd, :1]).astype(BF16)
            acc_scr[hd] += jnp.dot(vt_ref[0, jt, vrows[hd], :], p, preferred_element_type=F32)
        return carry

    lax.fori_loop(0, n_full, b_step, 0)

    def last_b(width):
        for hd in heads:
            p = jnp.exp2(s_last[hd, :width] - m_scr[hd, :1]).astype(BF16)
            acc = acc_scr[hd] + jnp.dot(vt_ref[0, n_full, vrows[hd], :width], p,
                                        preferred_element_type=F32)
            ot_scr[V_DIM * hd:V_DIM * (hd + 1), :] = acc[:V_DIM] / acc[V_DIM:V_DIM + 1]

    pl.when(odd)(functools.partial(last_b, tk))
    pl.when(jnp.logical_not(odd))(functools.partial(last_b, tq))

    gm = gm_ref[0]
    mla = _rms(ot_scr[...].T, onorm_ref[...]) * (gm * jax.nn.sigmoid(gm))
    mix_ssm = [ms_ref[j, 0] for j in range(ms_ref.shape[0])]
    mix = jnp.concatenate(mix_ssm + [mla], axis=-1).astype(BF16)
    y_ref[0] = x_ref[0] + jnp.dot(mix, w_out_ref[...], preferred_element_type=F32)


def _attn_causal_call(q, k, vt, gm, mix_ssm, x, wts, *, tq):
    b, s, d = x.shape
    tk = vt.shape[3]
    ssm_w = gm.shape[2]
    assert tk == 2 * tq and s % tk == 0 and tq % CHUNK == 0
    row = lambda j, i: (j, i, 0)
    res = lambda j, i: (j, 0, 0)
    scratch = [pltpu.VMEM((MLA_HEADS, s // tk - 1, tk, tq), F32),
               pltpu.VMEM((MLA_HEADS, SUBLANES, tq), F32),
               pltpu.VMEM((MLA_HEADS, tk, tq), F32),
               pltpu.VMEM((MLA_HEADS, SUBLANES, tq), F32),
               pltpu.VMEM((MLA_HEADS, VT_ROWS, tq), F32),
               pltpu.VMEM((MLA_HEADS * V_DIM, tq), F32)]
    return pl.pallas_call(
        functools.partial(_attn_causal_kernel, tq=tq, tk=tk),
        out_shape=jax.ShapeDtypeStruct((b, s, d), F32),
        grid=(b, s // tq),
        in_specs=[pl.BlockSpec((1, tq, HEAD_W), row), pl.BlockSpec((1, s, HEAD_W), res),
                  pl.BlockSpec((1,) + vt.shape[1:], lambda j, i: (j, 0, 0, 0)),
                  pl.BlockSpec((1, tq, ssm_w), row),
                  pl.BlockSpec((mix_ssm.shape[0], 1, tq, LANES), lambda j, i: (0, j, i, 0)),
                  pl.BlockSpec((1, tq, d), row),
                  _full(wts['onorm_mla'].shape), _full(wts['w_out'].shape)],
        out_specs=pl.BlockSpec((1, tq, d), row),
        scratch_shapes=scratch,
        compiler_params=pltpu.CompilerParams(
            dimension_semantics=("arbitrary", "arbitrary"), vmem_limit_bytes=VMEM_LIMIT),
        name="attn",
    )(q, k, vt, gm, mix_ssm, x, wts['onorm_mla'], wts['w_out'])


def _out_kernel(ms_ref, mla_ref, x_ref, w_out_ref, y_ref):
    bb, tq, d = x_ref.shape
    rows = bb * tq
    mix = jnp.concatenate([ms_ref[j].reshape(rows, LANES).astype(BF16) for j in range(ms_ref.shape[0])]
                          + [mla_ref[...].reshape(rows, mla_ref.shape[2])], axis=-1)
    y = x_ref[...].reshape(rows, d) + jnp.dot(mix, w_out_ref[...], preferred_element_type=F32)
    y_ref[...] = y.reshape(bb, tq, d)


def _out_call(mix_ssm, mla, x, wts, *, bb):
    b, s, d = x.shape
    row = lambda j: (j, 0, 0)
    return pl.pallas_call(
        _out_kernel,
        out_shape=jax.ShapeDtypeStruct((b, s, d), F32),
        grid=(b // bb,),
        in_specs=[pl.BlockSpec((mix_ssm.shape[0], bb, s, LANES), lambda j: (0, j, 0, 0)),
                  pl.BlockSpec((bb, s, mla.shape[2]), row), pl.BlockSpec((bb, s, d), row),
                  _full(wts['w_out'].shape)],
        out_specs=pl.BlockSpec((bb, s, d), row),
        compiler_params=pltpu.CompilerParams(
            dimension_semantics=("arbitrary",), vmem_limit_bytes=VMEM_LIMIT),
        name="outproj",
    )(mix_ssm, mla, x, wts['w_out'])


def _attn_cached_call(q, ckv_full, kr_full_t, k_last, v_last, gm, wts, *, tk):
    b, tq, _ = q.shape
    t_full, last_len = ckv_full.shape[1], k_last.shape[1]
    ssm_w = gm.shape[2]
    blk = lambda rows, width: pl.BlockSpec((1, rows, width), lambda j: (j, 0, 0))
    consts = [wts[n] for n in ('onorm_mla', 'w_uv', 'w_ukc', 'gkn', 'segk', 'place')]
    scratch = [pltpu.VMEM((1, t_full, HEAD_W), BF16),
               pltpu.VMEM((MLA_HEADS, t_full // tk, tq, tk), F32), pltpu.VMEM((MLA_HEADS, tq, LANES), F32),
               pltpu.VMEM((MLA_HEADS, tq, last_len), F32), pltpu.VMEM((MLA_HEADS, tq, LANES), F32),
               pltpu.VMEM((MLA_HEADS * tq, ckv_full.shape[2]), F32), pltpu.VMEM((MLA_HEADS * tq, 1), F32)]
    return pl.pallas_call(
        functools.partial(_attn_cached_kernel, tq=tq, tk=tk),
        out_shape=jax.ShapeDtypeStruct((b, tq, ssm_w), BF16),
        grid=(b,),
        in_specs=[blk(tq, HEAD_W), blk(t_full, ckv_full.shape[2]), blk(ROPE_DIM, t_full),
                  blk(last_len, HEAD_W), blk(last_len, HEAD_W), blk(tq, ssm_w)]
                 + [_full(c.shape) for c in consts],
        out_specs=blk(tq, ssm_w),
        scratch_shapes=scratch,
        compiler_params=pltpu.CompilerParams(
            dimension_semantics=("arbitrary",), vmem_limit_bytes=VMEM_LIMIT),
        name="attn_cached",
    )(q, ckv_full, kr_full_t, k_last, v_last, gm, *consts)


def _head_block_cols(w, pieces):
    k = w.shape[0]
    w3 = w.reshape(k, MLA_HEADS, w.shape[1] // MLA_HEADS)
    cols = [w3[:, :, p[0]:p[0] + p[1]] if isinstance(p, tuple) else jnp.zeros((k, MLA_HEADS, p), w.dtype)
            for p in pieces]
    return jnp.concatenate(cols, axis=-1).reshape(k, HEAD_W)


def _zoh_powers(lr, li, dt, n):
    mag = jnp.exp(lr * dt)
    ar, ai = mag * jnp.cos(li * dt), mag * jnp.sin(li * dt)
    pw = [(jnp.ones_like(ar), jnp.zeros_like(ai))]
    for _ in range(n):
        pr, pi = pw[-1]
        pw.append((pr * ar - pi * ai, pr * ai + pi * ar))
    return pw


def _disc_kernel(arow_ref, acol_ref, ldt_ref, bt_ref, cd_ref, ct_ref,
                 apr_ref, api_ref, wb_ref, wc_ref, wf_ref):
    r_blk, groups, cin = wb_ref.shape[1], wb_ref.shape[2], wb_ref.shape[3]
    n_state = acol_ref.shape[2]
    gpb = LANES // cin
    lr, li = arow_ref[0], arow_ref[1]
    dt = jnp.exp(ldt_ref[...])
    pw = _zoh_powers(lr, li, dt, r_blk)
    ar, ai = pw[1]
    den = lr * lr + li * li
    kr = ((ar - 1.0) * lr + ai * li) / den
    ki = (ai * lr - (ar - 1.0) * li) / den
    apr_ref[...], api_ref[...] = pw[r_blk]

    pwc = _zoh_powers(acol_ref[0], acol_ref[1], dt, r_blk)
    for j in range(r_blk):
        pr, pi = pwc[j + 1]
        wc_ref[0, j] = ct_ref[0] * pr - ct_ref[1] * pi
        wc_ref[1, j] = -(ct_ref[0] * pi + ct_ref[1] * pr)

    b_r = kr * bt_ref[0] - ki * bt_ref[1]
    b_i = kr * bt_ref[1] + ki * bt_ref[0]
    for i in range(r_blk):
        pr, pi = pw[r_blk - 1 - i]
        wb_ref[0, i] = pr * b_r - pi * b_i
        wb_ref[1, i] = pr * b_i + pi * b_r

    nt = (((1,), (1,)), ((), ()))
    first_copy = lax.broadcasted_iota(jnp.int32, (LANES, LANES), 1) < n_state
    block = lambda v, n: v[gpb * n:gpb * (n + 1)].reshape(LANES, LANES)
    for i in range(r_blk):
        for j in range(i):
            wf_ref[i, j] = jnp.zeros(wf_ref.shape[2:], F32)
    for m in range(r_blk):
        pr, pi = pw[m]
        ca_r = cd_ref[0] * pr - cd_ref[1] * pi
        ca_i = cd_ref[0] * pi + cd_ref[1] * pr
        for n in range(groups // gpb):
            f_t = (lax.dot_general(jnp.where(first_copy, block(b_r, n), 0.0), block(ca_r, n), nt,
                                   precision=lax.Precision.HIGHEST, preferred_element_type=F32)
                   - lax.dot_general(jnp.where(first_copy, block(b_i, n), 0.0), block(ca_i, n), nt,
                                     precision=lax.Precision.HIGHEST, preferred_element_type=F32))
            for i in range(r_blk - m):
                wf_ref[i, i + m, gpb * n:gpb * (n + 1)] = f_t.reshape(gpb, cin, LANES)


def _prepare_weights(norm_in, w_in, ssm_a_re, ssm_a_im, ssm_log_dt, ssm_b_re, ssm_b_im, ssm_c_re,
                     ssm_c_im, ssm_d, w_glu, b_glu, q_lora_norm, kv_lora_norm, w_uq, w_ukv,
                     q_nope_norm, k_nope_norm, q_rope_norm, k_rope_norm, out_norm_ssm,
                     out_norm_mla, w_out):
    groups, n_state = ssm_a_re.shape
    ssm_w = groups * SSM_GROUP
    row = lambda v: v.reshape(1, -1).astype(F32)
    tail = LANES - ROPE_DIM - NOPE_DIM

    o_kr = 2 * ssm_w + Q_LORA + KV_LORA
    w_in_p = jnp.concatenate(
        [w_in[:, :o_kr + ROPE_DIM], jnp.zeros((w_in.shape[0], LANES - ROPE_DIM), w_in.dtype),
         w_in[:, o_kr + ROPE_DIM:]], axis=1).astype(BF16)

    odd_head = (np.arange(HEAD_W) // LANES) % 2 == 1
    by_parity = lambda even, odd: jnp.where(odd_head[None, :], odd, even)
    w_uq_p = by_parity(_head_block_cols(w_uq, [(0, NOPE_DIM), (NOPE_DIM, ROPE_DIM), tail]),
                       _head_block_cols(w_uq, [(NOPE_DIM, ROPE_DIM), tail, (0, NOPE_DIM)])).astype(BF16)
    w_ukc = w_ukv.reshape(KV_LORA, MLA_HEADS, NOPE_DIM + V_DIM)[:, :, :NOPE_DIM]
    w_ukc = w_ukc.reshape(KV_LORA, MLA_HEADS * NOPE_DIM).astype(BF16)
    w_v_lo = _head_block_cols(w_ukv, [(NOPE_DIM, V_DIM), LANES - V_DIM])
    w_v_hi = _head_block_cols(w_ukv, [LANES - V_DIM, (NOPE_DIM, V_DIM)])
    w_uv = by_parity(w_v_lo, w_v_hi).astype(BF16)
    wvt = w_v_lo.reshape(KV_LORA, MLA_HEADS, LANES)[:, :, :VT_ROWS]
    wvt = jnp.transpose(wvt, (1, 2, 0)).reshape(MLA_HEADS * VT_ROWS, KV_LORA).astype(BF16)

    scale = (NOPE_DIM + ROPE_DIM) ** -0.5 * np.log2(np.e)
    vones = np.zeros((MLA_HEADS, LANES), np.float32)
    vones[0::2, V_DIM] = 1.0
    vones[1::2, 0] = 1.0
    vones = jnp.asarray(vones.reshape(1, HEAD_W))
    zeros = lambda n: jnp.zeros((n,), F32)
    gq = jnp.stack([jnp.concatenate([q_nope_norm, q_rope_norm, zeros(tail)]),
                    jnp.concatenate([q_rope_norm, zeros(tail), q_nope_norm])]) * scale
    gkr = jnp.concatenate([k_rope_norm, zeros(LANES - ROPE_DIM)])
    gkn = jnp.tile(k_nope_norm, 2 * LANES // NOPE_DIM)

    def seg_means(widths):
        m = np.zeros((sum(widths), sum(widths)), np.float32)
        o = 0
        for w in widths:
            m[o:o + w, o:o + w] = 1.0 / w
            o += w
        return m

    segq = jnp.asarray(seg_means([NOPE_DIM, ROPE_DIM, tail, ROPE_DIM, tail, NOPE_DIM]), BF16)
    segk = jnp.asarray(seg_means([NOPE_DIM] * (2 * LANES // NOPE_DIM)), BF16)
    place = jnp.asarray(np.eye(ROPE_DIM, LANES, dtype=np.float32), BF16)

    r_blk = SSM_STEPS_PER_SCAN
    lane_rep = lambda x: jnp.tile(x, (1,) * (x.ndim - 1) + (LANES // x.shape[-1],))
    a2 = jnp.stack([ssm_a_re, ssm_a_im]).astype(F32)
    b2 = jnp.stack([ssm_b_re, ssm_b_im]).astype(F32)
    c2 = jnp.stack([ssm_c_re, ssm_c_im]).astype(F32)
    ldt = ssm_log_dt.astype(F32)
    gl = jax.ShapeDtypeStruct((groups, 1, LANES), F32)
    a_re, a_im, wb5, wc5, wf5 = pl.pallas_call(
        _disc_kernel,
        out_shape=[gl, gl,
                   jax.ShapeDtypeStruct((2, r_blk, groups, SSM_GROUP, LANES), F32),
                   jax.ShapeDtypeStruct((2, r_blk, groups, n_state, LANES), F32),
                   jax.ShapeDtypeStruct((r_blk, r_blk, groups, SSM_GROUP, LANES), F32)],
        name="disc")(
        lane_rep(a2)[:, :, None, :], jnp.broadcast_to(a2[..., None], a2.shape + (LANES,)),
        jnp.broadcast_to(ldt[:, None, None], (groups, 1, LANES)),
        lane_rep(jnp.swapaxes(b2, 2, 3)), lane_rep(c2), lane_rep(jnp.swapaxes(c2, 2, 3)))
    a_re, a_im = a_re[:, 0, :n_state], a_im[:, 0, :n_state]

    return dict(
        ssm_w=ssm_w,
        norm_in=row(norm_in), w_in=w_in_p, qln=row(q_lora_norm), kvln=row(kv_lora_norm),
        w_uq=w_uq_p, w_ukc=w_ukc, w_uv=w_uv, wvt=wvt,
        gq=gq.astype(F32), gkr=row(gkr), gkn=row(gkn), segq=segq, segk=segk, vones=vones,
        place=place, a_re=row(a_re), a_im=row(a_im), wb5=wb5, wc5=wc5, wf5=wf5,
        ssm_d=row(ssm_d), w_glu=w_glu.astype(BF16), b_glu=row(b_glu), onorm_ssm=row(out_norm_ssm),
        onorm_mla=row(out_norm_mla), w_out=w_out.astype(BF16))


def _rope_tables(start, n):
    half = ROPE_DIM // 2
    inv = ROPE_THETA ** (-jnp.arange(half, dtype=F32) / half)
    ang = (start + jnp.arange(n)).astype(F32)[:, None] * inv[None, :]
    cos, sin = jnp.cos(ang), jnp.sin(ang)
    cos_t = jnp.concatenate([cos, cos, jnp.ones((n, LANES - ROPE_DIM), F32)], axis=1)
    sin_a = jnp.concatenate([-sin, jnp.zeros((n, LANES - half), F32)], axis=1)
    sin_b = jnp.concatenate([jnp.zeros((n, half), F32), sin, jnp.zeros((n, LANES - ROPE_DIM), F32)],
                            axis=1)
    return cos_t, sin_a, sin_b


def _mixer(x, pos0, h0re, h0im, past, wts, *, proj_bb, proj_tq, ssm_steps, attn_tk):
    u, gs, gm, q, k, v, ckv, kr = _proj_call(x, _rope_tables(pos0, x.shape[1]), wts,
                                             bb=proj_bb, tq=proj_tq, vt_tile=min(attn_tk, proj_tq))
    mix_ssm, hre, him = _ssm_call(u, gs, h0re, h0im, wts, steps=ssm_steps, unroll=True)
    if past is None:
        y = _attn_causal_call(q, k, v, gm, mix_ssm, x, wts, tq=attn_tk // 2)
    else:
        mla = _attn_cached_call(q, *past, k, v, gm, wts, tk=attn_tk)
        y = _out_call(mix_ssm, mla, x, wts, bb=proj_bb)
    if proj_bb == 1:
        kr = jnp.swapaxes(kr, 1, 2)
    return y, ckv, kr, hre, him


def kernel(x_prompt, x_sample, cache_ckv, cache_krope, state_ssm_re, state_ssm_im, norm_in, w_in, ssm_a_re, ssm_a_im, ssm_log_dt, ssm_b_re, ssm_b_im, ssm_c_re, ssm_c_im, ssm_d, w_glu, b_glu, q_lora_norm, kv_lora_norm, w_uq, w_ukv, q_nope_norm, k_nope_norm, q_rope_norm, k_rope_norm, out_norm_ssm, out_norm_mla, w_out):
    depth = norm_in.shape[0]
    assert depth == 1, "single mixer layer"
    params = (norm_in, w_in, ssm_a_re, ssm_a_im, ssm_log_dt, ssm_b_re, ssm_b_im, ssm_c_re, ssm_c_im,
              ssm_d, w_glu, b_glu, q_lora_norm, kv_lora_norm, w_uq, w_ukv, q_nope_norm, k_nope_norm,
              q_rope_norm, k_rope_norm, out_norm_ssm, out_norm_mla, w_out)
    drop_depth = lambda a: a.reshape(a.shape[1:])
    wts = _prepare_weights(*[drop_depth(p) for p in params])
    groups, n_state = ssm_a_re.shape[1:]
    bp, sp, _ = x_prompt.shape
    bs, ss, _ = x_sample.shape
    past_len = cache_ckv.shape[2]

    zero_state = jnp.zeros((bp, groups * n_state), F32)
    yp, ckv_p, kr_p, re_p, im_p = _mixer(
        x_prompt, 0, zero_state, zero_state, None, wts,
        proj_bb=1, proj_tq=1024, ssm_steps=64, attn_tk=512)
    ys, ckv_s, kr_s, re_s, im_s = _mixer(
        x_sample, past_len,
        state_ssm_re.reshape(bs, groups * n_state), state_ssm_im.reshape(bs, groups * n_state),
        (drop_depth(cache_ckv), jnp.swapaxes(drop_depth(cache_krope), 1, 2)), wts,
        proj_bb=bs // 2, proj_tq=ss, ssm_steps=ss, attn_tk=past_len)

    st = lambda h, bb: h.reshape(1, bb, groups, n_state)
    return (yp, ys, ckv_p[None], kr_p[None], st(re_p, bp), st(im_p, bp),
            ckv_s[None], kr_s[None], st(re_s, bs), st(im_s, bs))
```

```python
import functools

import numpy as np
import jax
import jax.numpy as jnp
from jax import lax
from jax.experimental import pallas as pl
from jax.experimental.pallas import tpu as pltpu

F32 = jnp.float32
BF16 = jnp.bfloat16

CHUNK = 64
SSM_GROUP = 16
MLA_HEADS = 8
NOPE_DIM = 64
ROPE_DIM = 32
V_DIM = 64
Q_LORA = 256
KV_LORA = 128
ROPE_THETA = 10000.0
EPS = 1e-6

LANES = 128
SUBLANES = 8
VT_ROWS = 80
HEAD_W = MLA_HEADS * LANES
GROUPS_PER_BLOCK = 8
SSM_STEPS_PER_SCAN = 4
VMEM_LIMIT = 56 * 1024 * 1024


def _rms(x, gain):
    return x * lax.rsqrt(jnp.mean(x * x, axis=-1, keepdims=True) + EPS) * gain


def _seg_rms(x, seg):
    ms = jnp.dot((x * x).astype(BF16), seg, preferred_element_type=F32)
    return x * lax.rsqrt(ms + EPS)


def _rope_block(x, cos_t, sin_a, sin_b):
    return (x * cos_t + pltpu.roll(x, LANES - ROPE_DIM // 2, 1) * sin_a
            + pltpu.roll(x, ROPE_DIM // 2, 1) * sin_b)


def _key_blocks(ckv_b, kr_blk, w_ukc_ref, gkn_ref, segk_ref, k_ref):
    kc = jnp.dot(ckv_b, w_ukc_ref[...], preferred_element_type=F32)
    kr_even = pltpu.roll(kr_blk, NOPE_DIM, 1)
    low = lax.broadcasted_iota(jnp.int32, (1, LANES), 1) < NOPE_DIM
    blk3 = k_ref.shape[:2] + (LANES,)
    for pp in range(kc.shape[1] // (2 * LANES)):
        kn = _seg_rms(kc[:, 2 * LANES * pp:2 * LANES * (pp + 1)], segk_ref[...]) * gkn_ref[...]
        for t in range(2):
            pair = kn[:, LANES * t:LANES * (t + 1)]
            he = 2 * (2 * pp + t)
            k_ref[:, :, LANES * he:LANES * (he + 1)] = (
                (jnp.where(low, pair, 0.0) + kr_even).astype(BF16).reshape(blk3))
            k_ref[:, :, LANES * (he + 1):LANES * (he + 2)] = (
                (jnp.where(low, 0.0, pair) + kr_blk).astype(BF16).reshape(blk3))


def _proj_kernel(x_ref, cos_ref, sa_ref, sb_ref, norm_in_ref, w_in_ref, qln_ref, kvln_ref,
                 w_uq_ref, w_ukc_ref, gq_ref, gkr_ref, gkn_ref, segq_ref, segk_ref, *rest,
                 ssm_w, transposed):
    v_weights, outs = (rest[:1], rest[1:]) if transposed else (rest[:2], rest[2:])
    u_ref, gs_ref, gm_ref, q_ref, k_ref, v_ref, ckv_ref, kr_ref = outs
    bb, tq, d = x_ref.shape
    x = x_ref[...].reshape(bb * tq, d)
    h = _rms(x, norm_in_ref[...])
    z = jnp.dot(h.astype(BF16), w_in_ref[...], preferred_element_type=F32)
    o = 0
    for dst in (u_ref, gs_ref):
        for j in range(ssm_w // LANES):
            dst[j] = z[:, o + LANES * j:o + LANES * (j + 1)].reshape(bb, tq, LANES)
        o += ssm_w
    c_q = z[:, o:o + Q_LORA]
    o += Q_LORA
    c_kv = z[:, o:o + KV_LORA]
    o += KV_LORA
    kr_raw = z[:, o:o + LANES]
    o += LANES
    gm_ref[...] = z[:, o:].reshape(gm_ref.shape)

    per_row = lambda t_ref: jnp.concatenate([t_ref[...]] * bb, axis=0)
    cos_t, sin_a, sin_b = per_row(cos_ref), per_row(sa_ref), per_row(sb_ref)
    even = lambda t: pltpu.roll(t, NOPE_DIM, 1)
    cos_e, sin_ae, sin_be = even(cos_t), even(sin_a), even(sin_b)
    seg = segq_ref[...]

    kr_ms = jnp.dot((kr_raw * kr_raw).astype(BF16), seg[LANES:, LANES:], preferred_element_type=F32)
    kr_blk = _rope_block(kr_raw * lax.rsqrt(kr_ms + EPS) * gkr_ref[...], cos_t, sin_a, sin_b)
    if transposed:
        kr_ref[0] = kr_blk.T[:ROPE_DIM]
    else:
        kr_ref[...] = kr_blk[:, :ROPE_DIM].reshape(kr_ref.shape)

    q = jnp.dot(_rms(c_q, qln_ref[...]).astype(BF16), w_uq_ref[...], preferred_element_type=F32)
    gq = gq_ref[...]
    for p in range(MLA_HEADS // 2):
        qn = _seg_rms(q[:, 2 * LANES * p:2 * LANES * (p + 1)], seg)
        blocks = (_rope_block(qn[:, :LANES] * gq[0:1], cos_e, sin_ae, sin_be),
                  _rope_block(qn[:, LANES:] * gq[1:2], cos_t, sin_a, sin_b))
        for j in range(2):
            hd = 2 * p + j
            q_ref[:, :, LANES * hd:LANES * (hd + 1)] = blocks[j].astype(BF16).reshape(bb, tq, LANES)

    ckv = _rms(c_kv, kvln_ref[...])
    ckv_ref[...] = ckv.reshape(ckv_ref.shape)
    _key_blocks(ckv.astype(BF16), kr_blk, w_ukc_ref, gkn_ref, segk_ref, k_ref)
    if transposed:
        v_t = jnp.dot(v_weights[0][...], ckv.T.astype(BF16), preferred_element_type=F32)
        ones_row = lax.broadcasted_iota(jnp.int32, v_t.shape, 0) % VT_ROWS == V_DIM
        v_t = jnp.where(ones_row, 1.0, v_t).astype(BF16)
        width = v_ref.shape[3]
        for t in range(v_ref.shape[1]):
            v_ref[0, t] = v_t[:, width * t:width * (t + 1)]
    else:
        v = jnp.dot(ckv.astype(BF16), v_weights[0][...], preferred_element_type=F32) + v_weights[1][...]
        v_ref[...] = v.astype(BF16).reshape(v_ref.shape)


def _full(shape):
    n = len(shape)
    return pl.BlockSpec(shape, lambda *_: (0,) * n)


def _proj_call(x, tables, wts, *, bb, tq, vt_tile):
    b, s, d = x.shape
    ssm_w = wts['ssm_w']
    cos_t, sin_a, sin_b = tables
    transposed = bb == 1
    v_weights = (wts['wvt'],) if transposed else (wts['w_uv'], wts['vones'])
    grid = (s // tq, b // bb)
    row = lambda i, j: (j, i, 0)
    tab = pl.BlockSpec((tq, LANES), lambda i, j: (i, 0))
    in_specs = [pl.BlockSpec((bb, tq, d), row), tab, tab, tab,
                _full(wts['norm_in'].shape), _full(wts['w_in'].shape), _full(wts['qln'].shape),
                _full(wts['kvln'].shape), _full(wts['w_uq'].shape), _full(wts['w_ukc'].shape),
                _full(wts['gq'].shape), _full(wts['gkr'].shape), _full(wts['gkn'].shape),
                _full(wts['segq'].shape), _full(wts['segk'].shape)] + [_full(w.shape) for w in v_weights]
    slabs = ssm_w // LANES
    tb_spec = pl.BlockSpec((slabs, bb, tq, LANES), lambda i, j: (0, j, i, 0))
    kr_shape, kr_spec = (((b, ROPE_DIM, s), pl.BlockSpec((1, ROPE_DIM, tq), lambda i, j: (j, 0, i)))
                         if transposed else
                         ((b, s, ROPE_DIM), pl.BlockSpec((bb, tq, ROPE_DIM), row)))
    v_shape, v_spec = (((b, s // vt_tile, MLA_HEADS * VT_ROWS, vt_tile),
                        pl.BlockSpec((1, tq // vt_tile, MLA_HEADS * VT_ROWS, vt_tile),
                                     lambda i, j: (j, i, 0, 0)))
                       if transposed else
                       ((b, s, HEAD_W), pl.BlockSpec((bb, tq, HEAD_W), row)))
    out_shape = [jax.ShapeDtypeStruct((slabs, b, s, LANES), F32),
                 jax.ShapeDtypeStruct((slabs, b, s, LANES), F32),
                 jax.ShapeDtypeStruct((b, s, ssm_w), F32),
                 jax.ShapeDtypeStruct((b, s, HEAD_W), BF16),
                 jax.ShapeDtypeStruct((b, s, HEAD_W), BF16),
                 jax.ShapeDtypeStruct(v_shape, BF16),
                 jax.ShapeDtypeStruct((b, s, KV_LORA), F32),
                 jax.ShapeDtypeStruct(kr_shape, F32)]
    out_specs = [tb_spec, tb_spec,
                 pl.BlockSpec((bb, tq, ssm_w), row),
                 pl.BlockSpec((bb, tq, HEAD_W), row), pl.BlockSpec((bb, tq, HEAD_W), row), v_spec,
                 pl.BlockSpec((bb, tq, KV_LORA), row), kr_spec]
    return pl.pallas_call(
        functools.partial(_proj_kernel, ssm_w=ssm_w, transposed=transposed),
        out_shape=out_shape, grid=grid, in_specs=in_specs, out_specs=out_specs,
        compiler_params=pltpu.CompilerParams(
            dimension_semantics=("arbitrary", "arbitrary"), vmem_limit_bytes=VMEM_LIMIT),
        name="proj",
    )(x, cos_t, sin_a, sin_b, wts['norm_in'], wts['w_in'], wts['qln'], wts['kvln'], wts['w_uq'],
      wts['w_ukc'], wts['gq'], wts['gkr'], wts['gkn'], wts['segq'], wts['segk'], *v_weights)


def _ssm_kernel(u_ref, gs_ref, h0re_ref, h0im_ref, are_ref, aim_ref, wb5_ref, wc5_ref, wf5_ref, d_ref,
                w_glu_ref, b_glu_ref, onorm_ref, mix_ref, hre_ref, him_ref, us_ref, xs_ref, y_ref,
                wb_ref, wc_ref, wf_ref, *, batch, steps, unroll):
    n_blocks, r_blk = wb_ref.shape[0], wb_ref.shape[1] // LANES
    half = wb_ref.shape[2] // 2
    gpb = wb5_ref.shape[2] // n_blocks
    n_k = steps // r_blk

    @pl.when(pl.program_id(0) == 0)
    def _():
        hre_ref[...] = h0re_ref[...]
        him_ref[...] = h0im_ref[...]

        def own_group(shape, row_div, lane_div):
            return (lax.broadcasted_iota(jnp.int32, shape, 0) // row_div
                    == lax.broadcasted_iota(jnp.int32, shape, 1) // lane_div)

        cin, n_state = wb5_ref.shape[3], half // gpb
        m_b = own_group((LANES, half), cin, n_state)
        m_c = own_group((half, LANES), n_state, cin)
        m_f = own_group((LANES, LANES), cin, cin)
        for gb in range(n_blocks):
            grp = slice(gpb * gb, gpb * (gb + 1))
            for ri in range(2):
                for i in range(r_blk):
                    piece = wb5_ref[ri, i, grp].reshape(LANES, LANES)
                    wide = jnp.concatenate([piece] * (half // LANES), axis=1)
                    wb_ref[gb, LANES * i:LANES * (i + 1), half * ri:half * (ri + 1)] = (
                        jnp.where(m_b, wide, 0.0).astype(BF16))
                    piece = wc5_ref[ri, i, grp].reshape(half, LANES)
                    wc_ref[gb, half * ri:half * (ri + 1), LANES * i:LANES * (i + 1)] = (
                        jnp.where(m_c, piece, 0.0).astype(BF16))
            for i in range(r_blk):
                for j in range(r_blk):
                    piece = wf5_ref[i, j, grp].reshape(LANES, LANES)
                    wf_ref[gb, LANES * i:LANES * (i + 1), LANES * j:LANES * (j + 1)] = (
                        jnp.where(m_f, piece, 0.0).astype(BF16))

    pitch = us_ref.shape[2] // n_k
    for b in range(batch):
        for j in range(n_blocks):
            for i in range(r_blk):
                us_ref[j, i, pl.ds(b, n_k, stride=pitch), :] = (
                    u_ref[j, b, pl.ds(i, n_k, stride=r_blk), :])

    def slab_rows(j, i):
        return jnp.concatenate([us_ref[j, i, pitch * k:pitch * k + batch, :] for k in range(n_k)], axis=0)

    u_slabs = [[slab_rows(j, i) for i in range(r_blk)] for j in range(n_blocks)]
    for gb in range(n_blocks):
        lhs = jnp.concatenate(u_slabs[gb], axis=-1).astype(BF16)
        xs = xs_ref
        xs[...] = jnp.dot(lhs, wb_ref[gb], preferred_element_type=F32)
        cols = slice(half * gb, half * (gb + 1))
        a_re = jnp.broadcast_to(are_ref[:, cols], (batch, half))
        a_im = jnp.broadcast_to(aim_ref[:, cols], (batch, half))

        def step(k, carry, xs=xs, a_re=a_re, a_im=a_im):
            h_re, h_im = carry
            rows = pl.ds(pl.multiple_of(k * batch, batch), batch)
            n_re = a_re * h_re - a_im * h_im + xs[rows, :half]
            n_im = a_re * h_im + a_im * h_re + xs[rows, half:]
            xs[rows, :half] = h_re
            xs[rows, half:] = h_im
            return n_re, n_im

        h_re, h_im = lax.fori_loop(0, n_k, step, (hre_ref[:, cols], him_ref[:, cols]), unroll=unroll)
        hre_ref[:, cols] = h_re
        him_ref[:, cols] = h_im
        y = (jnp.dot(xs[...].astype(BF16), wc_ref[gb], preferred_element_type=F32)
             + jnp.dot(lhs, wf_ref[gb], preferred_element_type=F32))
        for i in range(r_blk):
            y_ref[i, :, LANES * gb:LANES * (gb + 1)] = y[:, LANES * i:LANES * (i + 1)]

    rows = r_blk * n_k * batch
    u = jnp.concatenate([jnp.concatenate(u_slabs[j], axis=0) for j in range(n_blocks)], axis=-1)
    y = y_ref[...].reshape(rows, n_blocks * LANES) + d_ref[...] * u
    yg = jax.nn.gelu(y)
    glu = jnp.dot(yg.astype(BF16), w_glu_ref[...], preferred_element_type=F32) + b_glu_ref[...]
    out = _rms(yg * jax.nn.sigmoid(glu), onorm_ref[...])
    for j in range(n_blocks):
        for i in range(r_blk):
            for k in range(n_k):
                r0 = (i * n_k + k) * batch
                us_ref[j, i, pitch * k:pitch * k + batch, :] = out[r0:r0 + batch, LANES * j:LANES * (j + 1)]
    for b in range(batch):
        for j in range(n_blocks):
            for i in range(r_blk):
                tok = pl.ds(i, n_k, stride=r_blk)
                gs = gs_ref[j, b, tok, :]
                o = us_ref[j, i, pl.ds(b, n_k, stride=pitch), :]
                mix_ref[j, b, tok, :] = o * (gs * jax.nn.sigmoid(gs))


def _ssm_call(u, gs, h0re, h0im, wts, *, steps, unroll):
    n_blocks, batch, seq, _ = u.shape
    n_state = h0re.shape[1]
    r_blk = wts['wb5'].shape[1]
    width = 2 * n_state // n_blocks
    assert steps % r_blk == 0 and wts['wb5'].shape[3] * GROUPS_PER_BLOCK == LANES
    blk = pl.BlockSpec((n_blocks, batch, steps, LANES), lambda i: (0, 0, i, 0))
    names = ['a_re', 'a_im', 'wb5', 'wc5', 'wf5', 'ssm_d', 'w_glu', 'b_glu', 'onorm_ssm']
    n_rows = steps // r_blk * batch
    return pl.pallas_call(
        functools.partial(_ssm_kernel, batch=batch, steps=steps, unroll=unroll),
        out_shape=[jax.ShapeDtypeStruct(u.shape, F32),
                   jax.ShapeDtypeStruct((batch, n_state), F32),
                   jax.ShapeDtypeStruct((batch, n_state), F32)],
        grid=(seq // steps,),
        in_specs=[blk, blk, _full(h0re.shape), _full(h0im.shape)] + [_full(wts[n].shape) for n in names],
        out_specs=[blk, _full((batch, n_state)), _full((batch, n_state))],
        scratch_shapes=[pltpu.VMEM((n_blocks, r_blk, steps // r_blk * (batch + SUBLANES), LANES), F32),
                        pltpu.VMEM((n_rows, width), F32),
                        pltpu.VMEM((r_blk, n_rows, n_blocks * LANES), F32),
                        pltpu.VMEM((n_blocks, r_blk * LANES, width), BF16),
                        pltpu.VMEM((n_blocks, width, r_blk * LANES), BF16),
                        pltpu.VMEM((n_blocks, r_blk * LANES, r_blk * LANES), BF16)],
        compiler_params=pltpu.CompilerParams(
            dimension_semantics=("arbitrary",), vmem_limit_bytes=VMEM_LIMIT),
        name="ssm",
    )(u, gs, h0re, h0im, *[wts[n] for n in names])


def _attn_cached_kernel(q_ref, cf_ref, krf_ref, kl_ref, vl_ref, gm_ref, onorm_ref, w_uv_ref,
                        w_ukc_ref, gkn_ref, segk_ref, place_ref, y_ref,
                        kf_ref, s_full, ml_scr, s_last, m_scr, ctx_scr, den_scr, *, tq, tk):
    kr_blk = lax.dot_general(krf_ref[0].astype(BF16), place_ref[...], (((0,), (0,)), ((), ())),
                             preferred_element_type=F32)
    _key_blocks(cf_ref[0].astype(BF16), kr_blk, w_ukc_ref, gkn_ref, segk_ref, kf_ref)

    n_full = kf_ref.shape[1] // tk
    dn = (((1,), (1,)), ((), ()))
    lane = lax.broadcasted_iota(jnp.int32, (tq, LANES), 1)
    heads = range(MLA_HEADS)
    cols = [slice(LANES * hd, LANES * (hd + 1)) for hd in heads]

    def lane_tiles(s):
        return [s[:, LANES * c:LANES * (c + 1)] for c in range(s.shape[1] // LANES)]

    def probs(s, m_rep):
        if s.shape[1] % LANES == 0:
            p = jnp.concatenate([jnp.exp2(t - m_rep) for t in lane_tiles(s)], axis=-1)
        else:
            p = jnp.exp2(s - m_rep[:, :1])
        return p.astype(BF16)

    ml_scr[...] = jnp.full(ml_scr.shape, -jnp.inf, F32)

    def a_step(jt, carry):
        rows = pl.ds(pl.multiple_of(jt * tk, tk), tk)
        for hd in heads:
            s = lax.dot_general(q_ref[0, :, cols[hd]], kf_ref[0, rows, cols[hd]], dn,
                                preferred_element_type=F32)
            s_full[hd, jt] = s
            ml_scr[hd] = functools.reduce(jnp.maximum, lane_tiles(s), ml_scr[hd])
        return carry

    lax.fori_loop(0, n_full, a_step, 0)

    for hd in heads:
        s = lax.dot_general(q_ref[0, :, cols[hd]], kl_ref[0, :, cols[hd]], dn,
                            preferred_element_type=F32)
        s_last[hd] = s
        m = jnp.maximum(jnp.max(s, axis=-1, keepdims=True),
                        jnp.max(ml_scr[hd], axis=-1, keepdims=True))
        m_scr[hd] = jnp.broadcast_to(m, (tq, LANES))
    ctx_scr[...] = jnp.zeros(ctx_scr.shape, F32)
    den_scr[...] = jnp.zeros(den_scr.shape, F32)

    def b_step(jt, carry):
        rows = pl.ds(pl.multiple_of(jt * tk, tk), tk)
        p = jnp.concatenate([probs(s_full[hd, jt], m_scr[hd]) for hd in heads], axis=0)
        ctx_scr[...] += jnp.dot(p, cf_ref[0, rows, :].astype(BF16), preferred_element_type=F32)
        den_scr[...] += jnp.sum(p.astype(F32), axis=-1, keepdims=True)
        return carry

    lax.fori_loop(0, n_full, b_step, 0)

    outs = []
    for hd in heads:
        mine = slice(tq * hd, tq * (hd + 1))
        cached = jnp.dot(ctx_scr[mine].astype(BF16), w_uv_ref[:, cols[hd]], preferred_element_type=F32)
        p = probs(s_last[hd], m_scr[hd])
        new = jnp.dot(p, vl_ref[0, :, cols[hd]], preferred_element_type=F32)
        ones_col = V_DIM if hd % 2 == 0 else 0
        outs.append((cached + new) / (den_scr[mine] + new[:, ones_col:ones_col + 1]))
    attn = jnp.concatenate([jnp.where(lane < V_DIM, outs[e], outs[e + 1])
                            for e in range(0, MLA_HEADS, 2)], axis=-1)
    gm = gm_ref[0]
    y_ref[0] = (_rms(attn, onorm_ref[...]) * (gm * jax.nn.sigmoid(gm))).astype(BF16)


def _attn_causal_kernel(q_ref, k_ref, vt_ref, gm_ref, ms_ref, x_ref, onorm_ref, w_out_ref, y_ref,
                        s_full, ml_scr, s_last, m_scr, acc_scr, ot_scr, *, tq, tk):
    i = pl.program_id(1)
    n_full = (i * tq) // tk
    odd = i % 2 == 1
    nt = (((1,), (1,)), ((), ()))
    heads = range(MLA_HEADS)
    cols = [slice(LANES * hd, LANES * (hd + 1)) for hd in heads]
    vrows = [slice(VT_ROWS * hd, VT_ROWS * (hd + 1)) for hd in heads]

    def group_max(s):
        return jnp.max(s.reshape(s.shape[0] // SUBLANES, SUBLANES, s.shape[1]), axis=0)

    def chunk_mask(width):
        kc = lax.broadcasted_iota(jnp.int32, (width, tq), 0) // CHUNK
        qc = lax.broadcasted_iota(jnp.int32, (width, tq), 1) // CHUNK + (width - tq) // CHUNK
        return qc >= kc

    ml_scr[...] = jnp.full(ml_scr.shape, -jnp.inf, F32)

    def a_step(jt, carry):
        rows = pl.ds(pl.multiple_of(jt * tk, tk), tk)
        for hd in heads:
            s = lax.dot_general(k_ref[0, rows, cols[hd]], q_ref[0, :, cols[hd]], nt,
                                preferred_element_type=F32)
            s_full[hd, jt] = s
            ml_scr[hd] = jnp.maximum(ml_scr[hd], group_max(s))
        return carry

    lax.fori_loop(0, n_full, a_step, 0)

    def last_a(width, rows):
        mask = chunk_mask(width)
        for hd in heads:
            s = lax.dot_general(k_ref[0, rows, cols[hd]], q_ref[0, :, cols[hd]], nt,
                                preferred_element_type=F32)
            s = jnp.where(mask, s, -jnp.inf)
            s_last[hd, :width] = s
            m = jnp.max(jnp.maximum(ml_scr[hd], group_max(s)), axis=0, keepdims=True)
            m_scr[hd] = jnp.broadcast_to(m, m_scr.shape[1:])
            acc_scr[hd] = jnp.zeros(acc_scr.shape[1:], F32)

    pair_rows = pl.ds(pl.multiple_of(n_full * tk, tk), tk)
    pl.when(odd)(functools.partial(last_a, tk, pair_rows))
    pl.when(jnp.logical_not(odd))(functools.partial(last_a, tq, pl.ds(pl.multiple_of(i * tq, tq), tq)))

    def b_step(jt, carry):
        for hd in heads:
            p = jnp.exp2(s_full[hd, jt] - m_scr[hd, :1]).astype(BF16)
            acc_scr[hd] += jnp.dot(vt_ref[0, jt, vrows[hd], :], p, preferred_element_type=F32)
        return carry

    lax.fori_loop(0, n_full, b_step, 0)

    def last_b(width):
        for hd in heads:
            p = jnp.exp2(s_last[hd, :width] - m_scr[hd, :1]).astype(BF16)
            acc = acc_scr[hd] + jnp.dot(vt_ref[0, n_full, vrows[hd], :width], p,
                                        preferred_element_type=F32)
            ot_scr[V_DIM * hd:V_DIM * (hd + 1), :] = acc[:V_DIM] / acc[V_DIM:V_DIM + 1]

    pl.when(odd)(functools.partial(last_b, tk))
    pl.when(jnp.logical_not(odd))(functools.partial(last_b, tq))

    gm = gm_ref[0]
    mla = _rms(ot_scr[...].T, onorm_ref[...]) * (gm * jax.nn.sigmoid(gm))
    mix_ssm = [ms_ref[j, 0] for j in range(ms_ref.shape[0])]
    mix = jnp.concatenate(mix_ssm + [mla], axis=-1).astype(BF16)
    y_ref[0] = x_ref[0] + jnp.dot(mix, w_out_ref[...], preferred_element_type=F32)


def _attn_causal_call(q, k, vt, gm, mix_ssm, x, wts, *, tq):
    b, s, d = x.shape
    tk = vt.shape[3]
    ssm_w = gm.shape[2]
    assert tk == 2 * tq and s % tk == 0 and tq % CHUNK == 0
    row = lambda j, i: (j, i, 0)
    res = lambda j, i: (j, 0, 0)
    scratch = [pltpu.VMEM((MLA_HEADS, s // tk - 1, tk, tq), F32),
               pltpu.VMEM((MLA_HEADS, SUBLANES, tq), F32),
               pltpu.VMEM((MLA_HEADS, tk, tq), F32),
               pltpu.VMEM((MLA_HEADS, SUBLANES, tq), F32),
               pltpu.VMEM((MLA_HEADS, VT_ROWS, tq), F32),
               pltpu.VMEM((MLA_HEADS * V_DIM, tq), F32)]
    return pl.pallas_call(
        functools.partial(_attn_causal_kernel, tq=tq, tk=tk),
        out_shape=jax.ShapeDtypeStruct((b, s, d), F32),
        grid=(b, s // tq),
        in_specs=[pl.BlockSpec((1, tq, HEAD_W), row), pl.BlockSpec((1, s, HEAD_W), res),
                  pl.BlockSpec((1,) + vt.shape[1:], lambda j, i: (j, 0, 0, 0)),
                  pl.BlockSpec((1, tq, ssm_w), row),
                  pl.BlockSpec((mix_ssm.shape[0], 1, tq, LANES), lambda j, i: (0, j, i, 0)),
                  pl.BlockSpec((1, tq, d), row),
                  _full(wts['onorm_mla'].shape), _full(wts['w_out'].shape)],
        out_specs=pl.BlockSpec((1, tq, d), row),
        scratch_shapes=scratch,
        compiler_params=pltpu.CompilerParams(
            dimension_semantics=("arbitrary", "arbitrary"), vmem_limit_bytes=VMEM_LIMIT),
        name="attn",
    )(q, k, vt, gm, mix_ssm, x, wts['onorm_mla'], wts['w_out'])


def _out_kernel(ms_ref, mla_ref, x_ref, w_out_ref, y_ref):
    bb, tq, d = x_ref.shape
    rows = bb * tq
    mix = jnp.concatenate([ms_ref[j].reshape(rows, LANES).astype(BF16) for j in range(ms_ref.shape[0])]
                          + [mla_ref[...].reshape(rows, mla_ref.shape[2])], axis=-1)
    y = x_ref[...].reshape(rows, d) + jnp.dot(mix, w_out_ref[...], preferred_element_type=F32)
    y_ref[...] = y.reshape(bb, tq, d)


def _out_call(mix_ssm, mla, x, wts, *, bb):
    b, s, d = x.shape
    row = lambda j: (j, 0, 0)
    return pl.pallas_call(
        _out_kernel,
        out_shape=jax.ShapeDtypeStruct((b, s, d), F32),
        grid=(b // bb,),
        in_specs=[pl.BlockSpec((mix_ssm.shape[0], bb, s, LANES), lambda j: (0, j, 0, 0)),
                  pl.BlockSpec((bb, s, mla.shape[2]), row), pl.BlockSpec((bb, s, d), row),
                  _full(wts['w_out'].shape)],
        out_specs=pl.BlockSpec((bb, s, d), row),
        compiler_params=pltpu.CompilerParams(
            dimension_semantics=("arbitrary",), vmem_limit_bytes=VMEM_LIMIT),
        name="outproj",
    )(mix_ssm, mla, x, wts['w_out'])


def _attn_cached_call(q, ckv_full, kr_full_t, k_last, v_last, gm, wts, *, tk):
    b, tq, _ = q.shape
    t_full, last_len = ckv_full.shape[1], k_last.shape[1]
    ssm_w = gm.shape[2]
    blk = lambda rows, width: pl.BlockSpec((1, rows, width), lambda j: (j, 0, 0))
    consts = [wts[n] for n in ('onorm_mla', 'w_uv', 'w_ukc', 'gkn', 'segk', 'place')]
    scratch = [pltpu.VMEM((1, t_full, HEAD_W), BF16),
               pltpu.VMEM((MLA_HEADS, t_full // tk, tq, tk), F32), pltpu.VMEM((MLA_HEADS, tq, LANES), F32),
               pltpu.VMEM((MLA_HEADS, tq, last_len), F32), pltpu.VMEM((MLA_HEADS, tq, LANES), F32),
               pltpu.VMEM((MLA_HEADS * tq, ckv_full.shape[2]), F32), pltpu.VMEM((MLA_HEADS * tq, 1), F32)]
    return pl.pallas_call(
        functools.partial(_attn_cached_kernel, tq=tq, tk=tk),
        out_shape=jax.ShapeDtypeStruct((b, tq, ssm_w), BF16),
        grid=(b,),
        in_specs=[blk(tq, HEAD_W), blk(t_full, ckv_full.shape[2]), blk(ROPE_DIM, t_full),
                  blk(last_len, HEAD_W), blk(last_len, HEAD_W), blk(tq, ssm_w)]
                 + [_full(c.shape) for c in consts],
        out_specs=blk(tq, ssm_w),
        scratch_shapes=scratch,
        compiler_params=pltpu.CompilerParams(
            dimension_semantics=("arbitrary",), vmem_limit_bytes=VMEM_LIMIT),
        name="attn_cached",
    )(q, ckv_full, kr_full_t, k_last, v_last, gm, *consts)


def _head_block_cols(w, pieces):
    k = w.shape[0]
    w3 = w.reshape(k, MLA_HEADS, w.shape[1] // MLA_HEADS)
    cols = [w3[:, :, p[0]:p[0] + p[1]] if isinstance(p, tuple) else jnp.zeros((k, MLA_HEADS, p), w.dtype)
            for p in pieces]
    return jnp.concatenate(cols, axis=-1).reshape(k, HEAD_W)


def _zoh_powers(lr, li, dt, n):
    mag = jnp.exp(lr * dt)
    ar, ai = mag * jnp.cos(li * dt), mag * jnp.sin(li * dt)
    pw = [(jnp.ones_like(ar), jnp.zeros_like(ai))]
    for _ in range(n):
        pr, pi = pw[-1]
        pw.append((pr * ar - pi * ai, pr * ai + pi * ar))
    return pw


def _disc_kernel(arow_ref, acol_ref, ldt_ref, bt_ref, cd_ref, ct_ref,
                 apr_ref, api_ref, wb_ref, wc_ref, wf_ref):
    r_blk, groups, cin = wb_ref.shape[1], wb_ref.shape[2], wb_ref.shape[3]
    n_state = acol_ref.shape[2]
    gpb = LANES // cin
    lr, li = arow_ref[0], arow_ref[1]
    dt = jnp.exp(ldt_ref[...])
    pw = _zoh_powers(lr, li, dt, r_blk)
    ar, ai = pw[1]
    den = lr * lr + li * li
    kr = ((ar - 1.0) * lr + ai * li) / den
    ki = (ai * lr - (ar - 1.0) * li) / den
    apr_ref[...], api_ref[...] = pw[r_blk]

    pwc = _zoh_powers(acol_ref[0], acol_ref[1], dt, r_blk)
    for j in range(r_blk):
        pr, pi = pwc[j + 1]
        wc_ref[0, j] = ct_ref[0] * pr - ct_ref[1] * pi
        wc_ref[1, j] = -(ct_ref[0] * pi + ct_ref[1] * pr)

    b_r = kr * bt_ref[0] - ki * bt_ref[1]
    b_i = kr * bt_ref[1] + ki * bt_ref[0]
    for i in range(r_blk):
        pr, pi = pw[r_blk - 1 - i]
        wb_ref[0, i] = pr * b_r - pi * b_i
        wb_ref[1, i] = pr * b_i + pi * b_r

    nt = (((1,), (1,)), ((), ()))
    first_copy = lax.broadcasted_iota(jnp.int32, (LANES, LANES), 1) < n_state
    block = lambda v, n: v[gpb * n:gpb * (n + 1)].reshape(LANES, LANES)
    for i in range(r_blk):
        for j in range(i):
            wf_ref[i, j] = jnp.zeros(wf_ref.shape[2:], F32)
    for m in range(r_blk):
        pr, pi = pw[m]
        ca_r = cd_ref[0] * pr - cd_ref[1] * pi
        ca_i = cd_ref[0] * pi + cd_ref[1] * pr
        for n in range(groups // gpb):
            f_t = (lax.dot_general(jnp.where(first_copy, block(b_r, n), 0.0), block(ca_r, n), nt,
                                   precision=lax.Precision.HIGHEST, preferred_element_type=F32)
                   - lax.dot_general(jnp.where(first_copy, block(b_i, n), 0.0), block(ca_i, n), nt,
                                     precision=lax.Precision.HIGHEST, preferred_element_type=F32))
            for i in range(r_blk - m):
                wf_ref[i, i + m, gpb * n:gpb * (n + 1)] = f_t.reshape(gpb, cin, LANES)


def _prepare_weights(norm_in, w_in, ssm_a_re, ssm_a_im, ssm_log_dt, ssm_b_re, ssm_b_im, ssm_c_re,
                     ssm_c_im, ssm_d, w_glu, b_glu, q_lora_norm, kv_lora_norm, w_uq, w_ukv,
                     q_nope_norm, k_nope_norm, q_rope_norm, k_rope_norm, out_norm_ssm,
                     out_norm_mla, w_out):
    groups, n_state = ssm_a_re.shape
    ssm_w = groups * SSM_GROUP
    row = lambda v: v.reshape(1, -1).astype(F32)
    tail = LANES - ROPE_DIM - NOPE_DIM

    o_kr = 2 * ssm_w + Q_LORA + KV_LORA
    w_in_p = jnp.concatenate(
        [w_in[:, :o_kr + ROPE_DIM], jnp.zeros((w_in.shape[0], LANES - ROPE_DIM), w_in.dtype),
         w_in[:, o_kr + ROPE_DIM:]], axis=1).astype(BF16)

    odd_head = (np.arange(HEAD_W) // LANES) % 2 == 1
    by_parity = lambda even, odd: jnp.where(odd_head[None, :], odd, even)
    w_uq_p = by_parity(_head_block_cols(w_uq, [(0, NOPE_DIM), (NOPE_DIM, ROPE_DIM), tail]),
                       _head_block_cols(w_uq, [(NOPE_DIM, ROPE_DIM), tail, (0, NOPE_DIM)])).astype(BF16)
    w_ukc = w_ukv.reshape(KV_LORA, MLA_HEADS, NOPE_DIM + V_DIM)[:, :, :NOPE_DIM]
    w_ukc = w_ukc.reshape(KV_LORA, MLA_HEADS * NOPE_DIM).astype(BF16)
    w_v_lo = _head_block_cols(w_ukv, [(NOPE_DIM, V_DIM), LANES - V_DIM])
    w_v_hi = _head_block_cols(w_ukv, [LANES - V_DIM, (NOPE_DIM, V_DIM)])
    w_uv = by_parity(w_v_lo, w_v_hi).astype(BF16)
    wvt = w_v_lo.reshape(KV_LORA, MLA_HEADS, LANES)[:, :, :VT_ROWS]
    wvt = jnp.transpose(wvt, (1, 2, 0)).reshape(MLA_HEADS * VT_ROWS, KV_LORA).astype(BF16)

    scale = (NOPE_DIM + ROPE_DIM) ** -0.5 * np.log2(np.e)
    vones = np.zeros((MLA_HEADS, LANES), np.float32)
    vones[0::2, V_DIM] = 1.0
    vones[1::2, 0] = 1.0
    vones = jnp.asarray(vones.reshape(1, HEAD_W))
    zeros = lambda n: jnp.zeros((n,), F32)
    gq = jnp.stack([jnp.concatenate([q_nope_norm, q_rope_norm, zeros(tail)]),
                    jnp.concatenate([q_rope_norm, zeros(tail), q_nope_norm])]) * scale
    gkr = jnp.concatenate([k_rope_norm, zeros(LANES - ROPE_DIM)])
    gkn = jnp.tile(k_nope_norm, 2 * LANES // NOPE_DIM)

    def seg_means(widths):
        m = np.zeros((sum(widths), sum(widths)), np.float32)
        o = 0
        for w in widths:
            m[o:o + w, o:o + w] = 1.0 / w
            o += w
        return m

    segq = jnp.asarray(seg_means([NOPE_DIM, ROPE_DIM, tail, ROPE_DIM, tail, NOPE_DIM]), BF16)
    segk = jnp.asarray(seg_means([NOPE_DIM] * (2 * LANES // NOPE_DIM)), BF16)
    place = jnp.asarray(np.eye(ROPE_DIM, LANES, dtype=np.float32), BF16)

    r_blk = SSM_STEPS_PER_SCAN
    lane_rep = lambda x: jnp.tile(x, (1,) * (x.ndim - 1) + (LANES // x.shape[-1],))
    a2 = jnp.stack([ssm_a_re, ssm_a_im]).astype(F32)
    b2 = jnp.stack([ssm_b_re, ssm_b_im]).astype(F32)
    c2 = jnp.stack([ssm_c_re, ssm_c_im]).astype(F32)
    ldt = ssm_log_dt.astype(F32)
    gl = jax.ShapeDtypeStruct((groups, 1, LANES), F32)
    a_re, a_im, wb5, wc5, wf5 = pl.pallas_call(
        _disc_kernel,
        out_shape=[gl, gl,
                   jax.ShapeDtypeStruct((2, r_blk, groups, SSM_GROUP, LANES), F32),
                   jax.ShapeDtypeStruct((2, r_blk, groups, n_state, LANES), F32),
                   jax.ShapeDtypeStruct((r_blk, r_blk, groups, SSM_GROUP, LANES), F32)],
        name="disc")(
        lane_rep(a2)[:, :, None, :], jnp.broadcast_to(a2[..., None], a2.shape + (LANES,)),
        jnp.broadcast_to(ldt[:, None, None], (groups, 1, LANES)),
        lane_rep(jnp.swapaxes(b2, 2, 3)), lane_rep(c2), lane_rep(jnp.swapaxes(c2, 2, 3)))
    a_re, a_im = a_re[:, 0, :n_state], a_im[:, 0, :n_state]

    return dict(
        ssm_w=ssm_w,
        norm_in=row(norm_in), w_in=w_in_p, qln=row(q_lora_norm), kvln=row(kv_lora_norm),
        w_uq=w_uq_p, w_ukc=w_ukc, w_uv=w_uv, wvt=wvt,
        gq=gq.astype(F32), gkr=row(gkr), gkn=row(gkn), segq=segq, segk=segk, vones=vones,
        place=place, a_re=row(a_re), a_im=row(a_im), wb5=wb5, wc5=wc5, wf5=wf5,
        ssm_d=row(ssm_d), w_glu=w_glu.astype(BF16), b_glu=row(b_glu), onorm_ssm=row(out_norm_ssm),
        onorm_mla=row(out_norm_mla), w_out=w_out.astype(BF16))


def _rope_tables(start, n):
    half = ROPE_DIM // 2
    inv = ROPE_THETA ** (-jnp.arange(half, dtype=F32) / half)
    ang = (start + jnp.arange(n)).astype(F32)[:, None] * inv[None, :]
    cos, sin = jnp.cos(ang), jnp.sin(ang)
    cos_t = jnp.concatenate([cos, cos, jnp.ones((n, LANES - ROPE_DIM), F32)], axis=1)
    sin_a = jnp.concatenate([-sin, jnp.zeros((n, LANES - half), F32)], axis=1)
    sin_b = jnp.concatenate([jnp.zeros((n, half), F32), sin, jnp.zeros((n, LANES - ROPE_DIM), F32)],
                            axis=1)
    return cos_t, sin_a, sin_b


def _mixer(x, pos0, h0re, h0im, past, wts, *, proj_bb, proj_tq, ssm_steps, attn_tk):
    u, gs, gm, q, k, v, ckv, kr = _proj_call(x, _rope_tables(pos0, x.shape[1]), wts,
                                             bb=proj_bb, tq=proj_tq, vt_tile=min(attn_tk, proj_tq))
    mix_ssm, hre, him = _ssm_call(u, gs, h0re, h0im, wts, steps=ssm_steps, unroll=True)
    if past is None:
        y = _attn_causal_call(q, k, v, gm, mix_ssm, x, wts, tq=attn_tk // 2)
    else:
        mla = _attn_cached_call(q, *past, k, v, gm, wts, tk=attn_tk)
        y = _out_call(mix_ssm, mla, x, wts, bb=proj_bb)
    if proj_bb == 1:
        kr = jnp.swapaxes(kr, 1, 2)
    return y, ckv, kr, hre, him


def kernel(x_prompt, x_sample, cache_ckv, cache_krope, state_ssm_re, state_ssm_im, norm_in, w_in, ssm_a_re, ssm_a_im, ssm_log_dt, ssm_b_re, ssm_b_im, ssm_c_re, ssm_c_im, ssm_d, w_glu, b_glu, q_lora_norm, kv_lora_norm, w_uq, w_ukv, q_nope_norm, k_nope_norm, q_rope_norm, k_rope_norm, out_norm_ssm, out_norm_mla, w_out):
    depth = norm_in.shape[0]
    assert depth == 1, "single mixer layer"
    params = (norm_in, w_in, ssm_a_re, ssm_a_im, ssm_log_dt, ssm_b_re, ssm_b_im, ssm_c_re, ssm_c_im,
              ssm_d, w_glu, b_glu, q_lora_norm, kv_lora_norm, w_uq, w_ukv, q_nope_norm, k_nope_norm,
              q_rope_norm, k_rope_norm, out_norm_ssm, out_norm_mla, w_out)
    drop_depth = lambda a: a.reshape(a.shape[1:])
    wts = _prepare_weights(*[drop_depth(p) for p in params])
    groups, n_state = ssm_a_re.shape[1:]
    bp = x_prompt.shape[0]
    bs, ss, _ = x_sample.shape
    past_len = cache_ckv.shape[2]

    zero_state = jnp.zeros((bp, groups * n_state), F32)
    yp, ckv_p, kr_p, re_p, im_p = _mixer(
        x_prompt, 0, zero_state, zero_state, None, wts,
        proj_bb=1, proj_tq=1024, ssm_steps=64, attn_tk=512)
    ys, ckv_s, kr_s, re_s, im_s = _mixer(
        x_sample, past_len,
        state_ssm_re.reshape(bs, groups * n_state), state_ssm_im.reshape(bs, groups * n_state),
        (drop_depth(cache_ckv), jnp.swapaxes(drop_depth(cache_krope), 1, 2)), wts,
        proj_bb=bs // 2, proj_tq=ss, ssm_steps=ss, attn_tk=past_len)

    st = lambda h, bb: h.reshape(1, bb, groups, n_state)
    return (yp, ys, ckv_p[None], kr_p[None], st(re_p, bp), st(im_p, bp),
            ckv_s[None], kr_s[None], st(re_s, bs), st(im_s, bs))
```

```python
import functools

import numpy as np
import jax
import jax.numpy as jnp
from jax import lax
from jax.experimental import pallas as pl
from jax.experimental.pallas import tpu as pltpu

F32 = jnp.float32
BF16 = jnp.bfloat16

CHUNK = 64
SSM_GROUP = 16
MLA_HEADS = 8
NOPE_DIM = 64
ROPE_DIM = 32
V_DIM = 64
Q_LORA = 256
KV_LORA = 128
ROPE_THETA = 10000.0
EPS = 1e-6

LANES = 128
SUBLANES = 8
VT_ROWS = 80
HEAD_W = MLA_HEADS * LANES
GROUPS_PER_BLOCK = 8
SSM_STEPS_PER_SCAN = 4
VMEM_LIMIT = 56 * 1024 * 1024


def _rms(x, gain):
    return x * lax.rsqrt(jnp.mean(x * x, axis=-1, keepdims=True) + EPS) * gain


def _seg_rms(x, seg):
    ms = jnp.dot((x * x).astype(BF16), seg, preferred_element_type=F32)
    return x * lax.rsqrt(ms + EPS)


def _rope_block(x, cos_t, sin_a, sin_b):
    return (x * cos_t + pltpu.roll(x, LANES - ROPE_DIM // 2, 1) * sin_a
            + pltpu.roll(x, ROPE_DIM // 2, 1) * sin_b)


def _key_blocks(ckv_b, kr_blk, w_ukc_ref, gkn_ref, segk_ref, k_ref):
    kc = jnp.dot(ckv_b, w_ukc_ref[...], preferred_element_type=F32)
    kr_even = pltpu.roll(kr_blk, NOPE_DIM, 1)
    low = lax.broadcasted_iota(jnp.int32, (1, LANES), 1) < NOPE_DIM
    blk3 = k_ref.shape[:2] + (LANES,)
    for pp in range(kc.shape[1] // (2 * LANES)):
        kn = _seg_rms(kc[:, 2 * LANES * pp:2 * LANES * (pp + 1)], segk_ref[...]) * gkn_ref[...]
        for t in range(2):
            pair = kn[:, LANES * t:LANES * (t + 1)]
            he = 2 * (2 * pp + t)
            k_ref[:, :, LANES * he:LANES * (he + 1)] = (
                (jnp.where(low, pair, 0.0) + kr_even).astype(BF16).reshape(blk3))
            k_ref[:, :, LANES * (he + 1):LANES * (he + 2)] = (
                (jnp.where(low, 0.0, pair) + kr_blk).astype(BF16).reshape(blk3))


def _proj_kernel(x_ref, cos_ref, sa_ref, sb_ref, norm_in_ref, w_in_ref, qln_ref, kvln_ref,
                 w_uq_ref, w_ukc_ref, gq_ref, gkr_ref, gkn_ref, segq_ref, segk_ref, *rest,
                 ssm_w, transposed):
    v_weights, outs = (rest[:1], rest[1:]) if transposed else (rest[:2], rest[2:])
    u_ref, gs_ref, gm_ref, q_ref, k_ref, v_ref, ckv_ref, kr_ref = outs
    bb, tq, d = x_ref.shape
    x = x_ref[...].reshape(bb * tq, d)
    h = _rms(x, norm_in_ref[...])
    z = jnp.dot(h.astype(BF16), w_in_ref[...], preferred_element_type=F32)
    o = 0
    for dst in (u_ref, gs_ref):
        for j in range(ssm_w // LANES):
            dst[j] = z[:, o + LANES * j:o + LANES * (j + 1)].reshape(bb, tq, LANES)
        o += ssm_w
    c_q = z[:, o:o + Q_LORA]
    o += Q_LORA
    c_kv = z[:, o:o + KV_LORA]
    o += KV_LORA
    kr_raw = z[:, o:o + LANES]
    o += LANES
    gm_ref[...] = z[:, o:].reshape(gm_ref.shape)

    per_row = lambda t_ref: jnp.concatenate([t_ref[...]] * bb, axis=0)
    cos_t, sin_a, sin_b = per_row(cos_ref), per_row(sa_ref), per_row(sb_ref)
    even = lambda t: pltpu.roll(t, NOPE_DIM, 1)
    cos_e, sin_ae, sin_be = even(cos_t), even(sin_a), even(sin_b)
    seg = segq_ref[...]

    kr_ms = jnp.dot((kr_raw * kr_raw).astype(BF16), seg[LANES:, LANES:], preferred_element_type=F32)
    kr_blk = _rope_block(kr_raw * lax.rsqrt(kr_ms + EPS) * gkr_ref[...], cos_t, sin_a, sin_b)
    if transposed:
        kr_ref[0] = kr_blk.T[:ROPE_DIM]
    else:
        kr_ref[...] = kr_blk[:, :ROPE_DIM].reshape(kr_ref.shape)

    q = jnp.dot(_rms(c_q, qln_ref[...]).astype(BF16), w_uq_ref[...], preferred_element_type=F32)
    gq = gq_ref[...]
    for p in range(MLA_HEADS // 2):
        qn = _seg_rms(q[:, 2 * LANES * p:2 * LANES * (p + 1)], seg)
        blocks = (_rope_block(qn[:, :LANES] * gq[0:1], cos_e, sin_ae, sin_be),
                  _rope_block(qn[:, LANES:] * gq[1:2], cos_t, sin_a, sin_b))
        for j in range(2):
            hd = 2 * p + j
            q_ref[:, :, LANES * hd:LANES * (hd + 1)] = blocks[j].astype(BF16).reshape(bb, tq, LANES)

    ckv = _rms(c_kv, kvln_ref[...])
    ckv_ref[...] = ckv.reshape(ckv_ref.shape)
    _key_blocks(ckv.astype(BF16), kr_blk, w_ukc_ref, gkn_ref, segk_ref, k_ref)
    if transposed:
        v_t = jnp.dot(v_weights[0][...], ckv.T.astype(BF16), preferred_element_type=F32)
        ones_row = lax.broadcasted_iota(jnp.int32, v_t.shape, 0) % VT_ROWS == V_DIM
        v_t = jnp.where(ones_row, 1.0, v_t).astype(BF16)
        width = v_ref.shape[3]
        for t in range(v_ref.shape[1]):
            v_ref[0, t] = v_t[:, width * t:width * (t + 1)]
    else:
        v = jnp.dot(ckv.astype(BF16), v_weights[0][...], preferred_element_type=F32) + v_weights[1][...]
        v_ref[...] = v.astype(BF16).reshape(v_ref.shape)


def _full(shape):
    n = len(shape)
    return pl.BlockSpec(shape, lambda *_: (0,) * n)


def _proj_call(x, tables, wts, *, bb, tq, vt_tile):
    b, s, d = x.shape
    ssm_w = wts['ssm_w']
    cos_t, sin_a, sin_b = tables
    transposed = bb == 1
    v_weights = (wts['wvt'],) if transposed else (wts['w_uv'], wts['vones'])
    grid = (s // tq, b // bb)
    row = lambda i, j: (j, i, 0)
    tab = pl.BlockSpec((tq, LANES), lambda i, j: (i, 0))
    in_specs = [pl.BlockSpec((bb, tq, d), row), tab, tab, tab,
                _full(wts['norm_in'].shape), _full(wts['w_in'].shape), _full(wts['qln'].shape),
                _full(wts['kvln'].shape), _full(wts['w_uq'].shape), _full(wts['w_ukc'].shape),
                _full(wts['gq'].shape), _full(wts['gkr'].shape), _full(wts['gkn'].shape),
                _full(wts['segq'].shape), _full(wts['segk'].shape)] + [_full(w.shape) for w in v_weights]
    slabs = ssm_w // LANES
    tb_spec = pl.BlockSpec((slabs, bb, tq, LANES), lambda i, j: (0, j, i, 0))
    kr_shape, kr_spec = (((b, ROPE_DIM, s), pl.BlockSpec((1, ROPE_DIM, tq), lambda i, j: (j, 0, i)))
                         if transposed else
                         ((b, s, ROPE_DIM), pl.BlockSpec((bb, tq, ROPE_DIM), row)))
    v_shape, v_spec = (((b, s // vt_tile, MLA_HEADS * VT_ROWS, vt_tile),
                        pl.BlockSpec((1, tq // vt_tile, MLA_HEADS * VT_ROWS, vt_tile),
                                     lambda i, j: (j, i, 0, 0)))
                       if transposed else
                       ((b, s, HEAD_W), pl.BlockSpec((bb, tq, HEAD_W), row)))
    out_shape = [jax.ShapeDtypeStruct((slabs, b, s, LANES), F32),
                 jax.ShapeDtypeStruct((slabs, b, s, LANES), F32),
                 jax.ShapeDtypeStruct((b, s, ssm_w), F32),
                 jax.ShapeDtypeStruct((b, s, HEAD_W), BF16),
                 jax.ShapeDtypeStruct((b, s, HEAD_W), BF16),
                 jax.ShapeDtypeStruct(v_shape, BF16),
                 jax.ShapeDtypeStruct((b, s, KV_LORA), F32),
                 jax.ShapeDtypeStruct(kr_shape, F32)]
    out_specs = [tb_spec, tb_spec,
                 pl.BlockSpec((bb, tq, ssm_w), row),
                 pl.BlockSpec((bb, tq, HEAD_W), row), pl.BlockSpec((bb, tq, HEAD_W), row), v_spec,
                 pl.BlockSpec((bb, tq, KV_LORA), row), kr_spec]
    return pl.pallas_call(
        functools.partial(_proj_kernel, ssm_w=ssm_w, transposed=transposed),
        out_shape=out_shape, grid=grid, in_specs=in_specs, out_specs=out_specs,
        compiler_params=pltpu.CompilerParams(
            dimension_semantics=("arbitrary", "arbitrary"), vmem_limit_bytes=VMEM_LIMIT),
        name="proj",
    )(x, cos_t, sin_a, sin_b, wts['norm_in'], wts['w_in'], wts['qln'], wts['kvln'], wts['w_uq'],
      wts['w_ukc'], wts['gq'], wts['gkr'], wts['gkn'], wts['segq'], wts['segk'], *v_weights)


def _ssm_kernel(u_ref, gs_ref, h0re_ref, h0im_ref, are_ref, aim_ref, wb5_ref, wc5_ref, wf5_ref, d_ref,
                w_glu_ref, b_glu_ref, onorm_ref, mix_ref, hre_ref, him_ref, us_ref, xs_ref, y_ref,
                wb_ref, wc_ref, wf_ref, *, batch, steps, unroll):
    n_blocks, r_blk = wb_ref.shape[0], wb_ref.shape[1] // LANES
    half = wb_ref.shape[2] // 2
    gpb = wb5_ref.shape[2] // n_blocks
    n_k = steps // r_blk

    @pl.when(pl.program_id(0) == 0)
    def _():
        hre_ref[...] = h0re_ref[...]
        him_ref[...] = h0im_ref[...]

        def own_group(shape, row_div, lane_div):
            return (lax.broadcasted_iota(jnp.int32, shape, 0) // row_div
                    == lax.broadcasted_iota(jnp.int32, shape, 1) // lane_div)

        cin, n_state = wb5_ref.shape[3], half // gpb
        m_b = own_group((LANES, half), cin, n_state)
        m_c = own_group((half, LANES), n_state, cin)
        m_f = own_group((LANES, LANES), cin, cin)
        for gb in range(n_blocks):
            grp = slice(gpb * gb, gpb * (gb + 1))
            for ri in range(2):
                for i in range(r_blk):
                    piece = wb5_ref[ri, i, grp].reshape(LANES, LANES)
                    wide = jnp.concatenate([piece] * (half // LANES), axis=1)
                    wb_ref[gb, LANES * i:LANES * (i + 1), half * ri:half * (ri + 1)] = (
                        jnp.where(m_b, wide, 0.0).astype(BF16))
                    piece = wc5_ref[ri, i, grp].reshape(half, LANES)
                    wc_ref[gb, half * ri:half * (ri + 1), LANES * i:LANES * (i + 1)] = (
                        jnp.where(m_c, piece, 0.0).astype(BF16))
            for i in range(r_blk):
                for j in range(r_blk):
                    piece = wf5_ref[i, j, grp].reshape(LANES, LANES)
                    wf_ref[gb, LANES * i:LANES * (i + 1), LANES * j:LANES * (j + 1)] = (
                        jnp.where(m_f, piece, 0.0).astype(BF16))

    pitch = us_ref.shape[2] // n_k
    for b in range(batch):
        for j in range(n_blocks):
            for i in range(r_blk):
                us_ref[j, i, pl.ds(b, n_k, stride=pitch), :] = (
                    u_ref[j, b, pl.ds(i, n_k, stride=r_blk), :])

    def slab_rows(j, i):
        return jnp.concatenate([us_ref[j, i, pitch * k:pitch * k + batch, :] for k in range(n_k)], axis=0)

    u_slabs = [[slab_rows(j, i) for i in range(r_blk)] for j in range(n_blocks)]
    for gb in range(n_blocks):
        lhs = jnp.concatenate(u_slabs[gb], axis=-1).astype(BF16)
        xs = xs_ref
        xs[...] = jnp.dot(lhs, wb_ref[gb], preferred_element_type=F32)
        cols = slice(half * gb, half * (gb + 1))
        a_re = jnp.broadcast_to(are_ref[:, cols], (batch, half))
        a_im = jnp.broadcast_to(aim_ref[:, cols], (batch, half))

        def step(k, carry, xs=xs, a_re=a_re, a_im=a_im):
            h_re, h_im = carry
            rows = pl.ds(pl.multiple_of(k * batch, batch), batch)
            n_re = a_re * h_re - a_im * h_im + xs[rows, :half]
            n_im = a_re * h_im + a_im * h_re + xs[rows, half:]
            xs[rows, :half] = h_re
            xs[rows, half:] = h_im
            return n_re, n_im

        h_re, h_im = lax.fori_loop(0, n_k, step, (hre_ref[:, cols], him_ref[:, cols]), unroll=unroll)
        hre_ref[:, cols] = h_re
        him_ref[:, cols] = h_im
        y = (jnp.dot(xs[...].astype(BF16), wc_ref[gb], preferred_element_type=F32)
             + jnp.dot(lhs, wf_ref[gb], preferred_element_type=F32))
        for i in range(r_blk):
            y_ref[i, :, LANES * gb:LANES * (gb + 1)] = y[:, LANES * i:LANES * (i + 1)]

    rows = r_blk * n_k * batch
    u = jnp.concatenate([jnp.concatenate(u_slabs[j], axis=0) for j in range(n_blocks)], axis=-1)
    y = y_ref[...].reshape(rows, n_blocks * LANES) + d_ref[...] * u
    yg = jax.nn.gelu(y)
    glu = jnp.dot(yg.astype(BF16), w_glu_ref[...], preferred_element_type=F32) + b_glu_ref[...]
    out = _rms(yg * jax.nn.sigmoid(glu), onorm_ref[...])
    for j in range(n_blocks):
        for i in range(r_blk):
            for k in range(n_k):
                r0 = (i * n_k + k) * batch
                us_ref[j, i, pitch * k:pitch * k + batch, :] = out[r0:r0 + batch, LANES * j:LANES * (j + 1)]
    for b in range(batch):
        for j in range(n_blocks):
            for i in range(r_blk):
                tok = pl.ds(i, n_k, stride=r_blk)
                gs = gs_ref[j, b, tok, :]
                o = us_ref[j, i, pl.ds(b, n_k, stride=pitch), :]
                mix_ref[j, b, tok, :] = o * (gs * jax.nn.sigmoid(gs))


def _ssm_call(u, gs, h0re, h0im, wts, *, steps, unroll):
    n_blocks, batch, seq, _ = u.shape
    n_state = h0re.shape[1]
    r_blk = wts['wb5'].shape[1]
    width = 2 * n_state // n_blocks
    assert steps % r_blk == 0 and wts['wb5'].shape[3] * GROUPS_PER_BLOCK == LANES
    blk = pl.BlockSpec((n_blocks, batch, steps, LANES), lambda i: (0, 0, i, 0))
    names = ['a_re', 'a_im', 'wb5', 'wc5', 'wf5', 'ssm_d', 'w_glu', 'b_glu', 'onorm_ssm']
    n_rows = steps // r_blk * batch
    return pl.pallas_call(
        functools.partial(_ssm_kernel, batch=batch, steps=steps, unroll=unroll),
        out_shape=[jax.ShapeDtypeStruct(u.shape, F32),
                   jax.ShapeDtypeStruct((batch, n_state), F32),
                   jax.ShapeDtypeStruct((batch, n_state), F32)],
        grid=(seq // steps,),
        in_specs=[blk, blk, _full(h0re.shape), _full(h0im.shape)] + [_full(wts[n].shape) for n in names],
        out_specs=[blk, _full((batch, n_state)), _full((batch, n_state))],
        scratch_shapes=[pltpu.VMEM((n_blocks, r_blk, steps // r_blk * (batch + SUBLANES), LANES), F32),
                        pltpu.VMEM((n_rows, width), F32),
                        pltpu.VMEM((r_blk, n_rows, n_blocks * LANES), F32),
                        pltpu.VMEM((n_blocks, r_blk * LANES, width), BF16),
                        pltpu.VMEM((n_blocks, width, r_blk * LANES), BF16),
                        pltpu.VMEM((n_blocks, r_blk * LANES, r_blk * LANES), BF16)],
        compiler_params=pltpu.CompilerParams(
            dimension_semantics=("arbitrary",), vmem_limit_bytes=VMEM_LIMIT),
        name="ssm",
    )(u, gs, h0re, h0im, *[wts[n] for n in names])


def _attn_cached_kernel(q_ref, cf_ref, krf_ref, kl_ref, vl_ref, gm_ref, onorm_ref, w_uv_ref,
                        w_ukc_ref, gkn_ref, segk_ref, place_ref, y_ref,
                        kf_ref, s_full, ml_scr, s_last, m_scr, ctx_scr, den_scr, *, tq, tk):
    kr_blk = lax.dot_general(krf_ref[0].astype(BF16), place_ref[...], (((0,), (0,)), ((), ())),
                             preferred_element_type=F32)
    _key_blocks(cf_ref[0].astype(BF16), kr_blk, w_ukc_ref, gkn_ref, segk_ref, kf_ref)

    n_full = kf_ref.shape[1] // tk
    dn = (((1,), (1,)), ((), ()))
    lane = lax.broadcasted_iota(jnp.int32, (tq, LANES), 1)
    heads = range(MLA_HEADS)
    cols = [slice(LANES * hd, LANES * (hd + 1)) for hd in heads]

    def lane_tiles(s):
        return [s[:, LANES * c:LANES * (c + 1)] for c in range(s.shape[1] // LANES)]

    def probs(s, m_rep):
        if s.shape[1] % LANES == 0:
            p = jnp.concatenate([jnp.exp2(t - m_rep) for t in lane_tiles(s)], axis=-1)
        else:
            p = jnp.exp2(s - m_rep[:, :1])
        return p.astype(BF16)

    ml_scr[...] = jnp.full(ml_scr.shape, -jnp.inf, F32)

    def a_step(jt, carry):
        rows = pl.ds(pl.multiple_of(jt * tk, tk), tk)
        for hd in heads:
            s = lax.dot_general(q_ref[0, :, cols[hd]], kf_ref[0, rows, cols[hd]], dn,
                                preferred_element_type=F32)
            s_full[hd, jt] = s
            ml_scr[hd] = functools.reduce(jnp.maximum, lane_tiles(s), ml_scr[hd])
        return carry

    lax.fori_loop(0, n_full, a_step, 0)

    for hd in heads:
        s = lax.dot_general(q_ref[0, :, cols[hd]], kl_ref[0, :, cols[hd]], dn,
                            preferred_element_type=F32)
        s_last[hd] = s
        m = jnp.maximum(jnp.max(s, axis=-1, keepdims=True),
                        jnp.max(ml_scr[hd], axis=-1, keepdims=True))
        m_scr[hd] = jnp.broadcast_to(m, (tq, LANES))
    ctx_scr[...] = jnp.zeros(ctx_scr.shape, F32)
    den_scr[...] = jnp.zeros(den_scr.shape, F32)

    def b_step(jt, carry):
        rows = pl.ds(pl.multiple_of(jt * tk, tk), tk)
        p = jnp.concatenate([probs(s_full[hd, jt], m_scr[hd]) for hd in heads], axis=0)
        ctx_scr[...] += jnp.dot(p, cf_ref[0, rows, :].astype(BF16), preferred_element_type=F32)
        den_scr[...] += jnp.sum(p.astype(F32), axis=-1, keepdims=True)
        return carry

    lax.fori_loop(0, n_full, b_step, 0)

    outs = []
    for hd in heads:
        mine = slice(tq * hd, tq * (hd + 1))
        cached = jnp.dot(ctx_scr[mine].astype(BF16), w_uv_ref[:, cols[hd]], preferred_element_type=F32)
        p = probs(s_last[hd], m_scr[hd])
        new = jnp.dot(p, vl_ref[0, :, cols[hd]], preferred_element_type=F32)
        ones_col = V_DIM if hd % 2 == 0 else 0
        outs.append((cached + new) / (den_scr[mine] + new[:, ones_col:ones_col + 1]))
    attn = jnp.concatenate([jnp.where(lane < V_DIM, outs[e], outs[e + 1])
                            for e in range(0, MLA_HEADS, 2)], axis=-1)
    gm = gm_ref[0]
    y_ref[0] = (_rms(attn, onorm_ref[...]) * (gm * jax.nn.sigmoid(gm))).astype(BF16)


def _attn_causal_kernel(q_ref, k_ref, vt_ref, gm_ref, ms_ref, x_ref, onorm_ref, w_out_ref, y_ref,
                        s_full, ml_scr, s_last, m_scr, acc_scr, ot_scr, *, tq, tk):
    n_full = pl.program_id(1)
    nt = (((1,), (1,)), ((), ()))
    heads = range(MLA_HEADS)
    cols = [slice(LANES * hd, LANES * (hd + 1)) for hd in heads]
    vrows = [slice(VT_ROWS * hd, VT_ROWS * (hd + 1)) for hd in heads]

    def group_max(s):
        return jnp.max(s.reshape(s.shape[0] // SUBLANES, SUBLANES, s.shape[1]), axis=0)

    def chunk_mask(width):
        kc = lax.broadcasted_iota(jnp.int32, (width, tq), 0) // CHUNK
        qc = lax.broadcasted_iota(jnp.int32, (width, tq), 1) // CHUNK + (width - tq) // CHUNK
        return qc >= kc

    for t in range(tk // tq):
        mine = slice(tq * t, tq * (t + 1))
        width = tq * (t + 1)
        own_keys = pl.ds(pl.multiple_of(n_full * tk, tk), width)
        ml_scr[...] = jnp.full(ml_scr.shape, -jnp.inf, F32)

        def a_step(jt, carry, mine=mine):
            rows = pl.ds(pl.multiple_of(jt * tk, tk), tk)
            for hd in heads:
                s = lax.dot_general(k_ref[0, rows, cols[hd]], q_ref[0, mine, cols[hd]], nt,
                                    preferred_element_type=F32)
                s_full[hd, jt] = s
                ml_scr[hd] = jnp.maximum(ml_scr[hd], group_max(s))
            return carry

        lax.fori_loop(0, n_full, a_step, 0)

        mask = chunk_mask(width)
        for hd in heads:
            s = lax.dot_general(k_ref[0, own_keys, cols[hd]], q_ref[0, mine, cols[hd]], nt,
                                preferred_element_type=F32)
            s = jnp.where(mask, s, -jnp.inf)
            s_last[hd, :width] = s
            m = jnp.max(jnp.maximum(ml_scr[hd], group_max(s)), axis=0, keepdims=True)
            m_scr[hd] = jnp.broadcast_to(m, m_scr.shape[1:])
            acc_scr[hd] = jnp.zeros(acc_scr.shape[1:], F32)

        def b_step(jt, carry):
            for hd in heads:
                p = jnp.exp2(s_full[hd, jt] - m_scr[hd, :1]).astype(BF16)
                acc_scr[hd] += jnp.dot(vt_ref[0, jt, vrows[hd], :], p, preferred_element_type=F32)
            return carry

        lax.fori_loop(0, n_full, b_step, 0)

        for hd in heads:
            p = jnp.exp2(s_last[hd, :width] - m_scr[hd, :1]).astype(BF16)
            acc = acc_scr[hd] + jnp.dot(vt_ref[0, n_full, vrows[hd], :width], p,
                                        preferred_element_type=F32)
            ot_scr[V_DIM * hd:V_DIM * (hd + 1), :] = acc[:V_DIM] / acc[V_DIM:V_DIM + 1]

        gm = gm_ref[0, mine]
        mla = _rms(ot_scr[...].T, onorm_ref[...]) * (gm * jax.nn.sigmoid(gm))
        mix_ssm = [ms_ref[j, 0, mine] for j in range(ms_ref.shape[0])]
        mix = jnp.concatenate(mix_ssm + [mla], axis=-1).astype(BF16)
        y_ref[0, mine] = x_ref[0, mine] + jnp.dot(mix, w_out_ref[...], preferred_element_type=F32)


def _attn_causal_call(q, k, vt, gm, mix_ssm, x, wts, *, tq):
    b, s, d = x.shape
    tk = vt.shape[3]
    ssm_w = gm.shape[2]
    assert tk % tq == 0 and s % tk == 0 and tq % CHUNK == 0
    row = lambda j, i: (j, i, 0)
    res = lambda j, i: (j, 0, 0)
    scratch = [pltpu.VMEM((MLA_HEADS, s // tk - 1, tk, tq), F32),
               pltpu.VMEM((MLA_HEADS, SUBLANES, tq), F32),
               pltpu.VMEM((MLA_HEADS, tk, tq), F32),
               pltpu.VMEM((MLA_HEADS, SUBLANES, tq), F32),
               pltpu.VMEM((MLA_HEADS, VT_ROWS, tq), F32),
               pltpu.VMEM((MLA_HEADS * V_DIM, tq), F32)]
    return pl.pallas_call(
        functools.partial(_attn_causal_kernel, tq=tq, tk=tk),
        out_shape=jax.ShapeDtypeStruct((b, s, d), F32),
        grid=(b, s // tk),
        in_specs=[pl.BlockSpec((1, tk, HEAD_W), row), pl.BlockSpec((1, s, HEAD_W), res),
                  pl.BlockSpec((1,) + vt.shape[1:], lambda j, i: (j, 0, 0, 0)),
                  pl.BlockSpec((1, tk, ssm_w), row),
                  pl.BlockSpec((mix_ssm.shape[0], 1, tk, LANES), lambda j, i: (0, j, i, 0)),
                  pl.BlockSpec((1, tk, d), row),
                  _full(wts['onorm_mla'].shape), _full(wts['w_out'].shape)],
        out_specs=pl.BlockSpec((1, tk, d), row),
        scratch_shapes=scratch,
        compiler_params=pltpu.CompilerParams(
            dimension_semantics=("arbitrary", "arbitrary"), vmem_limit_bytes=VMEM_LIMIT),
        name="attn",
    )(q, k, vt, gm, mix_ssm, x, wts['onorm_mla'], wts['w_out'])


def _out_kernel(ms_ref, mla_ref, x_ref, w_out_ref, y_ref):
    bb, tq, d = x_ref.shape
    rows = bb * tq
    mix = jnp.concatenate([ms_ref[j].reshape(rows, LANES).astype(BF16) for j in range(ms_ref.shape[0])]
                          + [mla_ref[...].reshape(rows, mla_ref.shape[2])], axis=-1)
    y = x_ref[...].reshape(rows, d) + jnp.dot(mix, w_out_ref[...], preferred_element_type=F32)
    y_ref[...] = y.reshape(bb, tq, d)


def _out_call(mix_ssm, mla, x, wts, *, bb):
    b, s, d = x.shape
    row = lambda j: (j, 0, 0)
    return pl.pallas_call(
        _out_kernel,
        out_shape=jax.ShapeDtypeStruct((b, s, d), F32),
        grid=(b // bb,),
        in_specs=[pl.BlockSpec((mix_ssm.shape[0], bb, s, LANES), lambda j: (0, j, 0, 0)),
                  pl.BlockSpec((bb, s, mla.shape[2]), row), pl.BlockSpec((bb, s, d), row),
                  _full(wts['w_out'].shape)],
        out_specs=pl.BlockSpec((bb, s, d), row),
        compiler_params=pltpu.CompilerParams(
            dimension_semantics=("arbitrary",), vmem_limit_bytes=VMEM_LIMIT),
        name="outproj",
    )(mix_ssm, mla, x, wts['w_out'])


def _attn_cached_call(q, ckv_full, kr_full_t, k_last, v_last, gm, wts, *, tk):
    b, tq, _ = q.shape
    t_full, last_len = ckv_full.shape[1], k_last.shape[1]
    ssm_w = gm.shape[2]
    blk = lambda rows, width: pl.BlockSpec((1, rows, width), lambda j: (j, 0, 0))
    consts = [wts[n] for n in ('onorm_mla', 'w_uv', 'w_ukc', 'gkn', 'segk', 'place')]
    scratch = [pltpu.VMEM((1, t_full, HEAD_W), BF16),
               pltpu.VMEM((MLA_HEADS, t_full // tk, tq, tk), F32), pltpu.VMEM((MLA_HEADS, tq, LANES), F32),
               pltpu.VMEM((MLA_HEADS, tq, last_len), F32), pltpu.VMEM((MLA_HEADS, tq, LANES), F32),
               pltpu.VMEM((MLA_HEADS * tq, ckv_full.shape[2]), F32), pltpu.VMEM((MLA_HEADS * tq, 1), F32)]
    return pl.pallas_call(
        functools.partial(_attn_cached_kernel, tq=tq, tk=tk),
        out_shape=jax.ShapeDtypeStruct((b, tq, ssm_w), BF16),
        grid=(b,),
        in_specs=[blk(tq, HEAD_W), blk(t_full, ckv_full.shape[2]), blk(ROPE_DIM, t_full),
                  blk(last_len, HEAD_W), blk(last_len, HEAD_W), blk(tq, ssm_w)]
                 + [_full(c.shape) for c in consts],
        out_specs=blk(tq, ssm_w),
        scratch_shapes=scratch,
        compiler_params=pltpu.CompilerParams(
            dimension_semantics=("arbitrary",), vmem_limit_bytes=VMEM_LIMIT),
        name="attn_cached",
    )(q, ckv_full, kr_full_t, k_last, v_last, gm, *consts)


def _head_block_cols(w, pieces):
    k = w.shape[0]
    w3 = w.reshape(k, MLA_HEADS, w.shape[1] // MLA_HEADS)
    cols = [w3[:, :, p[0]:p[0] + p[1]] if isinstance(p, tuple) else jnp.zeros((k, MLA_HEADS, p), w.dtype)
            for p in pieces]
    return jnp.concatenate(cols, axis=-1).reshape(k, HEAD_W)


def _zoh_powers(lr, li, dt, n):
    mag = jnp.exp(lr * dt)
    ar, ai = mag * jnp.cos(li * dt), mag * jnp.sin(li * dt)
    pw = [(jnp.ones_like(ar), jnp.zeros_like(ai))]
    for _ in range(n):
        pr, pi = pw[-1]
        pw.append((pr * ar - pi * ai, pr * ai + pi * ar))
    return pw


def _disc_kernel(arow_ref, acol_ref, ldt_ref, bt_ref, cd_ref, ct_ref,
                 apr_ref, api_ref, wb_ref, wc_ref, wf_ref):
    r_blk, groups, cin = wb_ref.shape[1], wb_ref.shape[2], wb_ref.shape[3]
    n_state = acol_ref.shape[2]
    gpb = LANES // cin
    lr, li = arow_ref[0], arow_ref[1]
    dt = jnp.exp(ldt_ref[...])
    pw = _zoh_powers(lr, li, dt, r_blk)
    ar, ai = pw[1]
    den = lr * lr + li * li
    kr = ((ar - 1.0) * lr + ai * li) / den
    ki = (ai * lr - (ar - 1.0) * li) / den
    apr_ref[...], api_ref[...] = pw[r_blk]

    pwc = _zoh_powers(acol_ref[0], acol_ref[1], dt, r_blk)
    for j in range(r_blk):
        pr, pi = pwc[j + 1]
        wc_ref[0, j] = ct_ref[0] * pr - ct_ref[1] * pi
        wc_ref[1, j] = -(ct_ref[0] * pi + ct_ref[1] * pr)

    b_r = kr * bt_ref[0] - ki * bt_ref[1]
    b_i = kr * bt_ref[1] + ki * bt_ref[0]
    for i in range(r_blk):
        pr, pi = pw[r_blk - 1 - i]
        wb_ref[0, i] = pr * b_r - pi * b_i
        wb_ref[1, i] = pr * b_i + pi * b_r

    nt = (((1,), (1,)), ((), ()))
    first_copy = lax.broadcasted_iota(jnp.int32, (LANES, LANES), 1) < n_state
    block = lambda v, n: v[gpb * n:gpb * (n + 1)].reshape(LANES, LANES)
    for i in range(r_blk):
        for j in range(i):
            wf_ref[i, j] = jnp.zeros(wf_ref.shape[2:], F32)
    for m in range(r_blk):
        pr, pi = pw[m]
        ca_r = cd_ref[0] * pr - cd_ref[1] * pi
        ca_i = cd_ref[0] * pi + cd_ref[1] * pr
        for n in range(groups // gpb):
            f_t = (lax.dot_general(jnp.where(first_copy, block(b_r, n), 0.0), block(ca_r, n), nt,
                                   precision=lax.Precision.HIGHEST, preferred_element_type=F32)
                   - lax.dot_general(jnp.where(first_copy, block(b_i, n), 0.0), block(ca_i, n), nt,
                                     precision=lax.Precision.HIGHEST, preferred_element_type=F32))
            for i in range(r_blk - m):
                wf_ref[i, i + m, gpb * n:gpb * (n + 1)] = f_t.reshape(gpb, cin, LANES)


def _prepare_weights(norm_in, w_in, ssm_a_re, ssm_a_im, ssm_log_dt, ssm_b_re, ssm_b_im, ssm_c_re,
                     ssm_c_im, ssm_d, w_glu, b_glu, q_lora_norm, kv_lora_norm, w_uq, w_ukv,
                     q_nope_norm, k_nope_norm, q_rope_norm, k_rope_norm, out_norm_ssm,
                     out_norm_mla, w_out):
    groups, n_state = ssm_a_re.shape
    ssm_w = groups * SSM_GROUP
    row = lambda v: v.reshape(1, -1).astype(F32)
    tail = LANES - ROPE_DIM - NOPE_DIM

    o_kr = 2 * ssm_w + Q_LORA + KV_LORA
    w_in_p = jnp.concatenate(
        [w_in[:, :o_kr + ROPE_DIM], jnp.zeros((w_in.shape[0], LANES - ROPE_DIM), w_in.dtype),
         w_in[:, o_kr + ROPE_DIM:]], axis=1).astype(BF16)

    odd_head = (np.arange(HEAD_W) // LANES) % 2 == 1
    by_parity = lambda even, odd: jnp.where(odd_head[None, :], odd, even)
    w_uq_p = by_parity(_head_block_cols(w_uq, [(0, NOPE_DIM), (NOPE_DIM, ROPE_DIM), tail]),
                       _head_block_cols(w_uq, [(NOPE_DIM, ROPE_DIM), tail, (0, NOPE_DIM)])).astype(BF16)
    w_ukc = w_ukv.reshape(KV_LORA, MLA_HEADS, NOPE_DIM + V_DIM)[:, :, :NOPE_DIM]
    w_ukc = w_ukc.reshape(KV_LORA, MLA_HEADS * NOPE_DIM).astype(BF16)
    w_v_lo = _head_block_cols(w_ukv, [(NOPE_DIM, V_DIM), LANES - V_DIM])
    w_v_hi = _head_block_cols(w_ukv, [LANES - V_DIM, (NOPE_DIM, V_DIM)])
    w_uv = by_parity(w_v_lo, w_v_hi).astype(BF16)
    wvt = w_v_lo.reshape(KV_LORA, MLA_HEADS, LANES)[:, :, :VT_ROWS]
    wvt = jnp.transpose(wvt, (1, 2, 0)).reshape(MLA_HEADS * VT_ROWS, KV_LORA).astype(BF16)

    scale = (NOPE_DIM + ROPE_DIM) ** -0.5 * np.log2(np.e)
    vones = np.zeros((MLA_HEADS, LANES), np.float32)
    vones[0::2, V_DIM] = 1.0
    vones[1::2, 0] = 1.0
    vones = jnp.asarray(vones.reshape(1, HEAD_W))
    zeros = lambda n: jnp.zeros((n,), F32)
    gq = jnp.stack([jnp.concatenate([q_nope_norm, q_rope_norm, zeros(tail)]),
                    jnp.concatenate([q_rope_norm, zeros(tail), q_nope_norm])]) * scale
    gkr = jnp.concatenate([k_rope_norm, zeros(LANES - ROPE_DIM)])
    gkn = jnp.tile(k_nope_norm, 2 * LANES // NOPE_DIM)

    def seg_means(widths):
        m = np.zeros((sum(widths), sum(widths)), np.float32)
        o = 0
        for w in widths:
            m[o:o + w, o:o + w] = 1.0 / w
            o += w
        return m

    segq = jnp.asarray(seg_means([NOPE_DIM, ROPE_DIM, tail, ROPE_DIM, tail, NOPE_DIM]), BF16)
    segk = jnp.asarray(seg_means([NOPE_DIM] * (2 * LANES // NOPE_DIM)), BF16)
    place = jnp.asarray(np.eye(ROPE_DIM, LANES, dtype=np.float32), BF16)

    r_blk = SSM_STEPS_PER_SCAN
    lane_rep = lambda x: jnp.tile(x, (1,) * (x.ndim - 1) + (LANES // x.shape[-1],))
    a2 = jnp.stack([ssm_a_re, ssm_a_im]).astype(F32)
    b2 = jnp.stack([ssm_b_re, ssm_b_im]).astype(F32)
    c2 = jnp.stack([ssm_c_re, ssm_c_im]).astype(F32)
    ldt = ssm_log_dt.astype(F32)
    gl = jax.ShapeDtypeStruct((groups, 1, LANES), F32)
    a_re, a_im, wb5, wc5, wf5 = pl.pallas_call(
        _disc_kernel,
        out_shape=[gl, gl,
                   jax.ShapeDtypeStruct((2, r_blk, groups, SSM_GROUP, LANES), F32),
                   jax.ShapeDtypeStruct((2, r_blk, groups, n_state, LANES), F32),
                   jax.ShapeDtypeStruct((r_blk, r_blk, groups, SSM_GROUP, LANES), F32)],
        name="disc")(
        lane_rep(a2)[:, :, None, :], jnp.broadcast_to(a2[..., None], a2.shape + (LANES,)),
        jnp.broadcast_to(ldt[:, None, None], (groups, 1, LANES)),
        lane_rep(jnp.swapaxes(b2, 2, 3)), lane_rep(c2), lane_rep(jnp.swapaxes(c2, 2, 3)))
    a_re, a_im = a_re[:, 0, :n_state], a_im[:, 0, :n_state]

    return dict(
        ssm_w=ssm_w,
        norm_in=row(norm_in), w_in=w_in_p, qln=row(q_lora_norm), kvln=row(kv_lora_norm),
        w_uq=w_uq_p, w_ukc=w_ukc, w_uv=w_uv, wvt=wvt,
        gq=gq.astype(F32), gkr=row(gkr), gkn=row(gkn), segq=segq, segk=segk, vones=vones,
        place=place, a_re=row(a_re), a_im=row(a_im), wb5=wb5, wc5=wc5, wf5=wf5,
        ssm_d=row(ssm_d), w_glu=w_glu.astype(BF16), b_glu=row(b_glu), onorm_ssm=row(out_norm_ssm),
        onorm_mla=row(out_norm_mla), w_out=w_out.astype(BF16))


def _rope_tables(start, n):
    half = ROPE_DIM // 2
    inv = ROPE_THETA ** (-jnp.arange(half, dtype=F32) / half)
    ang = (start + jnp.arange(n)).astype(F32)[:, None] * inv[None, :]
    cos, sin = jnp.cos(ang), jnp.sin(ang)
    cos_t = jnp.concatenate([cos, cos, jnp.ones((n, LANES - ROPE_DIM), F32)], axis=1)
    sin_a = jnp.concatenate([-sin, jnp.zeros((n, LANES - half), F32)], axis=1)
    sin_b = jnp.concatenate([jnp.zeros((n, half), F32), sin, jnp.zeros((n, LANES - ROPE_DIM), F32)],
                            axis=1)
    return cos_t, sin_a, sin_b


def _mixer(x, pos0, h0re, h0im, past, wts, *, proj_bb, proj_tq, ssm_steps, attn_tk):
    u, gs, gm, q, k, v, ckv, kr = _proj_call(x, _rope_tables(pos0, x.shape[1]), wts,
                                             bb=proj_bb, tq=proj_tq, vt_tile=min(attn_tk, proj_tq))
    mix_ssm, hre, him = _ssm_call(u, gs, h0re, h0im, wts, steps=ssm_steps, unroll=True)
    if past is None:
        y = _attn_causal_call(q, k, v, gm, mix_ssm, x, wts, tq=attn_tk // 2)
    else:
        mla = _attn_cached_call(q, *past, k, v, gm, wts, tk=attn_tk)
        y = _out_call(mix_ssm, mla, x, wts, bb=proj_bb)
    if proj_bb == 1:
        kr = jnp.swapaxes(kr, 1, 2)
    return y, ckv, kr, hre, him


def kernel(x_prompt, x_sample, cache_ckv, cache_krope, state_ssm_re, state_ssm_im, norm_in, w_in, ssm_a_re, ssm_a_im, ssm_log_dt, ssm_b_re, ssm_b_im, ssm_c_re, ssm_c_im, ssm_d, w_glu, b_glu, q_lora_norm, kv_lora_norm, w_uq, w_ukv, q_nope_norm, k_nope_norm, q_rope_norm, k_rope_norm, out_norm_ssm, out_norm_mla, w_out):
    depth = norm_in.shape[0]
    assert depth == 1, "single mixer layer"
    params = (norm_in, w_in, ssm_a_re, ssm_a_im, ssm_log_dt, ssm_b_re, ssm_b_im, ssm_c_re, ssm_c_im,
              ssm_d, w_glu, b_glu, q_lora_norm, kv_lora_norm, w_uq, w_ukv, q_nope_norm, k_nope_norm,
              q_rope_norm, k_rope_norm, out_norm_ssm, out_norm_mla, w_out)
    drop_depth = lambda a: a.reshape(a.shape[1:])
    wts = _prepare_weights(*[drop_depth(p) for p in params])
    groups, n_state = ssm_a_re.shape[1:]
    bp = x_prompt.shape[0]
    bs, ss, _ = x_sample.shape
    past_len = cache_ckv.shape[2]

    zero_state = jnp.zeros((bp, groups * n_state), F32)
    yp, ckv_p, kr_p, re_p, im_p = _mixer(
        x_prompt, 0, zero_state, zero_state, None, wts,
        proj_bb=1, proj_tq=1024, ssm_steps=64, attn_tk=512)
    ys, ckv_s, kr_s, re_s, im_s = _mixer(
        x_sample, past_len,
        state_ssm_re.reshape(bs, groups * n_state), state_ssm_im.reshape(bs, groups * n_state),
        (drop_depth(cache_ckv), jnp.swapaxes(drop_depth(cache_krope), 1, 2)), wts,
        proj_bb=bs // 2, proj_tq=ss, ssm_steps=ss, attn_tk=past_len)

    st = lambda h, bb: h.reshape(1, bb, groups, n_state)
    return (yp, ys, ckv_p[None], kr_p[None], st(re_p, bp), st(im_p, bp),
            ckv_s[None], kr_s[None], st(re_s, bs), st(im_s, bs))
```

```python
import functools

import numpy as np
import jax
import jax.numpy as jnp
from jax import lax
from jax.experimental import pallas as pl
from jax.experimental.pallas import tpu as pltpu

F32 = jnp.float32
BF16 = jnp.bfloat16

CHUNK = 64
SSM_GROUP = 16
MLA_HEADS = 8
NOPE_DIM = 64
ROPE_DIM = 32
V_DIM = 64
Q_LORA = 256
KV_LORA = 128
ROPE_THETA = 10000.0
EPS = 1e-6

LANES = 128
SUBLANES = 8
VT_ROWS = 80
HEAD_W = MLA_HEADS * LANES
GROUPS_PER_BLOCK = 8
SSM_STEPS_PER_SCAN = 4
VMEM_LIMIT = 56 * 1024 * 1024


def _rms(x, gain):
    return x * lax.rsqrt(jnp.mean(x * x, axis=-1, keepdims=True) + EPS) * gain


def _seg_rms(x, seg):
    ms = jnp.dot((x * x).astype(BF16), seg, preferred_element_type=F32)
    return x * lax.rsqrt(ms + EPS)


def _rope_block(x, cos_t, sin_a, sin_b):
    return (x * cos_t + pltpu.roll(x, LANES - ROPE_DIM // 2, 1) * sin_a
            + pltpu.roll(x, ROPE_DIM // 2, 1) * sin_b)


def _key_blocks(ckv_b, kr_blk, w_ukc_ref, gkn_ref, segk_ref, k_ref):
    kc = jnp.dot(ckv_b, w_ukc_ref[...], preferred_element_type=F32)
    kr_even = pltpu.roll(kr_blk, NOPE_DIM, 1)
    low = lax.broadcasted_iota(jnp.int32, (1, LANES), 1) < NOPE_DIM
    blk3 = k_ref.shape[:2] + (LANES,)
    for pp in range(kc.shape[1] // (2 * LANES)):
        kn = _seg_rms(kc[:, 2 * LANES * pp:2 * LANES * (pp + 1)], segk_ref[...]) * gkn_ref[...]
        for t in range(2):
            pair = kn[:, LANES * t:LANES * (t + 1)]
            he = 2 * (2 * pp + t)
            k_ref[:, :, LANES * he:LANES * (he + 1)] = (
                (jnp.where(low, pair, 0.0) + kr_even).astype(BF16).reshape(blk3))
            k_ref[:, :, LANES * (he + 1):LANES * (he + 2)] = (
                (jnp.where(low, 0.0, pair) + kr_blk).astype(BF16).reshape(blk3))


def _proj_kernel(x_ref, cos_ref, sa_ref, sb_ref, norm_in_ref, w_in_ref, qln_ref, kvln_ref,
                 w_uq_ref, w_ukc_ref, gq_ref, gkr_ref, gkn_ref, segq_ref, segk_ref, *rest,
                 ssm_w, transposed):
    v_weights, outs = (rest[:1], rest[1:]) if transposed else (rest[:2], rest[2:])
    u_ref, gs_ref, gm_ref, q_ref, k_ref, v_ref, ckv_ref, kr_ref = outs
    bb, tq, d = x_ref.shape
    x = x_ref[...].reshape(bb * tq, d)
    h = _rms(x, norm_in_ref[...])
    z = jnp.dot(h.astype(BF16), w_in_ref[...], preferred_element_type=F32)
    o = 0
    for dst in (u_ref, gs_ref):
        for j in range(ssm_w // LANES):
            dst[j] = z[:, o + LANES * j:o + LANES * (j + 1)].reshape(bb, tq, LANES)
        o += ssm_w
    c_q = z[:, o:o + Q_LORA]
    o += Q_LORA
    c_kv = z[:, o:o + KV_LORA]
    o += KV_LORA
    kr_raw = z[:, o:o + LANES]
    o += LANES
    gm_ref[...] = z[:, o:].reshape(gm_ref.shape)

    per_row = lambda t_ref: jnp.concatenate([t_ref[...]] * bb, axis=0)
    cos_t, sin_a, sin_b = per_row(cos_ref), per_row(sa_ref), per_row(sb_ref)
    even = lambda t: pltpu.roll(t, NOPE_DIM, 1)
    cos_e, sin_ae, sin_be = even(cos_t), even(sin_a), even(sin_b)
    seg = segq_ref[...]

    kr_ms = jnp.dot((kr_raw * kr_raw).astype(BF16), seg[LANES:, LANES:], preferred_element_type=F32)
    kr_blk = _rope_block(kr_raw * lax.rsqrt(kr_ms + EPS) * gkr_ref[...], cos_t, sin_a, sin_b)
    if transposed:
        kr_ref[0] = kr_blk.T[:ROPE_DIM]
    else:
        kr_ref[...] = kr_blk[:, :ROPE_DIM].reshape(kr_ref.shape)

    q = jnp.dot(_rms(c_q, qln_ref[...]).astype(BF16), w_uq_ref[...], preferred_element_type=F32)
    gq = gq_ref[...]
    for p in range(MLA_HEADS // 2):
        qn = _seg_rms(q[:, 2 * LANES * p:2 * LANES * (p + 1)], seg)
        blocks = (_rope_block(qn[:, :LANES] * gq[0:1], cos_e, sin_ae, sin_be),
                  _rope_block(qn[:, LANES:] * gq[1:2], cos_t, sin_a, sin_b))
        for j in range(2):
            hd = 2 * p + j
            q_ref[:, :, LANES * hd:LANES * (hd + 1)] = blocks[j].astype(BF16).reshape(bb, tq, LANES)

    ckv = _rms(c_kv, kvln_ref[...])
    ckv_ref[...] = ckv.reshape(ckv_ref.shape)
    _key_blocks(ckv.astype(BF16), kr_blk, w_ukc_ref, gkn_ref, segk_ref, k_ref)
    if transposed:
        v_t = jnp.dot(v_weights[0][...], ckv.T.astype(BF16), preferred_element_type=F32)
        ones_row = lax.broadcasted_iota(jnp.int32, v_t.shape, 0) % VT_ROWS == V_DIM
        v_t = jnp.where(ones_row, 1.0, v_t).astype(BF16)
        width = v_ref.shape[3]
        for t in range(v_ref.shape[1]):
            v_ref[0, t] = v_t[:, width * t:width * (t + 1)]
    else:
        v = jnp.dot(ckv.astype(BF16), v_weights[0][...], preferred_element_type=F32) + v_weights[1][...]
        v_ref[...] = v.astype(BF16).reshape(v_ref.shape)


def _full(shape):
    n = len(shape)
    return pl.BlockSpec(shape, lambda *_: (0,) * n)


def _proj_call(x, tables, wts, *, bb, tq, vt_tile):
    b, s, d = x.shape
    ssm_w = wts['ssm_w']
    cos_t, sin_a, sin_b = tables
    transposed = bb == 1
    v_weights = (wts['wvt'],) if transposed else (wts['w_uv'], wts['vones'])
    grid = (s // tq, b // bb)
    row = lambda i, j: (j, i, 0)
    tab = pl.BlockSpec((tq, LANES), lambda i, j: (i, 0))
    in_specs = [pl.BlockSpec((bb, tq, d), row), tab, tab, tab,
                _full(wts['norm_in'].shape), _full(wts['w_in'].shape), _full(wts['qln'].shape),
                _full(wts['kvln'].shape), _full(wts['w_uq'].shape), _full(wts['w_ukc'].shape),
                _full(wts['gq'].shape), _full(wts['gkr'].shape), _full(wts['gkn'].shape),
                _full(wts['segq'].shape), _full(wts['segk'].shape)] + [_full(w.shape) for w in v_weights]
    slabs = ssm_w // LANES
    tb_spec = pl.BlockSpec((slabs, bb, tq, LANES), lambda i, j: (0, j, i, 0))
    kr_shape, kr_spec = (((b, ROPE_DIM, s), pl.BlockSpec((1, ROPE_DIM, tq), lambda i, j: (j, 0, i)))
                         if transposed else
                         ((b, s, ROPE_DIM), pl.BlockSpec((bb, tq, ROPE_DIM), row)))
    v_shape, v_spec = (((b, s // vt_tile, MLA_HEADS * VT_ROWS, vt_tile),
                        pl.BlockSpec((1, tq // vt_tile, MLA_HEADS * VT_ROWS, vt_tile),
                                     lambda i, j: (j, i, 0, 0)))
                       if transposed else
                       ((b, s, HEAD_W), pl.BlockSpec((bb, tq, HEAD_W), row)))
    out_shape = [jax.ShapeDtypeStruct((slabs, b, s, LANES), F32),
                 jax.ShapeDtypeStruct((slabs, b, s, LANES), F32),
                 jax.ShapeDtypeStruct((b, s, ssm_w), F32),
                 jax.ShapeDtypeStruct((b, s, HEAD_W), BF16),
                 jax.ShapeDtypeStruct((b, s, HEAD_W), BF16),
                 jax.ShapeDtypeStruct(v_shape, BF16),
                 jax.ShapeDtypeStruct((b, s, KV_LORA), F32),
                 jax.ShapeDtypeStruct(kr_shape, F32)]
    out_specs = [tb_spec, tb_spec,
                 pl.BlockSpec((bb, tq, ssm_w), row),
                 pl.BlockSpec((bb, tq, HEAD_W), row), pl.BlockSpec((bb, tq, HEAD_W), row), v_spec,
                 pl.BlockSpec((bb, tq, KV_LORA), row), kr_spec]
    return pl.pallas_call(
        functools.partial(_proj_kernel, ssm_w=ssm_w, transposed=transposed),
        out_shape=out_shape, grid=grid, in_specs=in_specs, out_specs=out_specs,
        compiler_params=pltpu.CompilerParams(
            dimension_semantics=("arbitrary", "arbitrary"), vmem_limit_bytes=VMEM_LIMIT),
        name="proj",
    )(x, cos_t, sin_a, sin_b, wts['norm_in'], wts['w_in'], wts['qln'], wts['kvln'], wts['w_uq'],
      wts['w_ukc'], wts['gq'], wts['gkr'], wts['gkn'], wts['segq'], wts['segk'], *v_weights)


def _ssm_kernel(u_ref, gs_ref, h0re_ref, h0im_ref, are_ref, aim_ref, wb5_ref, wc5_ref, wf5_ref, d_ref,
                w_glu_ref, b_glu_ref, onorm_ref, mix_ref, hre_ref, him_ref, us_ref, xs_ref, y_ref,
                wb_ref, wc_ref, wf_ref, *, batch, steps, unroll):
    n_blocks, r_blk = wb_ref.shape[0], wb_ref.shape[1] // LANES
    half = wb_ref.shape[2] // 2
    gpb = wb5_ref.shape[2] // n_blocks
    n_k = steps // r_blk

    @pl.when(pl.program_id(0) == 0)
    def _():
        hre_ref[...] = h0re_ref[...]
        him_ref[...] = h0im_ref[...]

        def own_group(shape, row_div, lane_div):
            return (lax.broadcasted_iota(jnp.int32, shape, 0) // row_div
                    == lax.broadcasted_iota(jnp.int32, shape, 1) // lane_div)

        cin, n_state = wb5_ref.shape[3], half // gpb
        m_b = own_group((LANES, half), cin, n_state)
        m_c = own_group((half, LANES), n_state, cin)
        m_f = own_group((LANES, LANES), cin, cin)
        for gb in range(n_blocks):
            grp = slice(gpb * gb, gpb * (gb + 1))
            for ri in range(2):
                for i in range(r_blk):
                    piece = wb5_ref[ri, i, grp].reshape(LANES, LANES)
                    wide = jnp.concatenate([piece] * (half // LANES), axis=1)
                    wb_ref[gb, LANES * i:LANES * (i + 1), half * ri:half * (ri + 1)] = (
                        jnp.where(m_b, wide, 0.0).astype(BF16))
                    piece = wc5_ref[ri, i, grp].reshape(half, LANES)
                    wc_ref[gb, half * ri:half * (ri + 1), LANES * i:LANES * (i + 1)] = (
                        jnp.where(m_c, piece, 0.0).astype(BF16))
            for i in range(r_blk):
                for j in range(r_blk):
                    piece = wf5_ref[i, j, grp].reshape(LANES, LANES)
                    wf_ref[gb, LANES * i:LANES * (i + 1), LANES * j:LANES * (j + 1)] = (
                        jnp.where(m_f, piece, 0.0).astype(BF16))

    pitch = us_ref.shape[2] // n_k
    for b in range(batch):
        for j in range(n_blocks):
            for i in range(r_blk):
                us_ref[j, i, pl.ds(b, n_k, stride=pitch), :] = (
                    u_ref[j, b, pl.ds(i, n_k, stride=r_blk), :])

    def slab_rows(j, i):
        return jnp.concatenate([us_ref[j, i, pitch * k:pitch * k + batch, :] for k in range(n_k)], axis=0)

    u_slabs = [[slab_rows(j, i) for i in range(r_blk)] for j in range(n_blocks)]
    for gb in range(n_blocks):
        lhs = jnp.concatenate(u_slabs[gb], axis=-1).astype(BF16)
        xs = xs_ref
        xs[...] = jnp.dot(lhs, wb_ref[gb], preferred_element_type=F32)
        cols = slice(half * gb, half * (gb + 1))
        a_re = jnp.broadcast_to(are_ref[:, cols], (batch, half))
        a_im = jnp.broadcast_to(aim_ref[:, cols], (batch, half))

        def step(k, carry, xs=xs, a_re=a_re, a_im=a_im):
            h_re, h_im = carry
            rows = pl.ds(pl.multiple_of(k * batch, batch), batch)
            n_re = a_re * h_re - a_im * h_im + xs[rows, :half]
            n_im = a_re * h_im + a_im * h_re + xs[rows, half:]
            xs[rows, :half] = h_re
            xs[rows, half:] = h_im
            return n_re, n_im

        h_re, h_im = lax.fori_loop(0, n_k, step, (hre_ref[:, cols], him_ref[:, cols]), unroll=unroll)
        hre_ref[:, cols] = h_re
        him_ref[:, cols] = h_im
        y = (jnp.dot(xs[...].astype(BF16), wc_ref[gb], preferred_element_type=F32)
             + jnp.dot(lhs, wf_ref[gb], preferred_element_type=F32))
        for i in range(r_blk):
            y_ref[i, :, LANES * gb:LANES * (gb + 1)] = y[:, LANES * i:LANES * (i + 1)]

    rows = r_blk * n_k * batch
    u = jnp.concatenate([jnp.concatenate(u_slabs[j], axis=0) for j in range(n_blocks)], axis=-1)
    y = y_ref[...].reshape(rows, n_blocks * LANES) + d_ref[...] * u
    yg = jax.nn.gelu(y)
    glu = jnp.dot(yg.astype(BF16), w_glu_ref[...], preferred_element_type=F32) + b_glu_ref[...]
    out = _rms(yg * jax.nn.sigmoid(glu), onorm_ref[...])
    for j in range(n_blocks):
        for i in range(r_blk):
            for k in range(n_k):
                r0 = (i * n_k + k) * batch
                us_ref[j, i, pitch * k:pitch * k + batch, :] = out[r0:r0 + batch, LANES * j:LANES * (j + 1)]
    for b in range(batch):
        for j in range(n_blocks):
            for i in range(r_blk):
                tok = pl.ds(i, n_k, stride=r_blk)
                gs = gs_ref[j, b, tok, :]
                o = us_ref[j, i, pl.ds(b, n_k, stride=pitch), :]
                mix_ref[j, b, tok, :] = o * (gs * jax.nn.sigmoid(gs))


def _ssm_call(u, gs, h0re, h0im, wts, *, steps, unroll):
    n_blocks, batch, seq, _ = u.shape
    n_state = h0re.shape[1]
    r_blk = wts['wb5'].shape[1]
    width = 2 * n_state // n_blocks
    assert steps % r_blk == 0 and wts['wb5'].shape[3] * GROUPS_PER_BLOCK == LANES
    blk = pl.BlockSpec((n_blocks, batch, steps, LANES), lambda i: (0, 0, i, 0))
    names = ['a_re', 'a_im', 'wb5', 'wc5', 'wf5', 'ssm_d', 'w_glu', 'b_glu', 'onorm_ssm']
    n_rows = steps // r_blk * batch
    return pl.pallas_call(
        functools.partial(_ssm_kernel, batch=batch, steps=steps, unroll=unroll),
        out_shape=[jax.ShapeDtypeStruct(u.shape, F32),
                   jax.ShapeDtypeStruct((batch, n_state), F32),
                   jax.ShapeDtypeStruct((batch, n_state), F32)],
        grid=(seq // steps,),
        in_specs=[blk, blk, _full(h0re.shape), _full(h0im.shape)] + [_full(wts[n].shape) for n in names],
        out_specs=[blk, _full((batch, n_state)), _full((batch, n_state))],
        scratch_shapes=[pltpu.VMEM((n_blocks, r_blk, steps // r_blk * (batch + SUBLANES), LANES), F32),
                        pltpu.VMEM((n_rows, width), F32),
                        pltpu.VMEM((r_blk, n_rows, n_blocks * LANES), F32),
                        pltpu.VMEM((n_blocks, r_blk * LANES, width), BF16),
                        pltpu.VMEM((n_blocks, width, r_blk * LANES), BF16),
                        pltpu.VMEM((n_blocks, r_blk * LANES, r_blk * LANES), BF16)],
        compiler_params=pltpu.CompilerParams(
            dimension_semantics=("arbitrary",), vmem_limit_bytes=VMEM_LIMIT),
        name="ssm",
    )(u, gs, h0re, h0im, *[wts[n] for n in names])


def _attn_cached_kernel(q_ref, cf_ref, krf_ref, kl_ref, vl_ref, gm_ref, onorm_ref, w_uv_ref,
                        w_ukc_ref, gkn_ref, segk_ref, place_ref, y_ref,
                        kf_ref, s_full, ml_scr, s_last, m_scr, ctx_scr, den_scr, *, tq, tk):
    kr_blk = lax.dot_general(krf_ref[0].astype(BF16), place_ref[...], (((0,), (0,)), ((), ())),
                             preferred_element_type=F32)
    _key_blocks(cf_ref[0].astype(BF16), kr_blk, w_ukc_ref, gkn_ref, segk_ref, kf_ref)

    n_full = kf_ref.shape[1] // tk
    dn = (((1,), (1,)), ((), ()))
    lane = lax.broadcasted_iota(jnp.int32, (tq, LANES), 1)
    heads = range(MLA_HEADS)
    cols = [slice(LANES * hd, LANES * (hd + 1)) for hd in heads]

    def lane_tiles(s):
        return [s[:, LANES * c:LANES * (c + 1)] for c in range(s.shape[1] // LANES)]

    def probs(s, m_rep):
        if s.shape[1] % LANES == 0:
            p = jnp.concatenate([jnp.exp2(t - m_rep) for t in lane_tiles(s)], axis=-1)
        else:
            p = jnp.exp2(s - m_rep[:, :1])
        return p.astype(BF16)

    ml_scr[...] = jnp.full(ml_scr.shape, -jnp.inf, F32)

    def a_step(jt, carry):
        rows = pl.ds(pl.multiple_of(jt * tk, tk), tk)
        for hd in heads:
            s = lax.dot_general(q_ref[0, :, cols[hd]], kf_ref[0, rows, cols[hd]], dn,
                                preferred_element_type=F32)
            s_full[hd, jt] = s
            ml_scr[hd] = functools.reduce(jnp.maximum, lane_tiles(s), ml_scr[hd])
        return carry

    lax.fori_loop(0, n_full, a_step, 0)

    for hd in heads:
        s = lax.dot_general(q_ref[0, :, cols[hd]], kl_ref[0, :, cols[hd]], dn,
                            preferred_element_type=F32)
        s_last[hd] = s
        m = jnp.maximum(jnp.max(s, axis=-1, keepdims=True),
                        jnp.max(ml_scr[hd], axis=-1, keepdims=True))
        m_scr[hd] = jnp.broadcast_to(m, (tq, LANES))
    ctx_scr[...] = jnp.zeros(ctx_scr.shape, F32)
    den_scr[...] = jnp.zeros(den_scr.shape, F32)

    def b_step(jt, carry):
        rows = pl.ds(pl.multiple_of(jt * tk, tk), tk)
        p = jnp.concatenate([probs(s_full[hd, jt], m_scr[hd]) for hd in heads], axis=0)
        ctx_scr[...] += jnp.dot(p, cf_ref[0, rows, :].astype(BF16), preferred_element_type=F32)
        den_scr[...] += jnp.sum(p.astype(F32), axis=-1, keepdims=True)
        return carry

    lax.fori_loop(0, n_full, b_step, 0)

    outs = []
    for hd in heads:
        mine = slice(tq * hd, tq * (hd + 1))
        cached = jnp.dot(ctx_scr[mine].astype(BF16), w_uv_ref[:, cols[hd]], preferred_element_type=F32)
        p = probs(s_last[hd], m_scr[hd])
        new = jnp.dot(p, vl_ref[0, :, cols[hd]], preferred_element_type=F32)
        ones_col = V_DIM if hd % 2 == 0 else 0
        outs.append((cached + new) / (den_scr[mine] + new[:, ones_col:ones_col + 1]))
    attn = jnp.concatenate([jnp.where(lane < V_DIM, outs[e], outs[e + 1])
                            for e in range(0, MLA_HEADS, 2)], axis=-1)
    gm = gm_ref[0]
    y_ref[0] = (_rms(attn, onorm_ref[...]) * (gm * jax.nn.sigmoid(gm))).astype(BF16)


def _attn_causal_kernel(q_ref, k_ref, vt_ref, gm_ref, ms_ref, x_ref, onorm_ref, w_out_ref, y_ref,
                        s_full, ml_scr, s_last, m_scr, acc_scr, ot_scr, *, tq, tk):
    n_full = pl.program_id(1)
    nt = (((1,), (1,)), ((), ()))
    heads = range(MLA_HEADS)
    cols = [slice(LANES * hd, LANES * (hd + 1)) for hd in heads]
    vrows = [slice(VT_ROWS * hd, VT_ROWS * (hd + 1)) for hd in heads]

    def group_max(s):
        return jnp.max(s.reshape(s.shape[0] // SUBLANES, SUBLANES, s.shape[1]), axis=0)

    def chunk_mask(width):
        kc = lax.broadcasted_iota(jnp.int32, (width, tq), 0) // CHUNK
        qc = lax.broadcasted_iota(jnp.int32, (width, tq), 1) // CHUNK + (width - tq) // CHUNK
        return qc >= kc

    n_tiles = tk // tq
    mine = [slice(tq * t, tq * (t + 1)) for t in range(n_tiles)]
    width = [tq * (t + 1) for t in range(n_tiles)]
    ml_scr[...] = jnp.full(ml_scr.shape, -jnp.inf, F32)

    def phase_a(t, jt):
        rows = pl.ds(pl.multiple_of(jt * tk, tk), tk)
        for hd in heads:
            s = lax.dot_general(k_ref[0, rows, cols[hd]], q_ref[0, mine[t], cols[hd]], nt,
                                preferred_element_type=F32)
            s_full[t, hd, jt] = s
            ml_scr[t, hd] = jnp.maximum(ml_scr[t, hd], group_max(s))

    def phase_b(t, jt):
        for hd in heads:
            p = jnp.exp2(s_full[t, hd, jt] - m_scr[t, hd, :1]).astype(BF16)
            acc_scr[t, hd] += jnp.dot(vt_ref[0, jt, vrows[hd], :], p, preferred_element_type=F32)

    def own_a(t):
        own_keys = pl.ds(pl.multiple_of(n_full * tk, tk), width[t])
        mask = chunk_mask(width[t])
        for hd in heads:
            s = lax.dot_general(k_ref[0, own_keys, cols[hd]], q_ref[0, mine[t], cols[hd]], nt,
                                preferred_element_type=F32)
            s = jnp.where(mask, s, -jnp.inf)
            s_last[hd, :width[t]] = s
            m = jnp.max(jnp.maximum(ml_scr[t, hd], group_max(s)), axis=0, keepdims=True)
            m_scr[t, hd] = jnp.broadcast_to(m, m_scr.shape[2:])
            acc_scr[t, hd] = jnp.zeros(acc_scr.shape[2:], F32)

    def finish(t):
        for hd in heads:
            p = jnp.exp2(s_last[hd, :width[t]] - m_scr[t, hd, :1]).astype(BF16)
            acc = acc_scr[t, hd] + jnp.dot(vt_ref[0, n_full, vrows[hd], :width[t]], p,
                                           preferred_element_type=F32)
            ot_scr[V_DIM * hd:V_DIM * (hd + 1), :] = acc[:V_DIM] / acc[V_DIM:V_DIM + 1]
        gm = gm_ref[0, mine[t]]
        mla = _rms(ot_scr[...].T, onorm_ref[...]) * (gm * jax.nn.sigmoid(gm))
        mix_ssm = [ms_ref[j, 0, mine[t]] for j in range(ms_ref.shape[0])]
        mix = jnp.concatenate(mix_ssm + [mla], axis=-1).astype(BF16)
        y_ref[0, mine[t]] = x_ref[0, mine[t]] + jnp.dot(mix, w_out_ref[...], preferred_element_type=F32)

    def loop(body):
        lax.fori_loop(0, n_full, lambda jt, c: (body(jt), c)[1], 0)

    loop(lambda jt: phase_a(0, jt))
    own_a(0)
    for t in range(1, n_tiles):
        loop(lambda jt, t=t: (phase_a(t, jt), phase_b(t - 1, jt)))
        finish(t - 1)
        own_a(t)
    loop(lambda jt: phase_b(n_tiles - 1, jt))
    finish(n_tiles - 1)


def _attn_causal_call(q, k, vt, gm, mix_ssm, x, wts, *, tq):
    b, s, d = x.shape
    tk = vt.shape[3]
    ssm_w = gm.shape[2]
    assert tk % tq == 0 and s % tk == 0 and tq % CHUNK == 0
    row = lambda j, i: (j, i, 0)
    res = lambda j, i: (j, 0, 0)
    n_tiles = tk // tq
    scratch = [pltpu.VMEM((n_tiles, MLA_HEADS, s // tk - 1, tk, tq), F32),
               pltpu.VMEM((n_tiles, MLA_HEADS, SUBLANES, tq), F32),
               pltpu.VMEM((MLA_HEADS, tk, tq), F32),
               pltpu.VMEM((n_tiles, MLA_HEADS, SUBLANES, tq), F32),
               pltpu.VMEM((n_tiles, MLA_HEADS, VT_ROWS, tq), F32),
               pltpu.VMEM((MLA_HEADS * V_DIM, tq), F32)]
    once = dict(pipeline_mode=pl.Buffered(1))
    return pl.pallas_call(
        functools.partial(_attn_causal_kernel, tq=tq, tk=tk),
        out_shape=jax.ShapeDtypeStruct((b, s, d), F32),
        grid=(b, s // tk),
        in_specs=[pl.BlockSpec((1, tk, HEAD_W), row), pl.BlockSpec((1, s, HEAD_W), res, **once),
                  pl.BlockSpec((1,) + vt.shape[1:], lambda j, i: (j, 0, 0, 0), **once),
                  pl.BlockSpec((1, tk, ssm_w), row),
                  pl.BlockSpec((mix_ssm.shape[0], 1, tk, LANES), lambda j, i: (0, j, i, 0)),
                  pl.BlockSpec((1, tk, d), row),
                  _full(wts['onorm_mla'].shape),
                  pl.BlockSpec(wts['w_out'].shape, lambda j, i: (0, 0), **once)],
        out_specs=pl.BlockSpec((1, tk, d), row),
        scratch_shapes=scratch,
        compiler_params=pltpu.CompilerParams(
            dimension_semantics=("arbitrary", "arbitrary"), vmem_limit_bytes=VMEM_LIMIT),
        name="attn",
    )(q, k, vt, gm, mix_ssm, x, wts['onorm_mla'], wts['w_out'])


def _out_kernel(ms_ref, mla_ref, x_ref, w_out_ref, y_ref):
    bb, tq, d = x_ref.shape
    rows = bb * tq
    mix = jnp.concatenate([ms_ref[j].reshape(rows, LANES).astype(BF16) for j in range(ms_ref.shape[0])]
                          + [mla_ref[...].reshape(rows, mla_ref.shape[2])], axis=-1)
    y = x_ref[...].reshape(rows, d) + jnp.dot(mix, w_out_ref[...], preferred_element_type=F32)
    y_ref[...] = y.reshape(bb, tq, d)


def _out_call(mix_ssm, mla, x, wts, *, bb):
    b, s, d = x.shape
    row = lambda j: (j, 0, 0)
    return pl.pallas_call(
        _out_kernel,
        out_shape=jax.ShapeDtypeStruct((b, s, d), F32),
        grid=(b // bb,),
        in_specs=[pl.BlockSpec((mix_ssm.shape[0], bb, s, LANES), lambda j: (0, j, 0, 0)),
                  pl.BlockSpec((bb, s, mla.shape[2]), row), pl.BlockSpec((bb, s, d), row),
                  _full(wts['w_out'].shape)],
        out_specs=pl.BlockSpec((bb, s, d), row),
        compiler_params=pltpu.CompilerParams(
            dimension_semantics=("arbitrary",), vmem_limit_bytes=VMEM_LIMIT),
        name="outproj",
    )(mix_ssm, mla, x, wts['w_out'])


def _attn_cached_call(q, ckv_full, kr_full_t, k_last, v_last, gm, wts, *, tk):
    b, tq, _ = q.shape
    t_full, last_len = ckv_full.shape[1], k_last.shape[1]
    ssm_w = gm.shape[2]
    blk = lambda rows, width: pl.BlockSpec((1, rows, width), lambda j: (j, 0, 0))
    consts = [wts[n] for n in ('onorm_mla', 'w_uv', 'w_ukc', 'gkn', 'segk', 'place')]
    scratch = [pltpu.VMEM((1, t_full, HEAD_W), BF16),
               pltpu.VMEM((MLA_HEADS, t_full // tk, tq, tk), F32), pltpu.VMEM((MLA_HEADS, tq, LANES), F32),
               pltpu.VMEM((MLA_HEADS, tq, last_len), F32), pltpu.VMEM((MLA_HEADS, tq, LANES), F32),
               pltpu.VMEM((MLA_HEADS * tq, ckv_full.shape[2]), F32), pltpu.VMEM((MLA_HEADS * tq, 1), F32)]
    return pl.pallas_call(
        functools.partial(_attn_cached_kernel, tq=tq, tk=tk),
        out_shape=jax.ShapeDtypeStruct((b, tq, ssm_w), BF16),
        grid=(b,),
        in_specs=[blk(tq, HEAD_W), blk(t_full, ckv_full.shape[2]), blk(ROPE_DIM, t_full),
                  blk(last_len, HEAD_W), blk(last_len, HEAD_W), blk(tq, ssm_w)]
                 + [_full(c.shape) for c in consts],
        out_specs=blk(tq, ssm_w),
        scratch_shapes=scratch,
        compiler_params=pltpu.CompilerParams(
            dimension_semantics=("arbitrary",), vmem_limit_bytes=VMEM_LIMIT),
        name="attn_cached",
    )(q, ckv_full, kr_full_t, k_last, v_last, gm, *consts)


def _head_block_cols(w, pieces):
    k = w.shape[0]
    w3 = w.reshape(k, MLA_HEADS, w.shape[1] // MLA_HEADS)
    cols = [w3[:, :, p[0]:p[0] + p[1]] if isinstance(p, tuple) else jnp.zeros((k, MLA_HEADS, p), w.dtype)
            for p in pieces]
    return jnp.concatenate(cols, axis=-1).reshape(k, HEAD_W)


def _zoh_powers(lr, li, dt, n):
    mag = jnp.exp(lr * dt)
    ar, ai = mag * jnp.cos(li * dt), mag * jnp.sin(li * dt)
    pw = [(jnp.ones_like(ar), jnp.zeros_like(ai))]
    for _ in range(n):
        pr, pi = pw[-1]
        pw.append((pr * ar - pi * ai, pr * ai + pi * ar))
    return pw


def _disc_kernel(arow_ref, acol_ref, ldt_ref, bt_ref, cd_ref, ct_ref,
                 apr_ref, api_ref, wb_ref, wc_ref, wf_ref):
    r_blk, groups, cin = wb_ref.shape[1], wb_ref.shape[2], wb_ref.shape[3]
    n_state = acol_ref.shape[2]
    gpb = LANES // cin
    lr, li = arow_ref[0], arow_ref[1]
    dt = jnp.exp(ldt_ref[...])
    pw = _zoh_powers(lr, li, dt, r_blk)
    ar, ai = pw[1]
    den = lr * lr + li * li
    kr = ((ar - 1.0) * lr + ai * li) / den
    ki = (ai * lr - (ar - 1.0) * li) / den
    apr_ref[...], api_ref[...] = pw[r_blk]

    pwc = _zoh_powers(acol_ref[0], acol_ref[1], dt, r_blk)
    for j in range(r_blk):
        pr, pi = pwc[j + 1]
        wc_ref[0, j] = ct_ref[0] * pr - ct_ref[1] * pi
        wc_ref[1, j] = -(ct_ref[0] * pi + ct_ref[1] * pr)

    b_r = kr * bt_ref[0] - ki * bt_ref[1]
    b_i = kr * bt_ref[1] + ki * bt_ref[0]
    for i in range(r_blk):
        pr, pi = pw[r_blk - 1 - i]
        wb_ref[0, i] = pr * b_r - pi * b_i
        wb_ref[1, i] = pr * b_i + pi * b_r

    nt = (((1,), (1,)), ((), ()))
    first_copy = lax.broadcasted_iota(jnp.int32, (LANES, LANES), 1) < n_state
    block = lambda v, n: v[gpb * n:gpb * (n + 1)].reshape(LANES, LANES)
    for i in range(r_blk):
        for j in range(i):
            wf_ref[i, j] = jnp.zeros(wf_ref.shape[2:], F32)
    for m in range(r_blk):
        pr, pi = pw[m]
        ca_r = cd_ref[0] * pr - cd_ref[1] * pi
        ca_i = cd_ref[0] * pi + cd_ref[1] * pr
        for n in range(groups // gpb):
            f_t = (lax.dot_general(jnp.where(first_copy, block(b_r, n), 0.0), block(ca_r, n), nt,
                                   precision=lax.Precision.HIGHEST, preferred_element_type=F32)
                   - lax.dot_general(jnp.where(first_copy, block(b_i, n), 0.0), block(ca_i, n), nt,
                                     precision=lax.Precision.HIGHEST, preferred_element_type=F32))
            for i in range(r_blk - m):
                wf_ref[i, i + m, gpb * n:gpb * (n + 1)] = f_t.reshape(gpb, cin, LANES)


def _prepare_weights(norm_in, w_in, ssm_a_re, ssm_a_im, ssm_log_dt, ssm_b_re, ssm_b_im, ssm_c_re,
                     ssm_c_im, ssm_d, w_glu, b_glu, q_lora_norm, kv_lora_norm, w_uq, w_ukv,
                     q_nope_norm, k_nope_norm, q_rope_norm, k_rope_norm, out_norm_ssm,
                     out_norm_mla, w_out):
    groups, n_state = ssm_a_re.shape
    ssm_w = groups * SSM_GROUP
    row = lambda v: v.reshape(1, -1).astype(F32)
    tail = LANES - ROPE_DIM - NOPE_DIM

    o_kr = 2 * ssm_w + Q_LORA + KV_LORA
    w_in_p = jnp.concatenate(
        [w_in[:, :o_kr + ROPE_DIM], jnp.zeros((w_in.shape[0], LANES - ROPE_DIM), w_in.dtype),
         w_in[:, o_kr + ROPE_DIM:]], axis=1).astype(BF16)

    odd_head = (np.arange(HEAD_W) // LANES) % 2 == 1
    by_parity = lambda even, odd: jnp.where(odd_head[None, :], odd, even)
    w_uq_p = by_parity(_head_block_cols(w_uq, [(0, NOPE_DIM), (NOPE_DIM, ROPE_DIM), tail]),
                       _head_block_cols(w_uq, [(NOPE_DIM, ROPE_DIM), tail, (0, NOPE_DIM)])).astype(BF16)
    w_ukc = w_ukv.reshape(KV_LORA, MLA_HEADS, NOPE_DIM + V_DIM)[:, :, :NOPE_DIM]
    w_ukc = w_ukc.reshape(KV_LORA, MLA_HEADS * NOPE_DIM).astype(BF16)
    w_v_lo = _head_block_cols(w_ukv, [(NOPE_DIM, V_DIM), LANES - V_DIM])
    w_v_hi = _head_block_cols(w_ukv, [LANES - V_DIM, (NOPE_DIM, V_DIM)])
    w_uv = by_parity(w_v_lo, w_v_hi).astype(BF16)
    wvt = w_v_lo.reshape(KV_LORA, MLA_HEADS, LANES)[:, :, :VT_ROWS]
    wvt = jnp.transpose(wvt, (1, 2, 0)).reshape(MLA_HEADS * VT_ROWS, KV_LORA).astype(BF16)

    scale = (NOPE_DIM + ROPE_DIM) ** -0.5 * np.log2(np.e)
    vones = np.zeros((MLA_HEADS, LANES), np.float32)
    vones[0::2, V_DIM] = 1.0
    vones[1::2, 0] = 1.0
    vones = jnp.asarray(vones.reshape(1, HEAD_W))
    zeros = lambda n: jnp.zeros((n,), F32)
    gq = jnp.stack([jnp.concatenate([q_nope_norm, q_rope_norm, zeros(tail)]),
                    jnp.concatenate([q_rope_norm, zeros(tail), q_nope_norm])]) * scale
    gkr = jnp.concatenate([k_rope_norm, zeros(LANES - ROPE_DIM)])
    gkn = jnp.tile(k_nope_norm, 2 * LANES // NOPE_DIM)

    def seg_means(widths):
        m = np.zeros((sum(widths), sum(widths)), np.float32)
        o = 0
        for w in widths:
            m[o:o + w, o:o + w] = 1.0 / w
            o += w
        return m

    segq = jnp.asarray(seg_means([NOPE_DIM, ROPE_DIM, tail, ROPE_DIM, tail, NOPE_DIM]), BF16)
    segk = jnp.asarray(seg_means([NOPE_DIM] * (2 * LANES // NOPE_DIM)), BF16)
    place = jnp.asarray(np.eye(ROPE_DIM, LANES, dtype=np.float32), BF16)

    r_blk = SSM_STEPS_PER_SCAN
    lane_rep = lambda x: jnp.tile(x, (1,) * (x.ndim - 1) + (LANES // x.shape[-1],))
    a2 = jnp.stack([ssm_a_re, ssm_a_im]).astype(F32)
    b2 = jnp.stack([ssm_b_re, ssm_b_im]).astype(F32)
    c2 = jnp.stack([ssm_c_re, ssm_c_im]).astype(F32)
    ldt = ssm_log_dt.astype(F32)
    gl = jax.ShapeDtypeStruct((groups, 1, LANES), F32)
    a_re, a_im, wb5, wc5, wf5 = pl.pallas_call(
        _disc_kernel,
        out_shape=[gl, gl,
                   jax.ShapeDtypeStruct((2, r_blk, groups, SSM_GROUP, LANES), F32),
                   jax.ShapeDtypeStruct((2, r_blk, groups, n_state, LANES), F32),
                   jax.ShapeDtypeStruct((r_blk, r_blk, groups, SSM_GROUP, LANES), F32)],
        name="disc")(
        lane_rep(a2)[:, :, None, :], jnp.broadcast_to(a2[..., None], a2.shape + (LANES,)),
        jnp.broadcast_to(ldt[:, None, None], (groups, 1, LANES)),
        lane_rep(jnp.swapaxes(b2, 2, 3)), lane_rep(c2), lane_rep(jnp.swapaxes(c2, 2, 3)))
    a_re, a_im = a_re[:, 0, :n_state], a_im[:, 0, :n_state]

    return dict(
        ssm_w=ssm_w,
        norm_in=row(norm_in), w_in=w_in_p, qln=row(q_lora_norm), kvln=row(kv_lora_norm),
        w_uq=w_uq_p, w_ukc=w_ukc, w_uv=w_uv, wvt=wvt,
        gq=gq.astype(F32), gkr=row(gkr), gkn=row(gkn), segq=segq, segk=segk, vones=vones,
        place=place, a_re=row(a_re), a_im=row(a_im), wb5=wb5, wc5=wc5, wf5=wf5,
        ssm_d=row(ssm_d), w_glu=w_glu.astype(BF16), b_glu=row(b_glu), onorm_ssm=row(out_norm_ssm),
        onorm_mla=row(out_norm_mla), w_out=w_out.astype(BF16))


def _rope_tables(start, n):
    half = ROPE_DIM // 2
    inv = ROPE_THETA ** (-jnp.arange(half, dtype=F32) / half)
    ang = (start + jnp.arange(n)).astype(F32)[:, None] * inv[None, :]
    cos, sin = jnp.cos(ang), jnp.sin(ang)
    cos_t = jnp.concatenate([cos, cos, jnp.ones((n, LANES - ROPE_DIM), F32)], axis=1)
    sin_a = jnp.concatenate([-sin, jnp.zeros((n, LANES - half), F32)], axis=1)
    sin_b = jnp.concatenate([jnp.zeros((n, half), F32), sin, jnp.zeros((n, LANES - ROPE_DIM), F32)],
                            axis=1)
    return cos_t, sin_a, sin_b


def _mixer(x, pos0, h0re, h0im, past, wts, *, proj_bb, proj_tq, ssm_steps, attn_tk):
    u, gs, gm, q, k, v, ckv, kr = _proj_call(x, _rope_tables(pos0, x.shape[1]), wts,
                                             bb=proj_bb, tq=proj_tq, vt_tile=min(attn_tk, proj_tq))
    mix_ssm, hre, him = _ssm_call(u, gs, h0re, h0im, wts, steps=ssm_steps, unroll=True)
    if past is None:
        y = _attn_causal_call(q, k, v, gm, mix_ssm, x, wts, tq=attn_tk // 2)
    else:
        mla = _attn_cached_call(q, *past, k, v, gm, wts, tk=attn_tk)
        y = _out_call(mix_ssm, mla, x, wts, bb=proj_bb)
    if proj_bb == 1:
        kr = jnp.swapaxes(kr, 1, 2)
    return y, ckv, kr, hre, him


def kernel(x_prompt, x_sample, cache_ckv, cache_krope, state_ssm_re, state_ssm_im, norm_in, w_in, ssm_a_re, ssm_a_im, ssm_log_dt, ssm_b_re, ssm_b_im, ssm_c_re, ssm_c_im, ssm_d, w_glu, b_glu, q_lora_norm, kv_lora_norm, w_uq, w_ukv, q_nope_norm, k_nope_norm, q_rope_norm, k_rope_norm, out_norm_ssm, out_norm_mla, w_out):
    depth = norm_in.shape[0]
    assert depth == 1, "single mixer layer"
    params = (norm_in, w_in, ssm_a_re, ssm_a_im, ssm_log_dt, ssm_b_re, ssm_b_im, ssm_c_re, ssm_c_im,
              ssm_d, w_glu, b_glu, q_lora_norm, kv_lora_norm, w_uq, w_ukv, q_nope_norm, k_nope_norm,
              q_rope_norm, k_rope_norm, out_norm_ssm, out_norm_mla, w_out)
    drop_depth = lambda a: a.reshape(a.shape[1:])
    wts = _prepare_weights(*[drop_depth(p) for p in params])
    groups, n_state = ssm_a_re.shape[1:]
    bp = x_prompt.shape[0]
    bs, ss, _ = x_sample.shape
    past_len = cache_ckv.shape[2]

    zero_state = jnp.zeros((bp, groups * n_state), F32)
    yp, ckv_p, kr_p, re_p, im_p = _mixer(
        x_prompt, 0, zero_state, zero_state, None, wts,
        proj_bb=1, proj_tq=1024, ssm_steps=64, attn_tk=512)
    ys, ckv_s, kr_s, re_s, im_s = _mixer(
        x_sample, past_len,
        state_ssm_re.reshape(bs, groups * n_state), state_ssm_im.reshape(bs, groups * n_state),
        (drop_depth(cache_ckv), jnp.swapaxes(drop_depth(cache_krope), 1, 2)), wts,
        proj_bb=bs // 2, proj_tq=ss, ssm_steps=ss, attn_tk=past_len)

    st = lambda h, bb: h.reshape(1, bb, groups, n_state)
    return (yp, ys, ckv_p[None], kr_p[None], st(re_p, bp), st(im_p, bp),
            ckv_s[None], kr_s[None], st(re_s, bs), st(im_s, bs))
```

```python
import functools

import numpy as np
import jax
import jax.numpy as jnp
from jax import lax
from jax.experimental import pallas as pl
from jax.experimental.pallas import tpu as pltpu

F32 = jnp.float32
BF16 = jnp.bfloat16

CHUNK = 64
SSM_GROUP = 16
MLA_HEADS = 8
NOPE_DIM = 64
ROPE_DIM = 32
V_DIM = 64
Q_LORA = 256
KV_LORA = 128
ROPE_THETA = 10000.0
EPS = 1e-6

LANES = 128
SUBLANES = 8
VT_ROWS = 80
HEAD_W = MLA_HEADS * LANES
GROUPS_PER_BLOCK = 8
SSM_STEPS_PER_SCAN = 4
VMEM_LIMIT = 56 * 1024 * 1024


def _rms(x, gain):
    return x * lax.rsqrt(jnp.mean(x * x, axis=-1, keepdims=True) + EPS) * gain


def _seg_rms(x, seg):
    ms = jnp.dot((x * x).astype(BF16), seg, preferred_element_type=F32)
    return x * lax.rsqrt(ms + EPS)


def _rope_block(x, cos_t, sin_a, sin_b):
    return (x * cos_t + pltpu.roll(x, LANES - ROPE_DIM // 2, 1) * sin_a
            + pltpu.roll(x, ROPE_DIM // 2, 1) * sin_b)


def _key_blocks(ckv_b, kr_blk, w_ukc_ref, gkn_ref, segk_ref, k_ref):
    kc = jnp.dot(ckv_b, w_ukc_ref[...], preferred_element_type=F32)
    kr_even = pltpu.roll(kr_blk, NOPE_DIM, 1)
    low = lax.broadcasted_iota(jnp.int32, (1, LANES), 1) < NOPE_DIM
    blk3 = k_ref.shape[:2] + (LANES,)
    for pp in range(kc.shape[1] // (2 * LANES)):
        kn = _seg_rms(kc[:, 2 * LANES * pp:2 * LANES * (pp + 1)], segk_ref[...]) * gkn_ref[...]
        for t in range(2):
            pair = kn[:, LANES * t:LANES * (t + 1)]
            he = 2 * (2 * pp + t)
            k_ref[:, :, LANES * he:LANES * (he + 1)] = (
                (jnp.where(low, pair, 0.0) + kr_even).astype(BF16).reshape(blk3))
            k_ref[:, :, LANES * (he + 1):LANES * (he + 2)] = (
                (jnp.where(low, 0.0, pair) + kr_blk).astype(BF16).reshape(blk3))


def _proj_kernel(x_ref, cos_ref, sa_ref, sb_ref, norm_in_ref, w_in_ref, qln_ref, kvln_ref,
                 w_uq_ref, w_ukc_ref, gq_ref, gkr_ref, gkn_ref, segq_ref, segk_ref, *rest,
                 ssm_w, transposed):
    v_weights, outs = (rest[:1], rest[1:]) if transposed else (rest[:2], rest[2:])
    u_ref, gs_ref, gm_ref, q_ref, k_ref, v_ref, ckv_ref, kr_ref = outs
    bb, tq, d = x_ref.shape
    x = x_ref[...].reshape(bb * tq, d)
    h = _rms(x, norm_in_ref[...])
    z = jnp.dot(h.astype(BF16), w_in_ref[...], preferred_element_type=F32)
    o = 0
    for dst in (u_ref, gs_ref):
        for j in range(ssm_w // LANES):
            dst[j] = z[:, o + LANES * j:o + LANES * (j + 1)].reshape(bb, tq, LANES)
        o += ssm_w
    c_q = z[:, o:o + Q_LORA]
    o += Q_LORA
    c_kv = z[:, o:o + KV_LORA]
    o += KV_LORA
    kr_raw = z[:, o:o + LANES]
    o += LANES
    gm_ref[...] = z[:, o:].reshape(gm_ref.shape)

    per_row = lambda t_ref: jnp.concatenate([t_ref[...]] * bb, axis=0)
    cos_t, sin_a, sin_b = per_row(cos_ref), per_row(sa_ref), per_row(sb_ref)
    even = lambda t: pltpu.roll(t, NOPE_DIM, 1)
    cos_e, sin_ae, sin_be = even(cos_t), even(sin_a), even(sin_b)
    seg = segq_ref[...]

    kr_ms = jnp.dot((kr_raw * kr_raw).astype(BF16), seg[LANES:, LANES:], preferred_element_type=F32)
    kr_blk = _rope_block(kr_raw * lax.rsqrt(kr_ms + EPS) * gkr_ref[...], cos_t, sin_a, sin_b)
    if transposed:
        kr_ref[0] = kr_blk.T[:ROPE_DIM]
    else:
        kr_ref[...] = kr_blk[:, :ROPE_DIM].reshape(kr_ref.shape)

    q = jnp.dot(_rms(c_q, qln_ref[...]).astype(BF16), w_uq_ref[...], preferred_element_type=F32)
    gq = gq_ref[...]
    for p in range(MLA_HEADS // 2):
        qn = _seg_rms(q[:, 2 * LANES * p:2 * LANES * (p + 1)], seg)
        blocks = (_rope_block(qn[:, :LANES] * gq[0:1], cos_e, sin_ae, sin_be),
                  _rope_block(qn[:, LANES:] * gq[1:2], cos_t, sin_a, sin_b))
        for j in range(2):
            hd = 2 * p + j
            q_ref[:, :, LANES * hd:LANES * (hd + 1)] = blocks[j].astype(BF16).reshape(bb, tq, LANES)

    ckv = _rms(c_kv, kvln_ref[...])
    ckv_ref[...] = ckv.reshape(ckv_ref.shape)
    _key_blocks(ckv.astype(BF16), kr_blk, w_ukc_ref, gkn_ref, segk_ref, k_ref)
    if transposed:
        v_t = jnp.dot(v_weights[0][...], ckv.T.astype(BF16), preferred_element_type=F32)
        ones_row = lax.broadcasted_iota(jnp.int32, v_t.shape, 0) % VT_ROWS == V_DIM
        v_t = jnp.where(ones_row, 1.0, v_t).astype(BF16)
        width = v_ref.shape[3]
        for t in range(v_ref.shape[1]):
            v_ref[0, t] = v_t[:, width * t:width * (t + 1)]
    else:
        v = jnp.dot(ckv.astype(BF16), v_weights[0][...], preferred_element_type=F32) + v_weights[1][...]
        v_ref[...] = v.astype(BF16).reshape(v_ref.shape)


def _full(shape):
    n = len(shape)
    return pl.BlockSpec(shape, lambda *_: (0,) * n)


def _proj_call(x, tables, wts, *, bb, tq, vt_tile):
    b, s, d = x.shape
    ssm_w = wts['ssm_w']
    cos_t, sin_a, sin_b = tables
    transposed = bb == 1
    v_weights = (wts['wvt'],) if transposed else (wts['w_uv'], wts['vones'])
    grid = (s // tq, b // bb)
    row = lambda i, j: (j, i, 0)
    tab = pl.BlockSpec((tq, LANES), lambda i, j: (i, 0))
    in_specs = [pl.BlockSpec((bb, tq, d), row), tab, tab, tab,
                _full(wts['norm_in'].shape), _full(wts['w_in'].shape), _full(wts['qln'].shape),
                _full(wts['kvln'].shape), _full(wts['w_uq'].shape), _full(wts['w_ukc'].shape),
                _full(wts['gq'].shape), _full(wts['gkr'].shape), _full(wts['gkn'].shape),
                _full(wts['segq'].shape), _full(wts['segk'].shape)] + [_full(w.shape) for w in v_weights]
    slabs = ssm_w // LANES
    tb_spec = pl.BlockSpec((slabs, bb, tq, LANES), lambda i, j: (0, j, i, 0))
    kr_shape, kr_spec = (((b, ROPE_DIM, s), pl.BlockSpec((1, ROPE_DIM, tq), lambda i, j: (j, 0, i)))
                         if transposed else
                         ((b, s, ROPE_DIM), pl.BlockSpec((bb, tq, ROPE_DIM), row)))
    v_shape, v_spec = (((b, s // vt_tile, MLA_HEADS * VT_ROWS, vt_tile),
                        pl.BlockSpec((1, tq // vt_tile, MLA_HEADS * VT_ROWS, vt_tile),
                                     lambda i, j: (j, i, 0, 0)))
                       if transposed else
                       ((b, s, HEAD_W), pl.BlockSpec((bb, tq, HEAD_W), row)))
    out_shape = [jax.ShapeDtypeStruct((slabs, b, s, LANES), F32),
                 jax.ShapeDtypeStruct((slabs, b, s, LANES), F32),
                 jax.ShapeDtypeStruct((b, s, ssm_w), F32),
                 jax.ShapeDtypeStruct((b, s, HEAD_W), BF16),
                 jax.ShapeDtypeStruct((b, s, HEAD_W), BF16),
                 jax.ShapeDtypeStruct(v_shape, BF16),
                 jax.ShapeDtypeStruct((b, s, KV_LORA), F32),
                 jax.ShapeDtypeStruct(kr_shape, F32)]
    out_specs = [tb_spec, tb_spec,
                 pl.BlockSpec((bb, tq, ssm_w), row),
                 pl.BlockSpec((bb, tq, HEAD_W), row), pl.BlockSpec((bb, tq, HEAD_W), row), v_spec,
                 pl.BlockSpec((bb, tq, KV_LORA), row), kr_spec]
    return pl.pallas_call(
        functools.partial(_proj_kernel, ssm_w=ssm_w, transposed=transposed),
        out_shape=out_shape, grid=grid, in_specs=in_specs, out_specs=out_specs,
        compiler_params=pltpu.CompilerParams(
            dimension_semantics=("arbitrary", "arbitrary"), vmem_limit_bytes=VMEM_LIMIT),
        name="proj",
    )(x, cos_t, sin_a, sin_b, wts['norm_in'], wts['w_in'], wts['qln'], wts['kvln'], wts['w_uq'],
      wts['w_ukc'], wts['gq'], wts['gkr'], wts['gkn'], wts['segq'], wts['segk'], *v_weights)


def _ssm_kernel(u_ref, gs_ref, h0re_ref, h0im_ref, are_ref, aim_ref, wb5_ref, wc5_ref, wf5_ref, d_ref,
                w_glu_ref, b_glu_ref, onorm_ref, mix_ref, hre_ref, him_ref, us_ref, xs_ref, y_ref,
                wb_ref, wc_ref, wf_ref, *, batch, steps, unroll):
    n_blocks, r_blk = wb_ref.shape[0], wb_ref.shape[1] // LANES
    half = wb_ref.shape[2] // 2
    gpb = wb5_ref.shape[2] // n_blocks
    n_k = steps // r_blk

    @pl.when(pl.program_id(0) == 0)
    def _():
        hre_ref[...] = h0re_ref[...]
        him_ref[...] = h0im_ref[...]

        def own_group(shape, row_div, lane_div):
            return (lax.broadcasted_iota(jnp.int32, shape, 0) // row_div
                    == lax.broadcasted_iota(jnp.int32, shape, 1) // lane_div)

        cin, n_state = wb5_ref.shape[3], half // gpb
        m_b = own_group((LANES, half), cin, n_state)
        m_c = own_group((half, LANES), n_state, cin)
        m_f = own_group((LANES, LANES), cin, cin)
        for gb in range(n_blocks):
            grp = slice(gpb * gb, gpb * (gb + 1))
            for ri in range(2):
                for i in range(r_blk):
                    piece = wb5_ref[ri, i, grp].reshape(LANES, LANES)
                    wide = jnp.concatenate([piece] * (half // LANES), axis=1)
                    wb_ref[gb, LANES * i:LANES * (i + 1), half * ri:half * (ri + 1)] = (
                        jnp.where(m_b, wide, 0.0).astype(BF16))
                    piece = wc5_ref[ri, i, grp].reshape(half, LANES)
                    wc_ref[gb, half * ri:half * (ri + 1), LANES * i:LANES * (i + 1)] = (
                        jnp.where(m_c, piece, 0.0).astype(BF16))
            for i in range(r_blk):
                for j in range(r_blk):
                    piece = wf5_ref[i, j, grp].reshape(LANES, LANES)
                    wf_ref[gb, LANES * i:LANES * (i + 1), LANES * j:LANES * (j + 1)] = (
                        jnp.where(m_f, piece, 0.0).astype(BF16))

    pitch = us_ref.shape[2] // n_k
    for b in range(batch):
        for j in range(n_blocks):
            for i in range(r_blk):
                us_ref[j, i, pl.ds(b, n_k, stride=pitch), :] = (
                    u_ref[j, b, pl.ds(i, n_k, stride=r_blk), :])

    def slab_rows(j, i):
        return jnp.concatenate([us_ref[j, i, pitch * k:pitch * k + batch, :] for k in range(n_k)], axis=0)

    u_slabs = [[slab_rows(j, i) for i in range(r_blk)] for j in range(n_blocks)]
    for gb in range(n_blocks):
        lhs = jnp.concatenate(u_slabs[gb], axis=-1).astype(BF16)
        xs = xs_ref
        xs[...] = jnp.dot(lhs, wb_ref[gb], preferred_element_type=F32)
        cols = slice(half * gb, half * (gb + 1))
        a_re = jnp.broadcast_to(are_ref[:, cols], (batch, half))
        a_im = jnp.broadcast_to(aim_ref[:, cols], (batch, half))

        def step(k, carry, xs=xs, a_re=a_re, a_im=a_im):
            h_re, h_im = carry
            rows = pl.ds(pl.multiple_of(k * batch, batch), batch)
            n_re = a_re * h_re - a_im * h_im + xs[rows, :half]
            n_im = a_re * h_im + a_im * h_re + xs[rows, half:]
            xs[rows, :half] = h_re
            xs[rows, half:] = h_im
            return n_re, n_im

        h_re, h_im = lax.fori_loop(0, n_k, step, (hre_ref[:, cols], him_ref[:, cols]), unroll=unroll)
        hre_ref[:, cols] = h_re
        him_ref[:, cols] = h_im
        y = (jnp.dot(xs[...].astype(BF16), wc_ref[gb], preferred_element_type=F32)
             + jnp.dot(lhs, wf_ref[gb], preferred_element_type=F32))
        for i in range(r_blk):
            y_ref[i, :, LANES * gb:LANES * (gb + 1)] = y[:, LANES * i:LANES * (i + 1)]

    rows = r_blk * n_k * batch
    u = jnp.concatenate([jnp.concatenate(u_slabs[j], axis=0) for j in range(n_blocks)], axis=-1)
    y = y_ref[...].reshape(rows, n_blocks * LANES) + d_ref[...] * u
    yg = jax.nn.gelu(y)
    glu = jnp.dot(yg.astype(BF16), w_glu_ref[...], preferred_element_type=F32) + b_glu_ref[...]
    out = _rms(yg * jax.nn.sigmoid(glu), onorm_ref[...])
    for j in range(n_blocks):
        for i in range(r_blk):
            for k in range(n_k):
                r0 = (i * n_k + k) * batch
                us_ref[j, i, pitch * k:pitch * k + batch, :] = out[r0:r0 + batch, LANES * j:LANES * (j + 1)]
    for b in range(batch):
        for j in range(n_blocks):
            for i in range(r_blk):
                tok = pl.ds(i, n_k, stride=r_blk)
                gs = gs_ref[j, b, tok, :]
                o = us_ref[j, i, pl.ds(b, n_k, stride=pitch), :]
                mix_ref[j, b, tok, :] = o * (gs * jax.nn.sigmoid(gs))


def _ssm_call(u, gs, h0re, h0im, wts, *, steps, unroll):
    n_blocks, batch, seq, _ = u.shape
    n_state = h0re.shape[1]
    r_blk = wts['wb5'].shape[1]
    width = 2 * n_state // n_blocks
    assert steps % r_blk == 0 and wts['wb5'].shape[3] * GROUPS_PER_BLOCK == LANES
    blk = pl.BlockSpec((n_blocks, batch, steps, LANES), lambda i: (0, 0, i, 0))
    names = ['a_re', 'a_im', 'wb5', 'wc5', 'wf5', 'ssm_d', 'w_glu', 'b_glu', 'onorm_ssm']
    n_rows = steps // r_blk * batch
    return pl.pallas_call(
        functools.partial(_ssm_kernel, batch=batch, steps=steps, unroll=unroll),
        out_shape=[jax.ShapeDtypeStruct(u.shape, F32),
                   jax.ShapeDtypeStruct((batch, n_state), F32),
                   jax.ShapeDtypeStruct((batch, n_state), F32)],
        grid=(seq // steps,),
        in_specs=[blk, blk, _full(h0re.shape), _full(h0im.shape)] + [_full(wts[n].shape) for n in names],
        out_specs=[blk, _full((batch, n_state)), _full((batch, n_state))],
        scratch_shapes=[pltpu.VMEM((n_blocks, r_blk, steps // r_blk * (batch + SUBLANES), LANES), F32),
                        pltpu.VMEM((n_rows, width), F32),
                        pltpu.VMEM((r_blk, n_rows, n_blocks * LANES), F32),
                        pltpu.VMEM((n_blocks, r_blk * LANES, width), BF16),
                        pltpu.VMEM((n_blocks, width, r_blk * LANES), BF16),
                        pltpu.VMEM((n_blocks, r_blk * LANES, r_blk * LANES), BF16)],
        compiler_params=pltpu.CompilerParams(
            dimension_semantics=("arbitrary",), vmem_limit_bytes=VMEM_LIMIT),
        name="ssm",
    )(u, gs, h0re, h0im, *[wts[n] for n in names])


def _attn_cached_kernel(q_ref, cf_ref, krf_ref, kl_ref, vl_ref, gm_ref, onorm_ref, w_uv_ref,
                        w_ukc_ref, gkn_ref, segk_ref, place_ref, y_ref,
                        kf_ref, s_full, ml_scr, s_last, m_scr, ctx_scr, den_scr, *, tq, tk):
    kr_blk = lax.dot_general(krf_ref[0].astype(BF16), place_ref[...], (((0,), (0,)), ((), ())),
                             preferred_element_type=F32)
    _key_blocks(cf_ref[0].astype(BF16), kr_blk, w_ukc_ref, gkn_ref, segk_ref, kf_ref)

    n_full = kf_ref.shape[1] // tk
    dn = (((1,), (1,)), ((), ()))
    lane = lax.broadcasted_iota(jnp.int32, (tq, LANES), 1)
    heads = range(MLA_HEADS)
    cols = [slice(LANES * hd, LANES * (hd + 1)) for hd in heads]

    def lane_tiles(s):
        return [s[:, LANES * c:LANES * (c + 1)] for c in range(s.shape[1] // LANES)]

    def probs(s, m_rep):
        if s.shape[1] % LANES == 0:
            p = jnp.concatenate([jnp.exp2(t - m_rep) for t in lane_tiles(s)], axis=-1)
        else:
            p = jnp.exp2(s - m_rep[:, :1])
        return p.astype(BF16)

    ml_scr[...] = jnp.full(ml_scr.shape, -jnp.inf, F32)

    def a_step(jt, carry):
        rows = pl.ds(pl.multiple_of(jt * tk, tk), tk)
        for hd in heads:
            s = lax.dot_general(q_ref[0, :, cols[hd]], kf_ref[0, rows, cols[hd]], dn,
                                preferred_element_type=F32)
            s_full[hd, jt] = s
            ml_scr[hd] = functools.reduce(jnp.maximum, lane_tiles(s), ml_scr[hd])
        return carry

    lax.fori_loop(0, n_full, a_step, 0)

    for hd in heads:
        s = lax.dot_general(q_ref[0, :, cols[hd]], kl_ref[0, :, cols[hd]], dn,
                            preferred_element_type=F32)
        s_last[hd] = s
        m = jnp.maximum(jnp.max(s, axis=-1, keepdims=True),
                        jnp.max(ml_scr[hd], axis=-1, keepdims=True))
        m_scr[hd] = jnp.broadcast_to(m, (tq, LANES))
    ctx_scr[...] = jnp.zeros(ctx_scr.shape, F32)
    den_scr[...] = jnp.zeros(den_scr.shape, F32)

    def b_step(jt, carry):
        rows = pl.ds(pl.multiple_of(jt * tk, tk), tk)
        p = jnp.concatenate([probs(s_full[hd, jt], m_scr[hd]) for hd in heads], axis=0)
        ctx_scr[...] += jnp.dot(p, cf_ref[0, rows, :].astype(BF16), preferred_element_type=F32)
        den_scr[...] += jnp.sum(p.astype(F32), axis=-1, keepdims=True)
        return carry

    lax.fori_loop(0, n_full, b_step, 0)

    outs = []
    for hd in heads:
        mine = slice(tq * hd, tq * (hd + 1))
        cached = jnp.dot(ctx_scr[mine].astype(BF16), w_uv_ref[:, cols[hd]], preferred_element_type=F32)
        p = probs(s_last[hd], m_scr[hd])
        new = jnp.dot(p, vl_ref[0, :, cols[hd]], preferred_element_type=F32)
        ones_col = V_DIM if hd % 2 == 0 else 0
        outs.append((cached + new) / (den_scr[mine] + new[:, ones_col:ones_col + 1]))
    attn = jnp.concatenate([jnp.where(lane < V_DIM, outs[e], outs[e + 1])
                            for e in range(0, MLA_HEADS, 2)], axis=-1)
    gm = gm_ref[0]
    y_ref[0] = (_rms(attn, onorm_ref[...]) * (gm * jax.nn.sigmoid(gm))).astype(BF16)


def _attn_causal_kernel(q_ref, k_ref, vt_ref, gm_ref, ms_ref, x_ref, onorm_ref, w_out_ref, y_ref,
                        s_full, ml_scr, s_last, m_scr, acc_scr, ot_scr, *, tq, tk):
    n_full = pl.program_id(1)
    nt = (((1,), (1,)), ((), ()))
    heads = range(MLA_HEADS)
    cols = [slice(LANES * hd, LANES * (hd + 1)) for hd in heads]
    vrows = [slice(VT_ROWS * hd, VT_ROWS * (hd + 1)) for hd in heads]

    def group_max(s):
        return jnp.max(s.reshape(s.shape[0] // SUBLANES, SUBLANES, s.shape[1]), axis=0)

    def chunk_mask(width):
        kc = lax.broadcasted_iota(jnp.int32, (width, tq), 0) // CHUNK
        qc = lax.broadcasted_iota(jnp.int32, (width, tq), 1) // CHUNK + (width - tq) // CHUNK
        return qc >= kc

    n_tiles = tk // tq
    mine = [slice(tq * t, tq * (t + 1)) for t in range(n_tiles)]
    width = [tq * (t + 1) for t in range(n_tiles)]
    ml_scr[...] = jnp.full(ml_scr.shape, -jnp.inf, F32)

    def phase_a(t, jt, hd):
        rows = pl.ds(pl.multiple_of(jt * tk, tk), tk)
        s = lax.dot_general(k_ref[0, rows, cols[hd]], q_ref[0, mine[t], cols[hd]], nt,
                            preferred_element_type=F32)
        s_full[hd, jt] = s
        ml_scr[t, hd] = jnp.maximum(ml_scr[t, hd], group_max(s))

    def phase_b(t, jt, hd):
        p = jnp.exp2(s_full[hd, jt] - m_scr[t, hd, :1]).astype(BF16)
        acc_scr[t, hd] += jnp.dot(vt_ref[0, jt, vrows[hd], :], p, preferred_element_type=F32)

    def own_a(t):
        own_keys = pl.ds(pl.multiple_of(n_full * tk, tk), width[t])
        mask = chunk_mask(width[t])
        for hd in heads:
            s = lax.dot_general(k_ref[0, own_keys, cols[hd]], q_ref[0, mine[t], cols[hd]], nt,
                                preferred_element_type=F32)
            s = jnp.where(mask, s, -jnp.inf)
            s_last[hd, :width[t]] = s
            m = jnp.max(jnp.maximum(ml_scr[t, hd], group_max(s)), axis=0, keepdims=True)
            m_scr[t, hd] = jnp.broadcast_to(m, m_scr.shape[2:])
            acc_scr[t, hd] = jnp.zeros(acc_scr.shape[2:], F32)

    def finish(t):
        for hd in heads:
            p = jnp.exp2(s_last[hd, :width[t]] - m_scr[t, hd, :1]).astype(BF16)
            acc = acc_scr[t, hd] + jnp.dot(vt_ref[0, n_full, vrows[hd], :width[t]], p,
                                           preferred_element_type=F32)
            ot_scr[V_DIM * hd:V_DIM * (hd + 1), :] = acc[:V_DIM] / acc[V_DIM:V_DIM + 1]
        gm = gm_ref[0, mine[t]]
        mla = _rms(ot_scr[...].T, onorm_ref[...]) * (gm * jax.nn.sigmoid(gm))
        mix_ssm = [ms_ref[j, 0, mine[t]] for j in range(ms_ref.shape[0])]
        mix = jnp.concatenate(mix_ssm + [mla], axis=-1).astype(BF16)
        y_ref[0, mine[t]] = x_ref[0, mine[t]] + jnp.dot(mix, w_out_ref[...], preferred_element_type=F32)

    def loop(body):
        def step(jt, carry):
            for hd in heads:
                body(jt, hd)
            return carry
        lax.fori_loop(0, n_full, step, 0)

    loop(lambda jt, hd: phase_a(0, jt, hd))
    own_a(0)
    for t in range(1, n_tiles):
        loop(lambda jt, hd, t=t: (phase_b(t - 1, jt, hd), phase_a(t, jt, hd)))
        finish(t - 1)
        own_a(t)
    loop(lambda jt, hd: phase_b(n_tiles - 1, jt, hd))
    finish(n_tiles - 1)


def _attn_causal_call(q, k, vt, gm, mix_ssm, x, wts, *, tq):
    b, s, d = x.shape
    tk = vt.shape[3]
    ssm_w = gm.shape[2]
    assert tk % tq == 0 and s % tk == 0 and tq % CHUNK == 0
    row = lambda j, i: (j, i, 0)
    res = lambda j, i: (j, 0, 0)
    n_tiles = tk // tq
    scratch = [pltpu.VMEM((MLA_HEADS, s // tk - 1, tk, tq), F32),
               pltpu.VMEM((n_tiles, MLA_HEADS, SUBLANES, tq), F32),
               pltpu.VMEM((MLA_HEADS, tk, tq), F32),
               pltpu.VMEM((n_tiles, MLA_HEADS, SUBLANES, tq), F32),
               pltpu.VMEM((n_tiles, MLA_HEADS, VT_ROWS, tq), F32),
               pltpu.VMEM((MLA_HEADS * V_DIM, tq), F32)]
    return pl.pallas_call(
        functools.partial(_attn_causal_kernel, tq=tq, tk=tk),
        out_shape=jax.ShapeDtypeStruct((b, s, d), F32),
        grid=(b, s // tk),
        in_specs=[pl.BlockSpec((1, tk, HEAD_W), row), pl.BlockSpec((1, s, HEAD_W), res),
                  pl.BlockSpec((1,) + vt.shape[1:], lambda j, i: (j, 0, 0, 0)),
                  pl.BlockSpec((1, tk, ssm_w), row),
                  pl.BlockSpec((mix_ssm.shape[0], 1, tk, LANES), lambda j, i: (0, j, i, 0)),
                  pl.BlockSpec((1, tk, d), row),
                  _full(wts['onorm_mla'].shape), _full(wts['w_out'].shape)],
        out_specs=pl.BlockSpec((1, tk, d), row),
        scratch_shapes=scratch,
        compiler_params=pltpu.CompilerParams(
            dimension_semantics=("arbitrary", "arbitrary"), vmem_limit_bytes=VMEM_LIMIT),
        name="attn",
    )(q, k, vt, gm, mix_ssm, x, wts['onorm_mla'], wts['w_out'])


def _out_kernel(ms_ref, mla_ref, x_ref, w_out_ref, y_ref):
    bb, tq, d = x_ref.shape
    rows = bb * tq
    mix = jnp.concatenate([ms_ref[j].reshape(rows, LANES).astype(BF16) for j in range(ms_ref.shape[0])]
                          + [mla_ref[...].reshape(rows, mla_ref.shape[2])], axis=-1)
    y = x_ref[...].reshape(rows, d) + jnp.dot(mix, w_out_ref[...], preferred_element_type=F32)
    y_ref[...] = y.reshape(bb, tq, d)


def _out_call(mix_ssm, mla, x, wts, *, bb):
    b, s, d = x.shape
    row = lambda j: (j, 0, 0)
    return pl.pallas_call(
        _out_kernel,
        out_shape=jax.ShapeDtypeStruct((b, s, d), F32),
        grid=(b // bb,),
        in_specs=[pl.BlockSpec((mix_ssm.shape[0], bb, s, LANES), lambda j: (0, j, 0, 0)),
                  pl.BlockSpec((bb, s, mla.shape[2]), row), pl.BlockSpec((bb, s, d), row),
                  _full(wts['w_out'].shape)],
        out_specs=pl.BlockSpec((bb, s, d), row),
        compiler_params=pltpu.CompilerParams(
            dimension_semantics=("arbitrary",), vmem_limit_bytes=VMEM_LIMIT),
        name="outproj",
    )(mix_ssm, mla, x, wts['w_out'])


def _attn_cached_call(q, ckv_full, kr_full_t, k_last, v_last, gm, wts, *, tk):
    b, tq, _ = q.shape
    t_full, last_len = ckv_full.shape[1], k_last.shape[1]
    ssm_w = gm.shape[2]
    blk = lambda rows, width: pl.BlockSpec((1, rows, width), lambda j: (j, 0, 0))
    consts = [wts[n] for n in ('onorm_mla', 'w_uv', 'w_ukc', 'gkn', 'segk', 'place')]
    scratch = [pltpu.VMEM((1, t_full, HEAD_W), BF16),
               pltpu.VMEM((MLA_HEADS, t_full // tk, tq, tk), F32), pltpu.VMEM((MLA_HEADS, tq, LANES), F32),
               pltpu.VMEM((MLA_HEADS, tq, last_len), F32), pltpu.VMEM((MLA_HEADS, tq, LANES), F32),
               pltpu.VMEM((MLA_HEADS * tq, ckv_full.shape[2]), F32), pltpu.VMEM((MLA_HEADS * tq, 1), F32)]
    return pl.pallas_call(
        functools.partial(_attn_cached_kernel, tq=tq, tk=tk),
        out_shape=jax.ShapeDtypeStruct((b, tq, ssm_w), BF16),
        grid=(b,),
        in_specs=[blk(tq, HEAD_W), blk(t_full, ckv_full.shape[2]), blk(ROPE_DIM, t_full),
                  blk(last_len, HEAD_W), blk(last_len, HEAD_W), blk(tq, ssm_w)]
                 + [_full(c.shape) for c in consts],
        out_specs=blk(tq, ssm_w),
        scratch_shapes=scratch,
        compiler_params=pltpu.CompilerParams(
            dimension_semantics=("arbitrary",), vmem_limit_bytes=VMEM_LIMIT),
        name="attn_cached",
    )(q, ckv_full, kr_full_t, k_last, v_last, gm, *consts)


def _head_block_cols(w, pieces):
    k = w.shape[0]
    w3 = w.reshape(k, MLA_HEADS, w.shape[1] // MLA_HEADS)
    cols = [w3[:, :, p[0]:p[0] + p[1]] if isinstance(p, tuple) else jnp.zeros((k, MLA_HEADS, p), w.dtype)
            for p in pieces]
    return jnp.concatenate(cols, axis=-1).reshape(k, HEAD_W)


def _zoh_powers(lr, li, dt, n):
    mag = jnp.exp(lr * dt)
    ar, ai = mag * jnp.cos(li * dt), mag * jnp.sin(li * dt)
    pw = [(jnp.ones_like(ar), jnp.zeros_like(ai))]
    for _ in range(n):
        pr, pi = pw[-1]
        pw.append((pr * ar - pi * ai, pr * ai + pi * ar))
    return pw


def _disc_kernel(arow_ref, acol_ref, ldt_ref, bt_ref, cd_ref, ct_ref,
                 apr_ref, api_ref, wb_ref, wc_ref, wf_ref):
    r_blk, groups, cin = wb_ref.shape[1], wb_ref.shape[2], wb_ref.shape[3]
    n_state = acol_ref.shape[2]
    gpb = LANES // cin
    lr, li = arow_ref[0], arow_ref[1]
    dt = jnp.exp(ldt_ref[...])
    pw = _zoh_powers(lr, li, dt, r_blk)
    ar, ai = pw[1]
    den = lr * lr + li * li
    kr = ((ar - 1.0) * lr + ai * li) / den
    ki = (ai * lr - (ar - 1.0) * li) / den
    apr_ref[...], api_ref[...] = pw[r_blk]

    pwc = _zoh_powers(acol_ref[0], acol_ref[1], dt, r_blk)
    for j in range(r_blk):
        pr, pi = pwc[j + 1]
        wc_ref[0, j] = ct_ref[0] * pr - ct_ref[1] * pi
        wc_ref[1, j] = -(ct_ref[0] * pi + ct_ref[1] * pr)

    b_r = kr * bt_ref[0] - ki * bt_ref[1]
    b_i = kr * bt_ref[1] + ki * bt_ref[0]
    for i in range(r_blk):
        pr, pi = pw[r_blk - 1 - i]
        wb_ref[0, i] = pr * b_r - pi * b_i
        wb_ref[1, i] = pr * b_i + pi * b_r

    nt = (((1,), (1,)), ((), ()))
    first_copy = lax.broadcasted_iota(jnp.int32, (LANES, LANES), 1) < n_state
    block = lambda v, n: v[gpb * n:gpb * (n + 1)].reshape(LANES, LANES)
    for i in range(r_blk):
        for j in range(i):
            wf_ref[i, j] = jnp.zeros(wf_ref.shape[2:], F32)
    for m in range(r_blk):
        pr, pi = pw[m]
        ca_r = cd_ref[0] * pr - cd_ref[1] * pi
        ca_i = cd_ref[0] * pi + cd_ref[1] * pr
        for n in range(groups // gpb):
            f_t = (lax.dot_general(jnp.where(first_copy, block(b_r, n), 0.0), block(ca_r, n), nt,
                                   precision=lax.Precision.HIGHEST, preferred_element_type=F32)
                   - lax.dot_general(jnp.where(first_copy, block(b_i, n), 0.0), block(ca_i, n), nt,
                                     precision=lax.Precision.HIGHEST, preferred_element_type=F32))
            for i in range(r_blk - m):
                wf_ref[i, i + m, gpb * n:gpb * (n + 1)] = f_t.reshape(gpb, cin, LANES)


def _prepare_weights(norm_in, w_in, ssm_a_re, ssm_a_im, ssm_log_dt, ssm_b_re, ssm_b_im, ssm_c_re,
                     ssm_c_im, ssm_d, w_glu, b_glu, q_lora_norm, kv_lora_norm, w_uq, w_ukv,
                     q_nope_norm, k_nope_norm, q_rope_norm, k_rope_norm, out_norm_ssm,
                     out_norm_mla, w_out):
    groups, n_state = ssm_a_re.shape
    ssm_w = groups * SSM_GROUP
    row = lambda v: v.reshape(1, -1).astype(F32)
    tail = LANES - ROPE_DIM - NOPE_DIM

    o_kr = 2 * ssm_w + Q_LORA + KV_LORA
    w_in_p = jnp.concatenate(
        [w_in[:, :o_kr + ROPE_DIM], jnp.zeros((w_in.shape[0], LANES - ROPE_DIM), w_in.dtype),
         w_in[:, o_kr + ROPE_DIM:]], axis=1).astype(BF16)

    odd_head = (np.arange(HEAD_W) // LANES) % 2 == 1
    by_parity = lambda even, odd: jnp.where(odd_head[None, :], odd, even)
    w_uq_p = by_parity(_head_block_cols(w_uq, [(0, NOPE_DIM), (NOPE_DIM, ROPE_DIM), tail]),
                       _head_block_cols(w_uq, [(NOPE_DIM, ROPE_DIM), tail, (0, NOPE_DIM)])).astype(BF16)
    w_ukc = w_ukv.reshape(KV_LORA, MLA_HEADS, NOPE_DIM + V_DIM)[:, :, :NOPE_DIM]
    w_ukc = w_ukc.reshape(KV_LORA, MLA_HEADS * NOPE_DIM).astype(BF16)
    w_v_lo = _head_block_cols(w_ukv, [(NOPE_DIM, V_DIM), LANES - V_DIM])
    w_v_hi = _head_block_cols(w_ukv, [LANES - V_DIM, (NOPE_DIM, V_DIM)])
    w_uv = by_parity(w_v_lo, w_v_hi).astype(BF16)
    wvt = w_v_lo.reshape(KV_LORA, MLA_HEADS, LANES)[:, :, :VT_ROWS]
    wvt = jnp.transpose(wvt, (1, 2, 0)).reshape(MLA_HEADS * VT_ROWS, KV_LORA).astype(BF16)

    scale = (NOPE_DIM + ROPE_DIM) ** -0.5 * np.log2(np.e)
    vones = np.zeros((MLA_HEADS, LANES), np.float32)
    vones[0::2, V_DIM] = 1.0
    vones[1::2, 0] = 1.0
    vones = jnp.asarray(vones.reshape(1, HEAD_W))
    zeros = lambda n: jnp.zeros((n,), F32)
    gq = jnp.stack([jnp.concatenate([q_nope_norm, q_rope_norm, zeros(tail)]),
                    jnp.concatenate([q_rope_norm, zeros(tail), q_nope_norm])]) * scale
    gkr = jnp.concatenate([k_rope_norm, zeros(LANES - ROPE_DIM)])
    gkn = jnp.tile(k_nope_norm, 2 * LANES // NOPE_DIM)

    def seg_means(widths):
        m = np.zeros((sum(widths), sum(widths)), np.float32)
        o = 0
        for w in widths:
            m[o:o + w, o:o + w] = 1.0 / w
            o += w
        return m

    segq = jnp.asarray(seg_means([NOPE_DIM, ROPE_DIM, tail, ROPE_DIM, tail, NOPE_DIM]), BF16)
    segk = jnp.asarray(seg_means([NOPE_DIM] * (2 * LANES // NOPE_DIM)), BF16)
    place = jnp.asarray(np.eye(ROPE_DIM, LANES, dtype=np.float32), BF16)

    r_blk = SSM_STEPS_PER_SCAN
    lane_rep = lambda x: jnp.tile(x, (1,) * (x.ndim - 1) + (LANES // x.shape[-1],))
    a2 = jnp.stack([ssm_a_re, ssm_a_im]).astype(F32)
    b2 = jnp.stack([ssm_b_re, ssm_b_im]).astype(F32)
    c2 = jnp.stack([ssm_c_re, ssm_c_im]).astype(F32)
    ldt = ssm_log_dt.astype(F32)
    gl = jax.ShapeDtypeStruct((groups, 1, LANES), F32)
    a_re, a_im, wb5, wc5, wf5 = pl.pallas_call(
        _disc_kernel,
        out_shape=[gl, gl,
                   jax.ShapeDtypeStruct((2, r_blk, groups, SSM_GROUP, LANES), F32),
                   jax.ShapeDtypeStruct((2, r_blk, groups, n_state, LANES), F32),
                   jax.ShapeDtypeStruct((r_blk, r_blk, groups, SSM_GROUP, LANES), F32)],
        name="disc")(
        lane_rep(a2)[:, :, None, :], jnp.broadcast_to(a2[..., None], a2.shape + (LANES,)),
        jnp.broadcast_to(ldt[:, None, None], (groups, 1, LANES)),
        lane_rep(jnp.swapaxes(b2, 2, 3)), lane_rep(c2), lane_rep(jnp.swapaxes(c2, 2, 3)))
    a_re, a_im = a_re[:, 0, :n_state], a_im[:, 0, :n_state]

    return dict(
        ssm_w=ssm_w,
        norm_in=row(norm_in), w_in=w_in_p, qln=row(q_lora_norm), kvln=row(kv_lora_norm),
        w_uq=w_uq_p, w_ukc=w_ukc, w_uv=w_uv, wvt=wvt,
        gq=gq.astype(F32), gkr=row(gkr), gkn=row(gkn), segq=segq, segk=segk, vones=vones,
        place=place, a_re=row(a_re), a_im=row(a_im), wb5=wb5, wc5=wc5, wf5=wf5,
        ssm_d=row(ssm_d), w_glu=w_glu.astype(BF16), b_glu=row(b_glu), onorm_ssm=row(out_norm_ssm),
        onorm_mla=row(out_norm_mla), w_out=w_out.astype(BF16))


def _rope_tables(start, n):
    half = ROPE_DIM // 2
    inv = ROPE_THETA ** (-jnp.arange(half, dtype=F32) / half)
    ang = (start + jnp.arange(n)).astype(F32)[:, None] * inv[None, :]
    cos, sin = jnp.cos(ang), jnp.sin(ang)
    cos_t = jnp.concatenate([cos, cos, jnp.ones((n, LANES - ROPE_DIM), F32)], axis=1)
    sin_a = jnp.concatenate([-sin, jnp.zeros((n, LANES - half), F32)], axis=1)
    sin_b = jnp.concatenate([jnp.zeros((n, half), F32), sin, jnp.zeros((n, LANES - ROPE_DIM), F32)],
                            axis=1)
    return cos_t, sin_a, sin_b


def _mixer(x, pos0, h0re, h0im, past, wts, *, proj_bb, proj_tq, ssm_steps, attn_tk):
    u, gs, gm, q, k, v, ckv, kr = _proj_call(x, _rope_tables(pos0, x.shape[1]), wts,
                                             bb=proj_bb, tq=proj_tq, vt_tile=min(attn_tk, proj_tq))
    mix_ssm, hre, him = _ssm_call(u, gs, h0re, h0im, wts, steps=ssm_steps, unroll=True)
    if past is None:
        y = _attn_causal_call(q, k, v, gm, mix_ssm, x, wts, tq=attn_tk // 2)
    else:
        mla = _attn_cached_call(q, *past, k, v, gm, wts, tk=attn_tk)
        y = _out_call(mix_ssm, mla, x, wts, bb=proj_bb)
    if proj_bb == 1:
        kr = jnp.swapaxes(kr, 1, 2)
    return y, ckv, kr, hre, him


def kernel(x_prompt, x_sample, cache_ckv, cache_krope, state_ssm_re, state_ssm_im, norm_in, w_in, ssm_a_re, ssm_a_im, ssm_log_dt, ssm_b_re, ssm_b_im, ssm_c_re, ssm_c_im, ssm_d, w_glu, b_glu, q_lora_norm, kv_lora_norm, w_uq, w_ukv, q_nope_norm, k_nope_norm, q_rope_norm, k_rope_norm, out_norm_ssm, out_norm_mla, w_out):
    depth = norm_in.shape[0]
    assert depth == 1, "single mixer layer"
    params = (norm_in, w_in, ssm_a_re, ssm_a_im, ssm_log_dt, ssm_b_re, ssm_b_im, ssm_c_re, ssm_c_im,
              ssm_d, w_glu, b_glu, q_lora_norm, kv_lora_norm, w_uq, w_ukv, q_nope_norm, k_nope_norm,
              q_rope_norm, k_rope_norm, out_norm_ssm, out_norm_mla, w_out)
    drop_depth = lambda a: a.reshape(a.shape[1:])
    wts = _prepare_weights(*[drop_depth(p) for p in params])
    groups, n_state = ssm_a_re.shape[1:]
    bp = x_prompt.shape[0]
    bs, ss, _ = x_sample.shape
    past_len = cache_ckv.shape[2]

    zero_state = jnp.zeros((bp, groups * n_state), F32)
    yp, ckv_p, kr_p, re_p, im_p = _mixer(
        x_prompt, 0, zero_state, zero_state, None, wts,
        proj_bb=1, proj_tq=1024, ssm_steps=64, attn_tk=512)
    ys, ckv_s, kr_s, re_s, im_s = _mixer(
        x_sample, past_len,
        state_ssm_re.reshape(bs, groups * n_state), state_ssm_im.reshape(bs, groups * n_state),
        (drop_depth(cache_ckv), jnp.swapaxes(drop_depth(cache_krope), 1, 2)), wts,
        proj_bb=bs // 2, proj_tq=ss, ssm_steps=ss, attn_tk=past_len)

    st = lambda h, bb: h.reshape(1, bb, groups, n_state)
    return (yp, ys, ckv_p[None], kr_p[None], st(re_p, bp), st(im_p, bp),
            ckv_s[None], kr_s[None], st(re_s, bs), st(im_s, bs))
```
